```python
import jax, jax.numpy as jnp
from jax import lax
import numpy as np

D_MODEL = 2048
BATCH = 8
SEQ = 8192
DEPTH = 4

CHUNK = 64
N_MEM = 256
N_A_LAYERS = DEPTH // 2
N_B_LAYERS = DEPTH - N_A_LAYERS
D_RNN = 3 * D_MODEL // 4
RNN_BLOCK = 128
N_RNN_BLOCKS = D_RNN // RNN_BLOCK
CONV_WIDTH = 4
LRU_C = 8.0
SB_HEAD_DIM = 128
D_SB = 3 * D_MODEL // 4
N_SB_HEADS = D_SB // SB_HEAD_DIM
N_MEM_HEADS = 4
MEM_HEAD_DIM = 128
D_MEM = N_MEM_HEADS * MEM_HEAD_DIM
D_MIX = D_RNN + D_MEM
Q_BLOCK = 128
EPS = 1e-6

kernel_name = "yoco_rglru_stickbreaking_memory_trunk"


def rms_norm(x, g):
    xf = x.astype(jnp.float32)
    y = xf * lax.rsqrt(jnp.mean(xf * xf, axis=-1, keepdims=True) + EPS)
    return (y * g.astype(jnp.float32)).astype(x.dtype)


def memory_attention(q, mem_k, mem_v):
    b, t, _ = q.shape
    qh = q.reshape(b, t, N_MEM_HEADS, MEM_HEAD_DIM)
    kh = mem_k.reshape(b, N_MEM, N_MEM_HEADS, MEM_HEAD_DIM)
    vh = mem_v.reshape(b, N_MEM, N_MEM_HEADS, MEM_HEAD_DIM)
    s = jnp.einsum('bthd,bmhd->bhtm', qh, kh).astype(jnp.float32) * (MEM_HEAD_DIM ** -0.5)
    p = jax.nn.softmax(s, axis=-1).astype(vh.dtype)
    o = jnp.einsum('bhtm,bmhd->bthd', p, vh)
    return o.reshape(b, t, D_MEM)


def causal_depthwise_conv(x, w, bias):
    y = lax.conv_general_dilated(
        x, w[:, None, :], window_strides=(1,), padding=[(CONV_WIDTH - 1, 0)],
        dimension_numbers=('NWC', 'WIO', 'NWC'), feature_group_count=x.shape[-1])
    return y + bias


def rg_lru(x, w_r, b_r, w_i, b_i, lam):
    b, t, _ = x.shape
    xb = x.reshape(b, t, N_RNN_BLOCKS, RNN_BLOCK)
    r = jax.nn.sigmoid(jnp.einsum('btnc,ncd->btnd', xb, w_r).reshape(b, t, D_RNN) + b_r)
    i = jax.nn.sigmoid(jnp.einsum('btnc,ncd->btnd', xb, w_i).reshape(b, t, D_RNN) + b_i)
    log_a = (-LRU_C * jax.nn.softplus(-lam.astype(jnp.float32))) * r.astype(jnp.float32)
    a = jnp.exp(log_a)
    u = jnp.sqrt(-jnp.expm1(2.0 * log_a)) * (i * x).astype(jnp.float32)

    def combine(left, right):
        a1, h1 = left
        a2, h2 = right
        return a1 * a2, a2 * h1 + h2

    _, h = lax.associative_scan(combine, (a, u), axis=1)
    return h.astype(x.dtype)


def stick_breaking_attention(q, k, v):
    b, t, _ = q.shape
    n_blk = t // Q_BLOCK
    qh = q.reshape(b, n_blk, Q_BLOCK, N_SB_HEADS, SB_HEAD_DIM).transpose(1, 0, 3, 2, 4)
    kh = k.reshape(b, t, N_SB_HEADS, SB_HEAD_DIM).transpose(0, 2, 1, 3)
    vh = v.reshape(b, t, N_SB_HEADS, SB_HEAD_DIM).transpose(0, 2, 1, 3)
    key_pos = jnp.arange(t)
    scale = SB_HEAD_DIM ** -0.5

    def one_block(args):
        q_blk, blk = args
        q_pos = blk * Q_BLOCK + jnp.arange(Q_BLOCK)
        mask = key_pos[None, :] < q_pos[:, None]
        z = jnp.einsum('bhqd,bhkd->bhqk', q_blk, kh).astype(jnp.float32) * scale
        log_beta = jax.nn.log_sigmoid(z)
        log_keep = jnp.where(mask, jax.nn.log_sigmoid(-z), 0.0)
        later = lax.cumsum(log_keep, axis=3, reverse=True) - log_keep
        w = jnp.where(mask, jnp.exp(log_beta + later), 0.0)
        return jnp.einsum('bhqk,bhkd->bhqd', w.astype(vh.dtype), vh)

    o = lax.map(one_block, (qh, jnp.arange(n_blk)))
    return o.transpose(1, 0, 3, 2, 4).reshape(b, t, D_SB)


def layer_a(h, g_norm, w_in, conv_w, conv_b, w_r, b_r, w_i, b_i, lam, w_out, mem_k, mem_v):
    u = rms_norm(h, g_norm)
    proj = u @ w_in
    x_rnn, g_rnn, q_mem, g_mem = jnp.split(proj, [D_RNN, 2 * D_RNN, 2 * D_RNN + D_MEM], axis=-1)
    x_rnn = causal_depthwise_conv(x_rnn, conv_w, conv_b)
    y_rnn = rg_lru(x_rnn, w_r, b_r, w_i, b_i, lam) * jax.nn.silu(g_rnn)
    y_mem = memory_attention(q_mem, mem_k, mem_v) * jax.nn.silu(g_mem)
    return h + jnp.concatenate([y_rnn, y_mem], axis=-1) @ w_out


def layer_b(h, g_norm, w_in, w_out, k_sb, v_sb, mem_k, mem_v):
    u = rms_norm(h, g_norm)
    proj = u @ w_in
    q_sb, g_sb, q_mem, g_mem = jnp.split(proj, [D_SB, 2 * D_SB, 2 * D_SB + D_MEM], axis=-1)
    y_sb = stick_breaking_attention(q_sb, k_sb, v_sb) * jax.nn.silu(g_sb)
    y_mem = memory_attention(q_mem, mem_k, mem_v) * jax.nn.silu(g_mem)
    return h + jnp.concatenate([y_sb, y_mem], axis=-1) @ w_out


def _fwd_setup_inputs(seed: int = 0) -> dict:
    key = jax.random.key(seed)
    ks = jax.random.split(key, 24)
    f32 = jnp.float32

    def normal(k, shape, scale):
        return jax.random.normal(k, shape, f32) * scale

    def gain(k, shape):
        return 1.0 + 0.02 * jax.random.normal(k, shape, f32)

    a_pow = jax.random.uniform(ks[13], (N_A_LAYERS, D_RNN), f32, 0.9, 0.999)
    s = a_pow ** (1.0 / LRU_C)
    lru_lambda = jnp.log(s) - jnp.log1p(-s)
    return {
        "x": normal(ks[0], (BATCH, SEQ, D_MODEL), 1.0),
        "mem": normal(ks[1], (BATCH, N_MEM, D_MODEL), 1.0),
        "mem_norm": gain(ks[2], (D_MODEL,)),
        "w_mem_kv": normal(ks[3], (DEPTH, D_MODEL, 2 * D_MEM), D_MODEL ** -0.5),
        "norm_a": gain(ks[4], (N_A_LAYERS, D_MODEL)),
        "w_in_a": normal(ks[5], (N_A_LAYERS, D_MODEL, 2 * D_RNN + 2 * D_MEM), D_MODEL ** -0.5),
        "conv_w": normal(ks[6], (N_A_LAYERS, CONV_WIDTH, D_RNN), CONV_WIDTH ** -0.5),
        "conv_b": normal(ks[7], (N_A_LAYERS, D_RNN), 0.01),
        "w_rec_gate": normal(ks[8], (N_A_LAYERS, N_RNN_BLOCKS, RNN_BLOCK, RNN_BLOCK), RNN_BLOCK ** -0.5),
        "b_rec_gate": normal(ks[9], (N_A_LAYERS, D_RNN), 0.01),
        "w_in_gate": normal(ks[10], (N_A_LAYERS, N_RNN_BLOCKS, RNN_BLOCK, RNN_BLOCK), RNN_BLOCK ** -0.5),
        "b_in_gate": normal(ks[11], (N_A_LAYERS, D_RNN), 0.01),
        "lru_lambda": lru_lambda,
        "w_out_a": normal(ks[12], (N_A_LAYERS, D_MIX, D_MODEL), D_MIX ** -0.5),
        "kv_norm": gain(ks[14], (D_MODEL,)),
        "w_kv": normal(ks[15], (D_MODEL, 2 * D_SB), D_MODEL ** -0.5),
        "norm_b": gain(ks[16], (N_B_LAYERS, D_MODEL)),
        "w_in_b": normal(ks[17], (N_B_LAYERS, D_MODEL, 2 * D_SB + 2 * D_MEM), D_MODEL ** -0.5),
        "w_out_b": normal(ks[18], (N_B_LAYERS, D_SB + D_MEM, D_MODEL), (D_SB + D_MEM) ** -0.5),
        "final_norm": gain(ks[19], (D_MODEL,)),
    }


def _fwd_reference(x, mem, mem_norm, w_mem_kv, norm_a, w_in_a, conv_w, conv_b, w_rec_gate, b_rec_gate,
              w_in_gate, b_in_gate, lru_lambda, w_out_a, kv_norm, w_kv, norm_b, w_in_b, w_out_b,
              final_norm):
    mem_n = rms_norm(mem, mem_norm)
    h = x
    k_sb = None
    v_sb = None
    for layer in range(DEPTH):
        mem_k, mem_v = jnp.split(mem_n @ w_mem_kv[layer], 2, axis=-1)
        if layer < N_A_LAYERS:
            h = layer_a(h, norm_a[layer], w_in_a[layer], conv_w[layer], conv_b[layer],
                        w_rec_gate[layer], b_rec_gate[layer], w_in_gate[layer], b_in_gate[layer],
                        lru_lambda[layer], w_out_a[layer], mem_k, mem_v)
            if layer == N_A_LAYERS - 1:
                k_sb, v_sb = jnp.split(rms_norm(h, kv_norm) @ w_kv, 2, axis=-1)
        else:
            j = layer - N_A_LAYERS
            h = layer_b(h, norm_b[j], w_in_b[j], w_out_b[j], k_sb, v_sb, mem_k, mem_v)
    return rms_norm(h, final_norm)


import jax as _jax
import jax.numpy as _jnp

TWIN_FORMAT = 'train_step'
FWD_PARAMS = ['x', 'mem', 'mem_norm', 'w_mem_kv', 'norm_a', 'w_in_a', 'conv_w', 'conv_b', 'w_rec_gate', 'b_rec_gate', 'w_in_gate', 'b_in_gate', 'lru_lambda', 'w_out_a', 'kv_norm', 'w_kv', 'norm_b', 'w_in_b', 'w_out_b', 'final_norm']
TWIN_WEIGHTS = ['mem_norm', 'w_mem_kv', 'norm_a', 'w_in_a', 'conv_w', 'conv_b', 'w_rec_gate', 'b_rec_gate', 'w_in_gate', 'b_in_gate', 'lru_lambda', 'w_out_a', 'kv_norm', 'w_kv', 'norm_b', 'w_in_b', 'w_out_b', 'final_norm']
TWIN_DIFF_INPUT = 'x'
TWIN_INPUTS = ['x', 'mem', 'mem_norm', 'w_mem_kv', 'norm_a', 'w_in_a', 'conv_w', 'conv_b', 'w_rec_gate', 'b_rec_gate', 'w_in_gate', 'b_in_gate', 'lru_lambda', 'w_out_a', 'kv_norm', 'w_kv', 'norm_b', 'w_in_b', 'w_out_b', 'final_norm', 'loss_target', 'm_mem_norm', 'm_w_mem_kv', 'm_norm_a', 'm_w_in_a', 'm_conv_w', 'm_conv_b', 'm_w_rec_gate', 'm_b_rec_gate', 'm_w_in_gate', 'm_b_in_gate', 'm_lru_lambda', 'm_w_out_a', 'm_kv_norm', 'm_w_kv', 'm_norm_b', 'm_w_in_b', 'm_w_out_b', 'm_final_norm', 'v_mem_norm', 'v_w_mem_kv', 'v_norm_a', 'v_w_in_a', 'v_conv_w', 'v_conv_b', 'v_w_rec_gate', 'v_b_rec_gate', 'v_w_in_gate', 'v_b_in_gate', 'v_lru_lambda', 'v_w_out_a', 'v_kv_norm', 'v_w_kv', 'v_norm_b', 'v_w_in_b', 'v_w_out_b', 'v_final_norm']
TWIN_OUTPUTS = ['loss', 'grad_x', 'grad_mem_norm', 'grad_w_mem_kv', 'grad_norm_a', 'grad_w_in_a', 'grad_conv_w', 'grad_conv_b', 'grad_w_rec_gate', 'grad_b_rec_gate', 'grad_w_in_gate', 'grad_b_in_gate', 'grad_lru_lambda', 'grad_w_out_a', 'grad_kv_norm', 'grad_w_kv', 'grad_norm_b', 'grad_w_in_b', 'grad_w_out_b', 'grad_final_norm', 'delta_mem_norm', 'delta_w_mem_kv', 'delta_norm_a', 'delta_w_in_a', 'delta_conv_w', 'delta_conv_b', 'delta_w_rec_gate', 'delta_b_rec_gate', 'delta_w_in_gate', 'delta_b_in_gate', 'delta_lru_lambda', 'delta_w_out_a', 'delta_kv_norm', 'delta_w_kv', 'delta_norm_b', 'delta_w_in_b', 'delta_w_out_b', 'delta_final_norm', 'new_m_mem_norm', 'new_m_w_mem_kv', 'new_m_norm_a', 'new_m_w_in_a', 'new_m_conv_w', 'new_m_conv_b', 'new_m_w_rec_gate', 'new_m_b_rec_gate', 'new_m_w_in_gate', 'new_m_b_in_gate', 'new_m_lru_lambda', 'new_m_w_out_a', 'new_m_kv_norm', 'new_m_w_kv', 'new_m_norm_b', 'new_m_w_in_b', 'new_m_w_out_b', 'new_m_final_norm', 'new_v_mem_norm', 'new_v_w_mem_kv', 'new_v_norm_a', 'new_v_w_in_a', 'new_v_conv_w', 'new_v_conv_b', 'new_v_w_rec_gate', 'new_v_b_rec_gate', 'new_v_w_in_gate', 'new_v_b_in_gate', 'new_v_lru_lambda', 'new_v_w_out_a', 'new_v_kv_norm', 'new_v_w_kv', 'new_v_norm_b', 'new_v_w_in_b', 'new_v_w_out_b', 'new_v_final_norm']
TWIN_LEAF_KINDS = {'loss': 'loss', 'grad_x': 'grad_x', 'grad_mem_norm': 'grad_w', 'grad_w_mem_kv': 'grad_w', 'grad_norm_a': 'grad_w', 'grad_w_in_a': 'grad_w', 'grad_conv_w': 'grad_w', 'grad_conv_b': 'grad_w', 'grad_w_rec_gate': 'grad_w', 'grad_b_rec_gate': 'grad_w', 'grad_w_in_gate': 'grad_w', 'grad_b_in_gate': 'grad_w', 'grad_lru_lambda': 'grad_w', 'grad_w_out_a': 'grad_w', 'grad_kv_norm': 'grad_w', 'grad_w_kv': 'grad_w', 'grad_norm_b': 'grad_w', 'grad_w_in_b': 'grad_w', 'grad_w_out_b': 'grad_w', 'grad_final_norm': 'grad_w', 'delta_mem_norm': 'delta_w', 'delta_w_mem_kv': 'delta_w', 'delta_norm_a': 'delta_w', 'delta_w_in_a': 'delta_w', 'delta_conv_w': 'delta_w', 'delta_conv_b': 'delta_w', 'delta_w_rec_gate': 'delta_w', 'delta_b_rec_gate': 'delta_w', 'delta_w_in_gate': 'delta_w', 'delta_b_in_gate': 'delta_w', 'delta_lru_lambda': 'delta_w', 'delta_w_out_a': 'delta_w', 'delta_kv_norm': 'delta_w', 'delta_w_kv': 'delta_w', 'delta_norm_b': 'delta_w', 'delta_w_in_b': 'delta_w', 'delta_w_out_b': 'delta_w', 'delta_final_norm': 'delta_w', 'new_m_mem_norm': 'new_m', 'new_m_w_mem_kv': 'new_m', 'new_m_norm_a': 'new_m', 'new_m_w_in_a': 'new_m', 'new_m_conv_w': 'new_m', 'new_m_conv_b': 'new_m', 'new_m_w_rec_gate': 'new_m', 'new_m_b_rec_gate': 'new_m', 'new_m_w_in_gate': 'new_m', 'new_m_b_in_gate': 'new_m', 'new_m_lru_lambda': 'new_m', 'new_m_w_out_a': 'new_m', 'new_m_kv_norm': 'new_m', 'new_m_w_kv': 'new_m', 'new_m_norm_b': 'new_m', 'new_m_w_in_b': 'new_m', 'new_m_w_out_b': 'new_m', 'new_m_final_norm': 'new_m', 'new_v_mem_norm': 'new_v', 'new_v_w_mem_kv': 'new_v', 'new_v_norm_a': 'new_v', 'new_v_w_in_a': 'new_v', 'new_v_conv_w': 'new_v', 'new_v_conv_b': 'new_v', 'new_v_w_rec_gate': 'new_v', 'new_v_b_rec_gate': 'new_v', 'new_v_w_in_gate': 'new_v', 'new_v_b_in_gate': 'new_v', 'new_v_lru_lambda': 'new_v', 'new_v_w_out_a': 'new_v', 'new_v_kv_norm': 'new_v', 'new_v_w_kv': 'new_v', 'new_v_norm_b': 'new_v', 'new_v_w_in_b': 'new_v', 'new_v_w_out_b': 'new_v', 'new_v_final_norm': 'new_v'}


def _forward(args):
    return _fwd_reference(*[args[k] for k in FWD_PARAMS])


def _output_shape():
    def fwd():
        inp = _fwd_setup_inputs(0)
        return _fwd_reference(*[inp[k] for k in FWD_PARAMS])
    out = _jax.eval_shape(fwd)
    return out.shape, out.dtype

N_MICROBATCH = 1
ADAM_LR = 0.001
ADAM_B1 = 0.9
ADAM_B2 = 0.999
ADAM_EPS = 1e-08
ADAM_WD = 0.01
ADAM_STEP = 10
PER_EXAMPLE_BATCH_AXIS = {'x': 0, 'mem': 0, 'loss_target': 0}
SHARED_INPUTS = []
_WEIGHT_DTYPES = {'mem_norm': _jnp.float32, 'w_mem_kv': _jnp.float32, 'norm_a': _jnp.float32, 'w_in_a': _jnp.float32, 'conv_w': _jnp.float32, 'conv_b': _jnp.float32, 'w_rec_gate': _jnp.float32, 'b_rec_gate': _jnp.float32, 'w_in_gate': _jnp.float32, 'b_in_gate': _jnp.float32, 'lru_lambda': _jnp.float32, 'w_out_a': _jnp.float32, 'kv_norm': _jnp.float32, 'w_kv': _jnp.float32, 'norm_b': _jnp.float32, 'w_in_b': _jnp.float32, 'w_out_b': _jnp.float32, 'final_norm': _jnp.float32}
MOMENT_SCALE = {'mem_norm': 1.120574e-02, 'w_mem_kv': 7.659311e-03, 'norm_a': 7.588553e-02, 'w_in_a': 5.393814e-02, 'conv_w': 6.968737e-02, 'conv_b': 8.118001e-01, 'w_rec_gate': 2.058921e-02, 'b_rec_gate': 1.755644e-02, 'w_in_gate': 3.723002e-02, 'b_in_gate': 2.528440e-02, 'lru_lambda': 3.630239e-02, 'w_out_a': 6.005213e-02, 'kv_norm': 5.629728e-02, 'w_kv': 4.623058e-02, 'norm_b': 4.174247e-02, 'w_in_b': 2.925086e-02, 'w_out_b': 3.662218e-02, 'final_norm': 3.197054e+01}


def _to_microbatches(a, axis):
    t = _jnp.moveaxis(a, axis, 0)
    t = t.reshape((N_MICROBATCH, t.shape[0] // N_MICROBATCH) + t.shape[1:])
    return _jnp.moveaxis(t, 1, axis + 1)


def setup_inputs(seed: int = 0) -> dict:
    inp = _fwd_setup_inputs(seed)
    key = _jax.random.fold_in(_jax.random.key(seed), 7919)
    shape, _ = _output_shape()
    out = dict(inp)
    out["loss_target"] = _jax.random.normal(_jax.random.fold_in(key, 0), shape, _jnp.float32)
    for i, name in enumerate(TWIN_WEIGHTS):
        w = inp[name].astype(_jnp.float32)
        if MOMENT_SCALE is None:
            s = _jnp.sqrt(_jnp.mean(_jnp.square(w)) + 1e-30)
        else:
            s = MOMENT_SCALE[name]
        km, kv = _jax.random.split(_jax.random.fold_in(key, i + 1))
        out[name] = w
        out["m_" + name] = s * _jax.random.normal(km, w.shape, _jnp.float32)
        out["v_" + name] = (s * s) * _jax.random.uniform(kv, w.shape, _jnp.float32, 0.5, 1.5)
    if N_MICROBATCH > 1:
        for name, axis in PER_EXAMPLE_BATCH_AXIS.items():
            out[name] = _to_microbatches(out[name], axis)
    return {'x': out['x'], 'mem': out['mem'], 'mem_norm': out['mem_norm'], 'w_mem_kv': out['w_mem_kv'], 'norm_a': out['norm_a'], 'w_in_a': out['w_in_a'], 'conv_w': out['conv_w'], 'conv_b': out['conv_b'], 'w_rec_gate': out['w_rec_gate'], 'b_rec_gate': out['b_rec_gate'], 'w_in_gate': out['w_in_gate'], 'b_in_gate': out['b_in_gate'], 'lru_lambda': out['lru_lambda'], 'w_out_a': out['w_out_a'], 'kv_norm': out['kv_norm'], 'w_kv': out['w_kv'], 'norm_b': out['norm_b'], 'w_in_b': out['w_in_b'], 'w_out_b': out['w_out_b'], 'final_norm': out['final_norm'], 'loss_target': out['loss_target'], 'm_mem_norm': out['m_mem_norm'], 'm_w_mem_kv': out['m_w_mem_kv'], 'm_norm_a': out['m_norm_a'], 'm_w_in_a': out['m_w_in_a'], 'm_conv_w': out['m_conv_w'], 'm_conv_b': out['m_conv_b'], 'm_w_rec_gate': out['m_w_rec_gate'], 'm_b_rec_gate': out['m_b_rec_gate'], 'm_w_in_gate': out['m_w_in_gate'], 'm_b_in_gate': out['m_b_in_gate'], 'm_lru_lambda': out['m_lru_lambda'], 'm_w_out_a': out['m_w_out_a'], 'm_kv_norm': out['m_kv_norm'], 'm_w_kv': out['m_w_kv'], 'm_norm_b': out['m_norm_b'], 'm_w_in_b': out['m_w_in_b'], 'm_w_out_b': out['m_w_out_b'], 'm_final_norm': out['m_final_norm'], 'v_mem_norm': out['v_mem_norm'], 'v_w_mem_kv': out['v_w_mem_kv'], 'v_norm_a': out['v_norm_a'], 'v_w_in_a': out['v_w_in_a'], 'v_conv_w': out['v_conv_w'], 'v_conv_b': out['v_conv_b'], 'v_w_rec_gate': out['v_w_rec_gate'], 'v_b_rec_gate': out['v_b_rec_gate'], 'v_w_in_gate': out['v_w_in_gate'], 'v_b_in_gate': out['v_b_in_gate'], 'v_lru_lambda': out['v_lru_lambda'], 'v_w_out_a': out['v_w_out_a'], 'v_kv_norm': out['v_kv_norm'], 'v_w_kv': out['v_w_kv'], 'v_norm_b': out['v_norm_b'], 'v_w_in_b': out['v_w_in_b'], 'v_w_out_b': out['v_w_out_b'], 'v_final_norm': out['v_final_norm']}


def _loss(weights, diff, rest, loss_target):
    with _jax.named_scope("forward"):
        args = {**rest, TWIN_DIFF_INPUT: diff, **{k: w.astype(_WEIGHT_DTYPES[k]) for k, w in weights.items()}}
        y = _forward(args)
    with _jax.named_scope("loss_head"):
        err = _jnp.square(y.astype(_jnp.float32) - loss_target)
        return 0.5 * _jnp.sum(_jnp.mean(err, axis=-1)) if err.ndim else 0.5 * err


def _adamw(w, g, m, v):
    m = ADAM_B1 * m + (1.0 - ADAM_B1) * g
    v = ADAM_B2 * v + (1.0 - ADAM_B2) * _jnp.square(g)
    m_hat = m / (1.0 - ADAM_B1 ** ADAM_STEP)
    v_hat = v / (1.0 - ADAM_B2 ** ADAM_STEP)
    delta = -ADAM_LR * (m_hat / (_jnp.sqrt(v_hat) + ADAM_EPS) + ADAM_WD * w)
    return delta, m, v


def reference(x, mem, mem_norm, w_mem_kv, norm_a, w_in_a, conv_w, conv_b, w_rec_gate, b_rec_gate, w_in_gate, b_in_gate, lru_lambda, w_out_a, kv_norm, w_kv, norm_b, w_in_b, w_out_b, final_norm, loss_target, m_mem_norm, m_w_mem_kv, m_norm_a, m_w_in_a, m_conv_w, m_conv_b, m_w_rec_gate, m_b_rec_gate, m_w_in_gate, m_b_in_gate, m_lru_lambda, m_w_out_a, m_kv_norm, m_w_kv, m_norm_b, m_w_in_b, m_w_out_b, m_final_norm, v_mem_norm, v_w_mem_kv, v_norm_a, v_w_in_a, v_conv_w, v_conv_b, v_w_rec_gate, v_b_rec_gate, v_w_in_gate, v_b_in_gate, v_lru_lambda, v_w_out_a, v_kv_norm, v_w_kv, v_norm_b, v_w_in_b, v_w_out_b, v_final_norm):
    given = dict(x=x, mem=mem, mem_norm=mem_norm, w_mem_kv=w_mem_kv, norm_a=norm_a, w_in_a=w_in_a, conv_w=conv_w, conv_b=conv_b, w_rec_gate=w_rec_gate, b_rec_gate=b_rec_gate, w_in_gate=w_in_gate, b_in_gate=b_in_gate, lru_lambda=lru_lambda, w_out_a=w_out_a, kv_norm=kv_norm, w_kv=w_kv, norm_b=norm_b, w_in_b=w_in_b, w_out_b=w_out_b, final_norm=final_norm, loss_target=loss_target, m_mem_norm=m_mem_norm, m_w_mem_kv=m_w_mem_kv, m_norm_a=m_norm_a, m_w_in_a=m_w_in_a, m_conv_w=m_conv_w, m_conv_b=m_conv_b, m_w_rec_gate=m_w_rec_gate, m_b_rec_gate=m_b_rec_gate, m_w_in_gate=m_w_in_gate, m_b_in_gate=m_b_in_gate, m_lru_lambda=m_lru_lambda, m_w_out_a=m_w_out_a, m_kv_norm=m_kv_norm, m_w_kv=m_w_kv, m_norm_b=m_norm_b, m_w_in_b=m_w_in_b, m_w_out_b=m_w_out_b, m_final_norm=m_final_norm, v_mem_norm=v_mem_norm, v_w_mem_kv=v_w_mem_kv, v_norm_a=v_norm_a, v_w_in_a=v_w_in_a, v_conv_w=v_conv_w, v_conv_b=v_conv_b, v_w_rec_gate=v_w_rec_gate, v_b_rec_gate=v_b_rec_gate, v_w_in_gate=v_w_in_gate, v_b_in_gate=v_b_in_gate, v_lru_lambda=v_lru_lambda, v_w_out_a=v_w_out_a, v_kv_norm=v_kv_norm, v_w_kv=v_w_kv, v_norm_b=v_norm_b, v_w_in_b=v_w_in_b, v_w_out_b=v_w_out_b, v_final_norm=v_final_norm)
    weights = {n: given[n] for n in TWIN_WEIGHTS}
    shared = {n: given[n] for n in SHARED_INPUTS}
    per_example = {n: given[n] for n in ['x', 'mem']}
    grad_fn = _jax.value_and_grad(_loss, argnums=(0, 1))

    def one_microbatch(ex, loss_target):
        ex = dict(ex)
        diff = ex.pop(TWIN_DIFF_INPUT)
        return grad_fn(weights, diff, {**shared, **ex}, loss_target)

    if N_MICROBATCH == 1:
        loss, (grad_w, grad_x) = one_microbatch(per_example, given["loss_target"])
    else:
        def body(carry, xs):
            loss_sum, grad_sum = carry
            l_k, (gw_k, gx_k) = one_microbatch(xs[0], xs[1])
            with _jax.named_scope("update"):
                return (loss_sum + l_k, _jax.tree.map(_jnp.add, grad_sum, gw_k)), gx_k

        init = (_jnp.zeros((), _jnp.float32), _jax.tree.map(_jnp.zeros_like, weights))
        (loss, grad_w), grad_x = _jax.lax.scan(body, init, (per_example, given["loss_target"]))
    with _jax.named_scope("update"):
        delta_w, new_m, new_v = {}, {}, {}
        for n in TWIN_WEIGHTS:
            delta_w[n], new_m[n], new_v[n] = _adamw(weights[n], grad_w[n], given["m_" + n], given["v_" + n])
    return (loss, grad_x, *[grad_w[n] for n in TWIN_WEIGHTS], *[delta_w[n] for n in TWIN_WEIGHTS],
            *[new_m[n] for n in TWIN_WEIGHTS], *[new_v[n] for n in TWIN_WEIGHTS])
```

```python
import functools

import jax
import jax.numpy as jnp
from jax import lax
from jax.experimental import pallas as pl
from jax.experimental.pallas import tpu as pltpu

F32 = jnp.float32
BF16 = jnp.bfloat16

RMS_EPS = 1e-6
LRU_C = 8.0
ADAM_LR = 0.001
ADAM_B1 = 0.9
ADAM_B2 = 0.999
ADAM_EPS = 1e-08
ADAM_WD = 0.01
ADAM_STEP = 10

LANES = 128
VMEM_LIMIT = 56 * 1024 * 1024
FLAT_W = 1024
FLAT_TR = 256
N_CHIPS = 4
MESH_AXES = ("x", "y", "c")

_NT = (((1,), (1,)), ((), ()))
_TN = (((0,), (0,)), ((), ()))


def _cparams(n_axes):
    return pltpu.CompilerParams(dimension_semantics=("arbitrary",) * n_axes, vmem_limit_bytes=VMEM_LIMIT)


def _sigmoid(x):
    return 1.0 / (1.0 + jnp.exp(-x))


def _log1p_pos(e):
    return jnp.where(e < 1e-3, e * (1.0 - e * (0.5 - e * (1.0 / 3.0))), jnp.log(1.0 + e))


def _neg_expm1(x):
    small = -x * (1.0 + x * (0.5 + x * (1.0 / 6.0 + x * (1.0 / 24.0))))
    return jnp.where(x > -0.05, small, 1.0 - jnp.exp(x))


def _tile(n, want):
    if n <= want:
        return n
    t = want
    while n % t:
        t -= LANES
    assert t > 0, (n, want)
    return t


def _norm_matmul(x, g, w, out_dtype, name):
    m, k = x.shape
    n = w.shape[1]
    tm, tn = _tile(m, 512), _tile(n, 1024)

    def body(x_ref, g_ref, w_ref, o_ref, u_ref):
        @pl.when(pl.program_id(1) == 0)
        def _():
            xf = x_ref[...]
            r = lax.rsqrt(jnp.mean(xf * xf, axis=-1, keepdims=True) + RMS_EPS)
            u_ref[...] = ((xf * r) * g_ref[...]).astype(BF16)

        o_ref[...] = jnp.dot(u_ref[...], w_ref[...], preferred_element_type=F32).astype(o_ref.dtype)

    return pl.pallas_call(
        body, name=name, grid=(m // tm, n // tn),
        in_specs=[pl.BlockSpec((tm, k), lambda i, j: (i, 0)), pl.BlockSpec((1, k), lambda i, j: (0, 0)),
                  pl.BlockSpec((k, tn), lambda i, j: (0, j))],
        out_specs=[pl.BlockSpec((tm, tn), lambda i, j: (i, j)), pl.BlockSpec((tm, k), lambda i, j: (i, 0))],
        out_shape=[jax.ShapeDtypeStruct((m, n), out_dtype), jax.ShapeDtypeStruct((m, k), BF16)],
        compiler_params=_cparams(2),
    )(x, g, w)


def _matmul_res(a, b, res, name):
    m, k = a.shape
    n = b.shape[1]
    tm, tn = _tile(m, 512), _tile(n, 1024)

    def body(a_ref, b_ref, r_ref, o_ref):
        o_ref[...] = r_ref[...] + jnp.dot(a_ref[...], b_ref[...], preferred_element_type=F32)

    return pl.pallas_call(
        body, name=name, grid=(m // tm, n // tn),
        in_specs=[pl.BlockSpec((tm, k), lambda i, j: (i, 0)), pl.BlockSpec((k, tn), lambda i, j: (0, j)),
                  pl.BlockSpec((tm, tn), lambda i, j: (i, j))],
        out_specs=pl.BlockSpec((tm, tn), lambda i, j: (i, j)),
        out_shape=jax.ShapeDtypeStruct((m, n), F32),
        compiler_params=_cparams(2),
    )(a, b, res)


def _matmul_nt(a, b, name):
    m, n = a.shape
    k = b.shape[0]
    tm, tk = _tile(m, 512), _tile(k, 512)

    def body(a_ref, b_ref, o_ref):
        o_ref[...] = lax.dot_general(a_ref[...].astype(BF16), b_ref[...], _NT, preferred_element_type=F32)

    return pl.pallas_call(
        body, name=name, grid=(m // tm, k // tk),
        in_specs=[pl.BlockSpec((tm, n), lambda i, j: (i, 0)), pl.BlockSpec((tk, n), lambda i, j: (j, 0))],
        out_specs=pl.BlockSpec((tm, tk), lambda i, j: (i, j)),
        out_shape=jax.ShapeDtypeStruct((m, k), F32),
        compiler_params=_cparams(2),
    )(a, b)


def _matmul_tn(a, b, name):
    m, k = a.shape
    n = b.shape[1]
    tm, tk, tn = _tile(m, 512), _tile(k, 512), _tile(n, 2048)

    def body(a_ref, b_ref, o_ref):
        part = lax.dot_general(a_ref[...].astype(BF16), b_ref[...].astype(BF16), _TN, preferred_element_type=F32)

        @pl.when(pl.program_id(2) == 0)
        def _():
            o_ref[...] = part

        @pl.when(pl.program_id(2) != 0)
        def _():
            o_ref[...] += part

    return pl.pallas_call(
        body, name=name, grid=(k // tk, n // tn, m // tm),
        in_specs=[pl.BlockSpec((tm, tk), lambda i, j, s: (s, i)), pl.BlockSpec((tm, tn), lambda i, j, s: (s, j))],
        out_specs=pl.BlockSpec((tk, tn), lambda i, j, s: (i, j)),
        out_shape=jax.ShapeDtypeStruct((k, n), F32),
        compiler_params=_cparams(3),
    )(a, b)


def _rms_bwd(du, h, g, dres, name):
    m, d = h.shape
    tm = _tile(m, 256)

    def body(du_ref, h_ref, g_ref, dres_ref, dx_ref, dg_ref):
        xf = h_ref[...]
        r = lax.rsqrt(jnp.mean(xf * xf, axis=-1, keepdims=True) + RMS_EPS)
        xhat = xf * r
        du_v = du_ref[...]
        dxn = du_v * g_ref[...]
        dx_ref[...] = dres_ref[...] + r * (dxn - xhat * jnp.mean(dxn * xhat, axis=-1, keepdims=True))
        part = jnp.sum(du_v * xhat, axis=0, keepdims=True)

        @pl.when(pl.program_id(0) == 0)
        def _():
            dg_ref[...] = part

        @pl.when(pl.program_id(0) != 0)
        def _():
            dg_ref[...] += part

    row = lambda i: (i, 0)
    return pl.pallas_call(
        body, name=name, grid=(m // tm,),
        in_specs=[pl.BlockSpec((tm, d), row), pl.BlockSpec((tm, d), row), pl.BlockSpec((1, d), lambda i: (0, 0)),
                  pl.BlockSpec((tm, d), row)],
        out_specs=[pl.BlockSpec((tm, d), row), pl.BlockSpec((1, d), lambda i: (0, 0))],
        out_shape=[jax.ShapeDtypeStruct((m, d), F32), jax.ShapeDtypeStruct((1, d), F32)],
        compiler_params=_cparams(1),
    )(du, h, g, dres)


def _final_loss_bwd(h, g, tgt, name):
    m, d = h.shape
    tm = _tile(m, 256)

    def body(h_ref, g_ref, t_ref, dx_ref, dg_ref, loss_ref):
        xf = h_ref[...]
        r = lax.rsqrt(jnp.mean(xf * xf, axis=-1, keepdims=True) + RMS_EPS)
        xhat = xf * r
        gv = g_ref[...]
        err = xhat * gv - t_ref[...]
        dy = err * (1.0 / d)
        dxn = dy * gv
        dx_ref[...] = r * (dxn - xhat * jnp.mean(dxn * xhat, axis=-1, keepdims=True))
        part = jnp.sum(dy * xhat, axis=0, keepdims=True)
        lpart = jnp.sum(jnp.sum(err * err, axis=0, keepdims=True), axis=1, keepdims=True) * (0.5 / d)

        @pl.when(pl.program_id(0) == 0)
        def _():
            dg_ref[...] = part
            loss_ref[...] = lpart

        @pl.when(pl.program_id(0) != 0)
        def _():
            dg_ref[...] += part
            loss_ref[...] += lpart

    row = lambda i: (i, 0)
    fixed = lambda i: (0, 0)
    return pl.pallas_call(
        body, name=name, grid=(m // tm,),
        in_specs=[pl.BlockSpec((tm, d), row), pl.BlockSpec((1, d), fixed), pl.BlockSpec((tm, d), row)],
        out_specs=[pl.BlockSpec((tm, d), row), pl.BlockSpec((1, d), fixed), pl.BlockSpec((1, 1), fixed)],
        out_shape=[jax.ShapeDtypeStruct((m, d), F32), jax.ShapeDtypeStruct((1, d), F32),
                   jax.ShapeDtypeStruct((1, 1), F32)],
        compiler_params=_cparams(1),
    )(h, g, tgt)


def _add_cast(a, b, name):
    m, n = a.shape
    tm = _tile(m, 512)

    def body(a_ref, b_ref, o_ref):
        o_ref[...] = (a_ref[...] + b_ref[...]).astype(BF16)

    row = lambda i: (i, 0)
    return pl.pallas_call(
        body, name=name, grid=(m // tm,),
        in_specs=[pl.BlockSpec((tm, n), row), pl.BlockSpec((tm, n), row)],
        out_specs=pl.BlockSpec((tm, n), row),
        out_shape=jax.ShapeDtypeStruct((m, n), BF16),
        compiler_params=_cparams(1),
    )(a, b)


def _mem_attn_fwd(proj, memkv, layer, dr, dm, name):
    t = proj.shape[0]
    nm = memkv.shape[0]
    tm = _tile(t, 512)
    nh = dm // LANES
    scale = LANES ** -0.5
    qb = (2 * dr) // dm

    def body(q_ref, g_ref, k_ref, v_ref, y_ref):
        for hh in range(nh):
            sl = slice(hh * LANES, (hh + 1) * LANES)
            s = lax.dot_general(q_ref[:, sl].astype(BF16), k_ref[:, sl], _NT, preferred_element_type=F32) * scale
            p = jnp.exp(s - jnp.max(s, axis=-1, keepdims=True))
            p = p / jnp.sum(p, axis=-1, keepdims=True)
            o = jnp.dot(p.astype(BF16), v_ref[:, sl], preferred_element_type=F32)
            gv = g_ref[:, sl]
            y_ref[:, sl] = (o * (gv * _sigmoid(gv))).astype(BF16)

    return pl.pallas_call(
        body, name=name, grid=(t // tm,),
        in_specs=[pl.BlockSpec((tm, dm), lambda i: (i, qb)), pl.BlockSpec((tm, dm), lambda i: (i, qb + 1)),
                  pl.BlockSpec((nm, dm), lambda i: (0, 2 * layer)), pl.BlockSpec((nm, dm), lambda i: (0, 2 * layer + 1))],
        out_specs=pl.BlockSpec((tm, dm), lambda i: (i, 0)),
        out_shape=jax.ShapeDtypeStruct((t, dm), BF16),
        compiler_params=_cparams(1),
    )(proj, proj, memkv, memkv)


def _mem_attn_bwd(proj, memkv, dmix, layer, dr, dm, name):
    t = proj.shape[0]
    nm = memkv.shape[0]
    tm = _tile(t, 512)
    nh = dm // LANES
    scale = LANES ** -0.5
    qb = (2 * dr) // dm
    yb = dr // dm

    def body(q_ref, g_ref, k_ref, v_ref, dy_ref, dq_ref, dg_ref, dk_ref, dv_ref):
        @pl.when(pl.program_id(0) == 0)
        def _():
            dk_ref[...] = jnp.zeros_like(dk_ref)
            dv_ref[...] = jnp.zeros_like(dv_ref)

        for hh in range(nh):
            sl = slice(hh * LANES, (hh + 1) * LANES)
            q = q_ref[:, sl].astype(BF16)
            k = k_ref[:, sl]
            v = v_ref[:, sl]
            s = lax.dot_general(q, k, _NT, preferred_element_type=F32) * scale
            p = jnp.exp(s - jnp.max(s, axis=-1, keepdims=True))
            p = p / jnp.sum(p, axis=-1, keepdims=True)
            p_bf = p.astype(BF16)
            o = jnp.dot(p_bf, v, preferred_element_type=F32)
            gv = g_ref[:, sl]
            sg = _sigmoid(gv)
            dy = dy_ref[:, sl]
            do = dy * (gv * sg)
            dg_ref[:, sl] = (dy * o * (sg * (1.0 + gv * (1.0 - sg)))).astype(BF16)
            do_bf = do.astype(BF16)
            dv_ref[:, sl] += lax.dot_general(p_bf, do_bf, _TN, preferred_element_type=F32)
            dp = lax.dot_general(do_bf, v, _NT, preferred_element_type=F32)
            ds = (p * (dp - jnp.sum(dp * p, axis=-1, keepdims=True)) * scale).astype(BF16)
            dq_ref[:, sl] = jnp.dot(ds, k, preferred_element_type=F32).astype(BF16)
            dk_ref[:, sl] += lax.dot_general(ds, q, _TN, preferred_element_type=F32)

    fixed = lambda i: (0, 0)
    return pl.pallas_call(
        body, name=name, grid=(t // tm,),
        in_specs=[pl.BlockSpec((tm, dm), lambda i: (i, qb)), pl.BlockSpec((tm, dm), lambda i: (i, qb + 1)),
                  pl.BlockSpec((nm, dm), lambda i: (0, 2 * layer)), pl.BlockSpec((nm, dm), lambda i: (0, 2 * layer + 1)),
                  pl.BlockSpec((tm, dm), lambda i: (i, yb))],
        out_specs=[pl.BlockSpec((tm, dm), lambda i: (i, 0)), pl.BlockSpec((tm, dm), lambda i: (i, 0)),
                   pl.BlockSpec((nm, dm), fixed), pl.BlockSpec((nm, dm), fixed)],
        out_shape=[jax.ShapeDtypeStruct((t, dm), BF16), jax.ShapeDtypeStruct((t, dm), BF16),
                   jax.ShapeDtypeStruct((nm, dm), F32), jax.ShapeDtypeStruct((nm, dm), F32)],
        compiler_params=_cparams(1),
    )(proj, proj, memkv, memkv, dmix)


LRU_CHUNK = 256


def _lru_gates(xc, vec, wr_ref, wi_ref):
    r = _sigmoid(jnp.dot(xc.astype(BF16), wr_ref[...], preferred_element_type=F32) + vec[1:2])
    i = _sigmoid(jnp.dot(xc.astype(BF16), wi_ref[...], preferred_element_type=F32) + vec[2:3])
    lam = vec[3:4]
    cl = -LRU_C * (jnp.maximum(-lam, 0.0) + _log1p_pos(jnp.exp(-jnp.abs(lam))))
    la = cl * r
    a = jnp.exp(la)
    s2 = _neg_expm1(2.0 * la)
    return r, i, cl, a, s2


def _lru_fwd(proj, vec, wr, wi, name):
    t = proj.shape[0]
    nb = wr.shape[0]
    dr = nb * LANES
    c = _tile(t, LRU_CHUNK)

    def body(x_ref, g_ref, vec_ref, wr_ref, wi_ref, y_ref, h_ref, xc_ref, carry_ref, xprev_ref):
        @pl.when(pl.program_id(1) == 0)
        def _():
            carry_ref[...] = jnp.zeros_like(carry_ref)
            xprev_ref[...] = jnp.zeros_like(xprev_ref)

        x = x_ref[...]
        vec = vec_ref[...]
        rows = lax.broadcasted_iota(jnp.int32, (c, LANES), 0)
        xprev = xprev_ref[...]
        xc = vec[7:8] * x + vec[0:1]
        for k in range(1, 4):
            xs = jnp.where(rows < k, pltpu.roll(xprev, k, 0), pltpu.roll(x, k, 0))
            xc = xc + vec[7 - k:8 - k] * xs
        xprev_ref[...] = x
        xc_ref[...] = xc

        r, i, cl, a, s2 = _lru_gates(xc, vec, wr_ref, wi_ref)
        hh = jnp.sqrt(s2) * (i * xc)
        aa = a
        d = 1
        while d < c:
            keep = rows >= d
            hh = jnp.where(keep, aa * pltpu.roll(hh, d, 0) + hh, hh)
            aa = jnp.where(keep, aa * pltpu.roll(aa, d, 0), aa)
            d *= 2
        hfull = hh + aa * carry_ref[7:8, :]
        carry_ref[...] = hfull[c - 8:c, :]
        h_ref[...] = hfull
        gv = g_ref[...]
        y_ref[...] = (hfull * (gv * _sigmoid(gv))).astype(BF16)

    blk = lambda n, s: (s, n)
    return pl.pallas_call(
        body, name=name, grid=(nb, t // c),
        in_specs=[pl.BlockSpec((c, LANES), blk), pl.BlockSpec((c, LANES), lambda n, s: (s, nb + n)),
                  pl.BlockSpec((8, LANES), lambda n, s: (0, n)),
                  pl.BlockSpec((None, LANES, LANES), lambda n, s: (n, 0, 0)),
                  pl.BlockSpec((None, LANES, LANES), lambda n, s: (n, 0, 0))],
        out_specs=[pl.BlockSpec((c, LANES), blk)] * 3,
        out_shape=[jax.ShapeDtypeStruct((t, dr), BF16), jax.ShapeDtypeStruct((t, dr), F32),
                   jax.ShapeDtypeStruct((t, dr), F32)],
        scratch_shapes=[pltpu.VMEM((8, LANES), F32), pltpu.VMEM((c, LANES), F32)],
        compiler_params=_cparams(2),
    )(proj, proj, vec, wr, wi)


def _lru_bwd(proj, xc_all, h_all, dmix, vec, wr, wi, name):
    t = proj.shape[0]
    nb = wr.shape[0]
    dr = nb * LANES
    c = _tile(t, LRU_CHUNK)
    nc = t // c

    def body(x_ref, g_ref, xc_ref, h_ref, dy_ref, vec_ref, wr_ref, wi_ref,
             dx_ref, dg_ref, dwr_ref, dwi_ref, dvec_ref, qcarry_ref, dxc_next_ref):
        @pl.when(pl.program_id(1) == 0)
        def _():
            qcarry_ref[...] = jnp.zeros_like(qcarry_ref)
            dxc_next_ref[...] = jnp.zeros_like(dxc_next_ref)
            dwr_ref[...] = jnp.zeros_like(dwr_ref)
            dwi_ref[...] = jnp.zeros_like(dwi_ref)
            dvec_ref[...] = jnp.zeros_like(dvec_ref)

        x = x_ref[...]
        xc = xc_ref[...]
        h = h_ref[...]
        dy = dy_ref[...]
        gv = g_ref[...]
        vec = vec_ref[...]
        rows = lax.broadcasted_iota(jnp.int32, (c, LANES), 0)

        r, i, cl, a, s2 = _lru_gates(xc, vec, wr_ref, wi_ref)
        s = jnp.sqrt(s2)
        ixc = i * xc
        u = s * ixc
        sg = _sigmoid(gv)
        dh = dy * (gv * sg)
        dg_ref[...] = (dy * h * (sg * (1.0 + gv * (1.0 - sg)))).astype(BF16)

        aa = a
        qq = a * dh
        d = 1
        while d < c:
            keep = rows < c - d
            qq = jnp.where(keep, qq + aa * pltpu.roll(qq, c - d, 0), qq)
            aa = jnp.where(keep, aa * pltpu.roll(aa, c - d, 0), aa)
            d *= 2
        qin = qcarry_ref[0:1, :]
        qfull = qq + aa * qin
        gt = dh + jnp.where(rows == c - 1, qin, pltpu.roll(qfull, c - 1, 0))
        qcarry_ref[...] = qfull[0:8, :]

        dla = gt * (h - u) - gt * ixc * (a * a) / s
        dixc = gt * s
        di = dixc * xc
        dxc = dixc * i
        dzr = (dla * cl) * (r * (1.0 - r))
        dzi = di * (i * (1.0 - i))
        dzr_bf = dzr.astype(BF16)
        dzi_bf = dzi.astype(BF16)
        dxc = dxc + lax.dot_general(dzr_bf, wr_ref[...], _NT, preferred_element_type=F32)
        dxc = dxc + lax.dot_general(dzi_bf, wi_ref[...], _NT, preferred_element_type=F32)
        xc_bf = xc.astype(BF16)
        dwr_ref[...] += lax.dot_general(xc_bf, dzr_bf, _TN, preferred_element_type=F32)
        dwi_ref[...] += lax.dot_general(xc_bf, dzi_bf, _TN, preferred_element_type=F32)

        lam = vec[3:4]
        dlam = jnp.sum(dla * r, axis=0, keepdims=True) * (LRU_C * _sigmoid(-lam))
        colsum = lambda v: jnp.sum(v, axis=0, keepdims=True)
        dxn = dxc_next_ref[...]
        dx = vec[7:8] * dxc
        dtaps = [None] * 4
        dtaps[3] = colsum(x * dxc)
        for k in range(1, 4):
            sh = jnp.where(rows < c - k, pltpu.roll(dxc, c - k, 0), pltpu.roll(dxn, c - k, 0))
            dx = dx + vec[7 - k:8 - k] * sh
            dtaps[3 - k] = colsum(x * sh)
        dxc_next_ref[...] = dxc
        dx_ref[...] = dx.astype(BF16)
        dvec_ref[...] += jnp.concatenate([colsum(dxc), colsum(dzr), colsum(dzi), dlam] + dtaps, axis=0)

    rev = lambda n, s: (nc - 1 - s, n)
    sq = lambda n, s: (n, 0, 0)
    return pl.pallas_call(
        body, name=name, grid=(nb, nc),
        in_specs=[pl.BlockSpec((c, LANES), rev), pl.BlockSpec((c, LANES), lambda n, s: (nc - 1 - s, nb + n)),
                  pl.BlockSpec((c, LANES), rev), pl.BlockSpec((c, LANES), rev), pl.BlockSpec((c, LANES), rev),
                  pl.BlockSpec((8, LANES), lambda n, s: (0, n)),
                  pl.BlockSpec((None, LANES, LANES), sq), pl.BlockSpec((None, LANES, LANES), sq)],
        out_specs=[pl.BlockSpec((c, LANES), rev), pl.BlockSpec((c, LANES), rev),
                   pl.BlockSpec((None, LANES, LANES), sq), pl.BlockSpec((None, LANES, LANES), sq),
                   pl.BlockSpec((None, 8, LANES), sq)],
        out_shape=[jax.ShapeDtypeStruct((t, dr), BF16), jax.ShapeDtypeStruct((t, dr), BF16),
                   jax.ShapeDtypeStruct((nb, LANES, LANES), F32), jax.ShapeDtypeStruct((nb, LANES, LANES), F32),
                   jax.ShapeDtypeStruct((nb, 8, LANES), F32)],
        scratch_shapes=[pltpu.VMEM((8, LANES), F32), pltpu.VMEM((c, LANES), F32)],
        compiler_params=_cparams(2),
    )(proj, proj, xc_all, h_all, dmix, vec, wr, wi)


SB_TQ = 256
SB_TK = LANES


def _sb_logits(q, k, scale, q0, k0, tq):
    z = lax.dot_general(q, k, _NT, preferred_element_type=F32) * scale
    lb = jnp.minimum(z, 0.0) - jnp.log(1.0 + jnp.exp(-jnp.abs(z)))
    qpos = q0 + lax.broadcasted_iota(jnp.int32, (tq, SB_TK), 0)
    kpos = k0 + lax.broadcasted_iota(jnp.int32, (tq, SB_TK), 1)
    mask = kpos < qpos
    lk = jnp.where(mask, lb - z, 0.0)
    return lb, lk, mask


def _split_dot(v, m):
    hi = v.astype(BF16)
    lo = (v - hi.astype(F32)).astype(BF16)
    return jnp.dot(hi, m, preferred_element_type=F32) + jnp.dot(lo, m, preferred_element_type=F32)


def _tri_ones(kind):
    jj = lax.broadcasted_iota(jnp.int32, (SB_TK, 2 * SB_TK), 0)
    ss = lax.broadcasted_iota(jnp.int32, (SB_TK, 2 * SB_TK), 1)
    rel = {"gt": jj > ss, "le": jj <= ss, "lt": jj < ss}[kind]
    return jnp.where((ss >= SB_TK) | rel, 1.0, 0.0).astype(BF16)


def _sb_fwd(proj, kv, name):
    t = proj.shape[0]
    ds = kv.shape[1] // 2
    nh = ds // LANES
    tq = _tile(t, SB_TQ)
    scale = LANES ** -0.5

    def body(q_ref, g_ref, k_ref, v_ref, y_ref, o_ref, tl_ref, acc_ref, run_ref):
        qi = pl.program_id(1)
        q = q_ref[...].astype(BF16)
        tri = _tri_ones("gt")
        acc_ref[...] = jnp.zeros_like(acc_ref)
        run_ref[...] = jnp.zeros_like(run_ref)
        nkb = (qi + 1) * (tq // SB_TK)

        def step(j, carry):
            kb = nkb - 1 - j
            k0 = pl.multiple_of(kb * SB_TK, SB_TK)
            lb, lk, mask = _sb_logits(q, k_ref[pl.ds(k0, SB_TK), :], scale, qi * tq, k0, tq)
            cum = _split_dot(lk, tri)
            w = jnp.where(mask, jnp.exp(lb + cum[:, :SB_TK] + run_ref[...]), 0.0)
            acc_ref[...] += jnp.dot(w.astype(BF16), v_ref[pl.ds(k0, SB_TK), :], preferred_element_type=F32)
            run_ref[...] += cum[:, SB_TK:]
            return carry

        lax.fori_loop(0, nkb, step, 0)
        o = acc_ref[...]
        o_ref[...] = o
        tl_ref[...] = run_ref[...]
        gv = g_ref[...]
        y_ref[...] = (o * (gv * _sigmoid(gv))).astype(BF16)

    blk = lambda h, i: (i, h)
    return pl.pallas_call(
        body, name=name, grid=(nh, t // tq),
        in_specs=[pl.BlockSpec((tq, LANES), blk), pl.BlockSpec((tq, LANES), lambda h, i: (i, nh + h)),
                  pl.BlockSpec((t, LANES), lambda h, i: (0, h)), pl.BlockSpec((t, LANES), lambda h, i: (0, nh + h))],
        out_specs=[pl.BlockSpec((tq, LANES), blk)] * 3,
        out_shape=[jax.ShapeDtypeStruct((t, ds), BF16), jax.ShapeDtypeStruct((t, ds), F32),
                   jax.ShapeDtypeStruct((t, ds), F32)],
        scratch_shapes=[pltpu.VMEM((tq, LANES), F32), pltpu.VMEM((tq, SB_TK), F32)],
        compiler_params=_cparams(2),
    )(proj, proj, kv, kv)


def _sb_bwd(proj, kv, o_all, tl_all, dmix, name):
    t = proj.shape[0]
    ds = kv.shape[1] // 2
    nh = ds // LANES
    tq = _tile(t, SB_TQ)
    scale = LANES ** -0.5

    def body(q_ref, g_ref, k_ref, v_ref, o_ref, tl_ref, dy_ref, dq_ref, dg_ref, dk_ref, dv_ref,
             acc_ref, runl_ref, rune_ref):
        qi = pl.program_id(1)

        @pl.when(qi == 0)
        def _():
            dk_ref[...] = jnp.zeros_like(dk_ref)
            dv_ref[...] = jnp.zeros_like(dv_ref)

        q = q_ref[...].astype(BF16)
        gv = g_ref[...]
        sg = _sigmoid(gv)
        dy = dy_ref[...]
        do_bf = (dy * (gv * sg)).astype(BF16)
        dg_ref[...] = (dy * o_ref[...] * (sg * (1.0 + gv * (1.0 - sg)))).astype(BF16)
        tri_le = _tri_ones("le")
        tri_lt = _tri_ones("lt")
        acc_ref[...] = jnp.zeros_like(acc_ref)
        runl_ref[...] = jnp.zeros_like(runl_ref)
        rune_ref[...] = jnp.zeros_like(rune_ref)
        nkb = (qi + 1) * (tq // SB_TK)

        def step(kb, carry):
            k0 = pl.multiple_of(kb * SB_TK, SB_TK)
            k = k_ref[pl.ds(k0, SB_TK), :]
            v = v_ref[pl.ds(k0, SB_TK), :]
            lb, lk, mask = _sb_logits(q, k, scale, qi * tq, k0, tq)
            cum = _split_dot(lk, tri_le)
            later = tl_ref[...] - (runl_ref[...] + cum[:, :SB_TK])
            w = jnp.where(mask, jnp.exp(lb + later), 0.0)
            w_bf = w.astype(BF16)
            e = lax.dot_general(do_bf, v, _NT, preferred_element_type=F32) * w
            cume = _split_dot(e, tri_lt)
            before = rune_ref[...] + cume[:, :SB_TK]
            dz = jnp.where(mask, (e - jnp.exp(lb) * (e + before)) * scale, 0.0).astype(BF16)
            acc_ref[...] += jnp.dot(dz, k, preferred_element_type=F32)
            dk_ref[pl.ds(k0, SB_TK), :] += lax.dot_general(dz, q, _TN, preferred_element_type=F32)
            dv_ref[pl.ds(k0, SB_TK), :] += lax.dot_general(w_bf, do_bf, _TN, preferred_element_type=F32)
            runl_ref[...] += cum[:, SB_TK:]
            rune_ref[...] += cume[:, SB_TK:]
            return carry

        lax.fori_loop(0, nkb, step, 0)
        dq_ref[...] = acc_ref[...].astype(BF16)

    blk = lambda h, i: (i, h)
    whole = lambda h, i: (0, h)
    return pl.pallas_call(
        body, name=name, grid=(nh, t // tq),
        in_specs=[pl.BlockSpec((tq, LANES), blk), pl.BlockSpec((tq, LANES), lambda h, i: (i, nh + h)),
                  pl.BlockSpec((t, LANES), whole), pl.BlockSpec((t, LANES), lambda h, i: (0, nh + h)),
                  pl.BlockSpec((tq, LANES), blk), pl.BlockSpec((tq, LANES), blk), pl.BlockSpec((tq, LANES), blk)],
        out_specs=[pl.BlockSpec((tq, LANES), blk), pl.BlockSpec((tq, LANES), blk),
                   pl.BlockSpec((t, LANES), whole), pl.BlockSpec((t, LANES), whole)],
        out_shape=[jax.ShapeDtypeStruct((t, ds), BF16), jax.ShapeDtypeStruct((t, ds), BF16),
                   jax.ShapeDtypeStruct((t, ds), F32), jax.ShapeDtypeStruct((t, ds), F32)],
        scratch_shapes=[pltpu.VMEM((tq, LANES), F32), pltpu.VMEM((tq, SB_TK), F32), pltpu.VMEM((tq, SB_TK), F32)],
        compiler_params=_cparams(2),
    )(proj, proj, kv, kv, o_all, tl_all, dmix)


ANY = pl.BlockSpec(memory_space=pl.ANY)
MESH = pl.DeviceIdType.MESH


def _place():
    x, y, c = lax.axis_index("x"), lax.axis_index("y"), lax.axis_index("c")
    chips = [(1 - x, y), (x, 1 - y), (1 - x, 1 - y)]
    return x, y, c, chips


def _remote(src, dst, send_sems, recv_sems, k, to):
    return pltpu.make_async_remote_copy(src_ref=src, dst_ref=dst, send_sem=send_sems.at[k], recv_sem=recv_sems.at[k],
                                        device_id=to, device_id_type=MESH)


def _chip_all_gather(shard, name):
    r, w = shard.shape
    rh = r // 2

    def body(x_ref, out_ref, send_sems, recv_sems, local_sem):
        x, y, c, chips = _place()
        me = 2 * x + y
        sibling = (x, y, 1 - c)
        mine_rows = pl.ds(c * rh, rh)
        other_rows = pl.ds((1 - c) * rh, rh)
        own = pltpu.make_async_copy(x_ref, out_ref.at[me], local_sem)
        own.start()
        first = [_remote(x_ref.at[mine_rows], out_ref.at[me, mine_rows], send_sems, recv_sems, k, (cx, cy, c))
                 for k, (cx, cy) in enumerate(chips)]
        for cp in first:
            cp.start()
        passed = []
        for k, (cx, cy) in enumerate(chips):
            got = out_ref.at[2 * cx + cy, mine_rows]
            _remote(got, got, send_sems, recv_sems, k, (cx, cy, c)).wait_recv()
            fwd = _remote(got, got, send_sems, recv_sems, 3 + k, sibling)
            fwd.start()
            passed.append(fwd)
        for k, (cx, cy) in enumerate(chips):
            got = out_ref.at[2 * cx + cy, other_rows]
            _remote(got, got, send_sems, recv_sems, 3 + k, sibling).wait_recv()
        for cp in first + passed:
            cp.wait_send()
        own.wait()

    return pl.pallas_call(
        body, name=name, in_specs=[ANY], out_specs=ANY,
        out_shape=jax.ShapeDtypeStruct((N_CHIPS, r, w), shard.dtype),
        scratch_shapes=[pltpu.SemaphoreType.DMA((6,)), pltpu.SemaphoreType.DMA((6,)), pltpu.SemaphoreType.DMA],
    )(shard)


def _sibling_take_half(s, name):
    n, r, w = s.shape
    rh = r // 2

    def body(s_ref, a_ref, send_sem, recv_sem):
        x, y, c, _ = _place()
        cp = pltpu.make_async_remote_copy(
            src_ref=s_ref.at[:, pl.ds((1 - c) * rh, rh), :], dst_ref=a_ref, send_sem=send_sem, recv_sem=recv_sem,
            device_id=(x, y, 1 - c), device_id_type=MESH)
        cp.start()
        cp.wait()

    return pl.pallas_call(
        body, name=name, in_specs=[ANY], out_specs=ANY, out_shape=jax.ShapeDtypeStruct((n, rh, w), s.dtype),
        scratch_shapes=[pltpu.SemaphoreType.DMA, pltpu.SemaphoreType.DMA],
    )(s)


def _add_own_half(s, a, c_arr, name):
    n, r, w = s.shape
    rh = r // 2
    nblk = rh // FLAT_TR

    def body(c_ref, s_ref, a_ref, o_ref):
        o_ref[...] = s_ref[...] + a_ref[...]

    return pl.pallas_call(
        body, name=name, out_shape=jax.ShapeDtypeStruct((n, rh, w), F32),
        grid_spec=pltpu.PrefetchScalarGridSpec(
            num_scalar_prefetch=1, grid=(n, nblk),
            in_specs=[pl.BlockSpec((None, FLAT_TR, w), lambda k, i, c_ref: (k, c_ref[0] * nblk + i, 0)),
                      pl.BlockSpec((None, FLAT_TR, w), lambda k, i, c_ref: (k, i, 0))],
            out_specs=pl.BlockSpec((None, FLAT_TR, w), lambda k, i, c_ref: (k, i, 0))),
        compiler_params=_cparams(2),
    )(c_arr, s, a)


def _chip_scatter(p, name):
    n, rh, w = p.shape

    def body(p_ref, b_ref, send_sems, recv_sems, local_sem):
        x, y, c, chips = _place()
        me = 2 * x + y
        own = pltpu.make_async_copy(p_ref.at[me], b_ref.at[me], local_sem)
        own.start()
        sends = [_remote(p_ref.at[2 * cx + cy], b_ref.at[me], send_sems, recv_sems, k, (cx, cy, c))
                 for k, (cx, cy) in enumerate(chips)]
        for cp in sends:
            cp.start()
        for k, (cx, cy) in enumerate(chips):
            got = b_ref.at[2 * cx + cy]
            _remote(got, got, send_sems, recv_sems, k, (cx, cy, c)).wait_recv()
        for cp in sends:
            cp.wait_send()
        own.wait()

    return pl.pallas_call(
        body, name=name, in_specs=[ANY], out_specs=ANY, out_shape=jax.ShapeDtypeStruct((n, rh, w), p.dtype),
        scratch_shapes=[pltpu.SemaphoreType.DMA((3,)), pltpu.SemaphoreType.DMA((3,)), pltpu.SemaphoreType.DMA],
    )(p)


def _sum_slots(b, name):
    n, rh, w = b.shape

    def body(b_ref, o_ref):
        o_ref[...] = ((b_ref[0] + b_ref[1]) + b_ref[2]) + b_ref[3]

    return pl.pallas_call(
        body, name=name, grid=(rh // FLAT_TR,),
        in_specs=[pl.BlockSpec((n, FLAT_TR, w), lambda i: (0, i, 0))],
        out_specs=pl.BlockSpec((FLAT_TR, w), lambda i: (i, 0)),
        out_shape=jax.ShapeDtypeStruct((rh, w), F32),
        compiler_params=_cparams(1),
    )(b)


def _sibling_join_halves(half, name):
    rh, w = half.shape

    def body(h_ref, g_ref, send_sem, recv_sem, local_sem):
        x, y, c, _ = _place()
        mine = g_ref.at[pl.ds(c * rh, rh)]
        own = pltpu.make_async_copy(h_ref, mine, local_sem)
        own.start()
        cp = pltpu.make_async_remote_copy(src_ref=h_ref, dst_ref=mine, send_sem=send_sem, recv_sem=recv_sem,
                                          device_id=(x, y, 1 - c), device_id_type=MESH)
        cp.start()
        cp.wait_send()
        theirs = g_ref.at[pl.ds((1 - c) * rh, rh)]
        pltpu.make_async_remote_copy(src_ref=theirs, dst_ref=theirs, send_sem=send_sem, recv_sem=recv_sem,
                                     device_id=(x, y, 1 - c), device_id_type=MESH).wait_recv()
        own.wait()

    return pl.pallas_call(
        body, name=name, in_specs=[ANY], out_specs=ANY, out_shape=jax.ShapeDtypeStruct((2 * rh, w), half.dtype),
        scratch_shapes=[pltpu.SemaphoreType.DMA, pltpu.SemaphoreType.DMA, pltpu.SemaphoreType.DMA],
    )(half)


def _adamw(g, w, m, v, name):
    r, wd = g.shape

    def body(g_ref, w_ref, m_ref, v_ref, d_ref, mo_ref, vo_ref):
        gv = g_ref[...]
        mn = ADAM_B1 * m_ref[...] + (1.0 - ADAM_B1) * gv
        vn = ADAM_B2 * v_ref[...] + (1.0 - ADAM_B2) * (gv * gv)
        m_hat = mn / (1.0 - ADAM_B1 ** ADAM_STEP)
        v_hat = vn / (1.0 - ADAM_B2 ** ADAM_STEP)
        d_ref[...] = -ADAM_LR * (m_hat / (jnp.sqrt(v_hat) + ADAM_EPS) + ADAM_WD * w_ref[...])
        mo_ref[...] = mn
        vo_ref[...] = vn

    row = lambda i: (i, 0)
    spec = pl.BlockSpec((FLAT_TR, wd), row)
    return pl.pallas_call(
        body, name=name, grid=(r // FLAT_TR,), in_specs=[spec] * 4, out_specs=[spec] * 3,
        out_shape=[jax.ShapeDtypeStruct((r, wd), F32)] * 3,
        compiler_params=_cparams(1),
    )(g, w, m, v)


def _reduce_to_shards(s):
    c_arr = lax.axis_index("c").astype(jnp.int32).reshape(1)
    a = _sibling_take_half(s, "grad_sibling_half")
    p = _add_own_half(s, a, c_arr, "grad_pair_sum")
    b = _chip_scatter(p, "grad_chip_scatter")
    half = _sum_slots(b, "grad_chip_sum")
    return _sibling_join_halves(half, "grad_sibling_join")


WEIGHTS = ("mem_norm", "w_mem_kv", "norm_a", "w_in_a", "conv_w", "conv_b", "w_rec_gate", "b_rec_gate", "w_in_gate",
           "b_in_gate", "lru_lambda", "w_out_a", "kv_norm", "w_kv", "norm_b", "w_in_b", "w_out_b", "final_norm")
SHARD_DIM = {"mem_norm": None, "w_mem_kv": 1, "norm_a": 1, "w_in_a": 2, "conv_w": 2, "conv_b": 1, "w_rec_gate": None,
             "b_rec_gate": 1, "w_in_gate": None, "b_in_gate": 1, "lru_lambda": 1, "w_out_a": 1, "kv_norm": None,
             "w_kv": 1, "norm_b": None, "w_in_b": 2, "w_out_b": 1, "final_norm": None}
BIG = ("w_mem_kv", "w_in_a", "w_out_a", "w_kv", "w_in_b", "w_out_b")
SMALL = ("norm_a", "conv_w", "conv_b", "b_rec_gate", "b_in_gate", "lru_lambda")


def _pad_rows(flat, row_multiple):
    per = FLAT_W * row_multiple
    n = flat.shape[0]
    total = -(-n // per) * per
    return jnp.pad(flat, (0, total - n)).reshape(total // FLAT_W, FLAT_W)


def _flatten(parts, row_multiple):
    return _pad_rows(jnp.concatenate([p.reshape(-1) for p in parts]), row_multiple)


def _unflatten(flat2d, shapes):
    flat = flat2d.reshape(-1)
    out, off = [], 0
    for shp in shapes:
        n = 1
        for s in shp:
            n *= s
        out.append(flat[off:off + n].reshape(shp))
        off += n
    return out


def _gather_weights(local):
    parts = [local[n].astype(BF16) for n in BIG]
    parts += [lax.bitcast_convert_type(local[n], BF16) for n in SMALL]
    shapes = [p.shape for p in parts]
    gathered = _chip_all_gather(_flatten(parts, 32), "weights_all_gather")
    per_chip = [_unflatten(gathered[k], shapes) for k in range(N_CHIPS)]
    full = {}
    for idx, n in enumerate(BIG + SMALL):
        pieces = [per_chip[k][idx] for k in range(N_CHIPS)]
        if n in SMALL:
            pieces = [lax.bitcast_convert_type(p, F32) for p in pieces]
        full[n] = jnp.concatenate(pieces, axis=SHARD_DIM[n])
    return full


def _piece(g, name, k):
    dim = SHARD_DIM[name]
    if dim is None:
        return g
    n = g.shape[dim] // N_CHIPS
    return lax.slice_in_dim(g, k * n, (k + 1) * n, axis=dim)


def _local_grads(x, mem, tgt, wts):
    t, d = x.shape
    depth = wts["w_mem_kv"].shape[0]
    n_a = wts["w_in_a"].shape[0]
    n_b = wts["w_in_b"].shape[0]
    nb = wts["w_rec_gate"].shape[1]
    dr = nb * LANES
    dm = wts["w_mem_kv"].shape[2] // 2
    row = lambda v: v.reshape(1, -1)

    wm_all = jnp.concatenate([wts["w_mem_kv"][l] for l in range(depth)], axis=1)
    memkv, memn_bf = _norm_matmul(mem, row(wts["mem_norm"]), wm_all, BF16, "mem_kv_proj")

    h = x
    saved = []
    vecs = []
    for l in range(n_a):
        proj, u_bf = _norm_matmul(h, row(wts["norm_a"][l]), wts["w_in_a"][l], F32, f"a{l}_in_proj")
        vec = jnp.concatenate([row(wts["conv_b"][l]), row(wts["b_rec_gate"][l]), row(wts["b_in_gate"][l]),
                               row(wts["lru_lambda"][l]), wts["conv_w"][l]], axis=0)
        vecs.append(vec)
        y_rnn, h_rnn, xc = _lru_fwd(proj, vec, wts["w_rec_gate"][l], wts["w_in_gate"][l], f"a{l}_lru_fwd")
        y_mem = _mem_attn_fwd(proj, memkv, l, dr, dm, f"a{l}_mem_fwd")
        mix = jnp.concatenate([y_rnn, y_mem], axis=1)
        h_next = _matmul_res(mix, wts["w_out_a"][l], h, f"a{l}_out_proj")
        saved.append((h, proj, u_bf, mix, h_rnn, xc))
        h = h_next

    h_kv = h
    kv, ukv_bf = _norm_matmul(h_kv, row(wts["kv_norm"]), wts["w_kv"], BF16, "kv_proj")

    for j in range(n_b):
        l = n_a + j
        proj, u_bf = _norm_matmul(h, row(wts["norm_b"][j]), wts["w_in_b"][j], F32, f"b{j}_in_proj")
        y_sb, o_sb, tl_sb = _sb_fwd(proj, kv, f"b{j}_sb_fwd")
        y_mem = _mem_attn_fwd(proj, memkv, l, dr, dm, f"b{j}_mem_fwd")
        mix = jnp.concatenate([y_sb, y_mem], axis=1)
        h_next = _matmul_res(mix, wts["w_out_b"][j], h, f"b{j}_out_proj")
        saved.append((h, proj, u_bf, mix, o_sb, tl_sb))
        h = h_next

    dh, d_final, loss = _final_loss_bwd(h, row(wts["final_norm"]), tgt, "final_loss_bwd")

    grads = {"final_norm": d_final.reshape(-1)}
    dmemkv = [None] * depth
    g_in_b, g_out_b, g_norm_b = [None] * n_b, [None] * n_b, [None] * n_b
    dks, dvs = [], []
    for j in reversed(range(n_b)):
        l = n_a + j
        h_in, proj, u_bf, mix, o_sb, tl_sb = saved[l]
        dmix = _matmul_nt(dh, wts["w_out_b"][j], f"b{j}_dmix")
        g_out_b[j] = _matmul_tn(mix, dh, f"b{j}_dw_out")
        dq, dg, dk, dv = _sb_bwd(proj, kv, o_sb, tl_sb, dmix, f"b{j}_sb_bwd")
        dqm, dgm, dkm, dvm = _mem_attn_bwd(proj, memkv, dmix, l, dr, dm, f"b{j}_mem_bwd")
        dmemkv[l] = (dkm, dvm)
        dproj = jnp.concatenate([dq, dg, dqm, dgm], axis=1)
        du = _matmul_nt(dproj, wts["w_in_b"][j], f"b{j}_du")
        g_in_b[j] = _matmul_tn(u_bf, dproj, f"b{j}_dw_in")
        dh, dgn = _rms_bwd(du, h_in, row(wts["norm_b"][j]), dh, f"b{j}_rms_bwd")
        g_norm_b[j] = dgn.reshape(-1)
        dks.append(dk)
        dvs.append(dv)
    assert n_b == 2
    dkv = jnp.concatenate([_add_cast(dks[0], dks[1], "dk_sum"), _add_cast(dvs[0], dvs[1], "dv_sum")], axis=1)
    du = _matmul_nt(dkv, wts["w_kv"], "kv_du")
    grads["w_kv"] = _matmul_tn(ukv_bf, dkv, "kv_dw")
    dh, dgn = _rms_bwd(du, h_kv, row(wts["kv_norm"]), dh, "kv_rms_bwd")
    grads["kv_norm"] = dgn.reshape(-1)

    g_in_a, g_out_a, g_norm_a = [None] * n_a, [None] * n_a, [None] * n_a
    g_wr, g_wi, g_vec = [None] * n_a, [None] * n_a, [None] * n_a
    for l in reversed(range(n_a)):
        h_in, proj, u_bf, mix, h_rnn, xc = saved[l]
        dmix = _matmul_nt(dh, wts["w_out_a"][l], f"a{l}_dmix")
        g_out_a[l] = _matmul_tn(mix, dh, f"a{l}_dw_out")
        dx, dg, g_wr[l], g_wi[l], dvec = _lru_bwd(proj, xc, h_rnn, dmix, vecs[l], wts["w_rec_gate"][l],
                                                  wts["w_in_gate"][l], f"a{l}_lru_bwd")
        g_vec[l] = dvec.transpose(1, 0, 2).reshape(8, dr)
        dqm, dgm, dkm, dvm = _mem_attn_bwd(proj, memkv, dmix, l, dr, dm, f"a{l}_mem_bwd")
        dmemkv[l] = (dkm, dvm)
        dproj = jnp.concatenate([dx, dg, dqm, dgm], axis=1)
        du = _matmul_nt(dproj, wts["w_in_a"][l], f"a{l}_du")
        g_in_a[l] = _matmul_tn(u_bf, dproj, f"a{l}_dw_in")
        dh, dgn = _rms_bwd(du, h_in, row(wts["norm_a"][l]), dh, f"a{l}_rms_bwd")
        g_norm_a[l] = dgn.reshape(-1)

    dmemkv_all = jnp.concatenate([jnp.concatenate(p, axis=1) for p in dmemkv], axis=1).astype(BF16)
    g_wm = _matmul_tn(memn_bf, dmemkv_all, "mem_dw")
    dmemn = _matmul_nt(dmemkv_all, wm_all, "mem_du")
    _, dgn = _rms_bwd(dmemn, mem, row(wts["mem_norm"]), jnp.zeros_like(mem), "mem_rms_bwd")
    grads["mem_norm"] = dgn.reshape(-1)
    grads["w_mem_kv"] = jnp.stack(jnp.split(g_wm, depth, axis=1))
    grads["norm_a"] = jnp.stack(g_norm_a)
    grads["w_in_a"] = jnp.stack(g_in_a)
    grads["w_out_a"] = jnp.stack(g_out_a)
    grads["w_rec_gate"] = jnp.stack(g_wr)
    grads["w_in_gate"] = jnp.stack(g_wi)
    gv = jnp.stack(g_vec)
    grads["conv_b"], grads["b_rec_gate"], grads["b_in_gate"], grads["lru_lambda"] = gv[:, 0], gv[:, 1], gv[:, 2], gv[:, 3]
    grads["conv_w"] = gv[:, 4:8]
    grads["norm_b"] = jnp.stack(g_norm_b)
    grads["w_in_b"] = jnp.stack(g_in_b)
    grads["w_out_b"] = jnp.stack(g_out_b)
    return loss, dh, grads


def kernel(x, mem, mem_norm, w_mem_kv, norm_a, w_in_a, conv_w, conv_b, w_rec_gate, b_rec_gate, w_in_gate, b_in_gate, lru_lambda, w_out_a, kv_norm, w_kv, norm_b, w_in_b, w_out_b, final_norm, loss_target, m_mem_norm, m_w_mem_kv, m_norm_a, m_w_in_a, m_conv_w, m_conv_b, m_w_rec_gate, m_b_rec_gate, m_w_in_gate, m_b_in_gate, m_lru_lambda, m_w_out_a, m_kv_norm, m_w_kv, m_norm_b, m_w_in_b, m_w_out_b, m_final_norm, v_mem_norm, v_w_mem_kv, v_norm_a, v_w_in_a, v_conv_w, v_conv_b, v_w_rec_gate, v_b_rec_gate, v_w_in_gate, v_b_in_gate, v_lru_lambda, v_w_out_a, v_kv_norm, v_w_kv, v_norm_b, v_w_in_b, v_w_out_b, v_final_norm):
    local = dict(mem_norm=mem_norm, w_mem_kv=w_mem_kv, norm_a=norm_a, w_in_a=w_in_a, conv_w=conv_w, conv_b=conv_b,
                 w_rec_gate=w_rec_gate, b_rec_gate=b_rec_gate, w_in_gate=w_in_gate, b_in_gate=b_in_gate,
                 lru_lambda=lru_lambda, w_out_a=w_out_a, kv_norm=kv_norm, w_kv=w_kv, norm_b=norm_b, w_in_b=w_in_b,
                 w_out_b=w_out_b, final_norm=final_norm)
    mom = dict(mem_norm=m_mem_norm, w_mem_kv=m_w_mem_kv, norm_a=m_norm_a, w_in_a=m_w_in_a, conv_w=m_conv_w,
               conv_b=m_conv_b, w_rec_gate=m_w_rec_gate, b_rec_gate=m_b_rec_gate, w_in_gate=m_w_in_gate,
               b_in_gate=m_b_in_gate, lru_lambda=m_lru_lambda, w_out_a=m_w_out_a, kv_norm=m_kv_norm, w_kv=m_w_kv,
               norm_b=m_norm_b, w_in_b=m_w_in_b, w_out_b=m_w_out_b, final_norm=m_final_norm)
    var = dict(mem_norm=v_mem_norm, w_mem_kv=v_w_mem_kv, norm_a=v_norm_a, w_in_a=v_w_in_a, conv_w=v_conv_w,
               conv_b=v_conv_b, w_rec_gate=v_w_rec_gate, b_rec_gate=v_b_rec_gate, w_in_gate=v_w_in_gate,
               b_in_gate=v_b_in_gate, lru_lambda=v_lru_lambda, w_out_a=v_w_out_a, kv_norm=v_kv_norm, w_kv=v_w_kv,
               norm_b=v_norm_b, w_in_b=v_w_in_b, w_out_b=v_w_out_b, final_norm=v_final_norm)

    wts = _gather_weights(local)
    for n in WEIGHTS:
        if SHARD_DIM[n] is None:
            wts[n] = local[n]
    wts["w_rec_gate"] = wts["w_rec_gate"].astype(BF16)
    wts["w_in_gate"] = wts["w_in_gate"].astype(BF16)

    loss, grad_x, grads = _local_grads(x[0], mem[0], loss_target[0], wts)

    row_multiple = 2 * FLAT_TR
    s = jnp.stack([_flatten([_piece(grads[n], n, k) for n in WEIGHTS], row_multiple) for k in range(N_CHIPS)])
    g_flat = _reduce_to_shards(s)
    w_flat = _flatten([local[n] for n in WEIGHTS], row_multiple)
    m_flat = _flatten([mom[n] for n in WEIGHTS], row_multiple)
    v_flat = _flatten([var[n] for n in WEIGHTS], row_multiple)
    d_flat, mo_flat, vo_flat = _adamw(g_flat, w_flat, m_flat, v_flat, "adamw")

    shapes = [local[n].shape for n in WEIGHTS]
    total_loss = lax.psum(loss[0, 0], MESH_AXES)
    return (total_loss, grad_x[None], *_unflatten(g_flat, shapes), *_unflatten(d_flat, shapes),
            *_unflatten(mo_flat, shapes), *_unflatten(vo_flat, shapes))
```

```python
import functools

import jax
import jax.numpy as jnp
from jax import lax
from jax.experimental import pallas as pl
from jax.experimental.pallas import tpu as pltpu

F32 = jnp.float32
BF16 = jnp.bfloat16

RMS_EPS = 1e-6
LRU_C = 8.0
ADAM_LR = 0.001
ADAM_B1 = 0.9
ADAM_B2 = 0.999
ADAM_EPS = 1e-08
ADAM_WD = 0.01
ADAM_STEP = 10

LANES = 128
VMEM_LIMIT = 56 * 1024 * 1024
FLAT_W = 1024
FLAT_TR = 256
N_CHIPS = 4
MESH_AXES = ("x", "y", "c")

_NT = (((1,), (1,)), ((), ()))
_TN = (((0,), (0,)), ((), ()))


def _cparams(n_axes):
    return pltpu.CompilerParams(dimension_semantics=("arbitrary",) * n_axes, vmem_limit_bytes=VMEM_LIMIT)


def _sigmoid(x):
    return 1.0 / (1.0 + jnp.exp(-x))


def _log1p_pos(e):
    return jnp.where(e < 1e-3, e * (1.0 - e * (0.5 - e * (1.0 / 3.0))), jnp.log(1.0 + e))


def _neg_expm1(x):
    small = -x * (1.0 + x * (0.5 + x * (1.0 / 6.0 + x * (1.0 / 24.0))))
    return jnp.where(x > -0.05, small, 1.0 - jnp.exp(x))


def _tile(n, want):
    if n <= want:
        return n
    t = want
    while n % t:
        t -= LANES
    assert t > 0, (n, want)
    return t


def _norm_matmul(x, g, w, out_dtype, name):
    m, k = x.shape
    n = w.shape[1]
    tm, tn = _tile(m, 512), _tile(n, 1024)

    def body(x_ref, g_ref, w_ref, o_ref, u_ref):
        @pl.when(pl.program_id(1) == 0)
        def _():
            xf = x_ref[...]
            r = lax.rsqrt(jnp.mean(xf * xf, axis=-1, keepdims=True) + RMS_EPS)
            u_ref[...] = ((xf * r) * g_ref[...]).astype(BF16)

        o_ref[...] = jnp.dot(u_ref[...], w_ref[...], preferred_element_type=F32).astype(o_ref.dtype)

    return pl.pallas_call(
        body, name=name, grid=(m // tm, n // tn),
        in_specs=[pl.BlockSpec((tm, k), lambda i, j: (i, 0)), pl.BlockSpec((1, k), lambda i, j: (0, 0)),
                  pl.BlockSpec((k, tn), lambda i, j: (0, j))],
        out_specs=[pl.BlockSpec((tm, tn), lambda i, j: (i, j)), pl.BlockSpec((tm, k), lambda i, j: (i, 0))],
        out_shape=[jax.ShapeDtypeStruct((m, n), out_dtype), jax.ShapeDtypeStruct((m, k), BF16)],
        compiler_params=_cparams(2),
    )(x, g, w)


def _matmul_res(a, b, res, name):
    m, k = a.shape
    n = b.shape[1]
    tm, tn = _tile(m, 512), _tile(n, 1024)

    def body(a_ref, b_ref, r_ref, o_ref):
        o_ref[...] = r_ref[...] + jnp.dot(a_ref[...], b_ref[...], preferred_element_type=F32)

    return pl.pallas_call(
        body, name=name, grid=(m // tm, n // tn),
        in_specs=[pl.BlockSpec((tm, k), lambda i, j: (i, 0)), pl.BlockSpec((k, tn), lambda i, j: (0, j)),
                  pl.BlockSpec((tm, tn), lambda i, j: (i, j))],
        out_specs=pl.BlockSpec((tm, tn), lambda i, j: (i, j)),
        out_shape=jax.ShapeDtypeStruct((m, n), F32),
        compiler_params=_cparams(2),
    )(a, b, res)


def _matmul_nt(a, b, name):
    m, n = a.shape
    k = b.shape[0]
    tm, tk = _tile(m, 512), _tile(k, 512)

    def body(a_ref, b_ref, o_ref):
        o_ref[...] = lax.dot_general(a_ref[...].astype(BF16), b_ref[...], _NT, preferred_element_type=F32)

    return pl.pallas_call(
        body, name=name, grid=(m // tm, k // tk),
        in_specs=[pl.BlockSpec((tm, n), lambda i, j: (i, 0)), pl.BlockSpec((tk, n), lambda i, j: (j, 0))],
        out_specs=pl.BlockSpec((tm, tk), lambda i, j: (i, j)),
        out_shape=jax.ShapeDtypeStruct((m, k), F32),
        compiler_params=_cparams(2),
    )(a, b)


def _matmul_tn(a, b, name):
    m, k = a.shape
    n = b.shape[1]
    tm, tk, tn = _tile(m, 512), _tile(k, 512), _tile(n, 2048)

    def body(a_ref, b_ref, o_ref):
        part = lax.dot_general(a_ref[...].astype(BF16), b_ref[...].astype(BF16), _TN, preferred_element_type=F32)

        @pl.when(pl.program_id(2) == 0)
        def _():
            o_ref[...] = part

        @pl.when(pl.program_id(2) != 0)
        def _():
            o_ref[...] += part

    return pl.pallas_call(
        body, name=name, grid=(k // tk, n // tn, m // tm),
        in_specs=[pl.BlockSpec((tm, tk), lambda i, j, s: (s, i)), pl.BlockSpec((tm, tn), lambda i, j, s: (s, j))],
        out_specs=pl.BlockSpec((tk, tn), lambda i, j, s: (i, j)),
        out_shape=jax.ShapeDtypeStruct((k, n), F32),
        compiler_params=_cparams(3),
    )(a, b)


def _rms_bwd(du, h, g, dres, name):
    m, d = h.shape
    tm = _tile(m, 256)

    def body(du_ref, h_ref, g_ref, dres_ref, dx_ref, dg_ref):
        xf = h_ref[...]
        r = lax.rsqrt(jnp.mean(xf * xf, axis=-1, keepdims=True) + RMS_EPS)
        xhat = xf * r
        du_v = du_ref[...]
        dxn = du_v * g_ref[...]
        dx_ref[...] = dres_ref[...] + r * (dxn - xhat * jnp.mean(dxn * xhat, axis=-1, keepdims=True))
        part = jnp.sum(du_v * xhat, axis=0, keepdims=True)

        @pl.when(pl.program_id(0) == 0)
        def _():
            dg_ref[...] = part

        @pl.when(pl.program_id(0) != 0)
        def _():
            dg_ref[...] += part

    row = lambda i: (i, 0)
    return pl.pallas_call(
        body, name=name, grid=(m // tm,),
        in_specs=[pl.BlockSpec((tm, d), row), pl.BlockSpec((tm, d), row), pl.BlockSpec((1, d), lambda i: (0, 0)),
                  pl.BlockSpec((tm, d), row)],
        out_specs=[pl.BlockSpec((tm, d), row), pl.BlockSpec((1, d), lambda i: (0, 0))],
        out_shape=[jax.ShapeDtypeStruct((m, d), F32), jax.ShapeDtypeStruct((1, d), F32)],
        compiler_params=_cparams(1),
    )(du, h, g, dres)


def _final_loss_bwd(h, g, tgt, name):
    m, d = h.shape
    tm = _tile(m, 256)

    def body(h_ref, g_ref, t_ref, dx_ref, dg_ref, loss_ref):
        xf = h_ref[...]
        r = lax.rsqrt(jnp.mean(xf * xf, axis=-1, keepdims=True) + RMS_EPS)
        xhat = xf * r
        gv = g_ref[...]
        err = xhat * gv - t_ref[...]
        dy = err * (1.0 / d)
        dxn = dy * gv
        dx_ref[...] = r * (dxn - xhat * jnp.mean(dxn * xhat, axis=-1, keepdims=True))
        part = jnp.sum(dy * xhat, axis=0, keepdims=True)
        lpart = jnp.sum(jnp.sum(err * err, axis=0, keepdims=True), axis=1, keepdims=True) * (0.5 / d)

        @pl.when(pl.program_id(0) == 0)
        def _():
            dg_ref[...] = part
            loss_ref[...] = lpart

        @pl.when(pl.program_id(0) != 0)
        def _():
            dg_ref[...] += part
            loss_ref[...] += lpart

    row = lambda i: (i, 0)
    fixed = lambda i: (0, 0)
    return pl.pallas_call(
        body, name=name, grid=(m // tm,),
        in_specs=[pl.BlockSpec((tm, d), row), pl.BlockSpec((1, d), fixed), pl.BlockSpec((tm, d), row)],
        out_specs=[pl.BlockSpec((tm, d), row), pl.BlockSpec((1, d), fixed), pl.BlockSpec((1, 1), fixed)],
        out_shape=[jax.ShapeDtypeStruct((m, d), F32), jax.ShapeDtypeStruct((1, d), F32),
                   jax.ShapeDtypeStruct((1, 1), F32)],
        compiler_params=_cparams(1),
    )(h, g, tgt)


def _add_cast(a, b, name):
    m, n = a.shape
    tm = _tile(m, 512)

    def body(a_ref, b_ref, o_ref):
        o_ref[...] = (a_ref[...] + b_ref[...]).astype(BF16)

    row = lambda i: (i, 0)
    return pl.pallas_call(
        body, name=name, grid=(m // tm,),
        in_specs=[pl.BlockSpec((tm, n), row), pl.BlockSpec((tm, n), row)],
        out_specs=pl.BlockSpec((tm, n), row),
        out_shape=jax.ShapeDtypeStruct((m, n), BF16),
        compiler_params=_cparams(1),
    )(a, b)


def _mem_attn_fwd(proj, memkv, layer, dr, dm, name):
    t = proj.shape[0]
    nm = memkv.shape[0]
    tm = _tile(t, 512)
    nh = dm // LANES
    scale = LANES ** -0.5
    qb = (2 * dr) // dm

    def body(q_ref, g_ref, k_ref, v_ref, y_ref):
        for hh in range(nh):
            sl = slice(hh * LANES, (hh + 1) * LANES)
            s = lax.dot_general(q_ref[:, sl].astype(BF16), k_ref[:, sl], _NT, preferred_element_type=F32) * scale
            p = jnp.exp(s - jnp.max(s, axis=-1, keepdims=True))
            p = p / jnp.sum(p, axis=-1, keepdims=True)
            o = jnp.dot(p.astype(BF16), v_ref[:, sl], preferred_element_type=F32)
            gv = g_ref[:, sl]
            y_ref[:, sl] = (o * (gv * _sigmoid(gv))).astype(BF16)

    return pl.pallas_call(
        body, name=name, grid=(t // tm,),
        in_specs=[pl.BlockSpec((tm, dm), lambda i: (i, qb)), pl.BlockSpec((tm, dm), lambda i: (i, qb + 1)),
                  pl.BlockSpec((nm, dm), lambda i: (0, 2 * layer)), pl.BlockSpec((nm, dm), lambda i: (0, 2 * layer + 1))],
        out_specs=pl.BlockSpec((tm, dm), lambda i: (i, 0)),
        out_shape=jax.ShapeDtypeStruct((t, dm), BF16),
        compiler_params=_cparams(1),
    )(proj, proj, memkv, memkv)


def _mem_attn_bwd(proj, memkv, dmix, layer, dr, dm, name):
    t = proj.shape[0]
    nm = memkv.shape[0]
    tm = _tile(t, 512)
    nh = dm // LANES
    scale = LANES ** -0.5
    qb = (2 * dr) // dm
    yb = dr // dm

    def body(q_ref, g_ref, k_ref, v_ref, dy_ref, dq_ref, dg_ref, dk_ref, dv_ref):
        @pl.when(pl.program_id(0) == 0)
        def _():
            dk_ref[...] = jnp.zeros_like(dk_ref)
            dv_ref[...] = jnp.zeros_like(dv_ref)

        for hh in range(nh):
            sl = slice(hh * LANES, (hh + 1) * LANES)
            q = q_ref[:, sl].astype(BF16)
            k = k_ref[:, sl]
            v = v_ref[:, sl]
            s = lax.dot_general(q, k, _NT, preferred_element_type=F32) * scale
            p = jnp.exp(s - jnp.max(s, axis=-1, keepdims=True))
            p = p / jnp.sum(p, axis=-1, keepdims=True)
            p_bf = p.astype(BF16)
            o = jnp.dot(p_bf, v, preferred_element_type=F32)
            gv = g_ref[:, sl]
            sg = _sigmoid(gv)
            dy = dy_ref[:, sl]
            do = dy * (gv * sg)
            dg_ref[:, sl] = (dy * o * (sg * (1.0 + gv * (1.0 - sg)))).astype(BF16)
            do_bf = do.astype(BF16)
            dv_ref[:, sl] += lax.dot_general(p_bf, do_bf, _TN, preferred_element_type=F32)
            dp = lax.dot_general(do_bf, v, _NT, preferred_element_type=F32)
            ds = (p * (dp - jnp.sum(dp * p, axis=-1, keepdims=True)) * scale).astype(BF16)
            dq_ref[:, sl] = jnp.dot(ds, k, preferred_element_type=F32).astype(BF16)
            dk_ref[:, sl] += lax.dot_general(ds, q, _TN, preferred_element_type=F32)

    fixed = lambda i: (0, 0)
    return pl.pallas_call(
        body, name=name, grid=(t // tm,),
        in_specs=[pl.BlockSpec((tm, dm), lambda i: (i, qb)), pl.BlockSpec((tm, dm), lambda i: (i, qb + 1)),
                  pl.BlockSpec((nm, dm), lambda i: (0, 2 * layer)), pl.BlockSpec((nm, dm), lambda i: (0, 2 * layer + 1)),
                  pl.BlockSpec((tm, dm), lambda i: (i, yb))],
        out_specs=[pl.BlockSpec((tm, dm), lambda i: (i, 0)), pl.BlockSpec((tm, dm), lambda i: (i, 0)),
                   pl.BlockSpec((nm, dm), fixed), pl.BlockSpec((nm, dm), fixed)],
        out_shape=[jax.ShapeDtypeStruct((t, dm), BF16), jax.ShapeDtypeStruct((t, dm), BF16),
                   jax.ShapeDtypeStruct((nm, dm), F32), jax.ShapeDtypeStruct((nm, dm), F32)],
        compiler_params=_cparams(1),
    )(proj, proj, memkv, memkv, dmix)


LRU_CHUNK = 256


def _lru_gates(xc, vec, wr_ref, wi_ref):
    r = _sigmoid(jnp.dot(xc.astype(BF16), wr_ref[...], preferred_element_type=F32) + vec[1:2])
    i = _sigmoid(jnp.dot(xc.astype(BF16), wi_ref[...], preferred_element_type=F32) + vec[2:3])
    lam = vec[3:4]
    cl = -LRU_C * (jnp.maximum(-lam, 0.0) + _log1p_pos(jnp.exp(-jnp.abs(lam))))
    la = cl * r
    a = jnp.exp(la)
    s2 = _neg_expm1(2.0 * la)
    return r, i, cl, a, s2


def _lru_fwd(proj, vec, wr, wi, name):
    t = proj.shape[0]
    nb = wr.shape[0]
    dr = nb * LANES
    c = _tile(t, LRU_CHUNK)

    def body(x_ref, g_ref, vec_ref, wr_ref, wi_ref, y_ref, h_ref, xc_ref, carry_ref, xprev_ref):
        @pl.when(pl.program_id(1) == 0)
        def _():
            carry_ref[...] = jnp.zeros_like(carry_ref)
            xprev_ref[...] = jnp.zeros_like(xprev_ref)

        x = x_ref[...]
        vec = vec_ref[...]
        rows = lax.broadcasted_iota(jnp.int32, (c, LANES), 0)
        xprev = xprev_ref[...]
        xc = vec[7:8] * x + vec[0:1]
        for k in range(1, 4):
            xs = jnp.where(rows < k, pltpu.roll(xprev, k, 0), pltpu.roll(x, k, 0))
            xc = xc + vec[7 - k:8 - k] * xs
        xprev_ref[...] = x
        xc_ref[...] = xc

        r, i, cl, a, s2 = _lru_gates(xc, vec, wr_ref, wi_ref)
        hh = jnp.sqrt(s2) * (i * xc)
        aa = a
        d = 1
        while d < c:
            keep = rows >= d
            hh = jnp.where(keep, aa * pltpu.roll(hh, d, 0) + hh, hh)
            aa = jnp.where(keep, aa * pltpu.roll(aa, d, 0), aa)
            d *= 2
        hfull = hh + aa * carry_ref[7:8, :]
        carry_ref[...] = hfull[c - 8:c, :]
        h_ref[...] = hfull
        gv = g_ref[...]
        y_ref[...] = (hfull * (gv * _sigmoid(gv))).astype(BF16)

    blk = lambda n, s: (s, n)
    return pl.pallas_call(
        body, name=name, grid=(nb, t // c),
        in_specs=[pl.BlockSpec((c, LANES), blk), pl.BlockSpec((c, LANES), lambda n, s: (s, nb + n)),
                  pl.BlockSpec((8, LANES), lambda n, s: (0, n)),
                  pl.BlockSpec((None, LANES, LANES), lambda n, s: (n, 0, 0)),
                  pl.BlockSpec((None, LANES, LANES), lambda n, s: (n, 0, 0))],
        out_specs=[pl.BlockSpec((c, LANES), blk)] * 3,
        out_shape=[jax.ShapeDtypeStruct((t, dr), BF16), jax.ShapeDtypeStruct((t, dr), F32),
                   jax.ShapeDtypeStruct((t, dr), F32)],
        scratch_shapes=[pltpu.VMEM((8, LANES), F32), pltpu.VMEM((c, LANES), F32)],
        compiler_params=_cparams(2),
    )(proj, proj, vec, wr, wi)


def _lru_bwd(proj, xc_all, h_all, dmix, vec, wr, wi, name):
    t = proj.shape[0]
    nb = wr.shape[0]
    dr = nb * LANES
    c = _tile(t, LRU_CHUNK)
    nc = t // c

    def body(x_ref, g_ref, xc_ref, h_ref, dy_ref, vec_ref, wr_ref, wi_ref,
             dx_ref, dg_ref, dwr_ref, dwi_ref, dvec_ref, qcarry_ref, dxc_next_ref):
        @pl.when(pl.program_id(1) == 0)
        def _():
            qcarry_ref[...] = jnp.zeros_like(qcarry_ref)
            dxc_next_ref[...] = jnp.zeros_like(dxc_next_ref)
            dwr_ref[...] = jnp.zeros_like(dwr_ref)
            dwi_ref[...] = jnp.zeros_like(dwi_ref)
            dvec_ref[...] = jnp.zeros_like(dvec_ref)

        x = x_ref[...]
        xc = xc_ref[...]
        h = h_ref[...]
        dy = dy_ref[...]
        gv = g_ref[...]
        vec = vec_ref[...]
        rows = lax.broadcasted_iota(jnp.int32, (c, LANES), 0)

        r, i, cl, a, s2 = _lru_gates(xc, vec, wr_ref, wi_ref)
        s = jnp.sqrt(s2)
        ixc = i * xc
        u = s * ixc
        sg = _sigmoid(gv)
        dh = dy * (gv * sg)
        dg_ref[...] = (dy * h * (sg * (1.0 + gv * (1.0 - sg)))).astype(BF16)

        aa = a
        qq = a * dh
        d = 1
        while d < c:
            keep = rows < c - d
            qq = jnp.where(keep, qq + aa * pltpu.roll(qq, c - d, 0), qq)
            aa = jnp.where(keep, aa * pltpu.roll(aa, c - d, 0), aa)
            d *= 2
        qin = qcarry_ref[0:1, :]
        qfull = qq + aa * qin
        gt = dh + jnp.where(rows == c - 1, qin, pltpu.roll(qfull, c - 1, 0))
        qcarry_ref[...] = qfull[0:8, :]

        dla = gt * (h - u) - gt * ixc * (a * a) / s
        dixc = gt * s
        di = dixc * xc
        dxc = dixc * i
        dzr = (dla * cl) * (r * (1.0 - r))
        dzi = di * (i * (1.0 - i))
        dzr_bf = dzr.astype(BF16)
        dzi_bf = dzi.astype(BF16)
        dxc = dxc + lax.dot_general(dzr_bf, wr_ref[...], _NT, preferred_element_type=F32)
        dxc = dxc + lax.dot_general(dzi_bf, wi_ref[...], _NT, preferred_element_type=F32)
        xc_bf = xc.astype(BF16)
        dwr_ref[...] += lax.dot_general(xc_bf, dzr_bf, _TN, preferred_element_type=F32)
        dwi_ref[...] += lax.dot_general(xc_bf, dzi_bf, _TN, preferred_element_type=F32)

        lam = vec[3:4]
        dlam = jnp.sum(dla * r, axis=0, keepdims=True) * (LRU_C * _sigmoid(-lam))
        colsum = lambda v: jnp.sum(v, axis=0, keepdims=True)
        dxn = dxc_next_ref[...]
        dx = vec[7:8] * dxc
        dtaps = [None] * 4
        dtaps[3] = colsum(x * dxc)
        for k in range(1, 4):
            sh = jnp.where(rows < c - k, pltpu.roll(dxc, c - k, 0), pltpu.roll(dxn, c - k, 0))
            dx = dx + vec[7 - k:8 - k] * sh
            dtaps[3 - k] = colsum(x * sh)
        dxc_next_ref[...] = dxc
        dx_ref[...] = dx.astype(BF16)
        dvec_ref[...] += jnp.concatenate([colsum(dxc), colsum(dzr), colsum(dzi), dlam] + dtaps, axis=0)

    rev = lambda n, s: (nc - 1 - s, n)
    sq = lambda n, s: (n, 0, 0)
    return pl.pallas_call(
        body, name=name, grid=(nb, nc),
        in_specs=[pl.BlockSpec((c, LANES), rev), pl.BlockSpec((c, LANES), lambda n, s: (nc - 1 - s, nb + n)),
                  pl.BlockSpec((c, LANES), rev), pl.BlockSpec((c, LANES), rev), pl.BlockSpec((c, LANES), rev),
                  pl.BlockSpec((8, LANES), lambda n, s: (0, n)),
                  pl.BlockSpec((None, LANES, LANES), sq), pl.BlockSpec((None, LANES, LANES), sq)],
        out_specs=[pl.BlockSpec((c, LANES), rev), pl.BlockSpec((c, LANES), rev),
                   pl.BlockSpec((None, LANES, LANES), sq), pl.BlockSpec((None, LANES, LANES), sq),
                   pl.BlockSpec((None, 8, LANES), sq)],
        out_shape=[jax.ShapeDtypeStruct((t, dr), BF16), jax.ShapeDtypeStruct((t, dr), BF16),
                   jax.ShapeDtypeStruct((nb, LANES, LANES), F32), jax.ShapeDtypeStruct((nb, LANES, LANES), F32),
                   jax.ShapeDtypeStruct((nb, 8, LANES), F32)],
        scratch_shapes=[pltpu.VMEM((8, LANES), F32), pltpu.VMEM((c, LANES), F32)],
        compiler_params=_cparams(2),
    )(proj, proj, xc_all, h_all, dmix, vec, wr, wi)


SB_TQ = 1024
SB_TK = 256


def _sb_logits(q, k, scale, diag):
    z = lax.dot_general(q, k, _NT, preferred_element_type=F32) * scale
    lb = jnp.minimum(z, 0.0) - jnp.log(1.0 + jnp.exp(-jnp.abs(z)))
    lk = lb - z
    mask = None
    if diag:
        mask = lax.broadcasted_iota(jnp.int32, z.shape, 1) < lax.broadcasted_iota(jnp.int32, z.shape, 0)
        lk = jnp.where(mask, lk, 0.0)
    return z, lb, lk, mask


def _split_dot(v, m):
    hi = v.astype(BF16)
    lo = (v - hi.astype(F32)).astype(BF16)
    return jnp.dot(hi, m, preferred_element_type=F32) + jnp.dot(lo, m, preferred_element_type=F32)


def _tri_ones(kind, tk):
    jj = lax.broadcasted_iota(jnp.int32, (tk, tk), 0)
    ss = lax.broadcasted_iota(jnp.int32, (tk, tk), 1)
    rel = {"ge": jj >= ss, "le": jj <= ss}[kind]
    return jnp.where(rel, 1.0, 0.0).astype(BF16)


def _sb_fwd(proj, kv, name):
    t = proj.shape[0]
    ds = kv.shape[1] // 2
    nh = ds // LANES
    tq = _tile(t, SB_TQ)
    tk = _tile(tq, SB_TK)
    nd = tq // tk
    scale = LANES ** -0.5

    def body(q_ref, g_ref, k_ref, v_ref, y_ref, o_ref, tl_ref, qbf_ref, acc_ref, run_ref):
        qi = pl.program_id(1)
        qbf_ref[...] = q_ref[...].astype(BF16)
        tri = _tri_ones("ge", tk)
        acc_ref[...] = jnp.zeros_like(acc_ref)
        run_ref[...] = jnp.zeros_like(run_ref)

        def block(k0, r0, diag):
            groups = [(slice(s0, s0 + tk), diag and s0 == r0) for s0 in range(r0, tq, tk)]
            k = k_ref[pl.ds(k0, tk), :]
            v = v_ref[pl.ds(k0, tk), :]
            logits = [_sb_logits(qbf_ref[rows, :], k, scale, dg) for rows, dg in groups]
            cums = [_split_dot(lk, tri) for _, _, lk, _ in logits]
            for (rows, dg), (z, _, _, mask), cum in zip(groups, logits, cums):
                run = run_ref[rows, :]
                w = jnp.exp(z + cum + run)
                if dg:
                    w = jnp.where(mask, w, 0.0)
                acc_ref[rows, :] += jnp.dot(w.astype(BF16), v, preferred_element_type=F32)
                run_ref[rows, :] = run + cum[:, 0:1]

        for u in reversed(range(nd)):
            block(pl.multiple_of(qi * tq + u * tk, tk), u * tk, True)

        def step(j, carry):
            block(pl.multiple_of((qi * nd - 1 - j) * tk, tk), 0, False)
            return carry

        lax.fori_loop(0, qi * nd, step, 0)
        o = acc_ref[...]
        o_ref[...] = o
        tl_ref[...] = jnp.broadcast_to(run_ref[...], (tq, LANES))
        gv = g_ref[...]
        y_ref[...] = (o * (gv * _sigmoid(gv))).astype(BF16)

    blk = lambda h, i: (i, h)
    return pl.pallas_call(
        body, name=name, grid=(nh, t // tq),
        in_specs=[pl.BlockSpec((tq, LANES), blk), pl.BlockSpec((tq, LANES), lambda h, i: (i, nh + h)),
                  pl.BlockSpec((t, LANES), lambda h, i: (0, h)), pl.BlockSpec((t, LANES), lambda h, i: (0, nh + h))],
        out_specs=[pl.BlockSpec((tq, LANES), blk)] * 3,
        out_shape=[jax.ShapeDtypeStruct((t, ds), BF16), jax.ShapeDtypeStruct((t, ds), F32),
                   jax.ShapeDtypeStruct((t, ds), F32)],
        scratch_shapes=[pltpu.VMEM((tq, LANES), BF16), pltpu.VMEM((tq, LANES), F32), pltpu.VMEM((tq, 1), F32)],
        compiler_params=_cparams(2),
    )(proj, proj, kv, kv)


def _sb_bwd(proj, kv, o_all, tl_all, dmix, name):
    t = proj.shape[0]
    ds = kv.shape[1] // 2
    nh = ds // LANES
    tq = _tile(t, SB_TQ)
    tk = _tile(tq, SB_TK)
    nd = tq // tk
    scale = LANES ** -0.5

    def body(q_ref, g_ref, k_ref, v_ref, o_ref, tl_ref, dy_ref, dq_ref, dg_ref, dk_ref, dv_ref,
             qbf_ref, dobf_ref, acc_ref, left_ref, rune_ref):
        qi = pl.program_id(1)

        @pl.when(qi == 0)
        def _():
            dk_ref[...] = jnp.zeros_like(dk_ref)
            dv_ref[...] = jnp.zeros_like(dv_ref)

        qbf_ref[...] = q_ref[...].astype(BF16)
        gv = g_ref[...]
        sg = _sigmoid(gv)
        dy = dy_ref[...]
        dobf_ref[...] = (dy * (gv * sg)).astype(BF16)
        dg_ref[...] = (dy * o_ref[...] * (sg * (1.0 + gv * (1.0 - sg)))).astype(BF16)
        tri = _tri_ones("le", tk)
        acc_ref[...] = jnp.zeros_like(acc_ref)
        left_ref[...] = tl_ref[:, 0:1]
        rune_ref[...] = jnp.zeros_like(rune_ref)

        def block(k0, r0, diag):
            groups = [(slice(s0, s0 + tk), diag and s0 == r0) for s0 in range(r0, tq, tk)]
            k = k_ref[pl.ds(k0, tk), :]
            v = v_ref[pl.ds(k0, tk), :]
            logits = [_sb_logits(qbf_ref[rows, :], k, scale, dg) for rows, dg in groups]
            dws = [lax.dot_general(dobf_ref[rows, :], v, _NT, preferred_element_type=F32) for rows, _ in groups]
            cums = [_split_dot(lk, tri) for _, _, lk, _ in logits]
            ws, es = [], []
            for (rows, dg), (_, lb, _, mask), cum, dw in zip(groups, logits, cums, dws):
                left = left_ref[rows, :]
                w = jnp.exp(lb + (left - cum))
                if dg:
                    w = jnp.where(mask, w, 0.0)
                left_ref[rows, :] = left - cum[:, tk - 1:tk]
                ws.append(w.astype(BF16))
                es.append(dw * w)
            cumes = [_split_dot(e, tri) for e in es]
            dzs = []
            for (rows, dg), (_, lb, _, mask), e, cume in zip(groups, logits, es, cumes):
                rune = rune_ref[rows, :]
                dz = (e - jnp.exp(lb) * (rune + cume)) * scale
                if dg:
                    dz = jnp.where(mask, dz, 0.0)
                rune_ref[rows, :] = rune + cume[:, tk - 1:tk]
                dz = dz.astype(BF16)
                acc_ref[rows, :] += jnp.dot(dz, k, preferred_element_type=F32)
                dzs.append(dz)
            rows_all = slice(r0, tq)
            cat = lambda parts: parts[0] if len(parts) == 1 else jnp.concatenate(parts, axis=0)
            dk_ref[pl.ds(k0, tk), :] += lax.dot_general(cat(dzs), qbf_ref[rows_all, :], _TN, preferred_element_type=F32)
            dv_ref[pl.ds(k0, tk), :] += lax.dot_general(cat(ws), dobf_ref[rows_all, :], _TN, preferred_element_type=F32)

        def step(kb, carry):
            block(pl.multiple_of(kb * tk, tk), 0, False)
            return carry

        lax.fori_loop(0, qi * nd, step, 0)
        for u in range(nd):
            block(pl.multiple_of(qi * tq + u * tk, tk), u * tk, True)
        dq_ref[...] = acc_ref[...].astype(BF16)

    blk = lambda h, i: (i, h)
    whole = lambda h, i: (0, h)
    return pl.pallas_call(
        body, name=name, grid=(nh, t // tq),
        in_specs=[pl.BlockSpec((tq, LANES), blk), pl.BlockSpec((tq, LANES), lambda h, i: (i, nh + h)),
                  pl.BlockSpec((t, LANES), whole), pl.BlockSpec((t, LANES), lambda h, i: (0, nh + h)),
                  pl.BlockSpec((tq, LANES), blk), pl.BlockSpec((tq, LANES), blk), pl.BlockSpec((tq, LANES), blk)],
        out_specs=[pl.BlockSpec((tq, LANES), blk), pl.BlockSpec((tq, LANES), blk),
                   pl.BlockSpec((t, LANES), whole), pl.BlockSpec((t, LANES), whole)],
        out_shape=[jax.ShapeDtypeStruct((t, ds), BF16), jax.ShapeDtypeStruct((t, ds), BF16),
                   jax.ShapeDtypeStruct((t, ds), F32), jax.ShapeDtypeStruct((t, ds), F32)],
        scratch_shapes=[pltpu.VMEM((tq, LANES), BF16), pltpu.VMEM((tq, LANES), BF16), pltpu.VMEM((tq, LANES), F32),
                        pltpu.VMEM((tq, 1), F32), pltpu.VMEM((tq, 1), F32)],
        compiler_params=_cparams(2),
    )(proj, proj, kv, kv, o_all, tl_all, dmix)


ANY = pl.BlockSpec(memory_space=pl.ANY)
MESH = pl.DeviceIdType.MESH


def _place():
    x, y, c = lax.axis_index("x"), lax.axis_index("y"), lax.axis_index("c")
    chips = [(1 - x, y), (x, 1 - y), (1 - x, 1 - y)]
    return x, y, c, chips


def _remote(src, dst, send_sems, recv_sems, k, to):
    return pltpu.make_async_remote_copy(src_ref=src, dst_ref=dst, send_sem=send_sems.at[k], recv_sem=recv_sems.at[k],
                                        device_id=to, device_id_type=MESH)


def _chip_all_gather(shard, name):
    r, w = shard.shape
    rh = r // 2

    def body(x_ref, out_ref, send_sems, recv_sems, local_sem):
        x, y, c, chips = _place()
        me = 2 * x + y
        sibling = (x, y, 1 - c)
        mine_rows = pl.ds(c * rh, rh)
        other_rows = pl.ds((1 - c) * rh, rh)
        own = pltpu.make_async_copy(x_ref, out_ref.at[me], local_sem)
        own.start()
        first = [_remote(x_ref.at[mine_rows], out_ref.at[me, mine_rows], send_sems, recv_sems, k, (cx, cy, c))
                 for k, (cx, cy) in enumerate(chips)]
        for cp in first:
            cp.start()
        passed = []
        for k, (cx, cy) in enumerate(chips):
            got = out_ref.at[2 * cx + cy, mine_rows]
            _remote(got, got, send_sems, recv_sems, k, (cx, cy, c)).wait_recv()
            fwd = _remote(got, got, send_sems, recv_sems, 3 + k, sibling)
            fwd.start()
            passed.append(fwd)
        for k, (cx, cy) in enumerate(chips):
            got = out_ref.at[2 * cx + cy, other_rows]
            _remote(got, got, send_sems, recv_sems, 3 + k, sibling).wait_recv()
        for cp in first + passed:
            cp.wait_send()
        own.wait()

    return pl.pallas_call(
        body, name=name, in_specs=[ANY], out_specs=ANY,
        out_shape=jax.ShapeDtypeStruct((N_CHIPS, r, w), shard.dtype),
        scratch_shapes=[pltpu.SemaphoreType.DMA((6,)), pltpu.SemaphoreType.DMA((6,)), pltpu.SemaphoreType.DMA],
    )(shard)


def _sibling_take_half(s, name):
    n, r, w = s.shape
    rh = r // 2

    def body(s_ref, a_ref, send_sem, recv_sem):
        x, y, c, _ = _place()
        cp = pltpu.make_async_remote_copy(
            src_ref=s_ref.at[:, pl.ds((1 - c) * rh, rh), :], dst_ref=a_ref, send_sem=send_sem, recv_sem=recv_sem,
            device_id=(x, y, 1 - c), device_id_type=MESH)
        cp.start()
        cp.wait()

    return pl.pallas_call(
        body, name=name, in_specs=[ANY], out_specs=ANY, out_shape=jax.ShapeDtypeStruct((n, rh, w), s.dtype),
        scratch_shapes=[pltpu.SemaphoreType.DMA, pltpu.SemaphoreType.DMA],
    )(s)


def _add_own_half(s, a, c_arr, name):
    n, r, w = s.shape
    rh = r // 2
    nblk = rh // FLAT_TR

    def body(c_ref, s_ref, a_ref, o_ref):
        o_ref[...] = s_ref[...] + a_ref[...]

    return pl.pallas_call(
        body, name=name, out_shape=jax.ShapeDtypeStruct((n, rh, w), F32),
        grid_spec=pltpu.PrefetchScalarGridSpec(
            num_scalar_prefetch=1, grid=(n, nblk),
            in_specs=[pl.BlockSpec((None, FLAT_TR, w), lambda k, i, c_ref: (k, c_ref[0] * nblk + i, 0)),
                      pl.BlockSpec((None, FLAT_TR, w), lambda k, i, c_ref: (k, i, 0))],
            out_specs=pl.BlockSpec((None, FLAT_TR, w), lambda k, i, c_ref: (k, i, 0))),
        compiler_params=_cparams(2),
    )(c_arr, s, a)


def _chip_scatter(p, name):
    n, rh, w = p.shape

    def body(p_ref, b_ref, send_sems, recv_sems, local_sem):
        x, y, c, chips = _place()
        me = 2 * x + y
        own = pltpu.make_async_copy(p_ref.at[me], b_ref.at[me], local_sem)
        own.start()
        sends = [_remote(p_ref.at[2 * cx + cy], b_ref.at[me], send_sems, recv_sems, k, (cx, cy, c))
                 for k, (cx, cy) in enumerate(chips)]
        for cp in sends:
            cp.start()
        for k, (cx, cy) in enumerate(chips):
            got = b_ref.at[2 * cx + cy]
            _remote(got, got, send_sems, recv_sems, k, (cx, cy, c)).wait_recv()
        for cp in sends:
            cp.wait_send()
        own.wait()

    return pl.pallas_call(
        body, name=name, in_specs=[ANY], out_specs=ANY, out_shape=jax.ShapeDtypeStruct((n, rh, w), p.dtype),
        scratch_shapes=[pltpu.SemaphoreType.DMA((3,)), pltpu.SemaphoreType.DMA((3,)), pltpu.SemaphoreType.DMA],
    )(p)


def _sum_slots(b, name):
    n, rh, w = b.shape

    def body(b_ref, o_ref):
        o_ref[...] = ((b_ref[0] + b_ref[1]) + b_ref[2]) + b_ref[3]

    return pl.pallas_call(
        body, name=name, grid=(rh // FLAT_TR,),
        in_specs=[pl.BlockSpec((n, FLAT_TR, w), lambda i: (0, i, 0))],
        out_specs=pl.BlockSpec((FLAT_TR, w), lambda i: (i, 0)),
        out_shape=jax.ShapeDtypeStruct((rh, w), F32),
        compiler_params=_cparams(1),
    )(b)


def _sibling_join_halves(half, name):
    rh, w = half.shape

    def body(h_ref, g_ref, send_sem, recv_sem, local_sem):
        x, y, c, _ = _place()
        mine = g_ref.at[pl.ds(c * rh, rh)]
        own = pltpu.make_async_copy(h_ref, mine, local_sem)
        own.start()
        cp = pltpu.make_async_remote_copy(src_ref=h_ref, dst_ref=mine, send_sem=send_sem, recv_sem=recv_sem,
                                          device_id=(x, y, 1 - c), device_id_type=MESH)
        cp.start()
        cp.wait_send()
        theirs = g_ref.at[pl.ds((1 - c) * rh, rh)]
        pltpu.make_async_remote_copy(src_ref=theirs, dst_ref=theirs, send_sem=send_sem, recv_sem=recv_sem,
                                     device_id=(x, y, 1 - c), device_id_type=MESH).wait_recv()
        own.wait()

    return pl.pallas_call(
        body, name=name, in_specs=[ANY], out_specs=ANY, out_shape=jax.ShapeDtypeStruct((2 * rh, w), half.dtype),
        scratch_shapes=[pltpu.SemaphoreType.DMA, pltpu.SemaphoreType.DMA, pltpu.SemaphoreType.DMA],
    )(half)


def _adamw(g, w, m, v, name):
    r, wd = g.shape

    def body(g_ref, w_ref, m_ref, v_ref, d_ref, mo_ref, vo_ref):
        gv = g_ref[...]
        mn = ADAM_B1 * m_ref[...] + (1.0 - ADAM_B1) * gv
        vn = ADAM_B2 * v_ref[...] + (1.0 - ADAM_B2) * (gv * gv)
        m_hat = mn / (1.0 - ADAM_B1 ** ADAM_STEP)
        v_hat = vn / (1.0 - ADAM_B2 ** ADAM_STEP)
        d_ref[...] = -ADAM_LR * (m_hat / (jnp.sqrt(v_hat) + ADAM_EPS) + ADAM_WD * w_ref[...])
        mo_ref[...] = mn
        vo_ref[...] = vn

    row = lambda i: (i, 0)
    spec = pl.BlockSpec((FLAT_TR, wd), row)
    return pl.pallas_call(
        body, name=name, grid=(r // FLAT_TR,), in_specs=[spec] * 4, out_specs=[spec] * 3,
        out_shape=[jax.ShapeDtypeStruct((r, wd), F32)] * 3,
        compiler_params=_cparams(1),
    )(g, w, m, v)


def _reduce_to_shards(s):
    c_arr = lax.axis_index("c").astype(jnp.int32).reshape(1)
    a = _sibling_take_half(s, "grad_sibling_half")
    p = _add_own_half(s, a, c_arr, "grad_pair_sum")
    b = _chip_scatter(p, "grad_chip_scatter")
    half = _sum_slots(b, "grad_chip_sum")
    return _sibling_join_halves(half, "grad_sibling_join")


WEIGHTS = ("mem_norm", "w_mem_kv", "norm_a", "w_in_a", "conv_w", "conv_b", "w_rec_gate", "b_rec_gate", "w_in_gate",
           "b_in_gate", "lru_lambda", "w_out_a", "kv_norm", "w_kv", "norm_b", "w_in_b", "w_out_b", "final_norm")
SHARD_DIM = {"mem_norm": None, "w_mem_kv": 1, "norm_a": 1, "w_in_a": 2, "conv_w": 2, "conv_b": 1, "w_rec_gate": None,
             "b_rec_gate": 1, "w_in_gate": None, "b_in_gate": 1, "lru_lambda": 1, "w_out_a": 1, "kv_norm": None,
             "w_kv": 1, "norm_b": None, "w_in_b": 2, "w_out_b": 1, "final_norm": None}
BIG = ("w_mem_kv", "w_in_a", "w_out_a", "w_kv", "w_in_b", "w_out_b")
SMALL = ("norm_a", "conv_w", "conv_b", "b_rec_gate", "b_in_gate", "lru_lambda")


def _pad_rows(flat, row_multiple):
    per = FLAT_W * row_multiple
    n = flat.shape[0]
    total = -(-n // per) * per
    return jnp.pad(flat, (0, total - n)).reshape(total // FLAT_W, FLAT_W)


def _flatten(parts, row_multiple):
    return _pad_rows(jnp.concatenate([p.reshape(-1) for p in parts]), row_multiple)


def _unflatten(flat2d, shapes):
    flat = flat2d.reshape(-1)
    out, off = [], 0
    for shp in shapes:
        n = 1
        for s in shp:
            n *= s
        out.append(flat[off:off + n].reshape(shp))
        off += n
    return out


def _gather_weights(local):
    parts = [local[n].astype(BF16) for n in BIG]
    parts += [lax.bitcast_convert_type(local[n], BF16) for n in SMALL]
    shapes = [p.shape for p in parts]
    gathered = _chip_all_gather(_flatten(parts, 32), "weights_all_gather")
    per_chip = [_unflatten(gathered[k], shapes) for k in range(N_CHIPS)]
    full = {}
    for idx, n in enumerate(BIG + SMALL):
        pieces = [per_chip[k][idx] for k in range(N_CHIPS)]
        if n in SMALL:
            pieces = [lax.bitcast_convert_type(p, F32) for p in pieces]
        full[n] = jnp.concatenate(pieces, axis=SHARD_DIM[n])
    return full


def _piece(g, name, k):
    dim = SHARD_DIM[name]
    if dim is None:
        return g
    n = g.shape[dim] // N_CHIPS
    return lax.slice_in_dim(g, k * n, (k + 1) * n, axis=dim)


def _local_grads(x, mem, tgt, wts):
    t, d = x.shape
    depth = wts["w_mem_kv"].shape[0]
    n_a = wts["w_in_a"].shape[0]
    n_b = wts["w_in_b"].shape[0]
    nb = wts["w_rec_gate"].shape[1]
    dr = nb * LANES
    dm = wts["w_mem_kv"].shape[2] // 2
    row = lambda v: v.reshape(1, -1)

    wm_all = jnp.concatenate([wts["w_mem_kv"][l] for l in range(depth)], axis=1)
    memkv, memn_bf = _norm_matmul(mem, row(wts["mem_norm"]), wm_all, BF16, "mem_kv_proj")

    h = x
    saved = []
    vecs = []
    for l in range(n_a):
        proj, u_bf = _norm_matmul(h, row(wts["norm_a"][l]), wts["w_in_a"][l], F32, f"a{l}_in_proj")
        vec = jnp.concatenate([row(wts["conv_b"][l]), row(wts["b_rec_gate"][l]), row(wts["b_in_gate"][l]),
                               row(wts["lru_lambda"][l]), wts["conv_w"][l]], axis=0)
        vecs.append(vec)
        y_rnn, h_rnn, xc = _lru_fwd(proj, vec, wts["w_rec_gate"][l], wts["w_in_gate"][l], f"a{l}_lru_fwd")
        y_mem = _mem_attn_fwd(proj, memkv, l, dr, dm, f"a{l}_mem_fwd")
        mix = jnp.concatenate([y_rnn, y_mem], axis=1)
        h_next = _matmul_res(mix, wts["w_out_a"][l], h, f"a{l}_out_proj")
        saved.append((h, proj, u_bf, mix, h_rnn, xc))
        h = h_next

    h_kv = h
    kv, ukv_bf = _norm_matmul(h_kv, row(wts["kv_norm"]), wts["w_kv"], BF16, "kv_proj")

    for j in range(n_b):
        l = n_a + j
        proj, u_bf = _norm_matmul(h, row(wts["norm_b"][j]), wts["w_in_b"][j], F32, f"b{j}_in_proj")
        y_sb, o_sb, tl_sb = _sb_fwd(proj, kv, f"b{j}_sb_fwd")
        y_mem = _mem_attn_fwd(proj, memkv, l, dr, dm, f"b{j}_mem_fwd")
        mix = jnp.concatenate([y_sb, y_mem], axis=1)
        h_next = _matmul_res(mix, wts["w_out_b"][j], h, f"b{j}_out_proj")
        saved.append((h, proj, u_bf, mix, o_sb, tl_sb))
        h = h_next

    dh, d_final, loss = _final_loss_bwd(h, row(wts["final_norm"]), tgt, "final_loss_bwd")

    grads = {"final_norm": d_final.reshape(-1)}
    dmemkv = [None] * depth
    g_in_b, g_out_b, g_norm_b = [None] * n_b, [None] * n_b, [None] * n_b
    dks, dvs = [], []
    for j in reversed(range(n_b)):
        l = n_a + j
        h_in, proj, u_bf, mix, o_sb, tl_sb = saved[l]
        dmix = _matmul_nt(dh, wts["w_out_b"][j], f"b{j}_dmix")
        g_out_b[j] = _matmul_tn(mix, dh, f"b{j}_dw_out")
        dq, dg, dk, dv = _sb_bwd(proj, kv, o_sb, tl_sb, dmix, f"b{j}_sb_bwd")
        dqm, dgm, dkm, dvm = _mem_attn_bwd(proj, memkv, dmix, l, dr, dm, f"b{j}_mem_bwd")
        dmemkv[l] = (dkm, dvm)
        dproj = jnp.concatenate([dq, dg, dqm, dgm], axis=1)
        du = _matmul_nt(dproj, wts["w_in_b"][j], f"b{j}_du")
        g_in_b[j] = _matmul_tn(u_bf, dproj, f"b{j}_dw_in")
        dh, dgn = _rms_bwd(du, h_in, row(wts["norm_b"][j]), dh, f"b{j}_rms_bwd")
        g_norm_b[j] = dgn.reshape(-1)
        dks.append(dk)
        dvs.append(dv)
    assert n_b == 2
    dkv = jnp.concatenate([_add_cast(dks[0], dks[1], "dk_sum"), _add_cast(dvs[0], dvs[1], "dv_sum")], axis=1)
    du = _matmul_nt(dkv, wts["w_kv"], "kv_du")
    grads["w_kv"] = _matmul_tn(ukv_bf, dkv, "kv_dw")
    dh, dgn = _rms_bwd(du, h_kv, row(wts["kv_norm"]), dh, "kv_rms_bwd")
    grads["kv_norm"] = dgn.reshape(-1)

    g_in_a, g_out_a, g_norm_a = [None] * n_a, [None] * n_a, [None] * n_a
    g_wr, g_wi, g_vec = [None] * n_a, [None] * n_a, [None] * n_a
    for l in reversed(range(n_a)):
        h_in, proj, u_bf, mix, h_rnn, xc = saved[l]
        dmix = _matmul_nt(dh, wts["w_out_a"][l], f"a{l}_dmix")
        g_out_a[l] = _matmul_tn(mix, dh, f"a{l}_dw_out")
        dx, dg, g_wr[l], g_wi[l], dvec = _lru_bwd(proj, xc, h_rnn, dmix, vecs[l], wts["w_rec_gate"][l],
                                                  wts["w_in_gate"][l], f"a{l}_lru_bwd")
        g_vec[l] = dvec.transpose(1, 0, 2).reshape(8, dr)
        dqm, dgm, dkm, dvm = _mem_attn_bwd(proj, memkv, dmix, l, dr, dm, f"a{l}_mem_bwd")
        dmemkv[l] = (dkm, dvm)
        dproj = jnp.concatenate([dx, dg, dqm, dgm], axis=1)
        du = _matmul_nt(dproj, wts["w_in_a"][l], f"a{l}_du")
        g_in_a[l] = _matmul_tn(u_bf, dproj, f"a{l}_dw_in")
        dh, dgn = _rms_bwd(du, h_in, row(wts["norm_a"][l]), dh, f"a{l}_rms_bwd")
        g_norm_a[l] = dgn.reshape(-1)

    dmemkv_all = jnp.concatenate([jnp.concatenate(p, axis=1) for p in dmemkv], axis=1).astype(BF16)
    g_wm = _matmul_tn(memn_bf, dmemkv_all, "mem_dw")
    dmemn = _matmul_nt(dmemkv_all, wm_all, "mem_du")
    _, dgn = _rms_bwd(dmemn, mem, row(wts["mem_norm"]), jnp.zeros_like(mem), "mem_rms_bwd")
    grads["mem_norm"] = dgn.reshape(-1)
    grads["w_mem_kv"] = jnp.stack(jnp.split(g_wm, depth, axis=1))
    grads["norm_a"] = jnp.stack(g_norm_a)
    grads["w_in_a"] = jnp.stack(g_in_a)
    grads["w_out_a"] = jnp.stack(g_out_a)
    grads["w_rec_gate"] = jnp.stack(g_wr)
    grads["w_in_gate"] = jnp.stack(g_wi)
    gv = jnp.stack(g_vec)
    grads["conv_b"], grads["b_rec_gate"], grads["b_in_gate"], grads["lru_lambda"] = gv[:, 0], gv[:, 1], gv[:, 2], gv[:, 3]
    grads["conv_w"] = gv[:, 4:8]
    grads["norm_b"] = jnp.stack(g_norm_b)
    grads["w_in_b"] = jnp.stack(g_in_b)
    grads["w_out_b"] = jnp.stack(g_out_b)
    return loss, dh, grads


def kernel(x, mem, mem_norm, w_mem_kv, norm_a, w_in_a, conv_w, conv_b, w_rec_gate, b_rec_gate, w_in_gate, b_in_gate, lru_lambda, w_out_a, kv_norm, w_kv, norm_b, w_in_b, w_out_b, final_norm, loss_target, m_mem_norm, m_w_mem_kv, m_norm_a, m_w_in_a, m_conv_w, m_conv_b, m_w_rec_gate, m_b_rec_gate, m_w_in_gate, m_b_in_gate, m_lru_lambda, m_w_out_a, m_kv_norm, m_w_kv, m_norm_b, m_w_in_b, m_w_out_b, m_final_norm, v_mem_norm, v_w_mem_kv, v_norm_a, v_w_in_a, v_conv_w, v_conv_b, v_w_rec_gate, v_b_rec_gate, v_w_in_gate, v_b_in_gate, v_lru_lambda, v_w_out_a, v_kv_norm, v_w_kv, v_norm_b, v_w_in_b, v_w_out_b, v_final_norm):
    local = dict(mem_norm=mem_norm, w_mem_kv=w_mem_kv, norm_a=norm_a, w_in_a=w_in_a, conv_w=conv_w, conv_b=conv_b,
                 w_rec_gate=w_rec_gate, b_rec_gate=b_rec_gate, w_in_gate=w_in_gate, b_in_gate=b_in_gate,
                 lru_lambda=lru_lambda, w_out_a=w_out_a, kv_norm=kv_norm, w_kv=w_kv, norm_b=norm_b, w_in_b=w_in_b,
                 w_out_b=w_out_b, final_norm=final_norm)
    mom = dict(mem_norm=m_mem_norm, w_mem_kv=m_w_mem_kv, norm_a=m_norm_a, w_in_a=m_w_in_a, conv_w=m_conv_w,
               conv_b=m_conv_b, w_rec_gate=m_w_rec_gate, b_rec_gate=m_b_rec_gate, w_in_gate=m_w_in_gate,
               b_in_gate=m_b_in_gate, lru_lambda=m_lru_lambda, w_out_a=m_w_out_a, kv_norm=m_kv_norm, w_kv=m_w_kv,
               norm_b=m_norm_b, w_in_b=m_w_in_b, w_out_b=m_w_out_b, final_norm=m_final_norm)
    var = dict(mem_norm=v_mem_norm, w_mem_kv=v_w_mem_kv, norm_a=v_norm_a, w_in_a=v_w_in_a, conv_w=v_conv_w,
               conv_b=v_conv_b, w_rec_gate=v_w_rec_gate, b_rec_gate=v_b_rec_gate, w_in_gate=v_w_in_gate,
               b_in_gate=v_b_in_gate, lru_lambda=v_lru_lambda, w_out_a=v_w_out_a, kv_norm=v_kv_norm, w_kv=v_w_kv,
               norm_b=v_norm_b, w_in_b=v_w_in_b, w_out_b=v_w_out_b, final_norm=v_final_norm)

    wts = _gather_weights(local)
    for n in WEIGHTS:
        if SHARD_DIM[n] is None:
            wts[n] = local[n]
    wts["w_rec_gate"] = wts["w_rec_gate"].astype(BF16)
    wts["w_in_gate"] = wts["w_in_gate"].astype(BF16)

    loss, grad_x, grads = _local_grads(x[0], mem[0], loss_target[0], wts)

    row_multiple = 2 * FLAT_TR
    s = jnp.stack([_flatten([_piece(grads[n], n, k) for n in WEIGHTS], row_multiple) for k in range(N_CHIPS)])
    g_flat = _reduce_to_shards(s)
    w_flat = _flatten([local[n] for n in WEIGHTS], row_multiple)
    m_flat = _flatten([mom[n] for n in WEIGHTS], row_multiple)
    v_flat = _flatten([var[n] for n in WEIGHTS], row_multiple)
    d_flat, mo_flat, vo_flat = _adamw(g_flat, w_flat, m_flat, v_flat, "adamw")

    shapes = [local[n].shape for n in WEIGHTS]
    total_loss = lax.psum(loss[0, 0], MESH_AXES)
    return (total_loss, grad_x[None], *_unflatten(g_flat, shapes), *_unflatten(d_flat, shapes),
            *_unflatten(mo_flat, shapes), *_unflatten(vo_flat, shapes))
```

```python
import functools

import jax
import jax.numpy as jnp
from jax import lax
from jax.experimental import pallas as pl
from jax.experimental.pallas import tpu as pltpu

F32 = jnp.float32
BF16 = jnp.bfloat16

RMS_EPS = 1e-6
LRU_C = 8.0
ADAM_LR = 0.001
ADAM_B1 = 0.9
ADAM_B2 = 0.999
ADAM_EPS = 1e-08
ADAM_WD = 0.01
ADAM_STEP = 10

LANES = 128
VMEM_LIMIT = 56 * 1024 * 1024
FLAT_W = 1024
FLAT_TR = 256
N_CHIPS = 4
MESH_AXES = ("x", "y", "c")

_NT = (((1,), (1,)), ((), ()))
_TN = (((0,), (0,)), ((), ()))
ANY = pl.BlockSpec(memory_space=pl.ANY)
MESH = pl.DeviceIdType.MESH


def _cparams(n_axes):
    return pltpu.CompilerParams(dimension_semantics=("arbitrary",) * n_axes, vmem_limit_bytes=VMEM_LIMIT)


def _sigmoid(x):
    return 1.0 / (1.0 + jnp.exp(-x))


def _log1p_pos(e):
    return jnp.where(e < 1e-3, e * (1.0 - e * (0.5 - e * (1.0 / 3.0))), jnp.log(1.0 + e))


def _neg_expm1(x):
    small = -x * (1.0 + x * (0.5 + x * (1.0 / 6.0 + x * (1.0 / 24.0))))
    return jnp.where(x > -0.05, small, 1.0 - jnp.exp(x))


def _tile(n, want):
    if n <= want:
        return n
    t = want
    while n % t:
        t -= LANES
    assert t > 0, (n, want)
    return t


def _norm_matmul(x, g, w, out_dtype, name):
    m, k = x.shape
    n = w.shape[1]
    tm, tn = _tile(m, 512), _tile(n, 1024)

    def body(x_ref, g_ref, w_ref, o_ref, u_ref):
        @pl.when(pl.program_id(1) == 0)
        def _():
            xf = x_ref[...]
            r = lax.rsqrt(jnp.mean(xf * xf, axis=-1, keepdims=True) + RMS_EPS)
            u_ref[...] = ((xf * r) * g_ref[...]).astype(BF16)

        o_ref[...] = jnp.dot(u_ref[...], w_ref[...], preferred_element_type=F32).astype(o_ref.dtype)

    return pl.pallas_call(
        body, name=name, grid=(m // tm, n // tn),
        in_specs=[pl.BlockSpec((tm, k), lambda i, j: (i, 0)), pl.BlockSpec((1, k), lambda i, j: (0, 0)),
                  pl.BlockSpec((k, tn), lambda i, j: (0, j))],
        out_specs=[pl.BlockSpec((tm, tn), lambda i, j: (i, j)), pl.BlockSpec((tm, k), lambda i, j: (i, 0))],
        out_shape=[jax.ShapeDtypeStruct((m, n), out_dtype), jax.ShapeDtypeStruct((m, k), BF16)],
        compiler_params=_cparams(2),
    )(x, g, w)


def _matmul_res(a, b, res, name):
    m, k = a.shape
    n = b.shape[1]
    tm, tn = _tile(m, 512), _tile(n, 1024)

    def body(a_ref, b_ref, r_ref, o_ref):
        o_ref[...] = r_ref[...] + jnp.dot(a_ref[...], b_ref[...], preferred_element_type=F32)

    return pl.pallas_call(
        body, name=name, grid=(m // tm, n // tn),
        in_specs=[pl.BlockSpec((tm, k), lambda i, j: (i, 0)), pl.BlockSpec((k, tn), lambda i, j: (0, j)),
                  pl.BlockSpec((tm, tn), lambda i, j: (i, j))],
        out_specs=pl.BlockSpec((tm, tn), lambda i, j: (i, j)),
        out_shape=jax.ShapeDtypeStruct((m, n), F32),
        compiler_params=_cparams(2),
    )(a, b, res)


def _matmul_nt(a, b, name):
    m, n = a.shape
    k = b.shape[0]
    tm, tk = _tile(m, 512), _tile(k, 512)

    def body(a_ref, b_ref, o_ref):
        o_ref[...] = lax.dot_general(a_ref[...].astype(BF16), b_ref[...], _NT, preferred_element_type=F32)

    return pl.pallas_call(
        body, name=name, grid=(m // tm, k // tk),
        in_specs=[pl.BlockSpec((tm, n), lambda i, j: (i, 0)), pl.BlockSpec((tk, n), lambda i, j: (j, 0))],
        out_specs=pl.BlockSpec((tm, tk), lambda i, j: (i, j)),
        out_shape=jax.ShapeDtypeStruct((m, k), F32),
        compiler_params=_cparams(2),
    )(a, b)


def _matmul_tn(a, b, name, tk, tn, out_shape, out_block, out_index, into=None):
    m, k = a.shape
    n = b.shape[1]
    tm = _tile(m, 512)

    def body(a_ref, b_ref, *rest):
        o_ref = rest[-1]
        part = lax.dot_general(a_ref[...].astype(BF16), b_ref[...].astype(BF16), _TN, preferred_element_type=F32)

        @pl.when(pl.program_id(2) == 0)
        def _():
            o_ref[...] = part

        @pl.when(pl.program_id(2) != 0)
        def _():
            o_ref[...] += part

    in_specs = [pl.BlockSpec((tm, tk), lambda i, j, s: (s, i)), pl.BlockSpec((tm, tn), lambda i, j, s: (s, j))]
    args = [a, b]
    if into is not None:
        in_specs.append(ANY)
        args.append(into)
    return pl.pallas_call(
        body, name=name, grid=(k // tk, n // tn, m // tm), in_specs=in_specs,
        out_specs=pl.BlockSpec(out_block, lambda i, j, s: out_index(i, j)),
        out_shape=jax.ShapeDtypeStruct(out_shape, F32),
        input_output_aliases={} if into is None else {2: 0},
        compiler_params=_cparams(3),
    )(*args)


def _dw_cols(a, b, layer, n_layers, into, name):
    k, n = a.shape[1], b.shape[1]
    pn = n // N_CHIPS
    tk = _tile(k, 512)
    return _matmul_tn(a, b, name, tk, pn, (N_CHIPS, n_layers, k, pn), (None, None, tk, pn),
                      lambda i, j: (j, layer, i, 0), into)


def _dw_rows(a, b, layer, n_layers, into, name):
    k, n = a.shape[1], b.shape[1]
    pk = k // N_CHIPS
    tn = _tile(n, 2048)
    return _matmul_tn(a, b, name, pk, tn, (N_CHIPS, n_layers, pk, n), (None, None, pk, tn),
                      lambda i, j: (i, layer, 0, j), into)


def _rms_bwd(du, h, g, dres, name):
    m, d = h.shape
    tm = _tile(m, 256)

    def body(du_ref, h_ref, g_ref, dres_ref, dx_ref, dg_ref):
        xf = h_ref[...]
        r = lax.rsqrt(jnp.mean(xf * xf, axis=-1, keepdims=True) + RMS_EPS)
        xhat = xf * r
        du_v = du_ref[...]
        dxn = du_v * g_ref[...]
        dx_ref[...] = dres_ref[...] + r * (dxn - xhat * jnp.mean(dxn * xhat, axis=-1, keepdims=True))
        part = jnp.sum(du_v * xhat, axis=0, keepdims=True)

        @pl.when(pl.program_id(0) == 0)
        def _():
            dg_ref[...] = part

        @pl.when(pl.program_id(0) != 0)
        def _():
            dg_ref[...] += part

    row = lambda i: (i, 0)
    return pl.pallas_call(
        body, name=name, grid=(m // tm,),
        in_specs=[pl.BlockSpec((tm, d), row), pl.BlockSpec((tm, d), row), pl.BlockSpec((1, d), lambda i: (0, 0)),
                  pl.BlockSpec((tm, d), row)],
        out_specs=[pl.BlockSpec((tm, d), row), pl.BlockSpec((1, d), lambda i: (0, 0))],
        out_shape=[jax.ShapeDtypeStruct((m, d), F32), jax.ShapeDtypeStruct((1, d), F32)],
        compiler_params=_cparams(1),
    )(du, h, g, dres)


def _final_loss_bwd(h, g, tgt, name):
    m, d = h.shape
    tm = _tile(m, 256)

    def body(h_ref, g_ref, t_ref, dx_ref, dg_ref, loss_ref):
        xf = h_ref[...]
        r = lax.rsqrt(jnp.mean(xf * xf, axis=-1, keepdims=True) + RMS_EPS)
        xhat = xf * r
        gv = g_ref[...]
        err = xhat * gv - t_ref[...]
        dy = err * (1.0 / d)
        dxn = dy * gv
        dx_ref[...] = r * (dxn - xhat * jnp.mean(dxn * xhat, axis=-1, keepdims=True))
        part = jnp.sum(dy * xhat, axis=0, keepdims=True)
        lpart = jnp.sum(jnp.sum(err * err, axis=0, keepdims=True), axis=1, keepdims=True) * (0.5 / d)

        @pl.when(pl.program_id(0) == 0)
        def _():
            dg_ref[...] = part
            loss_ref[...] = lpart

        @pl.when(pl.program_id(0) != 0)
        def _():
            dg_ref[...] += part
            loss_ref[...] += lpart

    row = lambda i: (i, 0)
    fixed = lambda i: (0, 0)
    return pl.pallas_call(
        body, name=name, grid=(m // tm,),
        in_specs=[pl.BlockSpec((tm, d), row), pl.BlockSpec((1, d), fixed), pl.BlockSpec((tm, d), row)],
        out_specs=[pl.BlockSpec((tm, d), row), pl.BlockSpec((1, d), fixed), pl.BlockSpec((1, 1), fixed)],
        out_shape=[jax.ShapeDtypeStruct((m, d), F32), jax.ShapeDtypeStruct((1, d), F32),
                   jax.ShapeDtypeStruct((1, 1), F32)],
        compiler_params=_cparams(1),
    )(h, g, tgt)


def _add_cast(a, b, name):
    m, n = a.shape
    tm = _tile(m, 512)

    def body(a_ref, b_ref, o_ref):
        o_ref[...] = (a_ref[...] + b_ref[...]).astype(BF16)

    row = lambda i: (i, 0)
    return pl.pallas_call(
        body, name=name, grid=(m // tm,),
        in_specs=[pl.BlockSpec((tm, n), row), pl.BlockSpec((tm, n), row)],
        out_specs=pl.BlockSpec((tm, n), row),
        out_shape=jax.ShapeDtypeStruct((m, n), BF16),
        compiler_params=_cparams(1),
    )(a, b)


def _mem_attn_fwd(proj, memkv, layer, dr, dm, name):
    t = proj.shape[0]
    nm = memkv.shape[0]
    tm = _tile(t, 512)
    nh = dm // LANES
    scale = LANES ** -0.5
    qb = (2 * dr) // dm

    def body(q_ref, g_ref, k_ref, v_ref, y_ref):
        for hh in range(nh):
            sl = slice(hh * LANES, (hh + 1) * LANES)
            s = lax.dot_general(q_ref[:, sl].astype(BF16), k_ref[:, sl], _NT, preferred_element_type=F32) * scale
            p = jnp.exp(s - jnp.max(s, axis=-1, keepdims=True))
            p = p / jnp.sum(p, axis=-1, keepdims=True)
            o = jnp.dot(p.astype(BF16), v_ref[:, sl], preferred_element_type=F32)
            gv = g_ref[:, sl]
            y_ref[:, sl] = (o * (gv * _sigmoid(gv))).astype(BF16)

    return pl.pallas_call(
        body, name=name, grid=(t // tm,),
        in_specs=[pl.BlockSpec((tm, dm), lambda i: (i, qb)), pl.BlockSpec((tm, dm), lambda i: (i, qb + 1)),
                  pl.BlockSpec((nm, dm), lambda i: (0, 2 * layer)), pl.BlockSpec((nm, dm), lambda i: (0, 2 * layer + 1))],
        out_specs=pl.BlockSpec((tm, dm), lambda i: (i, 0)),
        out_shape=jax.ShapeDtypeStruct((t, dm), BF16),
        compiler_params=_cparams(1),
    )(proj, proj, memkv, memkv)


def _mem_attn_bwd(proj, memkv, dmix, layer, dr, dm, name):
    t = proj.shape[0]
    nm = memkv.shape[0]
    tm = _tile(t, 512)
    nh = dm // LANES
    scale = LANES ** -0.5
    qb = (2 * dr) // dm
    yb = dr // dm

    def body(q_ref, g_ref, k_ref, v_ref, dy_ref, dq_ref, dg_ref, dk_ref, dv_ref):
        @pl.when(pl.program_id(0) == 0)
        def _():
            dk_ref[...] = jnp.zeros_like(dk_ref)
            dv_ref[...] = jnp.zeros_like(dv_ref)

        for hh in range(nh):
            sl = slice(hh * LANES, (hh + 1) * LANES)
            q = q_ref[:, sl].astype(BF16)
            k = k_ref[:, sl]
            v = v_ref[:, sl]
            s = lax.dot_general(q, k, _NT, preferred_element_type=F32) * scale
            p = jnp.exp(s - jnp.max(s, axis=-1, keepdims=True))
            p = p / jnp.sum(p, axis=-1, keepdims=True)
            p_bf = p.astype(BF16)
            o = jnp.dot(p_bf, v, preferred_element_type=F32)
            gv = g_ref[:, sl]
            sg = _sigmoid(gv)
            dy = dy_ref[:, sl]
            do = dy * (gv * sg)
            dg_ref[:, sl] = (dy * o * (sg * (1.0 + gv * (1.0 - sg)))).astype(BF16)
            do_bf = do.astype(BF16)
            dv_ref[:, sl] += lax.dot_general(p_bf, do_bf, _TN, preferred_element_type=F32)
            dp = lax.dot_general(do_bf, v, _NT, preferred_element_type=F32)
            ds = (p * (dp - jnp.sum(dp * p, axis=-1, keepdims=True)) * scale).astype(BF16)
            dq_ref[:, sl] = jnp.dot(ds, k, preferred_element_type=F32).astype(BF16)
            dk_ref[:, sl] += lax.dot_general(ds, q, _TN, preferred_element_type=F32)

    fixed = lambda i: (0, 0)
    return pl.pallas_call(
        body, name=name, grid=(t // tm,),
        in_specs=[pl.BlockSpec((tm, dm), lambda i: (i, qb)), pl.BlockSpec((tm, dm), lambda i: (i, qb + 1)),
                  pl.BlockSpec((nm, dm), lambda i: (0, 2 * layer)), pl.BlockSpec((nm, dm), lambda i: (0, 2 * layer + 1)),
                  pl.BlockSpec((tm, dm), lambda i: (i, yb))],
        out_specs=[pl.BlockSpec((tm, dm), lambda i: (i, 0)), pl.BlockSpec((tm, dm), lambda i: (i, 0)),
                   pl.BlockSpec((nm, dm), fixed), pl.BlockSpec((nm, dm), fixed)],
        out_shape=[jax.ShapeDtypeStruct((t, dm), BF16), jax.ShapeDtypeStruct((t, dm), BF16),
                   jax.ShapeDtypeStruct((nm, dm), F32), jax.ShapeDtypeStruct((nm, dm), F32)],
        compiler_params=_cparams(1),
    )(proj, proj, memkv, memkv, dmix)


LRU_CHUNK = 256


def _lru_gates(xc, vec, wr_ref, wi_ref):
    r = _sigmoid(jnp.dot(xc.astype(BF16), wr_ref[...], preferred_element_type=F32) + vec[1:2])
    i = _sigmoid(jnp.dot(xc.astype(BF16), wi_ref[...], preferred_element_type=F32) + vec[2:3])
    lam = vec[3:4]
    cl = -LRU_C * (jnp.maximum(-lam, 0.0) + _log1p_pos(jnp.exp(-jnp.abs(lam))))
    la = cl * r
    a = jnp.exp(la)
    s2 = _neg_expm1(2.0 * la)
    return r, i, cl, a, s2


def _lru_fwd(proj, vec, wr, wi, name):
    t = proj.shape[0]
    nb = wr.shape[0]
    dr = nb * LANES
    c = _tile(t, LRU_CHUNK)

    def body(x_ref, g_ref, vec_ref, wr_ref, wi_ref, y_ref, h_ref, xc_ref, carry_ref, xprev_ref):
        @pl.when(pl.program_id(1) == 0)
        def _():
            carry_ref[...] = jnp.zeros_like(carry_ref)
            xprev_ref[...] = jnp.zeros_like(xprev_ref)

        x = x_ref[...]
        vec = vec_ref[...]
        rows = lax.broadcasted_iota(jnp.int32, (c, LANES), 0)
        xprev = xprev_ref[...]
        xc = vec[7:8] * x + vec[0:1]
        for k in range(1, 4):
            xs = jnp.where(rows < k, pltpu.roll(xprev, k, 0), pltpu.roll(x, k, 0))
            xc = xc + vec[7 - k:8 - k] * xs
        xprev_ref[...] = x
        xc_ref[...] = xc

        r, i, cl, a, s2 = _lru_gates(xc, vec, wr_ref, wi_ref)
        hh = jnp.sqrt(s2) * (i * xc)
        aa = a
        d = 1
        while d < c:
            keep = rows >= d
            hh = jnp.where(keep, aa * pltpu.roll(hh, d, 0) + hh, hh)
            aa = jnp.where(keep, aa * pltpu.roll(aa, d, 0), aa)
            d *= 2
        hfull = hh + aa * carry_ref[7:8, :]
        carry_ref[...] = hfull[c - 8:c, :]
        h_ref[...] = hfull
        gv = g_ref[...]
        y_ref[...] = (hfull * (gv * _sigmoid(gv))).astype(BF16)

    blk = lambda n, s: (s, n)
    return pl.pallas_call(
        body, name=name, grid=(nb, t // c),
        in_specs=[pl.BlockSpec((c, LANES), blk), pl.BlockSpec((c, LANES), lambda n, s: (s, nb + n)),
                  pl.BlockSpec((8, LANES), lambda n, s: (0, n)),
                  pl.BlockSpec((None, LANES, LANES), lambda n, s: (n, 0, 0)),
                  pl.BlockSpec((None, LANES, LANES), lambda n, s: (n, 0, 0))],
        out_specs=[pl.BlockSpec((c, LANES), blk)] * 3,
        out_shape=[jax.ShapeDtypeStruct((t, dr), BF16), jax.ShapeDtypeStruct((t, dr), F32),
                   jax.ShapeDtypeStruct((t, dr), F32)],
        scratch_shapes=[pltpu.VMEM((8, LANES), F32), pltpu.VMEM((c, LANES), F32)],
        compiler_params=_cparams(2),
    )(proj, proj, vec, wr, wi)


def _lru_bwd(proj, xc_all, h_all, dmix, vec, wr, wi, name):
    t = proj.shape[0]
    nb = wr.shape[0]
    dr = nb * LANES
    c = _tile(t, LRU_CHUNK)
    nc = t // c

    def body(x_ref, g_ref, xc_ref, h_ref, dy_ref, vec_ref, wr_ref, wi_ref,
             dx_ref, dg_ref, dwr_ref, dwi_ref, dvec_ref, qcarry_ref, dxc_next_ref):
        @pl.when(pl.program_id(1) == 0)
        def _():
            qcarry_ref[...] = jnp.zeros_like(qcarry_ref)
            dxc_next_ref[...] = jnp.zeros_like(dxc_next_ref)
            dwr_ref[...] = jnp.zeros_like(dwr_ref)
            dwi_ref[...] = jnp.zeros_like(dwi_ref)
            dvec_ref[...] = jnp.zeros_like(dvec_ref)

        x = x_ref[...]
        xc = xc_ref[...]
        h = h_ref[...]
        dy = dy_ref[...]
        gv = g_ref[...]
        vec = vec_ref[...]
        rows = lax.broadcasted_iota(jnp.int32, (c, LANES), 0)

        r, i, cl, a, s2 = _lru_gates(xc, vec, wr_ref, wi_ref)
        s = jnp.sqrt(s2)
        ixc = i * xc
        u = s * ixc
        sg = _sigmoid(gv)
        dh = dy * (gv * sg)
        dg_ref[...] = (dy * h * (sg * (1.0 + gv * (1.0 - sg)))).astype(BF16)

        aa = a
        qq = a * dh
        d = 1
        while d < c:
            keep = rows < c - d
            qq = jnp.where(keep, qq + aa * pltpu.roll(qq, c - d, 0), qq)
            aa = jnp.where(keep, aa * pltpu.roll(aa, c - d, 0), aa)
            d *= 2
        qin = qcarry_ref[0:1, :]
        qfull = qq + aa * qin
        gt = dh + jnp.where(rows == c - 1, qin, pltpu.roll(qfull, c - 1, 0))
        qcarry_ref[...] = qfull[0:8, :]

        dla = gt * (h - u) - gt * ixc * (a * a) / s
        dixc = gt * s
        di = dixc * xc
        dxc = dixc * i
        dzr = (dla * cl) * (r * (1.0 - r))
        dzi = di * (i * (1.0 - i))
        dzr_bf = dzr.astype(BF16)
        dzi_bf = dzi.astype(BF16)
        dxc = dxc + lax.dot_general(dzr_bf, wr_ref[...], _NT, preferred_element_type=F32)
        dxc = dxc + lax.dot_general(dzi_bf, wi_ref[...], _NT, preferred_element_type=F32)
        xc_bf = xc.astype(BF16)
        dwr_ref[...] += lax.dot_general(xc_bf, dzr_bf, _TN, preferred_element_type=F32)
        dwi_ref[...] += lax.dot_general(xc_bf, dzi_bf, _TN, preferred_element_type=F32)

        lam = vec[3:4]
        dlam = jnp.sum(dla * r, axis=0, keepdims=True) * (LRU_C * _sigmoid(-lam))
        colsum = lambda v: jnp.sum(v, axis=0, keepdims=True)
        dxn = dxc_next_ref[...]
        dx = vec[7:8] * dxc
        dtaps = [None] * 4
        dtaps[3] = colsum(x * dxc)
        for k in range(1, 4):
            sh = jnp.where(rows < c - k, pltpu.roll(dxc, c - k, 0), pltpu.roll(dxn, c - k, 0))
            dx = dx + vec[7 - k:8 - k] * sh
            dtaps[3 - k] = colsum(x * sh)
        dxc_next_ref[...] = dxc
        dx_ref[...] = dx.astype(BF16)
        dvec_ref[...] += jnp.concatenate([colsum(dxc), colsum(dzr), colsum(dzi), dlam] + dtaps, axis=0)

    rev = lambda n, s: (nc - 1 - s, n)
    sq = lambda n, s: (n, 0, 0)
    return pl.pallas_call(
        body, name=name, grid=(nb, nc),
        in_specs=[pl.BlockSpec((c, LANES), rev), pl.BlockSpec((c, LANES), lambda n, s: (nc - 1 - s, nb + n)),
                  pl.BlockSpec((c, LANES), rev), pl.BlockSpec((c, LANES), rev), pl.BlockSpec((c, LANES), rev),
                  pl.BlockSpec((8, LANES), lambda n, s: (0, n)),
                  pl.BlockSpec((None, LANES, LANES), sq), pl.BlockSpec((None, LANES, LANES), sq)],
        out_specs=[pl.BlockSpec((c, LANES), rev), pl.BlockSpec((c, LANES), rev),
                   pl.BlockSpec((None, LANES, LANES), sq), pl.BlockSpec((None, LANES, LANES), sq),
                   pl.BlockSpec((None, 8, LANES), sq)],
        out_shape=[jax.ShapeDtypeStruct((t, dr), BF16), jax.ShapeDtypeStruct((t, dr), BF16),
                   jax.ShapeDtypeStruct((nb, LANES, LANES), F32), jax.ShapeDtypeStruct((nb, LANES, LANES), F32),
                   jax.ShapeDtypeStruct((nb, 8, LANES), F32)],
        scratch_shapes=[pltpu.VMEM((8, LANES), F32), pltpu.VMEM((c, LANES), F32)],
        compiler_params=_cparams(2),
    )(proj, proj, xc_all, h_all, dmix, vec, wr, wi)


SB_TQ = 1024
SB_TK = 256


def _sb_logits(q, k, scale, diag):
    z = lax.dot_general(q, k, _NT, preferred_element_type=F32) * scale
    lb = jnp.minimum(z, 0.0) - jnp.log(1.0 + jnp.exp(-jnp.abs(z)))
    lk = lb - z
    mask = None
    if diag:
        mask = lax.broadcasted_iota(jnp.int32, z.shape, 1) < lax.broadcasted_iota(jnp.int32, z.shape, 0)
        lk = jnp.where(mask, lk, 0.0)
    return z, lb, lk, mask


def _split_dot(v, m):
    hi = v.astype(BF16)
    lo = (v - hi.astype(F32)).astype(BF16)
    return jnp.dot(hi, m, preferred_element_type=F32) + jnp.dot(lo, m, preferred_element_type=F32)


def _tri_ones(kind, tk):
    jj = lax.broadcasted_iota(jnp.int32, (tk, tk), 0)
    ss = lax.broadcasted_iota(jnp.int32, (tk, tk), 1)
    rel = {"ge": jj >= ss, "le": jj <= ss}[kind]
    return jnp.where(rel, 1.0, 0.0).astype(BF16)


def _sb_fwd(proj, kv, name):
    t = proj.shape[0]
    ds = kv.shape[1] // 2
    nh = ds // LANES
    tq = _tile(t, SB_TQ)
    tk = _tile(tq, SB_TK)
    nd = tq // tk
    scale = LANES ** -0.5

    def body(q_ref, g_ref, k_ref, v_ref, y_ref, o_ref, tl_ref, qbf_ref, acc_ref, run_ref):
        qi = pl.program_id(1)
        qbf_ref[...] = q_ref[...].astype(BF16)
        tri = _tri_ones("ge", tk)
        acc_ref[...] = jnp.zeros_like(acc_ref)
        run_ref[...] = jnp.zeros_like(run_ref)

        def block(k0, r0, diag):
            groups = [(slice(s0, s0 + tk), diag and s0 == r0) for s0 in range(r0, tq, tk)]
            k = k_ref[pl.ds(k0, tk), :]
            v = v_ref[pl.ds(k0, tk), :]
            logits = [_sb_logits(qbf_ref[rows, :], k, scale, dg) for rows, dg in groups]
            cums = [_split_dot(lk, tri) for _, _, lk, _ in logits]
            for (rows, dg), (z, _, _, mask), cum in zip(groups, logits, cums):
                run = run_ref[rows, :]
                w = jnp.exp(z + cum + run)
                if dg:
                    w = jnp.where(mask, w, 0.0)
                acc_ref[rows, :] += jnp.dot(w.astype(BF16), v, preferred_element_type=F32)
                run_ref[rows, :] = run + cum[:, 0:1]

        for u in reversed(range(nd)):
            block(pl.multiple_of(qi * tq + u * tk, tk), u * tk, True)

        def step(j, carry):
            block(pl.multiple_of((qi * nd - 1 - j) * tk, tk), 0, False)
            return carry

        lax.fori_loop(0, qi * nd, step, 0)
        o = acc_ref[...]
        o_ref[...] = o
        tl_ref[...] = jnp.broadcast_to(run_ref[...], (tq, LANES))
        gv = g_ref[...]
        y_ref[...] = (o * (gv * _sigmoid(gv))).astype(BF16)

    blk = lambda h, i: (i, h)
    return pl.pallas_call(
        body, name=name, grid=(nh, t // tq),
        in_specs=[pl.BlockSpec((tq, LANES), blk), pl.BlockSpec((tq, LANES), lambda h, i: (i, nh + h)),
                  pl.BlockSpec((t, LANES), lambda h, i: (0, h)), pl.BlockSpec((t, LANES), lambda h, i: (0, nh + h))],
        out_specs=[pl.BlockSpec((tq, LANES), blk)] * 3,
        out_shape=[jax.ShapeDtypeStruct((t, ds), BF16), jax.ShapeDtypeStruct((t, ds), F32),
                   jax.ShapeDtypeStruct((t, ds), F32)],
        scratch_shapes=[pltpu.VMEM((tq, LANES), BF16), pltpu.VMEM((tq, LANES), F32), pltpu.VMEM((tq, 1), F32)],
        compiler_params=_cparams(2),
    )(proj, proj, kv, kv)


def _sb_bwd(proj, kv, o_all, tl_all, dmix, name):
    t = proj.shape[0]
    ds = kv.shape[1] // 2
    nh = ds // LANES
    tq = _tile(t, SB_TQ)
    tk = _tile(tq, SB_TK)
    nd = tq // tk
    scale = LANES ** -0.5

    def body(q_ref, g_ref, k_ref, v_ref, o_ref, tl_ref, dy_ref, dq_ref, dg_ref, dk_ref, dv_ref,
             qbf_ref, dobf_ref, acc_ref, left_ref, rune_ref):
        qi = pl.program_id(1)

        @pl.when(qi == 0)
        def _():
            dk_ref[...] = jnp.zeros_like(dk_ref)
            dv_ref[...] = jnp.zeros_like(dv_ref)

        qbf_ref[...] = q_ref[...].astype(BF16)
        gv = g_ref[...]
        sg = _sigmoid(gv)
        dy = dy_ref[...]
        dobf_ref[...] = (dy * (gv * sg)).astype(BF16)
        dg_ref[...] = (dy * o_ref[...] * (sg * (1.0 + gv * (1.0 - sg)))).astype(BF16)
        tri = _tri_ones("le", tk)
        acc_ref[...] = jnp.zeros_like(acc_ref)
        left_ref[...] = tl_ref[:, 0:1]
        rune_ref[...] = jnp.zeros_like(rune_ref)

        def block(k0, r0, diag):
            groups = [(slice(s0, s0 + tk), diag and s0 == r0) for s0 in range(r0, tq, tk)]
            k = k_ref[pl.ds(k0, tk), :]
            v = v_ref[pl.ds(k0, tk), :]
            logits = [_sb_logits(qbf_ref[rows, :], k, scale, dg) for rows, dg in groups]
            dws = [lax.dot_general(dobf_ref[rows, :], v, _NT, preferred_element_type=F32) for rows, _ in groups]
            cums = [_split_dot(lk, tri) for _, _, lk, _ in logits]
            ws, es = [], []
            for (rows, dg), (_, lb, _, mask), cum, dw in zip(groups, logits, cums, dws):
                left = left_ref[rows, :]
                w = jnp.exp(lb + (left - cum))
                if dg:
                    w = jnp.where(mask, w, 0.0)
                left_ref[rows, :] = left - cum[:, tk - 1:tk]
                ws.append(w.astype(BF16))
                es.append(dw * w)
            cumes = [_split_dot(e, tri) for e in es]
            dzs = []
            for (rows, dg), (_, lb, _, mask), e, cume in zip(groups, logits, es, cumes):
                rune = rune_ref[rows, :]
                dz = (e - jnp.exp(lb) * (rune + cume)) * scale
                if dg:
                    dz = jnp.where(mask, dz, 0.0)
                rune_ref[rows, :] = rune + cume[:, tk - 1:tk]
                dz = dz.astype(BF16)
                acc_ref[rows, :] += jnp.dot(dz, k, preferred_element_type=F32)
                dzs.append(dz)
            rows_all = slice(r0, tq)
            cat = lambda parts: parts[0] if len(parts) == 1 else jnp.concatenate(parts, axis=0)
            dk_ref[pl.ds(k0, tk), :] += lax.dot_general(cat(dzs), qbf_ref[rows_all, :], _TN, preferred_element_type=F32)
            dv_ref[pl.ds(k0, tk), :] += lax.dot_general(cat(ws), dobf_ref[rows_all, :], _TN, preferred_element_type=F32)

        def step(kb, carry):
            block(pl.multiple_of(kb * tk, tk), 0, False)
            return carry

        lax.fori_loop(0, qi * nd, step, 0)
        for u in range(nd):
            block(pl.multiple_of(qi * tq + u * tk, tk), u * tk, True)
        dq_ref[...] = acc_ref[...].astype(BF16)

    blk = lambda h, i: (i, h)
    whole = lambda h, i: (0, h)
    return pl.pallas_call(
        body, name=name, grid=(nh, t // tq),
        in_specs=[pl.BlockSpec((tq, LANES), blk), pl.BlockSpec((tq, LANES), lambda h, i: (i, nh + h)),
                  pl.BlockSpec((t, LANES), whole), pl.BlockSpec((t, LANES), lambda h, i: (0, nh + h)),
                  pl.BlockSpec((tq, LANES), blk), pl.BlockSpec((tq, LANES), blk), pl.BlockSpec((tq, LANES), blk)],
        out_specs=[pl.BlockSpec((tq, LANES), blk), pl.BlockSpec((tq, LANES), blk),
                   pl.BlockSpec((t, LANES), whole), pl.BlockSpec((t, LANES), whole)],
        out_shape=[jax.ShapeDtypeStruct((t, ds), BF16), jax.ShapeDtypeStruct((t, ds), BF16),
                   jax.ShapeDtypeStruct((t, ds), F32), jax.ShapeDtypeStruct((t, ds), F32)],
        scratch_shapes=[pltpu.VMEM((tq, LANES), BF16), pltpu.VMEM((tq, LANES), BF16), pltpu.VMEM((tq, LANES), F32),
                        pltpu.VMEM((tq, 1), F32), pltpu.VMEM((tq, 1), F32)],
        compiler_params=_cparams(2),
    )(proj, proj, kv, kv, o_all, tl_all, dmix)


def _place():
    x, y, c = lax.axis_index("x"), lax.axis_index("y"), lax.axis_index("c")
    chips = [(1 - x, y), (x, 1 - y), (1 - x, 1 - y)]
    return x, y, c, chips


def _remote(src, dst, send_sems, recv_sems, k, to):
    return pltpu.make_async_remote_copy(src_ref=src, dst_ref=dst, send_sem=send_sems.at[k], recv_sem=recv_sems.at[k],
                                        device_id=to, device_id_type=MESH)


def _place_own(shard, idx, name):
    r, w = shard.shape
    tr = _tile(r, FLAT_TR)

    def body(idx_ref, x_ref, o_ref):
        o_ref[...] = x_ref[...]

    return pl.pallas_call(
        body, name=name, out_shape=jax.ShapeDtypeStruct((N_CHIPS, r, w), shard.dtype),
        grid_spec=pltpu.PrefetchScalarGridSpec(
            num_scalar_prefetch=1, grid=(r // tr,),
            in_specs=[pl.BlockSpec((tr, w), lambda i, idx_ref: (i, 0))],
            out_specs=pl.BlockSpec((None, tr, w), lambda i, idx_ref: (idx_ref[1], i, 0))),
        compiler_params=_cparams(1),
    )(idx, shard)


def _chip_all_gather(shard, idx, name):
    r, w = shard.shape
    rh = r // 2

    def body(x_ref, buf_ref, out_ref, send_sems, recv_sems):
        x, y, c, chips = _place()
        me = 2 * x + y
        sibling = (x, y, 1 - c)
        mine_rows = pl.ds(c * rh, rh)
        other_rows = pl.ds((1 - c) * rh, rh)
        first = [_remote(x_ref.at[mine_rows], out_ref.at[me, mine_rows], send_sems, recv_sems, k, (cx, cy, c))
                 for k, (cx, cy) in enumerate(chips)]
        for cp in first:
            cp.start()
        passed = []
        for k, (cx, cy) in enumerate(chips):
            got = out_ref.at[2 * cx + cy, mine_rows]
            _remote(got, got, send_sems, recv_sems, k, (cx, cy, c)).wait_recv()
            fwd = _remote(got, got, send_sems, recv_sems, 3 + k, sibling)
            fwd.start()
            passed.append(fwd)
        for k, (cx, cy) in enumerate(chips):
            got = out_ref.at[2 * cx + cy, other_rows]
            _remote(got, got, send_sems, recv_sems, 3 + k, sibling).wait_recv()
        for cp in first + passed:
            cp.wait_send()

    return pl.pallas_call(
        body, name=name, in_specs=[ANY, ANY], out_specs=ANY,
        out_shape=jax.ShapeDtypeStruct((N_CHIPS, r, w), shard.dtype),
        input_output_aliases={1: 0},
        scratch_shapes=[pltpu.SemaphoreType.DMA((6,)), pltpu.SemaphoreType.DMA((6,))],
    )(shard, _place_own(shard, idx, name + "_own"))


def _sibling_take_half(ss, name):
    n = len(ss)

    def body(*refs):
        s_refs, a_refs, send_sems, recv_sems = refs[:n], refs[n:2 * n], refs[2 * n], refs[2 * n + 1]
        x, y, c, _ = _place()
        cps = []
        for t in range(n):
            rh = s_refs[t].shape[1] // 2
            cps.append(_remote(s_refs[t].at[:, pl.ds((1 - c) * rh, rh), :], a_refs[t], send_sems, recv_sems, t,
                               (x, y, 1 - c)))
        for cp in cps:
            cp.start()
        for cp in cps:
            cp.wait()

    return pl.pallas_call(
        body, name=name, in_specs=[ANY] * n, out_specs=[ANY] * n,
        out_shape=[jax.ShapeDtypeStruct((s.shape[0], s.shape[1] // 2, s.shape[2]), s.dtype) for s in ss],
        scratch_shapes=[pltpu.SemaphoreType.DMA((n,)), pltpu.SemaphoreType.DMA((n,))],
    )(*ss)


def _pair_sum(s, a, idx, dtype, name):
    n, r, w = s.shape
    rh = r // 2
    tr = _tile(rh, FLAT_TR)
    nblk = rh // tr

    def body(idx_ref, s_ref, a_ref, o_ref):
        o_ref[...] = (s_ref[...] + a_ref[...]).astype(dtype)

    return pl.pallas_call(
        body, name=name, out_shape=jax.ShapeDtypeStruct((n, rh, w), dtype),
        grid_spec=pltpu.PrefetchScalarGridSpec(
            num_scalar_prefetch=1, grid=(n, nblk),
            in_specs=[pl.BlockSpec((None, tr, w), lambda k, i, idx_ref: (k, idx_ref[0] * nblk + i, 0)),
                      pl.BlockSpec((None, tr, w), lambda k, i, idx_ref: (k, i, 0))],
            out_specs=pl.BlockSpec((None, tr, w), lambda k, i, idx_ref: (k, i, 0))),
        compiler_params=_cparams(2),
    )(idx, s, a)


def _chip_scatter(ps, name):
    n = len(ps)

    def body(*refs):
        p_refs, b_refs, send_sems, recv_sems = refs[:n], refs[n:2 * n], refs[2 * n], refs[2 * n + 1]
        x, y, c, chips = _place()
        me = 2 * x + y
        sends = [_remote(p_refs[t].at[2 * cx + cy], b_refs[t].at[me], send_sems, recv_sems, 3 * t + k, (cx, cy, c))
                 for t in range(n) for k, (cx, cy) in enumerate(chips)]
        for cp in sends:
            cp.start()
        for t in range(n):
            for k, (cx, cy) in enumerate(chips):
                got = b_refs[t].at[2 * cx + cy]
                _remote(got, got, send_sems, recv_sems, 3 * t + k, (cx, cy, c)).wait_recv()
        for cp in sends:
            cp.wait_send()

    return pl.pallas_call(
        body, name=name, in_specs=[ANY] * n, out_specs=[ANY] * n,
        out_shape=[jax.ShapeDtypeStruct(p.shape, p.dtype) for p in ps],
        scratch_shapes=[pltpu.SemaphoreType.DMA((3 * n,)), pltpu.SemaphoreType.DMA((3 * n,))],
    )(*ps)


def _chip_sum(p, b, idx, name):
    n, rh, w = p.shape
    tr = _tile(rh, FLAT_TR)
    nblk = rh // tr

    def body(idx_ref, p_ref, b0_ref, b1_ref, b2_ref, b3_ref, o_ref):
        me = idx_ref[1]
        own = p_ref[...]
        t = [jnp.where(me == k, own, b_ref[...]).astype(F32) for k, b_ref in enumerate((b0_ref, b1_ref, b2_ref, b3_ref))]
        o_ref[...] = ((t[0] + t[1]) + t[2]) + t[3]

    def other(k):
        return lambda i, idx_ref: (jnp.where(idx_ref[1] == k, (k + 1) % N_CHIPS, k), i, 0)

    return pl.pallas_call(
        body, name=name, out_shape=jax.ShapeDtypeStruct((2 * rh, w), F32),
        grid_spec=pltpu.PrefetchScalarGridSpec(
            num_scalar_prefetch=1, grid=(nblk,),
            in_specs=[pl.BlockSpec((None, tr, w), lambda i, idx_ref: (idx_ref[1], i, 0))]
            + [pl.BlockSpec((None, tr, w), other(k)) for k in range(N_CHIPS)],
            out_specs=pl.BlockSpec((tr, w), lambda i, idx_ref: (idx_ref[0] * nblk + i, 0))),
        compiler_params=_cparams(1),
    )(idx, p, b, b, b, b)


def _sibling_join(gs, name):
    n = len(gs)

    def body(*refs):
        g_refs, send_sems, recv_sems = refs[n:2 * n], refs[2 * n], refs[2 * n + 1]
        x, y, c, _ = _place()
        cps = []
        for t in range(n):
            rh = g_refs[t].shape[0] // 2
            mine = g_refs[t].at[pl.ds(c * rh, rh)]
            cps.append(_remote(mine, mine, send_sems, recv_sems, t, (x, y, 1 - c)))
        for cp in cps:
            cp.start()
        for t in range(n):
            rh = g_refs[t].shape[0] // 2
            theirs = g_refs[t].at[pl.ds((1 - c) * rh, rh)]
            _remote(theirs, theirs, send_sems, recv_sems, t, (x, y, 1 - c)).wait_recv()
        for cp in cps:
            cp.wait_send()

    return pl.pallas_call(
        body, name=name, in_specs=[ANY] * n, out_specs=[ANY] * n,
        out_shape=[jax.ShapeDtypeStruct(g.shape, g.dtype) for g in gs],
        input_output_aliases={t: t for t in range(n)},
        scratch_shapes=[pltpu.SemaphoreType.DMA((n,)), pltpu.SemaphoreType.DMA((n,))],
    )(*gs)


def _adamw(g, w, m, v, name):
    r, wd = g.shape

    def body(g_ref, w_ref, m_ref, v_ref, d_ref, mo_ref, vo_ref):
        gv = g_ref[...]
        mn = ADAM_B1 * m_ref[...] + (1.0 - ADAM_B1) * gv
        vn = ADAM_B2 * v_ref[...] + (1.0 - ADAM_B2) * (gv * gv)
        m_hat = mn / (1.0 - ADAM_B1 ** ADAM_STEP)
        v_hat = vn / (1.0 - ADAM_B2 ** ADAM_STEP)
        d_ref[...] = -ADAM_LR * (m_hat / (jnp.sqrt(v_hat) + ADAM_EPS) + ADAM_WD * w_ref[...])
        mo_ref[...] = mn
        vo_ref[...] = vn

    tr = _tile(r, FLAT_TR)
    row = lambda i: (i, 0)
    spec = pl.BlockSpec((tr, wd), row)
    return pl.pallas_call(
        body, name=name, grid=(r // tr,), in_specs=[spec] * 4, out_specs=[spec] * 3,
        out_shape=[jax.ShapeDtypeStruct((r, wd), F32)] * 3,
        compiler_params=_cparams(1),
    )(g, w, m, v)


def _reduce_to_shards(ss, idx, names):
    a = _sibling_take_half(ss, "grad_sibling_half")
    p = [_pair_sum(s, a_t, idx, BF16 if n in BIG else F32, f"grad_pair_sum_{n}") for s, a_t, n in zip(ss, a, names)]
    b = _chip_scatter(p, "grad_chip_scatter")
    g = [_chip_sum(p_t, b_t, idx, f"grad_chip_sum_{n}") for p_t, b_t, n in zip(p, b, names)]
    return _sibling_join(g, "grad_sibling_join")


WEIGHTS = ("mem_norm", "w_mem_kv", "norm_a", "w_in_a", "conv_w", "conv_b", "w_rec_gate", "b_rec_gate", "w_in_gate",
           "b_in_gate", "lru_lambda", "w_out_a", "kv_norm", "w_kv", "norm_b", "w_in_b", "w_out_b", "final_norm")
SHARD_DIM = {"mem_norm": None, "w_mem_kv": 1, "norm_a": 1, "w_in_a": 2, "conv_w": 2, "conv_b": 1, "w_rec_gate": None,
             "b_rec_gate": 1, "w_in_gate": None, "b_in_gate": 1, "lru_lambda": 1, "w_out_a": 1, "kv_norm": None,
             "w_kv": 1, "norm_b": None, "w_in_b": 2, "w_out_b": 1, "final_norm": None}
BIG = ("w_mem_kv", "w_in_a", "w_out_a", "w_kv", "w_in_b", "w_out_b")
SMALL = ("norm_a", "conv_w", "conv_b", "b_rec_gate", "b_in_gate", "lru_lambda")


def _pad_rows(flat, row_multiple):
    per = FLAT_W * row_multiple
    n = flat.shape[0]
    total = -(-n // per) * per
    return jnp.pad(flat, (0, total - n)).reshape(total // FLAT_W, FLAT_W)


def _flatten(parts, row_multiple):
    return _pad_rows(jnp.concatenate([p.reshape(-1) for p in parts]), row_multiple)


def _unflatten(flat2d, shapes):
    flat = flat2d.reshape(-1)
    out, off = [], 0
    for shp in shapes:
        n = 1
        for s in shp:
            n *= s
        out.append(flat[off:off + n].reshape(shp))
        off += n
    return out


def _gather_weights(local, idx):
    parts = [local[n].astype(BF16) for n in BIG]
    parts += [lax.bitcast_convert_type(local[n], BF16) for n in SMALL]
    shapes = [p.shape for p in parts]
    gathered = _chip_all_gather(_flatten(parts, 2 * FLAT_TR), idx, "weights_all_gather")
    per_chip = [_unflatten(gathered[k], shapes) for k in range(N_CHIPS)]
    full = {}
    for idx, n in enumerate(BIG + SMALL):
        pieces = [per_chip[k][idx] for k in range(N_CHIPS)]
        if n in SMALL:
            pieces = [lax.bitcast_convert_type(p, F32) for p in pieces]
        full[n] = jnp.concatenate(pieces, axis=SHARD_DIM[n])
    return full


def _piece(g, name, k):
    dim = SHARD_DIM[name]
    if dim is None:
        return g
    n = g.shape[dim] // N_CHIPS
    return lax.slice_in_dim(g, k * n, (k + 1) * n, axis=dim)


def _local_grads(x, mem, tgt, wts):
    t, d = x.shape
    depth = wts["w_mem_kv"].shape[0]
    n_a = wts["w_in_a"].shape[0]
    n_b = wts["w_in_b"].shape[0]
    nb = wts["w_rec_gate"].shape[1]
    dr = nb * LANES
    dm = wts["w_mem_kv"].shape[2] // 2
    row = lambda v: v.reshape(1, -1)

    wm_all = jnp.concatenate([wts["w_mem_kv"][l] for l in range(depth)], axis=1)
    memkv, memn_bf = _norm_matmul(mem, row(wts["mem_norm"]), wm_all, BF16, "mem_kv_proj")

    h = x
    saved = []
    vecs = []
    for l in range(n_a):
        proj, u_bf = _norm_matmul(h, row(wts["norm_a"][l]), wts["w_in_a"][l], F32, f"a{l}_in_proj")
        vec = jnp.concatenate([row(wts["conv_b"][l]), row(wts["b_rec_gate"][l]), row(wts["b_in_gate"][l]),
                               row(wts["lru_lambda"][l]), wts["conv_w"][l]], axis=0)
        vecs.append(vec)
        y_rnn, h_rnn, xc = _lru_fwd(proj, vec, wts["w_rec_gate"][l], wts["w_in_gate"][l], f"a{l}_lru_fwd")
        y_mem = _mem_attn_fwd(proj, memkv, l, dr, dm, f"a{l}_mem_fwd")
        mix = jnp.concatenate([y_rnn, y_mem], axis=1)
        h_next = _matmul_res(mix, wts["w_out_a"][l], h, f"a{l}_out_proj")
        saved.append((h, proj, u_bf, mix, h_rnn, xc))
        h = h_next

    h_kv = h
    kv, ukv_bf = _norm_matmul(h_kv, row(wts["kv_norm"]), wts["w_kv"], BF16, "kv_proj")

    for j in range(n_b):
        l = n_a + j
        proj, u_bf = _norm_matmul(h, row(wts["norm_b"][j]), wts["w_in_b"][j], F32, f"b{j}_in_proj")
        y_sb, o_sb, tl_sb = _sb_fwd(proj, kv, f"b{j}_sb_fwd")
        y_mem = _mem_attn_fwd(proj, memkv, l, dr, dm, f"b{j}_mem_fwd")
        mix = jnp.concatenate([y_sb, y_mem], axis=1)
        h_next = _matmul_res(mix, wts["w_out_b"][j], h, f"b{j}_out_proj")
        saved.append((h, proj, u_bf, mix, o_sb, tl_sb))
        h = h_next

    dh, d_final, loss = _final_loss_bwd(h, row(wts["final_norm"]), tgt, "final_loss_bwd")

    grads = {"final_norm": d_final.reshape(-1)}
    big = {}
    dmemkv = [None] * depth
    g_norm_b = [None] * n_b
    dks, dvs = [], []
    for j in reversed(range(n_b)):
        l = n_a + j
        h_in, proj, u_bf, mix, o_sb, tl_sb = saved[l]
        dmix = _matmul_nt(dh, wts["w_out_b"][j], f"b{j}_dmix")
        big["w_out_b"] = _dw_rows(mix, dh, j, n_b, big.get("w_out_b"), f"b{j}_dw_out")
        dq, dg, dk, dv = _sb_bwd(proj, kv, o_sb, tl_sb, dmix, f"b{j}_sb_bwd")
        dqm, dgm, dkm, dvm = _mem_attn_bwd(proj, memkv, dmix, l, dr, dm, f"b{j}_mem_bwd")
        dmemkv[l] = (dkm, dvm)
        dproj = jnp.concatenate([dq, dg, dqm, dgm], axis=1)
        du = _matmul_nt(dproj, wts["w_in_b"][j], f"b{j}_du")
        big["w_in_b"] = _dw_cols(u_bf, dproj, j, n_b, big.get("w_in_b"), f"b{j}_dw_in")
        dh, dgn = _rms_bwd(du, h_in, row(wts["norm_b"][j]), dh, f"b{j}_rms_bwd")
        g_norm_b[j] = dgn.reshape(-1)
        dks.append(dk)
        dvs.append(dv)
    assert n_b == 2
    dkv = jnp.concatenate([_add_cast(dks[0], dks[1], "dk_sum"), _add_cast(dvs[0], dvs[1], "dv_sum")], axis=1)
    du = _matmul_nt(dkv, wts["w_kv"], "kv_du")
    big["w_kv"] = _dw_cols(ukv_bf, dkv, 0, 1, None, "kv_dw")
    dh, dgn = _rms_bwd(du, h_kv, row(wts["kv_norm"]), dh, "kv_rms_bwd")
    grads["kv_norm"] = dgn.reshape(-1)

    g_norm_a = [None] * n_a
    g_wr, g_wi, g_vec = [None] * n_a, [None] * n_a, [None] * n_a
    for l in reversed(range(n_a)):
        h_in, proj, u_bf, mix, h_rnn, xc = saved[l]
        dmix = _matmul_nt(dh, wts["w_out_a"][l], f"a{l}_dmix")
        big["w_out_a"] = _dw_rows(mix, dh, l, n_a, big.get("w_out_a"), f"a{l}_dw_out")
        dx, dg, g_wr[l], g_wi[l], dvec = _lru_bwd(proj, xc, h_rnn, dmix, vecs[l], wts["w_rec_gate"][l],
                                                  wts["w_in_gate"][l], f"a{l}_lru_bwd")
        g_vec[l] = dvec.transpose(1, 0, 2).reshape(8, dr)
        dqm, dgm, dkm, dvm = _mem_attn_bwd(proj, memkv, dmix, l, dr, dm, f"a{l}_mem_bwd")
        dmemkv[l] = (dkm, dvm)
        dproj = jnp.concatenate([dx, dg, dqm, dgm], axis=1)
        du = _matmul_nt(dproj, wts["w_in_a"][l], f"a{l}_du")
        big["w_in_a"] = _dw_cols(u_bf, dproj, l, n_a, big.get("w_in_a"), f"a{l}_dw_in")
        dh, dgn = _rms_bwd(du, h_in, row(wts["norm_a"][l]), dh, f"a{l}_rms_bwd")
        g_norm_a[l] = dgn.reshape(-1)

    dmemkv_all = jnp.concatenate([jnp.concatenate(p, axis=1) for p in dmemkv], axis=1).astype(BF16)
    pk = d // N_CHIPS
    big["w_mem_kv"] = _matmul_tn(memn_bf, dmemkv_all, "mem_dw", pk, 2 * dm, (N_CHIPS, depth, pk, 2 * dm),
                                 (None, None, pk, 2 * dm), lambda i, j: (i, j, 0, 0))
    dmemn = _matmul_nt(dmemkv_all, wm_all, "mem_du")
    _, dgn = _rms_bwd(dmemn, mem, row(wts["mem_norm"]), jnp.zeros_like(mem), "mem_rms_bwd")
    grads["mem_norm"] = dgn.reshape(-1)
    grads["norm_a"] = jnp.stack(g_norm_a)
    grads["w_rec_gate"] = jnp.stack(g_wr)
    grads["w_in_gate"] = jnp.stack(g_wi)
    gv = jnp.stack(g_vec)
    grads["conv_b"], grads["b_rec_gate"], grads["b_in_gate"], grads["lru_lambda"] = gv[:, 0], gv[:, 1], gv[:, 2], gv[:, 3]
    grads["conv_w"] = gv[:, 4:8]
    grads["norm_b"] = jnp.stack(g_norm_b)
    big = {n: g.reshape(N_CHIPS, -1, g.shape[-1]) for n, g in big.items()}
    return loss, dh, grads, big


def kernel(x, mem, mem_norm, w_mem_kv, norm_a, w_in_a, conv_w, conv_b, w_rec_gate, b_rec_gate, w_in_gate, b_in_gate, lru_lambda, w_out_a, kv_norm, w_kv, norm_b, w_in_b, w_out_b, final_norm, loss_target, m_mem_norm, m_w_mem_kv, m_norm_a, m_w_in_a, m_conv_w, m_conv_b, m_w_rec_gate, m_b_rec_gate, m_w_in_gate, m_b_in_gate, m_lru_lambda, m_w_out_a, m_kv_norm, m_w_kv, m_norm_b, m_w_in_b, m_w_out_b, m_final_norm, v_mem_norm, v_w_mem_kv, v_norm_a, v_w_in_a, v_conv_w, v_conv_b, v_w_rec_gate, v_b_rec_gate, v_w_in_gate, v_b_in_gate, v_lru_lambda, v_w_out_a, v_kv_norm, v_w_kv, v_norm_b, v_w_in_b, v_w_out_b, v_final_norm):
    local = dict(mem_norm=mem_norm, w_mem_kv=w_mem_kv, norm_a=norm_a, w_in_a=w_in_a, conv_w=conv_w, conv_b=conv_b,
                 w_rec_gate=w_rec_gate, b_rec_gate=b_rec_gate, w_in_gate=w_in_gate, b_in_gate=b_in_gate,
                 lru_lambda=lru_lambda, w_out_a=w_out_a, kv_norm=kv_norm, w_kv=w_kv, norm_b=norm_b, w_in_b=w_in_b,
                 w_out_b=w_out_b, final_norm=final_norm)
    mom = dict(mem_norm=m_mem_norm, w_mem_kv=m_w_mem_kv, norm_a=m_norm_a, w_in_a=m_w_in_a, conv_w=m_conv_w,
               conv_b=m_conv_b, w_rec_gate=m_w_rec_gate, b_rec_gate=m_b_rec_gate, w_in_gate=m_w_in_gate,
               b_in_gate=m_b_in_gate, lru_lambda=m_lru_lambda, w_out_a=m_w_out_a, kv_norm=m_kv_norm, w_kv=m_w_kv,
               norm_b=m_norm_b, w_in_b=m_w_in_b, w_out_b=m_w_out_b, final_norm=m_final_norm)
    var = dict(mem_norm=v_mem_norm, w_mem_kv=v_w_mem_kv, norm_a=v_norm_a, w_in_a=v_w_in_a, conv_w=v_conv_w,
               conv_b=v_conv_b, w_rec_gate=v_w_rec_gate, b_rec_gate=v_b_rec_gate, w_in_gate=v_w_in_gate,
               b_in_gate=v_b_in_gate, lru_lambda=v_lru_lambda, w_out_a=v_w_out_a, kv_norm=v_kv_norm, w_kv=v_w_kv,
               norm_b=v_norm_b, w_in_b=v_w_in_b, w_out_b=v_w_out_b, final_norm=v_final_norm)

    idx = jnp.stack([lax.axis_index("c"), 2 * lax.axis_index("x") + lax.axis_index("y")]).astype(jnp.int32)

    wts = _gather_weights(local, idx)
    for n in WEIGHTS:
        if SHARD_DIM[n] is None:
            wts[n] = local[n]
    wts["w_rec_gate"] = wts["w_rec_gate"].astype(BF16)
    wts["w_in_gate"] = wts["w_in_gate"].astype(BF16)

    loss, grad_x, grads, big = _local_grads(x[0], mem[0], loss_target[0], wts)

    rest = [n for n in WEIGHTS if n not in BIG]
    row_multiple = 2 * FLAT_TR
    s_rest = jnp.stack([_flatten([_piece(grads[n], n, k) for n in rest], row_multiple) for k in range(N_CHIPS)])
    reduced = _reduce_to_shards([big[n] for n in BIG] + [s_rest], idx, list(BIG) + ["rest"])

    g_out, d_out, m_out, v_out = {}, {}, {}, {}
    for n, g in zip(BIG, reduced):
        shape = local[n].shape
        flat = lambda a: a.reshape(-1, shape[-1])
        d, mo, vo = _adamw(g, flat(local[n]), flat(mom[n]), flat(var[n]), f"adamw_{n}")
        g_out[n], d_out[n], m_out[n], v_out[n] = (a.reshape(shape) for a in (g, d, mo, vo))
    g_rest = reduced[-1]
    d_rest, m_rest, v_rest = _adamw(g_rest, *(_flatten([src[n] for n in rest], row_multiple) for src in (local, mom, var)),
                                    "adamw_rest")
    shapes = [local[n].shape for n in rest]
    for out, flat2d in ((g_out, g_rest), (d_out, d_rest), (m_out, m_rest), (v_out, v_rest)):
        out.update(zip(rest, _unflatten(flat2d, shapes)))

    total_loss = lax.psum(loss[0, 0], MESH_AXES)
    return (total_loss, grad_x[None], *[g_out[n] for n in WEIGHTS], *[d_out[n] for n in WEIGHTS],
            *[m_out[n] for n in WEIGHTS], *[v_out[n] for n in WEIGHTS])
```

```python
import functools

import jax
import jax.numpy as jnp
from jax import lax
from jax.experimental import pallas as pl
from jax.experimental.pallas import tpu as pltpu

F32 = jnp.float32
BF16 = jnp.bfloat16

RMS_EPS = 1e-6
LRU_C = 8.0
ADAM_LR = 0.001
ADAM_B1 = 0.9
ADAM_B2 = 0.999
ADAM_EPS = 1e-08
ADAM_WD = 0.01
ADAM_STEP = 10

LANES = 128
VMEM_LIMIT = 56 * 1024 * 1024
FLAT_W = 1024
FLAT_TR = 256
N_CHIPS = 4
MESH_AXES = ("x", "y", "c")

_NT = (((1,), (1,)), ((), ()))
_TN = (((0,), (0,)), ((), ()))
ANY = pl.BlockSpec(memory_space=pl.ANY)
MESH = pl.DeviceIdType.MESH


def _cparams(n_axes):
    return pltpu.CompilerParams(dimension_semantics=("arbitrary",) * n_axes, vmem_limit_bytes=VMEM_LIMIT)


def _sigmoid(x):
    return 1.0 / (1.0 + jnp.exp(-x))


def _log1p_pos(e):
    return jnp.where(e < 1e-3, e * (1.0 - e * (0.5 - e * (1.0 / 3.0))), jnp.log(1.0 + e))


def _neg_expm1(x):
    small = -x * (1.0 + x * (0.5 + x * (1.0 / 6.0 + x * (1.0 / 24.0))))
    return jnp.where(x > -0.05, small, 1.0 - jnp.exp(x))


def _tile(n, want):
    if n <= want:
        return n
    t = want
    while n % t:
        t -= LANES
    assert t > 0, (n, want)
    return t


def _norm_matmul(x, g, w, out_dtype, name):
    m, k = x.shape
    n = w.shape[1]
    tm, tn = _tile(m, 1024), _tile(n, 1024)

    def body(x_ref, g_ref, w_ref, o_ref, u_ref):
        @pl.when(pl.program_id(1) == 0)
        def _():
            xf = x_ref[...]
            r = lax.rsqrt(jnp.mean(xf * xf, axis=-1, keepdims=True) + RMS_EPS)
            u_ref[...] = ((xf * r) * g_ref[...]).astype(BF16)

        o_ref[...] = jnp.dot(u_ref[...], w_ref[...], preferred_element_type=F32).astype(o_ref.dtype)

    return pl.pallas_call(
        body, name=name, grid=(m // tm, n // tn),
        in_specs=[pl.BlockSpec((tm, k), lambda i, j: (i, 0)), pl.BlockSpec((1, k), lambda i, j: (0, 0)),
                  pl.BlockSpec((k, tn), lambda i, j: (0, j))],
        out_specs=[pl.BlockSpec((tm, tn), lambda i, j: (i, j)), pl.BlockSpec((tm, k), lambda i, j: (i, 0))],
        out_shape=[jax.ShapeDtypeStruct((m, n), out_dtype), jax.ShapeDtypeStruct((m, k), BF16)],
        compiler_params=_cparams(2),
    )(x, g, w)


def _matmul_res(a, b, res, name):
    m, k = a.shape
    n = b.shape[1]
    tm, tn = _tile(m, 1024), _tile(n, 1024)

    def body(a_ref, b_ref, r_ref, o_ref):
        o_ref[...] = r_ref[...] + jnp.dot(a_ref[...], b_ref[...], preferred_element_type=F32)

    return pl.pallas_call(
        body, name=name, grid=(m // tm, n // tn),
        in_specs=[pl.BlockSpec((tm, k), lambda i, j: (i, 0)), pl.BlockSpec((k, tn), lambda i, j: (0, j)),
                  pl.BlockSpec((tm, tn), lambda i, j: (i, j))],
        out_specs=pl.BlockSpec((tm, tn), lambda i, j: (i, j)),
        out_shape=jax.ShapeDtypeStruct((m, n), F32),
        compiler_params=_cparams(2),
    )(a, b, res)


def _matmul_nt(a, b, name):
    m, n = a.shape
    k = b.shape[0]
    tm, tk = _tile(m, 1024), _tile(k, 512)

    def body(a_ref, b_ref, o_ref):
        o_ref[...] = lax.dot_general(a_ref[...].astype(BF16), b_ref[...], _NT, preferred_element_type=F32)

    return pl.pallas_call(
        body, name=name, grid=(m // tm, k // tk),
        in_specs=[pl.BlockSpec((tm, n), lambda i, j: (i, 0)), pl.BlockSpec((tk, n), lambda i, j: (j, 0))],
        out_specs=pl.BlockSpec((tm, tk), lambda i, j: (i, j)),
        out_shape=jax.ShapeDtypeStruct((m, k), F32),
        compiler_params=_cparams(2),
    )(a, b)


def _matmul_tn(a, b, name, tk, tn, out_shape, out_block, out_index, into=None):
    m, k = a.shape
    n = b.shape[1]
    tm = _tile(m, 2048 if b.dtype == BF16 else 1024)

    def body(a_ref, b_ref, *rest):
        o_ref = rest[-1]
        part = lax.dot_general(a_ref[...].astype(BF16), b_ref[...].astype(BF16), _TN, preferred_element_type=F32)

        @pl.when(pl.program_id(2) == 0)
        def _():
            o_ref[...] = part

        @pl.when(pl.program_id(2) != 0)
        def _():
            o_ref[...] += part

    in_specs = [pl.BlockSpec((tm, tk), lambda i, j, s: (s, i)), pl.BlockSpec((tm, tn), lambda i, j, s: (s, j))]
    args = [a, b]
    if into is not None:
        in_specs.append(ANY)
        args.append(into)
    return pl.pallas_call(
        body, name=name, grid=(k // tk, n // tn, m // tm), in_specs=in_specs,
        out_specs=pl.BlockSpec(out_block, lambda i, j, s: out_index(i, j)),
        out_shape=jax.ShapeDtypeStruct(out_shape, F32),
        input_output_aliases={} if into is None else {2: 0},
        compiler_params=_cparams(3),
    )(*args)


def _dw_cols(a, b, layer, n_layers, into, name):
    k, n = a.shape[1], b.shape[1]
    pn = n // N_CHIPS
    tk = _tile(k, 512)
    return _matmul_tn(a, b, name, tk, pn, (N_CHIPS, n_layers, k, pn), (None, None, tk, pn),
                      lambda i, j: (j, layer, i, 0), into)


def _dw_rows(a, b, layer, n_layers, into, name):
    k, n = a.shape[1], b.shape[1]
    pk = k // N_CHIPS
    tn = _tile(n, 2048)
    return _matmul_tn(a, b, name, pk, tn, (N_CHIPS, n_layers, pk, n), (None, None, pk, tn),
                      lambda i, j: (i, layer, 0, j), into)


def _rms_bwd(du, h, g, dres, name):
    m, d = h.shape
    tm = _tile(m, 256)

    def body(du_ref, h_ref, g_ref, dres_ref, dx_ref, dg_ref):
        xf = h_ref[...]
        r = lax.rsqrt(jnp.mean(xf * xf, axis=-1, keepdims=True) + RMS_EPS)
        xhat = xf * r
        du_v = du_ref[...]
        dxn = du_v * g_ref[...]
        dx_ref[...] = dres_ref[...] + r * (dxn - xhat * jnp.mean(dxn * xhat, axis=-1, keepdims=True))
        part = jnp.sum(du_v * xhat, axis=0, keepdims=True)

        @pl.when(pl.program_id(0) == 0)
        def _():
            dg_ref[...] = part

        @pl.when(pl.program_id(0) != 0)
        def _():
            dg_ref[...] += part

    row = lambda i: (i, 0)
    return pl.pallas_call(
        body, name=name, grid=(m // tm,),
        in_specs=[pl.BlockSpec((tm, d), row), pl.BlockSpec((tm, d), row), pl.BlockSpec((1, d), lambda i: (0, 0)),
                  pl.BlockSpec((tm, d), row)],
        out_specs=[pl.BlockSpec((tm, d), row), pl.BlockSpec((1, d), lambda i: (0, 0))],
        out_shape=[jax.ShapeDtypeStruct((m, d), F32), jax.ShapeDtypeStruct((1, d), F32)],
        compiler_params=_cparams(1),
    )(du, h, g, dres)


def _final_loss_bwd(h, g, tgt, name):
    m, d = h.shape
    tm = _tile(m, 256)

    def body(h_ref, g_ref, t_ref, dx_ref, dg_ref, loss_ref):
        xf = h_ref[...]
        r = lax.rsqrt(jnp.mean(xf * xf, axis=-1, keepdims=True) + RMS_EPS)
        xhat = xf * r
        gv = g_ref[...]
        err = xhat * gv - t_ref[...]
        dy = err * (1.0 / d)
        dxn = dy * gv
        dx_ref[...] = r * (dxn - xhat * jnp.mean(dxn * xhat, axis=-1, keepdims=True))
        part = jnp.sum(dy * xhat, axis=0, keepdims=True)
        lpart = jnp.sum(jnp.sum(err * err, axis=0, keepdims=True), axis=1, keepdims=True) * (0.5 / d)

        @pl.when(pl.program_id(0) == 0)
        def _():
            dg_ref[...] = part
            loss_ref[...] = lpart

        @pl.when(pl.program_id(0) != 0)
        def _():
            dg_ref[...] += part
            loss_ref[...] += lpart

    row = lambda i: (i, 0)
    fixed = lambda i: (0, 0)
    return pl.pallas_call(
        body, name=name, grid=(m // tm,),
        in_specs=[pl.BlockSpec((tm, d), row), pl.BlockSpec((1, d), fixed), pl.BlockSpec((tm, d), row)],
        out_specs=[pl.BlockSpec((tm, d), row), pl.BlockSpec((1, d), fixed), pl.BlockSpec((1, 1), fixed)],
        out_shape=[jax.ShapeDtypeStruct((m, d), F32), jax.ShapeDtypeStruct((1, d), F32),
                   jax.ShapeDtypeStruct((1, 1), F32)],
        compiler_params=_cparams(1),
    )(h, g, tgt)


def _add_cast(a, b, name):
    m, n = a.shape
    tm, tn = _tile(m, 512), _tile(n, 2048)

    def body(a_ref, b_ref, o_ref):
        o_ref[...] = (a_ref[...] + b_ref[...]).astype(BF16)

    blk = lambda i, j: (i, j)
    return pl.pallas_call(
        body, name=name, grid=(m // tm, n // tn),
        in_specs=[pl.BlockSpec((tm, tn), blk), pl.BlockSpec((tm, tn), blk)],
        out_specs=pl.BlockSpec((tm, tn), blk),
        out_shape=jax.ShapeDtypeStruct((m, n), BF16),
        compiler_params=_cparams(2),
    )(a, b)


def _mem_attn_fwd(proj, memkv, layer, dr, dm, name):
    t = proj.shape[0]
    nm = memkv.shape[0]
    tm = _tile(t, 512)
    nh = dm // LANES
    scale = LANES ** -0.5
    qb = (2 * dr) // dm

    def body(q_ref, g_ref, k_ref, v_ref, y_ref):
        for hh in range(nh):
            sl = slice(hh * LANES, (hh + 1) * LANES)
            s = lax.dot_general(q_ref[:, sl].astype(BF16), k_ref[:, sl], _NT, preferred_element_type=F32) * scale
            p = jnp.exp(s - jnp.max(s, axis=-1, keepdims=True))
            p = p / jnp.sum(p, axis=-1, keepdims=True)
            o = jnp.dot(p.astype(BF16), v_ref[:, sl], preferred_element_type=F32)
            gv = g_ref[:, sl]
            y_ref[:, sl] = (o * (gv * _sigmoid(gv))).astype(BF16)

    return pl.pallas_call(
        body, name=name, grid=(t // tm,),
        in_specs=[pl.BlockSpec((tm, dm), lambda i: (i, qb)), pl.BlockSpec((tm, dm), lambda i: (i, qb + 1)),
                  pl.BlockSpec((nm, dm), lambda i: (0, 2 * layer)), pl.BlockSpec((nm, dm), lambda i: (0, 2 * layer + 1))],
        out_specs=pl.BlockSpec((tm, dm), lambda i: (i, 0)),
        out_shape=jax.ShapeDtypeStruct((t, dm), BF16),
        compiler_params=_cparams(1),
    )(proj, proj, memkv, memkv)


def _mem_attn_bwd(proj, memkv, dmix, layer, dr, dm, name):
    t = proj.shape[0]
    nm = memkv.shape[0]
    tm = _tile(t, 512)
    nh = dm // LANES
    scale = LANES ** -0.5
    qb = (2 * dr) // dm
    yb = dr // dm

    def body(q_ref, g_ref, k_ref, v_ref, dy_ref, dq_ref, dg_ref, dk_ref, dv_ref):
        @pl.when(pl.program_id(0) == 0)
        def _():
            dk_ref[...] = jnp.zeros_like(dk_ref)
            dv_ref[...] = jnp.zeros_like(dv_ref)

        for hh in range(nh):
            sl = slice(hh * LANES, (hh + 1) * LANES)
            q = q_ref[:, sl].astype(BF16)
            k = k_ref[:, sl]
            v = v_ref[:, sl]
            s = lax.dot_general(q, k, _NT, preferred_element_type=F32) * scale
            p = jnp.exp(s - jnp.max(s, axis=-1, keepdims=True))
            p = p / jnp.sum(p, axis=-1, keepdims=True)
            p_bf = p.astype(BF16)
            o = jnp.dot(p_bf, v, preferred_element_type=F32)
            gv = g_ref[:, sl]
            sg = _sigmoid(gv)
            dy = dy_ref[:, sl]
            do = dy * (gv * sg)
            dg_ref[:, sl] = (dy * o * (sg * (1.0 + gv * (1.0 - sg)))).astype(BF16)
            do_bf = do.astype(BF16)
            dv_ref[:, sl] += lax.dot_general(p_bf, do_bf, _TN, preferred_element_type=F32)
            dp = lax.dot_general(do_bf, v, _NT, preferred_element_type=F32)
            ds = (p * (dp - jnp.sum(dp * p, axis=-1, keepdims=True)) * scale).astype(BF16)
            dq_ref[:, sl] = jnp.dot(ds, k, preferred_element_type=F32).astype(BF16)
            dk_ref[:, sl] += lax.dot_general(ds, q, _TN, preferred_element_type=F32)

    fixed = lambda i: (0, 0)
    return pl.pallas_call(
        body, name=name, grid=(t // tm,),
        in_specs=[pl.BlockSpec((tm, dm), lambda i: (i, qb)), pl.BlockSpec((tm, dm), lambda i: (i, qb + 1)),
                  pl.BlockSpec((nm, dm), lambda i: (0, 2 * layer)), pl.BlockSpec((nm, dm), lambda i: (0, 2 * layer + 1)),
                  pl.BlockSpec((tm, dm), lambda i: (i, yb))],
        out_specs=[pl.BlockSpec((tm, dm), lambda i: (i, 0)), pl.BlockSpec((tm, dm), lambda i: (i, 0)),
                   pl.BlockSpec((nm, dm), fixed), pl.BlockSpec((nm, dm), fixed)],
        out_shape=[jax.ShapeDtypeStruct((t, dm), BF16), jax.ShapeDtypeStruct((t, dm), BF16),
                   jax.ShapeDtypeStruct((nm, dm), F32), jax.ShapeDtypeStruct((nm, dm), F32)],
        compiler_params=_cparams(1),
    )(proj, proj, memkv, memkv, dmix)


LRU_CHUNK = 256


def _lru_gates(xc, vec, wr_ref, wi_ref):
    r = _sigmoid(jnp.dot(xc.astype(BF16), wr_ref[...], preferred_element_type=F32) + vec[1:2])
    i = _sigmoid(jnp.dot(xc.astype(BF16), wi_ref[...], preferred_element_type=F32) + vec[2:3])
    lam = vec[3:4]
    cl = -LRU_C * (jnp.maximum(-lam, 0.0) + _log1p_pos(jnp.exp(-jnp.abs(lam))))
    la = cl * r
    a = jnp.exp(la)
    s2 = _neg_expm1(2.0 * la)
    return r, i, cl, a, s2


def _lru_fwd(proj, vec, wr, wi, name):
    t = proj.shape[0]
    nb = wr.shape[0]
    dr = nb * LANES
    c = _tile(t, LRU_CHUNK)

    def body(x_ref, g_ref, vec_ref, wr_ref, wi_ref, y_ref, h_ref, xc_ref, carry_ref, xprev_ref):
        @pl.when(pl.program_id(1) == 0)
        def _():
            carry_ref[...] = jnp.zeros_like(carry_ref)
            xprev_ref[...] = jnp.zeros_like(xprev_ref)

        x = x_ref[...]
        vec = vec_ref[...]
        rows = lax.broadcasted_iota(jnp.int32, (c, LANES), 0)
        xprev = xprev_ref[...]
        xc = vec[7:8] * x + vec[0:1]
        for k in range(1, 4):
            xs = jnp.where(rows < k, pltpu.roll(xprev, k, 0), pltpu.roll(x, k, 0))
            xc = xc + vec[7 - k:8 - k] * xs
        xprev_ref[...] = x
        xc_ref[...] = xc

        r, i, cl, a, s2 = _lru_gates(xc, vec, wr_ref, wi_ref)
        hh = jnp.sqrt(s2) * (i * xc)
        aa = a
        d = 1
        while d < c:
            keep = rows >= d
            hh = jnp.where(keep, aa * pltpu.roll(hh, d, 0) + hh, hh)
            aa = jnp.where(keep, aa * pltpu.roll(aa, d, 0), aa)
            d *= 2
        hfull = hh + aa * carry_ref[7:8, :]
        carry_ref[...] = hfull[c - 8:c, :]
        h_ref[...] = hfull
        gv = g_ref[...]
        y_ref[...] = (hfull * (gv * _sigmoid(gv))).astype(BF16)

    blk = lambda n, s: (s, n)
    return pl.pallas_call(
        body, name=name, grid=(nb, t // c),
        in_specs=[pl.BlockSpec((c, LANES), blk), pl.BlockSpec((c, LANES), lambda n, s: (s, nb + n)),
                  pl.BlockSpec((8, LANES), lambda n, s: (0, n)),
                  pl.BlockSpec((None, LANES, LANES), lambda n, s: (n, 0, 0)),
                  pl.BlockSpec((None, LANES, LANES), lambda n, s: (n, 0, 0))],
        out_specs=[pl.BlockSpec((c, LANES), blk)] * 3,
        out_shape=[jax.ShapeDtypeStruct((t, dr), BF16), jax.ShapeDtypeStruct((t, dr), F32),
                   jax.ShapeDtypeStruct((t, dr), F32)],
        scratch_shapes=[pltpu.VMEM((8, LANES), F32), pltpu.VMEM((c, LANES), F32)],
        compiler_params=_cparams(2),
    )(proj, proj, vec, wr, wi)


def _lru_bwd(proj, xc_all, h_all, dmix, vec, wr, wi, name):
    t = proj.shape[0]
    nb = wr.shape[0]
    dr = nb * LANES
    c = _tile(t, LRU_CHUNK)
    nc = t // c

    def body(x_ref, g_ref, xc_ref, h_ref, dy_ref, vec_ref, wr_ref, wi_ref,
             dx_ref, dg_ref, dwr_ref, dwi_ref, dvec_ref, qcarry_ref, dxc_next_ref):
        @pl.when(pl.program_id(1) == 0)
        def _():
            qcarry_ref[...] = jnp.zeros_like(qcarry_ref)
            dxc_next_ref[...] = jnp.zeros_like(dxc_next_ref)
            dwr_ref[...] = jnp.zeros_like(dwr_ref)
            dwi_ref[...] = jnp.zeros_like(dwi_ref)
            dvec_ref[...] = jnp.zeros_like(dvec_ref)

        x = x_ref[...]
        xc = xc_ref[...]
        h = h_ref[...]
        dy = dy_ref[...]
        gv = g_ref[...]
        vec = vec_ref[...]
        rows = lax.broadcasted_iota(jnp.int32, (c, LANES), 0)

        r, i, cl, a, s2 = _lru_gates(xc, vec, wr_ref, wi_ref)
        s = jnp.sqrt(s2)
        ixc = i * xc
        u = s * ixc
        sg = _sigmoid(gv)
        dh = dy * (gv * sg)
        dg_ref[...] = (dy * h * (sg * (1.0 + gv * (1.0 - sg)))).astype(BF16)

        aa = a
        qq = a * dh
        d = 1
        while d < c:
            keep = rows < c - d
            qq = jnp.where(keep, qq + aa * pltpu.roll(qq, c - d, 0), qq)
            aa = jnp.where(keep, aa * pltpu.roll(aa, c - d, 0), aa)
            d *= 2
        qin = qcarry_ref[0:1, :]
        qfull = qq + aa * qin
        gt = dh + jnp.where(rows == c - 1, qin, pltpu.roll(qfull, c - 1, 0))
        qcarry_ref[...] = qfull[0:8, :]

        dla = gt * (h - u) - gt * ixc * (a * a) / s
        dixc = gt * s
        di = dixc * xc
        dxc = dixc * i
        dzr = (dla * cl) * (r * (1.0 - r))
        dzi = di * (i * (1.0 - i))
        dzr_bf = dzr.astype(BF16)
        dzi_bf = dzi.astype(BF16)
        dxc = dxc + lax.dot_general(dzr_bf, wr_ref[...], _NT, preferred_element_type=F32)
        dxc = dxc + lax.dot_general(dzi_bf, wi_ref[...], _NT, preferred_element_type=F32)
        xc_bf = xc.astype(BF16)
        dwr_ref[...] += lax.dot_general(xc_bf, dzr_bf, _TN, preferred_element_type=F32)
        dwi_ref[...] += lax.dot_general(xc_bf, dzi_bf, _TN, preferred_element_type=F32)

        lam = vec[3:4]
        dlam = jnp.sum(dla * r, axis=0, keepdims=True) * (LRU_C * _sigmoid(-lam))
        colsum = lambda v: jnp.sum(v, axis=0, keepdims=True)
        dxn = dxc_next_ref[...]
        dx = vec[7:8] * dxc
        dtaps = [None] * 4
        dtaps[3] = colsum(x * dxc)
        for k in range(1, 4):
            sh = jnp.where(rows < c - k, pltpu.roll(dxc, c - k, 0), pltpu.roll(dxn, c - k, 0))
            dx = dx + vec[7 - k:8 - k] * sh
            dtaps[3 - k] = colsum(x * sh)
        dxc_next_ref[...] = dxc
        dx_ref[...] = dx.astype(BF16)
        dvec_ref[...] += jnp.concatenate([colsum(dxc), colsum(dzr), colsum(dzi), dlam] + dtaps, axis=0)

    rev = lambda n, s: (nc - 1 - s, n)
    sq = lambda n, s: (n, 0, 0)
    return pl.pallas_call(
        body, name=name, grid=(nb, nc),
        in_specs=[pl.BlockSpec((c, LANES), rev), pl.BlockSpec((c, LANES), lambda n, s: (nc - 1 - s, nb + n)),
                  pl.BlockSpec((c, LANES), rev), pl.BlockSpec((c, LANES), rev), pl.BlockSpec((c, LANES), rev),
                  pl.BlockSpec((8, LANES), lambda n, s: (0, n)),
                  pl.BlockSpec((None, LANES, LANES), sq), pl.BlockSpec((None, LANES, LANES), sq)],
        out_specs=[pl.BlockSpec((c, LANES), rev), pl.BlockSpec((c, LANES), rev),
                   pl.BlockSpec((None, LANES, LANES), sq), pl.BlockSpec((None, LANES, LANES), sq),
                   pl.BlockSpec((None, 8, LANES), sq)],
        out_shape=[jax.ShapeDtypeStruct((t, dr), BF16), jax.ShapeDtypeStruct((t, dr), BF16),
                   jax.ShapeDtypeStruct((nb, LANES, LANES), F32), jax.ShapeDtypeStruct((nb, LANES, LANES), F32),
                   jax.ShapeDtypeStruct((nb, 8, LANES), F32)],
        scratch_shapes=[pltpu.VMEM((8, LANES), F32), pltpu.VMEM((c, LANES), F32)],
        compiler_params=_cparams(2),
    )(proj, proj, xc_all, h_all, dmix, vec, wr, wi)


SB_TQ = 1024
SB_TK = 256


def _sb_softplus(z, diag):
    sp = jnp.maximum(z, 0.0) + jnp.log(1.0 + jnp.exp(-jnp.abs(z)))
    mask = None
    if diag:
        mask = lax.broadcasted_iota(jnp.int32, z.shape, 1) < lax.broadcasted_iota(jnp.int32, z.shape, 0)
        sp = jnp.where(mask, sp, 0.0)
    return sp, mask


def _split_dot(v, m):
    hi = v.astype(BF16)
    lo = (v - hi.astype(F32)).astype(BF16)
    return jnp.dot(hi, m, preferred_element_type=F32) + jnp.dot(lo, m, preferred_element_type=F32)


def _tri_ones(kind, tk):
    jj = lax.broadcasted_iota(jnp.int32, (tk, tk), 0)
    ss = lax.broadcasted_iota(jnp.int32, (tk, tk), 1)
    rel = {"ge": jj >= ss, "le": jj <= ss}[kind]
    return jnp.where(rel, 1.0, 0.0).astype(BF16)


def _sb_fwd(proj, kv, name):
    t = proj.shape[0]
    ds = kv.shape[1] // 2
    nh = ds // LANES
    tq = _tile(t, SB_TQ)
    tk = _tile(tq, SB_TK)
    nd = tq // tk
    assert nd % 2 == 0 or t == tq
    scale = LANES ** -0.5

    def body(q_ref, g_ref, k_ref, v_ref, y_ref, o_ref, tl_ref, qbf_ref, acc_ref, run_ref, z_ref, w_ref):
        qi = pl.program_id(1)
        qbf_ref[...] = q_ref[...].astype(BF16)
        tri = _tri_ones("ge", tk)
        acc_ref[...] = jnp.zeros_like(acc_ref)
        run_ref[...] = jnp.zeros_like(run_ref)
        w_ref[...] = jnp.zeros_like(w_ref)
        all_rows = [slice(s0, s0 + tk) for s0 in range(0, tq, tk)]

        def weights(zs, groups):
            sps = [_sb_softplus(z, dg) for z, (_, dg) in zip(zs, groups)]
            cums = [_split_dot(sp, tri) for sp, _ in sps]
            ws = []
            for z, (rows, dg), (_, mask), cum in zip(zs, groups, sps, cums):
                run = run_ref[rows, :]
                w = jnp.exp(z - cum - run)
                if dg:
                    w = jnp.where(mask, w, 0.0)
                run_ref[rows, :] = run + cum[:, 0:1]
                ws.append(w.astype(BF16))
            return ws

        for u in reversed(range(nd)):
            groups = [(slice(s0, s0 + tk), s0 == u * tk) for s0 in range(u * tk, tq, tk)]
            k0 = pl.multiple_of(qi * tq + u * tk, tk)
            k = k_ref[pl.ds(k0, tk), :]
            v = v_ref[pl.ds(k0, tk), :]
            zs = [lax.dot_general(qbf_ref[rows, :], k, _NT, preferred_element_type=F32) * scale for rows, _ in groups]
            for (rows, _), w in zip(groups, weights(zs, groups)):
                acc_ref[rows, :] += jnp.dot(w, v, preferred_element_type=F32)

        n = qi * nd

        def logits_into(slot, kb):
            k = k_ref[pl.ds(pl.multiple_of(kb * tk, tk), tk), :]
            for rows in all_rows:
                z_ref[slot, rows, :] = lax.dot_general(qbf_ref[rows, :], k, _NT, preferred_element_type=F32) * scale

        def add_values(kb):
            v = v_ref[pl.ds(pl.multiple_of(kb * tk, tk), tk), :]
            for rows in all_rows:
                acc_ref[rows, :] += jnp.dot(w_ref[rows, :], v, preferred_element_type=F32)

        @pl.when(n > 0)
        def _():
            logits_into(0, n - 1)

        def half_step(j, slot):
            kb = n - 1 - j
            add_values(jnp.minimum(kb + 1, n - 1))
            logits_into(1 - slot, jnp.maximum(kb - 1, 0))
            groups = [(rows, False) for rows in all_rows]
            for rows, w in zip(all_rows, weights([z_ref[slot, rows, :] for rows in all_rows], groups)):
                w_ref[rows, :] = w

        def step(i, carry):
            half_step(2 * i, 0)
            half_step(2 * i + 1, 1)
            return carry

        lax.fori_loop(0, n // 2, step, 0)
        add_values(0)
        o = acc_ref[...]
        o_ref[...] = o
        tl_ref[...] = jnp.broadcast_to(run_ref[...], (tq, LANES))
        gv = g_ref[...]
        y_ref[...] = (o * (gv * _sigmoid(gv))).astype(BF16)

    blk = lambda h, i: (i, h)
    return pl.pallas_call(
        body, name=name, grid=(nh, t // tq),
        in_specs=[pl.BlockSpec((tq, LANES), blk), pl.BlockSpec((tq, LANES), lambda h, i: (i, nh + h)),
                  pl.BlockSpec((t, LANES), lambda h, i: (0, h)), pl.BlockSpec((t, LANES), lambda h, i: (0, nh + h))],
        out_specs=[pl.BlockSpec((tq, LANES), blk)] * 3,
        out_shape=[jax.ShapeDtypeStruct((t, ds), BF16), jax.ShapeDtypeStruct((t, ds), F32),
                   jax.ShapeDtypeStruct((t, ds), F32)],
        scratch_shapes=[pltpu.VMEM((tq, LANES), BF16), pltpu.VMEM((tq, LANES), F32), pltpu.VMEM((tq, 1), F32),
                        pltpu.VMEM((2, tq, tk), F32), pltpu.VMEM((tq, tk), BF16)],
        compiler_params=_cparams(2),
    )(proj, proj, kv, kv)


def _sb_bwd(proj, kv, o_all, tl_all, dmix, name):
    t = proj.shape[0]
    ds = kv.shape[1] // 2
    nh = ds // LANES
    tq = _tile(t, SB_TQ)
    tk = _tile(tq, SB_TK)
    nd = tq // tk
    scale = LANES ** -0.5

    def body(q_ref, g_ref, k_ref, v_ref, o_ref, tl_ref, dy_ref, dq_ref, dg_ref, dk_ref, dv_ref,
             qbf_ref, dobf_ref, qt_ref, dot_ref, acc_ref, left_ref, rune_ref, z_ref, dw_ref, wp_ref, dzp_ref):
        qi = pl.program_id(1)

        @pl.when(qi == 0)
        def _():
            dk_ref[...] = jnp.zeros_like(dk_ref)
            dv_ref[...] = jnp.zeros_like(dv_ref)

        qbf_ref[...] = q_ref[...].astype(BF16)
        qt_ref[...] = q_ref[...].T.astype(BF16)
        gv = g_ref[...]
        sg = _sigmoid(gv)
        dy = dy_ref[...]
        do = dy * (gv * sg)
        dobf_ref[...] = do.astype(BF16)
        dot_ref[...] = do.T.astype(BF16)
        dg_ref[...] = (dy * o_ref[...] * (sg * (1.0 + gv * (1.0 - sg)))).astype(BF16)
        tri = _tri_ones("le", tk)
        acc_ref[...] = jnp.zeros_like(acc_ref)
        left_ref[...] = tl_ref[:, 0:1]
        rune_ref[...] = jnp.zeros_like(rune_ref)
        wp_ref[...] = jnp.zeros_like(wp_ref)
        dzp_ref[...] = jnp.zeros_like(dzp_ref)
        all_rows = [slice(s0, s0 + tk) for s0 in range(0, tq, tk)]
        cat = lambda parts: parts[0] if len(parts) == 1 else jnp.concatenate(parts, axis=0)

        def grads(zs, dws, groups):
            sps = [_sb_softplus(z, dg) for z, (_, dg) in zip(zs, groups)]
            cums = [_split_dot(sp, tri) for sp, _ in sps]
            ws, es, lbs = [], [], []
            for z, (rows, dg), (sp, mask), cum, dw in zip(zs, groups, sps, cums, dws):
                left = left_ref[rows, :]
                lb = z - sp
                w = jnp.exp(lb - (left - cum))
                if dg:
                    w = jnp.where(mask, w, 0.0)
                left_ref[rows, :] = left - cum[:, tk - 1:tk]
                ws.append(w.astype(BF16))
                es.append(dw * w)
                lbs.append(lb)
            cumes = [_split_dot(e, tri) for e in es]
            dzs = []
            for (rows, dg), (_, mask), lb, e, cume in zip(groups, sps, lbs, es, cumes):
                rune = rune_ref[rows, :]
                dz = (e - jnp.exp(lb) * (rune + cume)) * scale
                if dg:
                    dz = jnp.where(mask, dz, 0.0)
                rune_ref[rows, :] = rune + cume[:, tk - 1:tk]
                dzs.append(dz.astype(BF16))
            return dzs, ws

        n = qi * nd

        def logits_into(slot, kb):
            k0 = pl.multiple_of(kb * tk, tk)
            k = k_ref[pl.ds(k0, tk), :]
            v = v_ref[pl.ds(k0, tk), :]
            for rows in all_rows:
                z_ref[slot, rows, :] = lax.dot_general(qbf_ref[rows, :], k, _NT, preferred_element_type=F32) * scale
                dw_ref[slot, rows, :] = lax.dot_general(dobf_ref[rows, :], v, _NT, preferred_element_type=F32)

        def apply_stored(kb):
            k0 = pl.multiple_of(kb * tk, tk)
            k = k_ref[pl.ds(k0, tk), :]
            for rows in all_rows:
                acc_ref[rows, :] += jnp.dot(dzp_ref[rows, :], k, preferred_element_type=F32)
            dk_ref[:, pl.ds(k0, tk)] += jnp.dot(qt_ref[...], dzp_ref[...], preferred_element_type=F32)
            dv_ref[:, pl.ds(k0, tk)] += jnp.dot(dot_ref[...], wp_ref[...], preferred_element_type=F32)

        @pl.when(n > 0)
        def _():
            logits_into(0, 0)

        def half_step(j, slot):
            apply_stored(jnp.maximum(j - 1, 0))
            logits_into(1 - slot, jnp.minimum(j + 1, n - 1))
            groups = [(rows, False) for rows in all_rows]
            dzs, ws = grads([z_ref[slot, rows, :] for rows in all_rows], [dw_ref[slot, rows, :] for rows in all_rows],
                            groups)
            for rows, dz, w in zip(all_rows, dzs, ws):
                dzp_ref[rows, :] = dz
                wp_ref[rows, :] = w

        def step(i, carry):
            half_step(2 * i, 0)
            half_step(2 * i + 1, 1)
            return carry

        lax.fori_loop(0, n // 2, step, 0)
        apply_stored(jnp.maximum(n - 1, 0))

        for u in range(nd):
            groups = [(slice(s0, s0 + tk), s0 == u * tk) for s0 in range(u * tk, tq, tk)]
            k0 = pl.multiple_of(qi * tq + u * tk, tk)
            k = k_ref[pl.ds(k0, tk), :]
            v = v_ref[pl.ds(k0, tk), :]
            zs = [lax.dot_general(qbf_ref[rows, :], k, _NT, preferred_element_type=F32) * scale for rows, _ in groups]
            dws = [lax.dot_general(dobf_ref[rows, :], v, _NT, preferred_element_type=F32) for rows, _ in groups]
            dzs, ws = grads(zs, dws, groups)
            for (rows, _), dz in zip(groups, dzs):
                acc_ref[rows, :] += jnp.dot(dz, k, preferred_element_type=F32)
            rows_all = slice(u * tk, tq)
            dk_ref[:, pl.ds(k0, tk)] += jnp.dot(qt_ref[:, rows_all], cat(dzs), preferred_element_type=F32)
            dv_ref[:, pl.ds(k0, tk)] += jnp.dot(dot_ref[:, rows_all], cat(ws), preferred_element_type=F32)
        dq_ref[...] = acc_ref[...].astype(BF16)

    blk = lambda h, i: (i, h)
    whole = lambda h, i: (0, h)
    return pl.pallas_call(
        body, name=name, grid=(nh, t // tq),
        in_specs=[pl.BlockSpec((tq, LANES), blk), pl.BlockSpec((tq, LANES), lambda h, i: (i, nh + h)),
                  pl.BlockSpec((t, LANES), whole), pl.BlockSpec((t, LANES), lambda h, i: (0, nh + h)),
                  pl.BlockSpec((tq, LANES), blk), pl.BlockSpec((tq, LANES), blk), pl.BlockSpec((tq, LANES), blk)],
        out_specs=[pl.BlockSpec((tq, LANES), blk), pl.BlockSpec((tq, LANES), blk),
                   pl.BlockSpec((LANES, t), lambda h, i: (h, 0)), pl.BlockSpec((LANES, t), lambda h, i: (h, 0))],
        out_shape=[jax.ShapeDtypeStruct((t, ds), BF16), jax.ShapeDtypeStruct((t, ds), BF16),
                   jax.ShapeDtypeStruct((ds, t), F32), jax.ShapeDtypeStruct((ds, t), F32)],
        scratch_shapes=[pltpu.VMEM((tq, LANES), BF16), pltpu.VMEM((tq, LANES), BF16),
                        pltpu.VMEM((LANES, tq), BF16), pltpu.VMEM((LANES, tq), BF16), pltpu.VMEM((tq, LANES), F32),
                        pltpu.VMEM((tq, 1), F32), pltpu.VMEM((tq, 1), F32),
                        pltpu.VMEM((2, tq, tk), F32), pltpu.VMEM((2, tq, tk), F32),
                        pltpu.VMEM((tq, tk), BF16), pltpu.VMEM((tq, tk), BF16)],
        compiler_params=_cparams(2),
    )(proj, proj, kv, kv, o_all, tl_all, dmix)


def _place():
    x, y, c = lax.axis_index("x"), lax.axis_index("y"), lax.axis_index("c")
    chips = [(1 - x, y), (x, 1 - y), (1 - x, 1 - y)]
    return x, y, c, chips


def _remote(src, dst, send_sems, recv_sems, k, to):
    return pltpu.make_async_remote_copy(src_ref=src, dst_ref=dst, send_sem=send_sems.at[k], recv_sem=recv_sems.at[k],
                                        device_id=to, device_id_type=MESH)


def _my_chip():
    return 2 * lax.axis_index("x") + lax.axis_index("y")


def _place_own(shard, name):
    r, w = shard.shape
    tr = _tile(r, FLAT_TR)

    def body(x_ref, o_ref):
        o_ref[...] = x_ref[...]

    return pl.pallas_call(
        body, name=name, out_shape=jax.ShapeDtypeStruct((N_CHIPS, r, w), shard.dtype), grid=(r // tr,),
        in_specs=[pl.BlockSpec((tr, w), lambda i: (i, 0))],
        out_specs=pl.BlockSpec((None, tr, w), lambda i: (_my_chip(), i, 0)),
        compiler_params=_cparams(1),
    )(shard)


def _chip_all_gather(shard, name):
    r, w = shard.shape
    rh = r // 2

    def body(x_ref, buf_ref, out_ref, send_sems, recv_sems):
        x, y, c, chips = _place()
        me = 2 * x + y
        sibling = (x, y, 1 - c)
        mine_rows = pl.ds(c * rh, rh)
        other_rows = pl.ds((1 - c) * rh, rh)
        first = [_remote(x_ref.at[mine_rows], out_ref.at[me, mine_rows], send_sems, recv_sems, k, (cx, cy, c))
                 for k, (cx, cy) in enumerate(chips)]
        for cp in first:
            cp.start()
        passed = []
        for k, (cx, cy) in enumerate(chips):
            got = out_ref.at[2 * cx + cy, mine_rows]
            _remote(got, got, send_sems, recv_sems, k, (cx, cy, c)).wait_recv()
            fwd = _remote(got, got, send_sems, recv_sems, 3 + k, sibling)
            fwd.start()
            passed.append(fwd)
        for k, (cx, cy) in enumerate(chips):
            got = out_ref.at[2 * cx + cy, other_rows]
            _remote(got, got, send_sems, recv_sems, 3 + k, sibling).wait_recv()
        for cp in first + passed:
            cp.wait_send()

    return pl.pallas_call(
        body, name=name, in_specs=[ANY, ANY], out_specs=ANY,
        out_shape=jax.ShapeDtypeStruct((N_CHIPS, r, w), shard.dtype),
        input_output_aliases={1: 0},
        scratch_shapes=[pltpu.SemaphoreType.DMA((6,)), pltpu.SemaphoreType.DMA((6,))],
    )(shard, _place_own(shard, name + "_own"))


def _sibling_take_half(ss, name):
    n = len(ss)

    def body(*refs):
        s_refs, a_refs, send_sems, recv_sems = refs[:n], refs[n:2 * n], refs[2 * n], refs[2 * n + 1]
        x, y, c, _ = _place()
        cps = []
        for t in range(n):
            rh = s_refs[t].shape[1] // 2
            cps.append(_remote(s_refs[t].at[:, pl.ds((1 - c) * rh, rh), :], a_refs[t], send_sems, recv_sems, t,
                               (x, y, 1 - c)))
        for cp in cps:
            cp.start()
        for cp in cps:
            cp.wait()

    return pl.pallas_call(
        body, name=name, in_specs=[ANY] * n, out_specs=[ANY] * n,
        out_shape=[jax.ShapeDtypeStruct((s.shape[0], s.shape[1] // 2, s.shape[2]), s.dtype) for s in ss],
        scratch_shapes=[pltpu.SemaphoreType.DMA((n,)), pltpu.SemaphoreType.DMA((n,))],
    )(*ss)


def _pair_sum(s, a, dtype, name):
    n, r, w = s.shape
    rh = r // 2
    tr = _tile(rh, FLAT_TR)
    nblk = rh // tr

    def body(s_ref, a_ref, o_ref):
        o_ref[...] = (s_ref[...] + a_ref[...]).astype(dtype)

    return pl.pallas_call(
        body, name=name, out_shape=jax.ShapeDtypeStruct((n, rh, w), dtype), grid=(n, nblk),
        in_specs=[pl.BlockSpec((None, tr, w), lambda k, i: (k, lax.axis_index("c") * nblk + i, 0)),
                  pl.BlockSpec((None, tr, w), lambda k, i: (k, i, 0))],
        out_specs=pl.BlockSpec((None, tr, w), lambda k, i: (k, i, 0)),
        compiler_params=_cparams(2),
    )(s, a)


def _chip_scatter(ps, name):
    n = len(ps)

    def body(*refs):
        p_refs, b_refs, send_sems, recv_sems = refs[:n], refs[n:2 * n], refs[2 * n], refs[2 * n + 1]
        x, y, c, chips = _place()
        me = 2 * x + y
        sends = [_remote(p_refs[t].at[2 * cx + cy], b_refs[t].at[me], send_sems, recv_sems, 3 * t + k, (cx, cy, c))
                 for t in range(n) for k, (cx, cy) in enumerate(chips)]
        for cp in sends:
            cp.start()
        for t in range(n):
            for k, (cx, cy) in enumerate(chips):
                got = b_refs[t].at[2 * cx + cy]
                _remote(got, got, send_sems, recv_sems, 3 * t + k, (cx, cy, c)).wait_recv()
        for cp in sends:
            cp.wait_send()

    return pl.pallas_call(
        body, name=name, in_specs=[ANY] * n, out_specs=[ANY] * n,
        out_shape=[jax.ShapeDtypeStruct(p.shape, p.dtype) for p in ps],
        scratch_shapes=[pltpu.SemaphoreType.DMA((3 * n,)), pltpu.SemaphoreType.DMA((3 * n,))],
    )(*ps)


def _chip_sum(p, b, name):
    n, rh, w = p.shape
    tr = _tile(rh, FLAT_TR)
    nblk = rh // tr

    def body(p_ref, b0_ref, b1_ref, b2_ref, b3_ref, o_ref):
        me = _my_chip()
        own = p_ref[...]
        t = [jnp.where(me == k, own, b_ref[...]).astype(F32) for k, b_ref in enumerate((b0_ref, b1_ref, b2_ref, b3_ref))]
        o_ref[...] = ((t[0] + t[1]) + t[2]) + t[3]

    def other(k):
        return lambda i: (jnp.where(_my_chip() == k, (k + 1) % N_CHIPS, k), i, 0)

    return pl.pallas_call(
        body, name=name, out_shape=jax.ShapeDtypeStruct((2 * rh, w), F32), grid=(nblk,),
        in_specs=[pl.BlockSpec((None, tr, w), lambda i: (_my_chip(), i, 0))]
        + [pl.BlockSpec((None, tr, w), other(k)) for k in range(N_CHIPS)],
        out_specs=pl.BlockSpec((tr, w), lambda i: (lax.axis_index("c") * nblk + i, 0)),
        compiler_params=_cparams(1),
    )(p, b, b, b, b)


def _sibling_join(gs, name):
    n = len(gs)

    def body(*refs):
        g_refs, send_sems, recv_sems = refs[n:2 * n], refs[2 * n], refs[2 * n + 1]
        x, y, c, _ = _place()
        cps = []
        for t in range(n):
            rh = g_refs[t].shape[0] // 2
            mine = g_refs[t].at[pl.ds(c * rh, rh)]
            cps.append(_remote(mine, mine, send_sems, recv_sems, t, (x, y, 1 - c)))
        for cp in cps:
            cp.start()
        for t in range(n):
            rh = g_refs[t].shape[0] // 2
            theirs = g_refs[t].at[pl.ds((1 - c) * rh, rh)]
            _remote(theirs, theirs, send_sems, recv_sems, t, (x, y, 1 - c)).wait_recv()
        for cp in cps:
            cp.wait_send()

    return pl.pallas_call(
        body, name=name, in_specs=[ANY] * n, out_specs=[ANY] * n,
        out_shape=[jax.ShapeDtypeStruct(g.shape, g.dtype) for g in gs],
        input_output_aliases={t: t for t in range(n)},
        scratch_shapes=[pltpu.SemaphoreType.DMA((n,)), pltpu.SemaphoreType.DMA((n,))],
    )(*gs)


def _adamw(g, w, m, v, name):
    r, wd = g.shape

    def body(g_ref, w_ref, m_ref, v_ref, d_ref, mo_ref, vo_ref):
        gv = g_ref[...]
        mn = ADAM_B1 * m_ref[...] + (1.0 - ADAM_B1) * gv
        vn = ADAM_B2 * v_ref[...] + (1.0 - ADAM_B2) * (gv * gv)
        m_hat = mn / (1.0 - ADAM_B1 ** ADAM_STEP)
        v_hat = vn / (1.0 - ADAM_B2 ** ADAM_STEP)
        d_ref[...] = -ADAM_LR * (m_hat / (jnp.sqrt(v_hat) + ADAM_EPS) + ADAM_WD * w_ref[...])
        mo_ref[...] = mn
        vo_ref[...] = vn

    tr = _tile(r, FLAT_TR)
    row = lambda i: (i, 0)
    spec = pl.BlockSpec((tr, wd), row)
    return pl.pallas_call(
        body, name=name, grid=(r // tr,), in_specs=[spec] * 4, out_specs=[spec] * 3,
        out_shape=[jax.ShapeDtypeStruct((r, wd), F32)] * 3,
        compiler_params=_cparams(1),
    )(g, w, m, v)


def _reduce_to_shards(ss, names):
    a = _sibling_take_half(ss, "grad_sibling_half")
    p = [_pair_sum(s, a_t, BF16 if n in BIG else F32, f"grad_pair_sum_{n}") for s, a_t, n in zip(ss, a, names)]
    b = _chip_scatter(p, "grad_chip_scatter")
    g = [_chip_sum(p_t, b_t, f"grad_chip_sum_{n}") for p_t, b_t, n in zip(p, b, names)]
    return _sibling_join(g, "grad_sibling_join")


WEIGHTS = ("mem_norm", "w_mem_kv", "norm_a", "w_in_a", "conv_w", "conv_b", "w_rec_gate", "b_rec_gate", "w_in_gate",
           "b_in_gate", "lru_lambda", "w_out_a", "kv_norm", "w_kv", "norm_b", "w_in_b", "w_out_b", "final_norm")
SHARD_DIM = {"mem_norm": None, "w_mem_kv": 1, "norm_a": 1, "w_in_a": 2, "conv_w": 2, "conv_b": 1, "w_rec_gate": None,
             "b_rec_gate": 1, "w_in_gate": None, "b_in_gate": 1, "lru_lambda": 1, "w_out_a": 1, "kv_norm": None,
             "w_kv": 1, "norm_b": None, "w_in_b": 2, "w_out_b": 1, "final_norm": None}
BIG = ("w_mem_kv", "w_in_a", "w_out_a", "w_kv", "w_in_b", "w_out_b")
SMALL = ("norm_a", "conv_w", "conv_b", "b_rec_gate", "b_in_gate", "lru_lambda")


def _pad_rows(flat, row_multiple):
    per = FLAT_W * row_multiple
    n = flat.shape[0]
    total = -(-n // per) * per
    return jnp.pad(flat, (0, total - n)).reshape(total // FLAT_W, FLAT_W)


def _flatten(parts, row_multiple):
    return _pad_rows(jnp.concatenate([p.reshape(-1) for p in parts]), row_multiple)


def _unflatten(flat2d, shapes):
    flat = flat2d.reshape(-1)
    out, off = [], 0
    for shp in shapes:
        n = 1
        for s in shp:
            n *= s
        out.append(flat[off:off + n].reshape(shp))
        off += n
    return out


def _gather_weights(local):
    parts = [local[n].astype(BF16) for n in BIG]
    parts += [lax.bitcast_convert_type(local[n], BF16) for n in SMALL]
    shapes = [p.shape for p in parts]
    gathered = _chip_all_gather(_flatten(parts, 2 * FLAT_TR), "weights_all_gather")
    per_chip = [_unflatten(gathered[k], shapes) for k in range(N_CHIPS)]
    full = {}
    for idx, n in enumerate(BIG + SMALL):
        pieces = [per_chip[k][idx] for k in range(N_CHIPS)]
        if n in SMALL:
            pieces = [lax.bitcast_convert_type(p, F32) for p in pieces]
        full[n] = jnp.concatenate(pieces, axis=SHARD_DIM[n])
    return full


def _piece(g, name, k):
    dim = SHARD_DIM[name]
    if dim is None:
        return g
    n = g.shape[dim] // N_CHIPS
    return lax.slice_in_dim(g, k * n, (k + 1) * n, axis=dim)


def _local_grads(x, mem, tgt, wts):
    t, d = x.shape
    depth = wts["w_mem_kv"].shape[0]
    n_a = wts["w_in_a"].shape[0]
    n_b = wts["w_in_b"].shape[0]
    nb = wts["w_rec_gate"].shape[1]
    dr = nb * LANES
    dm = wts["w_mem_kv"].shape[2] // 2
    row = lambda v: v.reshape(1, -1)

    wm_all = jnp.concatenate([wts["w_mem_kv"][l] for l in range(depth)], axis=1)
    memkv, memn_bf = _norm_matmul(mem, row(wts["mem_norm"]), wm_all, BF16, "mem_kv_proj")

    h = x
    saved = []
    vecs = []
    for l in range(n_a):
        proj, u_bf = _norm_matmul(h, row(wts["norm_a"][l]), wts["w_in_a"][l], F32, f"a{l}_in_proj")
        vec = jnp.concatenate([row(wts["conv_b"][l]), row(wts["b_rec_gate"][l]), row(wts["b_in_gate"][l]),
                               row(wts["lru_lambda"][l]), wts["conv_w"][l]], axis=0)
        vecs.append(vec)
        y_rnn, h_rnn, xc = _lru_fwd(proj, vec, wts["w_rec_gate"][l], wts["w_in_gate"][l], f"a{l}_lru_fwd")
        y_mem = _mem_attn_fwd(proj, memkv, l, dr, dm, f"a{l}_mem_fwd")
        mix = jnp.concatenate([y_rnn, y_mem], axis=1)
        h_next = _matmul_res(mix, wts["w_out_a"][l], h, f"a{l}_out_proj")
        saved.append((h, proj, u_bf, mix, h_rnn, xc))
        h = h_next

    h_kv = h
    kv, ukv_bf = _norm_matmul(h_kv, row(wts["kv_norm"]), wts["w_kv"], BF16, "kv_proj")

    for j in range(n_b):
        l = n_a + j
        proj, u_bf = _norm_matmul(h, row(wts["norm_b"][j]), wts["w_in_b"][j], F32, f"b{j}_in_proj")
        y_sb, o_sb, tl_sb = _sb_fwd(proj, kv, f"b{j}_sb_fwd")
        y_mem = _mem_attn_fwd(proj, memkv, l, dr, dm, f"b{j}_mem_fwd")
        mix = jnp.concatenate([y_sb, y_mem], axis=1)
        h_next = _matmul_res(mix, wts["w_out_b"][j], h, f"b{j}_out_proj")
        saved.append((h, proj, u_bf, mix, o_sb, tl_sb))
        h = h_next

    dh, d_final, loss = _final_loss_bwd(h, row(wts["final_norm"]), tgt, "final_loss_bwd")

    grads = {"final_norm": d_final.reshape(-1)}
    big = {}
    dmemkv = [None] * depth
    g_norm_b = [None] * n_b
    dks, dvs = [], []
    for j in reversed(range(n_b)):
        l = n_a + j
        h_in, proj, u_bf, mix, o_sb, tl_sb = saved[l]
        dmix = _matmul_nt(dh, wts["w_out_b"][j], f"b{j}_dmix")
        big["w_out_b"] = _dw_rows(mix, dh, j, n_b, big.get("w_out_b"), f"b{j}_dw_out")
        dq, dg, dk, dv = _sb_bwd(proj, kv, o_sb, tl_sb, dmix, f"b{j}_sb_bwd")
        dqm, dgm, dkm, dvm = _mem_attn_bwd(proj, memkv, dmix, l, dr, dm, f"b{j}_mem_bwd")
        dmemkv[l] = (dkm, dvm)
        dproj = jnp.concatenate([dq, dg, dqm, dgm], axis=1)
        du = _matmul_nt(dproj, wts["w_in_b"][j], f"b{j}_du")
        big["w_in_b"] = _dw_cols(u_bf, dproj, j, n_b, big.get("w_in_b"), f"b{j}_dw_in")
        dh, dgn = _rms_bwd(du, h_in, row(wts["norm_b"][j]), dh, f"b{j}_rms_bwd")
        g_norm_b[j] = dgn.reshape(-1)
        dks.append(dk)
        dvs.append(dv)
    assert n_b == 2
    dkv = jnp.concatenate([_add_cast(dks[0], dks[1], "dk_sum").T, _add_cast(dvs[0], dvs[1], "dv_sum").T], axis=1)
    du = _matmul_nt(dkv, wts["w_kv"], "kv_du")
    big["w_kv"] = _dw_cols(ukv_bf, dkv, 0, 1, None, "kv_dw")
    dh, dgn = _rms_bwd(du, h_kv, row(wts["kv_norm"]), dh, "kv_rms_bwd")
    grads["kv_norm"] = dgn.reshape(-1)

    g_norm_a = [None] * n_a
    g_wr, g_wi, g_vec = [None] * n_a, [None] * n_a, [None] * n_a
    for l in reversed(range(n_a)):
        h_in, proj, u_bf, mix, h_rnn, xc = saved[l]
        dmix = _matmul_nt(dh, wts["w_out_a"][l], f"a{l}_dmix")
        big["w_out_a"] = _dw_rows(mix, dh, l, n_a, big.get("w_out_a"), f"a{l}_dw_out")
        dx, dg, g_wr[l], g_wi[l], dvec = _lru_bwd(proj, xc, h_rnn, dmix, vecs[l], wts["w_rec_gate"][l],
                                                  wts["w_in_gate"][l], f"a{l}_lru_bwd")
        g_vec[l] = dvec.transpose(1, 0, 2).reshape(8, dr)
        dqm, dgm, dkm, dvm = _mem_attn_bwd(proj, memkv, dmix, l, dr, dm, f"a{l}_mem_bwd")
        dmemkv[l] = (dkm, dvm)
        dproj = jnp.concatenate([dx, dg, dqm, dgm], axis=1)
        du = _matmul_nt(dproj, wts["w_in_a"][l], f"a{l}_du")
        big["w_in_a"] = _dw_cols(u_bf, dproj, l, n_a, big.get("w_in_a"), f"a{l}_dw_in")
        dh, dgn = _rms_bwd(du, h_in, row(wts["norm_a"][l]), dh, f"a{l}_rms_bwd")
        g_norm_a[l] = dgn.reshape(-1)

    dmemkv_all = jnp.concatenate([jnp.concatenate(p, axis=1) for p in dmemkv], axis=1).astype(BF16)
    pk = d // N_CHIPS
    big["w_mem_kv"] = _matmul_tn(memn_bf, dmemkv_all, "mem_dw", pk, 2 * dm, (N_CHIPS, depth, pk, 2 * dm),
                                 (None, None, pk, 2 * dm), lambda i, j: (i, j, 0, 0))
    dmemn = _matmul_nt(dmemkv_all, wm_all, "mem_du")
    _, dgn = _rms_bwd(dmemn, mem, row(wts["mem_norm"]), jnp.zeros_like(mem), "mem_rms_bwd")
    grads["mem_norm"] = dgn.reshape(-1)
    grads["norm_a"] = jnp.stack(g_norm_a)
    grads["w_rec_gate"] = jnp.stack(g_wr)
    grads["w_in_gate"] = jnp.stack(g_wi)
    gv = jnp.stack(g_vec)
    grads["conv_b"], grads["b_rec_gate"], grads["b_in_gate"], grads["lru_lambda"] = gv[:, 0], gv[:, 1], gv[:, 2], gv[:, 3]
    grads["conv_w"] = gv[:, 4:8]
    grads["norm_b"] = jnp.stack(g_norm_b)
    big = {n: g.reshape(N_CHIPS, -1, g.shape[-1]) for n, g in big.items()}
    return loss, dh, grads, big


def kernel(x, mem, mem_norm, w_mem_kv, norm_a, w_in_a, conv_w, conv_b, w_rec_gate, b_rec_gate, w_in_gate, b_in_gate, lru_lambda, w_out_a, kv_norm, w_kv, norm_b, w_in_b, w_out_b, final_norm, loss_target, m_mem_norm, m_w_mem_kv, m_norm_a, m_w_in_a, m_conv_w, m_conv_b, m_w_rec_gate, m_b_rec_gate, m_w_in_gate, m_b_in_gate, m_lru_lambda, m_w_out_a, m_kv_norm, m_w_kv, m_norm_b, m_w_in_b, m_w_out_b, m_final_norm, v_mem_norm, v_w_mem_kv, v_norm_a, v_w_in_a, v_conv_w, v_conv_b, v_w_rec_gate, v_b_rec_gate, v_w_in_gate, v_b_in_gate, v_lru_lambda, v_w_out_a, v_kv_norm, v_w_kv, v_norm_b, v_w_in_b, v_w_out_b, v_final_norm):
    local = dict(mem_norm=mem_norm, w_mem_kv=w_mem_kv, norm_a=norm_a, w_in_a=w_in_a, conv_w=conv_w, conv_b=conv_b,
                 w_rec_gate=w_rec_gate, b_rec_gate=b_rec_gate, w_in_gate=w_in_gate, b_in_gate=b_in_gate,
                 lru_lambda=lru_lambda, w_out_a=w_out_a, kv_norm=kv_norm, w_kv=w_kv, norm_b=norm_b, w_in_b=w_in_b,
                 w_out_b=w_out_b, final_norm=final_norm)
    mom = dict(mem_norm=m_mem_norm, w_mem_kv=m_w_mem_kv, norm_a=m_norm_a, w_in_a=m_w_in_a, conv_w=m_conv_w,
               conv_b=m_conv_b, w_rec_gate=m_w_rec_gate, b_rec_gate=m_b_rec_gate, w_in_gate=m_w_in_gate,
               b_in_gate=m_b_in_gate, lru_lambda=m_lru_lambda, w_out_a=m_w_out_a, kv_norm=m_kv_norm, w_kv=m_w_kv,
               norm_b=m_norm_b, w_in_b=m_w_in_b, w_out_b=m_w_out_b, final_norm=m_final_norm)
    var = dict(mem_norm=v_mem_norm, w_mem_kv=v_w_mem_kv, norm_a=v_norm_a, w_in_a=v_w_in_a, conv_w=v_conv_w,
               conv_b=v_conv_b, w_rec_gate=v_w_rec_gate, b_rec_gate=v_b_rec_gate, w_in_gate=v_w_in_gate,
               b_in_gate=v_b_in_gate, lru_lambda=v_lru_lambda, w_out_a=v_w_out_a, kv_norm=v_kv_norm, w_kv=v_w_kv,
               norm_b=v_norm_b, w_in_b=v_w_in_b, w_out_b=v_w_out_b, final_norm=v_final_norm)

    wts = _gather_weights(local)
    for n in WEIGHTS:
        if SHARD_DIM[n] is None:
            wts[n] = local[n]
    wts["w_rec_gate"] = wts["w_rec_gate"].astype(BF16)
    wts["w_in_gate"] = wts["w_in_gate"].astype(BF16)

    loss, grad_x, grads, big = _local_grads(x[0], mem[0], loss_target[0], wts)

    rest = [n for n in WEIGHTS if n not in BIG]
    row_multiple = 2 * FLAT_TR
    s_rest = jnp.stack([_flatten([_piece(grads[n], n, k) for n in rest], row_multiple) for k in range(N_CHIPS)])
    reduced = _reduce_to_shards([big[n] for n in BIG] + [s_rest], list(BIG) + ["rest"])

    g_out, d_out, m_out, v_out = {}, {}, {}, {}
    for n, g in zip(BIG, reduced):
        shape = local[n].shape
        flat = lambda a: a.reshape(-1, shape[-1])
        d, mo, vo = _adamw(g, flat(local[n]), flat(mom[n]), flat(var[n]), f"adamw_{n}")
        g_out[n], d_out[n], m_out[n], v_out[n] = (a.reshape(shape) for a in (g, d, mo, vo))
    g_rest = reduced[-1]
    d_rest, m_rest, v_rest = _adamw(g_rest, *(_flatten([src[n] for n in rest], row_multiple) for src in (local, mom, var)),
                                    "adamw_rest")
    shapes = [local[n].shape for n in rest]
    for out, flat2d in ((g_out, g_rest), (d_out, d_rest), (m_out, m_rest), (v_out, v_rest)):
        out.update(zip(rest, _unflatten(flat2d, shapes)))

    total_loss = lax.psum(loss[0, 0], MESH_AXES)
    return (total_loss, grad_x[None], *[g_out[n] for n in WEIGHTS], *[d_out[n] for n in WEIGHTS],
            *[m_out[n] for n in WEIGHTS], *[v_out[n] for n in WEIGHTS])
```

```python
import functools

import jax
import jax.numpy as jnp
from jax import lax
from jax.experimental import pallas as pl
from jax.experimental.pallas import tpu as pltpu

F32 = jnp.float32
BF16 = jnp.bfloat16

RMS_EPS = 1e-6
LRU_C = 8.0
ADAM_LR = 0.001
ADAM_B1 = 0.9
ADAM_B2 = 0.999
ADAM_EPS = 1e-08
ADAM_WD = 0.01
ADAM_STEP = 10

LANES = 128
VMEM_LIMIT = 56 * 1024 * 1024
FLAT_W = 1024
FLAT_TR = 256
N_CHIPS = 4
MESH_AXES = ("x", "y", "c")

_NT = (((1,), (1,)), ((), ()))
_TN = (((0,), (0,)), ((), ()))
ANY = pl.BlockSpec(memory_space=pl.ANY)
MESH = pl.DeviceIdType.MESH


def _cparams(n_axes):
    return pltpu.CompilerParams(dimension_semantics=("arbitrary",) * n_axes, vmem_limit_bytes=VMEM_LIMIT)


def _sigmoid(x):
    return 1.0 / (1.0 + jnp.exp(-x))


def _log1p_pos(e):
    return jnp.where(e < 1e-3, e * (1.0 - e * (0.5 - e * (1.0 / 3.0))), jnp.log(1.0 + e))


def _neg_expm1(x):
    small = -x * (1.0 + x * (0.5 + x * (1.0 / 6.0 + x * (1.0 / 24.0))))
    return jnp.where(x > -0.05, small, 1.0 - jnp.exp(x))


def _tile(n, want):
    if n <= want:
        return n
    t = want
    while n % t:
        t -= LANES
    assert t > 0, (n, want)
    return t


def _norm_matmul(x, g, w, out_dtype, name):
    m, k = x.shape
    n = w.shape[1]
    tm, tn = _tile(m, 1024), _tile(n, 1024)

    def body(x_ref, g_ref, w_ref, o_ref, u_ref):
        @pl.when(pl.program_id(1) == 0)
        def _():
            xf = x_ref[...]
            r = lax.rsqrt(jnp.mean(xf * xf, axis=-1, keepdims=True) + RMS_EPS)
            u_ref[...] = ((xf * r) * g_ref[...]).astype(BF16)

        o_ref[...] = jnp.dot(u_ref[...], w_ref[...], preferred_element_type=F32).astype(o_ref.dtype)

    return pl.pallas_call(
        body, name=name, grid=(m // tm, n // tn),
        in_specs=[pl.BlockSpec((tm, k), lambda i, j: (i, 0)), pl.BlockSpec((1, k), lambda i, j: (0, 0)),
                  pl.BlockSpec((k, tn), lambda i, j: (0, j))],
        out_specs=[pl.BlockSpec((tm, tn), lambda i, j: (i, j)), pl.BlockSpec((tm, k), lambda i, j: (i, 0))],
        out_shape=[jax.ShapeDtypeStruct((m, n), out_dtype), jax.ShapeDtypeStruct((m, k), BF16)],
        compiler_params=_cparams(2),
    )(x, g, w)


def _matmul_res(a, b, res, name):
    m, k = a.shape
    n = b.shape[1]
    tm, tn = _tile(m, 1024), _tile(n, 1024)

    def body(a_ref, b_ref, r_ref, o_ref):
        o_ref[...] = r_ref[...] + jnp.dot(a_ref[...], b_ref[...], preferred_element_type=F32)

    return pl.pallas_call(
        body, name=name, grid=(m // tm, n // tn),
        in_specs=[pl.BlockSpec((tm, k), lambda i, j: (i, 0)), pl.BlockSpec((k, tn), lambda i, j: (0, j)),
                  pl.BlockSpec((tm, tn), lambda i, j: (i, j))],
        out_specs=pl.BlockSpec((tm, tn), lambda i, j: (i, j)),
        out_shape=jax.ShapeDtypeStruct((m, n), F32),
        compiler_params=_cparams(2),
    )(a, b, res)


def _matmul_nt(a, b, name):
    m, n = a.shape
    k = b.shape[0]
    tm, tk = _tile(m, 1024), _tile(k, 512)

    def body(a_ref, b_ref, o_ref):
        o_ref[...] = lax.dot_general(a_ref[...].astype(BF16), b_ref[...], _NT, preferred_element_type=F32)

    return pl.pallas_call(
        body, name=name, grid=(m // tm, k // tk),
        in_specs=[pl.BlockSpec((tm, n), lambda i, j: (i, 0)), pl.BlockSpec((tk, n), lambda i, j: (j, 0))],
        out_specs=pl.BlockSpec((tm, tk), lambda i, j: (i, j)),
        out_shape=jax.ShapeDtypeStruct((m, k), F32),
        compiler_params=_cparams(2),
    )(a, b)


def _matmul_tn(a, b, name, tk, tn, out_shape, out_block, out_index, into=None):
    m, k = a.shape
    n = b.shape[1]
    tm = _tile(m, 2048 if b.dtype == BF16 else 1024)

    def body(a_ref, b_ref, *rest):
        o_ref = rest[-1]
        part = lax.dot_general(a_ref[...].astype(BF16), b_ref[...].astype(BF16), _TN, preferred_element_type=F32)

        @pl.when(pl.program_id(2) == 0)
        def _():
            o_ref[...] = part

        @pl.when(pl.program_id(2) != 0)
        def _():
            o_ref[...] += part

    in_specs = [pl.BlockSpec((tm, tk), lambda i, j, s: (s, i)), pl.BlockSpec((tm, tn), lambda i, j, s: (s, j))]
    args = [a, b]
    if into is not None:
        in_specs.append(ANY)
        args.append(into)
    return pl.pallas_call(
        body, name=name, grid=(k // tk, n // tn, m // tm), in_specs=in_specs,
        out_specs=pl.BlockSpec(out_block, lambda i, j, s: out_index(i, j)),
        out_shape=jax.ShapeDtypeStruct(out_shape, F32),
        input_output_aliases={} if into is None else {2: 0},
        compiler_params=_cparams(3),
    )(*args)


def _dw_cols(a, b, layer, n_layers, into, name):
    k, n = a.shape[1], b.shape[1]
    pn = n // N_CHIPS
    tk = _tile(k, 512)
    return _matmul_tn(a, b, name, tk, pn, (N_CHIPS, n_layers, k, pn), (None, None, tk, pn),
                      lambda i, j: (j, layer, i, 0), into)


def _dw_rows(a, b, layer, n_layers, into, name):
    k, n = a.shape[1], b.shape[1]
    pk = k // N_CHIPS
    tn = _tile(n, 2048)
    return _matmul_tn(a, b, name, pk, tn, (N_CHIPS, n_layers, pk, n), (None, None, pk, tn),
                      lambda i, j: (i, layer, 0, j), into)


def _rms_bwd(du, h, g, dres, name):
    m, d = h.shape
    tm = _tile(m, 256)

    def body(du_ref, h_ref, g_ref, dres_ref, dx_ref, dg_ref):
        xf = h_ref[...]
        r = lax.rsqrt(jnp.mean(xf * xf, axis=-1, keepdims=True) + RMS_EPS)
        xhat = xf * r
        du_v = du_ref[...]
        dxn = du_v * g_ref[...]
        dx_ref[...] = dres_ref[...] + r * (dxn - xhat * jnp.mean(dxn * xhat, axis=-1, keepdims=True))
        part = jnp.sum(du_v * xhat, axis=0, keepdims=True)

        @pl.when(pl.program_id(0) == 0)
        def _():
            dg_ref[...] = part

        @pl.when(pl.program_id(0) != 0)
        def _():
            dg_ref[...] += part

    row = lambda i: (i, 0)
    return pl.pallas_call(
        body, name=name, grid=(m // tm,),
        in_specs=[pl.BlockSpec((tm, d), row), pl.BlockSpec((tm, d), row), pl.BlockSpec((1, d), lambda i: (0, 0)),
                  pl.BlockSpec((tm, d), row)],
        out_specs=[pl.BlockSpec((tm, d), row), pl.BlockSpec((1, d), lambda i: (0, 0))],
        out_shape=[jax.ShapeDtypeStruct((m, d), F32), jax.ShapeDtypeStruct((1, d), F32)],
        compiler_params=_cparams(1),
    )(du, h, g, dres)


def _final_loss_bwd(h, g, tgt, name):
    m, d = h.shape
    tm = _tile(m, 256)

    def body(h_ref, g_ref, t_ref, dx_ref, dg_ref, loss_ref):
        xf = h_ref[...]
        r = lax.rsqrt(jnp.mean(xf * xf, axis=-1, keepdims=True) + RMS_EPS)
        xhat = xf * r
        gv = g_ref[...]
        err = xhat * gv - t_ref[...]
        dy = err * (1.0 / d)
        dxn = dy * gv
        dx_ref[...] = r * (dxn - xhat * jnp.mean(dxn * xhat, axis=-1, keepdims=True))
        part = jnp.sum(dy * xhat, axis=0, keepdims=True)
        lpart = jnp.sum(jnp.sum(err * err, axis=0, keepdims=True), axis=1, keepdims=True) * (0.5 / d)

        @pl.when(pl.program_id(0) == 0)
        def _():
            dg_ref[...] = part
            loss_ref[...] = lpart

        @pl.when(pl.program_id(0) != 0)
        def _():
            dg_ref[...] += part
            loss_ref[...] += lpart

    row = lambda i: (i, 0)
    fixed = lambda i: (0, 0)
    return pl.pallas_call(
        body, name=name, grid=(m // tm,),
        in_specs=[pl.BlockSpec((tm, d), row), pl.BlockSpec((1, d), fixed), pl.BlockSpec((tm, d), row)],
        out_specs=[pl.BlockSpec((tm, d), row), pl.BlockSpec((1, d), fixed), pl.BlockSpec((1, 1), fixed)],
        out_shape=[jax.ShapeDtypeStruct((m, d), F32), jax.ShapeDtypeStruct((1, d), F32),
                   jax.ShapeDtypeStruct((1, 1), F32)],
        compiler_params=_cparams(1),
    )(h, g, tgt)


def _add_cast(a, b, name):
    m, n = a.shape
    tm, tn = _tile(m, 512), _tile(n, 2048)

    def body(a_ref, b_ref, o_ref):
        o_ref[...] = (a_ref[...] + b_ref[...]).astype(BF16)

    blk = lambda i, j: (i, j)
    return pl.pallas_call(
        body, name=name, grid=(m // tm, n // tn),
        in_specs=[pl.BlockSpec((tm, tn), blk), pl.BlockSpec((tm, tn), blk)],
        out_specs=pl.BlockSpec((tm, tn), blk),
        out_shape=jax.ShapeDtypeStruct((m, n), BF16),
        compiler_params=_cparams(2),
    )(a, b)


def _mem_attn_fwd(proj, memkv, layer, dr, dm, name):
    t = proj.shape[0]
    nm = memkv.shape[0]
    tm = _tile(t, 512)
    nh = dm // LANES
    scale = LANES ** -0.5
    qb = (2 * dr) // dm

    def body(q_ref, g_ref, k_ref, v_ref, y_ref):
        for hh in range(nh):
            sl = slice(hh * LANES, (hh + 1) * LANES)
            s = lax.dot_general(q_ref[:, sl].astype(BF16), k_ref[:, sl], _NT, preferred_element_type=F32) * scale
            p = jnp.exp(s - jnp.max(s, axis=-1, keepdims=True))
            p = p / jnp.sum(p, axis=-1, keepdims=True)
            o = jnp.dot(p.astype(BF16), v_ref[:, sl], preferred_element_type=F32)
            gv = g_ref[:, sl]
            y_ref[:, sl] = (o * (gv * _sigmoid(gv))).astype(BF16)

    return pl.pallas_call(
        body, name=name, grid=(t // tm,),
        in_specs=[pl.BlockSpec((tm, dm), lambda i: (i, qb)), pl.BlockSpec((tm, dm), lambda i: (i, qb + 1)),
                  pl.BlockSpec((nm, dm), lambda i: (0, 2 * layer)), pl.BlockSpec((nm, dm), lambda i: (0, 2 * layer + 1))],
        out_specs=pl.BlockSpec((tm, dm), lambda i: (i, 0)),
        out_shape=jax.ShapeDtypeStruct((t, dm), BF16),
        compiler_params=_cparams(1),
    )(proj, proj, memkv, memkv)


def _mem_attn_bwd(proj, memkv, dmix, layer, dr, dm, name):
    t = proj.shape[0]
    nm = memkv.shape[0]
    tm = _tile(t, 512)
    nh = dm // LANES
    scale = LANES ** -0.5
    qb = (2 * dr) // dm
    yb = dr // dm

    def body(q_ref, g_ref, k_ref, v_ref, dy_ref, dq_ref, dg_ref, dk_ref, dv_ref):
        @pl.when(pl.program_id(0) == 0)
        def _():
            dk_ref[...] = jnp.zeros_like(dk_ref)
            dv_ref[...] = jnp.zeros_like(dv_ref)

        for hh in range(nh):
            sl = slice(hh * LANES, (hh + 1) * LANES)
            q = q_ref[:, sl].astype(BF16)
            k = k_ref[:, sl]
            v = v_ref[:, sl]
            s = lax.dot_general(q, k, _NT, preferred_element_type=F32) * scale
            p = jnp.exp(s - jnp.max(s, axis=-1, keepdims=True))
            p = p / jnp.sum(p, axis=-1, keepdims=True)
            p_bf = p.astype(BF16)
            o = jnp.dot(p_bf, v, preferred_element_type=F32)
            gv = g_ref[:, sl]
            sg = _sigmoid(gv)
            dy = dy_ref[:, sl]
            do = dy * (gv * sg)
            dg_ref[:, sl] = (dy * o * (sg * (1.0 + gv * (1.0 - sg)))).astype(BF16)
            do_bf = do.astype(BF16)
            dv_ref[:, sl] += lax.dot_general(p_bf, do_bf, _TN, preferred_element_type=F32)
            dp = lax.dot_general(do_bf, v, _NT, preferred_element_type=F32)
            ds = (p * (dp - jnp.sum(dp * p, axis=-1, keepdims=True)) * scale).astype(BF16)
            dq_ref[:, sl] = jnp.dot(ds, k, preferred_element_type=F32).astype(BF16)
            dk_ref[:, sl] += lax.dot_general(ds, q, _TN, preferred_element_type=F32)

    fixed = lambda i: (0, 0)
    return pl.pallas_call(
        body, name=name, grid=(t // tm,),
        in_specs=[pl.BlockSpec((tm, dm), lambda i: (i, qb)), pl.BlockSpec((tm, dm), lambda i: (i, qb + 1)),
                  pl.BlockSpec((nm, dm), lambda i: (0, 2 * layer)), pl.BlockSpec((nm, dm), lambda i: (0, 2 * layer + 1)),
                  pl.BlockSpec((tm, dm), lambda i: (i, yb))],
        out_specs=[pl.BlockSpec((tm, dm), lambda i: (i, 0)), pl.BlockSpec((tm, dm), lambda i: (i, 0)),
                   pl.BlockSpec((nm, dm), fixed), pl.BlockSpec((nm, dm), fixed)],
        out_shape=[jax.ShapeDtypeStruct((t, dm), BF16), jax.ShapeDtypeStruct((t, dm), BF16),
                   jax.ShapeDtypeStruct((nm, dm), F32), jax.ShapeDtypeStruct((nm, dm), F32)],
        compiler_params=_cparams(1),
    )(proj, proj, memkv, memkv, dmix)


LRU_CHUNK = 256


def _lru_gates(xc, vec, wr_ref, wi_ref):
    r = _sigmoid(jnp.dot(xc.astype(BF16), wr_ref[...], preferred_element_type=F32) + vec[1:2])
    i = _sigmoid(jnp.dot(xc.astype(BF16), wi_ref[...], preferred_element_type=F32) + vec[2:3])
    lam = vec[3:4]
    cl = -LRU_C * (jnp.maximum(-lam, 0.0) + _log1p_pos(jnp.exp(-jnp.abs(lam))))
    la = cl * r
    a = jnp.exp(la)
    s2 = _neg_expm1(2.0 * la)
    return r, i, cl, a, s2


def _lru_fwd(proj, vec, wr, wi, name):
    t = proj.shape[0]
    nb = wr.shape[0]
    dr = nb * LANES
    c = _tile(t, LRU_CHUNK)

    def body(x_ref, g_ref, vec_ref, wr_ref, wi_ref, y_ref, h_ref, xc_ref, carry_ref, xprev_ref):
        @pl.when(pl.program_id(1) == 0)
        def _():
            carry_ref[...] = jnp.zeros_like(carry_ref)
            xprev_ref[...] = jnp.zeros_like(xprev_ref)

        x = x_ref[...]
        vec = vec_ref[...]
        rows = lax.broadcasted_iota(jnp.int32, (c, LANES), 0)
        xprev = xprev_ref[...]
        xc = vec[7:8] * x + vec[0:1]
        for k in range(1, 4):
            xs = jnp.where(rows < k, pltpu.roll(xprev, k, 0), pltpu.roll(x, k, 0))
            xc = xc + vec[7 - k:8 - k] * xs
        xprev_ref[...] = x
        xc_ref[...] = xc

        r, i, cl, a, s2 = _lru_gates(xc, vec, wr_ref, wi_ref)
        hh = jnp.sqrt(s2) * (i * xc)
        aa = a
        d = 1
        while d < c:
            keep = rows >= d
            hh = jnp.where(keep, aa * pltpu.roll(hh, d, 0) + hh, hh)
            aa = jnp.where(keep, aa * pltpu.roll(aa, d, 0), aa)
            d *= 2
        hfull = hh + aa * carry_ref[7:8, :]
        carry_ref[...] = hfull[c - 8:c, :]
        h_ref[...] = hfull
        gv = g_ref[...]
        y_ref[...] = (hfull * (gv * _sigmoid(gv))).astype(BF16)

    blk = lambda n, s: (s, n)
    return pl.pallas_call(
        body, name=name, grid=(nb, t // c),
        in_specs=[pl.BlockSpec((c, LANES), blk), pl.BlockSpec((c, LANES), lambda n, s: (s, nb + n)),
                  pl.BlockSpec((8, LANES), lambda n, s: (0, n)),
                  pl.BlockSpec((None, LANES, LANES), lambda n, s: (n, 0, 0)),
                  pl.BlockSpec((None, LANES, LANES), lambda n, s: (n, 0, 0))],
        out_specs=[pl.BlockSpec((c, LANES), blk)] * 3,
        out_shape=[jax.ShapeDtypeStruct((t, dr), BF16), jax.ShapeDtypeStruct((t, dr), F32),
                   jax.ShapeDtypeStruct((t, dr), F32)],
        scratch_shapes=[pltpu.VMEM((8, LANES), F32), pltpu.VMEM((c, LANES), F32)],
        compiler_params=_cparams(2),
    )(proj, proj, vec, wr, wi)


def _lru_bwd(proj, xc_all, h_all, dmix, vec, wr, wi, name):
    t = proj.shape[0]
    nb = wr.shape[0]
    dr = nb * LANES
    c = _tile(t, LRU_CHUNK)
    nc = t // c

    def body(x_ref, g_ref, xc_ref, h_ref, dy_ref, vec_ref, wr_ref, wi_ref,
             dx_ref, dg_ref, dwr_ref, dwi_ref, dvec_ref, qcarry_ref, dxc_next_ref):
        @pl.when(pl.program_id(1) == 0)
        def _():
            qcarry_ref[...] = jnp.zeros_like(qcarry_ref)
            dxc_next_ref[...] = jnp.zeros_like(dxc_next_ref)
            dwr_ref[...] = jnp.zeros_like(dwr_ref)
            dwi_ref[...] = jnp.zeros_like(dwi_ref)
            dvec_ref[...] = jnp.zeros_like(dvec_ref)

        x = x_ref[...]
        xc = xc_ref[...]
        h = h_ref[...]
        dy = dy_ref[...]
        gv = g_ref[...]
        vec = vec_ref[...]
        rows = lax.broadcasted_iota(jnp.int32, (c, LANES), 0)

        r, i, cl, a, s2 = _lru_gates(xc, vec, wr_ref, wi_ref)
        s = jnp.sqrt(s2)
        ixc = i * xc
        u = s * ixc
        sg = _sigmoid(gv)
        dh = dy * (gv * sg)
        dg_ref[...] = (dy * h * (sg * (1.0 + gv * (1.0 - sg)))).astype(BF16)

        aa = a
        qq = a * dh
        d = 1
        while d < c:
            keep = rows < c - d
            qq = jnp.where(keep, qq + aa * pltpu.roll(qq, c - d, 0), qq)
            aa = jnp.where(keep, aa * pltpu.roll(aa, c - d, 0), aa)
            d *= 2
        qin = qcarry_ref[0:1, :]
        qfull = qq + aa * qin
        gt = dh + jnp.where(rows == c - 1, qin, pltpu.roll(qfull, c - 1, 0))
        qcarry_ref[...] = qfull[0:8, :]

        dla = gt * (h - u) - gt * ixc * (a * a) / s
        dixc = gt * s
        di = dixc * xc
        dxc = dixc * i
        dzr = (dla * cl) * (r * (1.0 - r))
        dzi = di * (i * (1.0 - i))
        dzr_bf = dzr.astype(BF16)
        dzi_bf = dzi.astype(BF16)
        dxc = dxc + lax.dot_general(dzr_bf, wr_ref[...], _NT, preferred_element_type=F32)
        dxc = dxc + lax.dot_general(dzi_bf, wi_ref[...], _NT, preferred_element_type=F32)
        xc_bf = xc.astype(BF16)
        dwr_ref[...] += lax.dot_general(xc_bf, dzr_bf, _TN, preferred_element_type=F32)
        dwi_ref[...] += lax.dot_general(xc_bf, dzi_bf, _TN, preferred_element_type=F32)

        lam = vec[3:4]
        dlam = jnp.sum(dla * r, axis=0, keepdims=True) * (LRU_C * _sigmoid(-lam))
        colsum = lambda v: jnp.sum(v, axis=0, keepdims=True)
        dxn = dxc_next_ref[...]
        dx = vec[7:8] * dxc
        dtaps = [None] * 4
        dtaps[3] = colsum(x * dxc)
        for k in range(1, 4):
            sh = jnp.where(rows < c - k, pltpu.roll(dxc, c - k, 0), pltpu.roll(dxn, c - k, 0))
            dx = dx + vec[7 - k:8 - k] * sh
            dtaps[3 - k] = colsum(x * sh)
        dxc_next_ref[...] = dxc
        dx_ref[...] = dx.astype(BF16)
        dvec_ref[...] += jnp.concatenate([colsum(dxc), colsum(dzr), colsum(dzi), dlam] + dtaps, axis=0)

    rev = lambda n, s: (nc - 1 - s, n)
    sq = lambda n, s: (n, 0, 0)
    return pl.pallas_call(
        body, name=name, grid=(nb, nc),
        in_specs=[pl.BlockSpec((c, LANES), rev), pl.BlockSpec((c, LANES), lambda n, s: (nc - 1 - s, nb + n)),
                  pl.BlockSpec((c, LANES), rev), pl.BlockSpec((c, LANES), rev), pl.BlockSpec((c, LANES), rev),
                  pl.BlockSpec((8, LANES), lambda n, s: (0, n)),
                  pl.BlockSpec((None, LANES, LANES), sq), pl.BlockSpec((None, LANES, LANES), sq)],
        out_specs=[pl.BlockSpec((c, LANES), rev), pl.BlockSpec((c, LANES), rev),
                   pl.BlockSpec((None, LANES, LANES), sq), pl.BlockSpec((None, LANES, LANES), sq),
                   pl.BlockSpec((None, 8, LANES), sq)],
        out_shape=[jax.ShapeDtypeStruct((t, dr), BF16), jax.ShapeDtypeStruct((t, dr), BF16),
                   jax.ShapeDtypeStruct((nb, LANES, LANES), F32), jax.ShapeDtypeStruct((nb, LANES, LANES), F32),
                   jax.ShapeDtypeStruct((nb, 8, LANES), F32)],
        scratch_shapes=[pltpu.VMEM((8, LANES), F32), pltpu.VMEM((c, LANES), F32)],
        compiler_params=_cparams(2),
    )(proj, proj, xc_all, h_all, dmix, vec, wr, wi)


SB_TQ = 1024
SB_TK = 256


def _sb_softplus(z, diag):
    sp = jnp.maximum(z, 0.0) + jnp.log(1.0 + jnp.exp(-jnp.abs(z)))
    mask = None
    if diag:
        mask = lax.broadcasted_iota(jnp.int32, z.shape, 1) < lax.broadcasted_iota(jnp.int32, z.shape, 0)
        sp = jnp.where(mask, sp, 0.0)
    return sp, mask


def _split_dot(v, m):
    hi = v.astype(BF16)
    lo = (v - hi.astype(F32)).astype(BF16)
    return jnp.dot(hi, m, preferred_element_type=F32) + jnp.dot(lo, m, preferred_element_type=F32)


def _tri_ones(kind, tk):
    jj = lax.broadcasted_iota(jnp.int32, (tk, tk), 0)
    ss = lax.broadcasted_iota(jnp.int32, (tk, tk), 1)
    rel = {"ge": jj >= ss, "le": jj <= ss}[kind]
    return jnp.where(rel, 1.0, 0.0).astype(BF16)


def _sb_fwd(proj, kv, name):
    t = proj.shape[0]
    ds = kv.shape[1] // 2
    nh = ds // LANES
    tq = _tile(t, SB_TQ)
    tk = _tile(tq, SB_TK)
    nd = tq // tk
    assert nd % 2 == 0 or t == tq
    scale = LANES ** -0.5

    def body(q_ref, g_ref, k_ref, v_ref, y_ref, o_ref, tl_ref, qbf_ref, acc_ref, run_ref, z_ref, w_ref):
        qi = pl.program_id(1)
        qbf_ref[...] = q_ref[...].astype(BF16)
        tri = _tri_ones("ge", tk)
        acc_ref[...] = jnp.zeros_like(acc_ref)
        run_ref[...] = jnp.zeros_like(run_ref)
        all_rows = [slice(s0, s0 + tk) for s0 in range(0, tq, tk)]

        def weights(zs, groups):
            sps = [_sb_softplus(z, dg) for z, (_, dg) in zip(zs, groups)]
            cums = [_split_dot(sp, tri) for sp, _ in sps]
            ws = []
            for z, (rows, dg), (_, mask), cum in zip(zs, groups, sps, cums):
                run = run_ref[rows, :]
                w = jnp.exp(z - cum - run)
                if dg:
                    w = jnp.where(mask, w, 0.0)
                run_ref[rows, :] = run + cum[:, 0:1]
                ws.append(w.astype(BF16))
            return ws

        n = qi * nd

        def rows_from(u):
            return [slice(s0, s0 + tk) for s0 in range(u * tk, tq, tk)]

        def logits_into(slot, kb, rows_list):
            k = k_ref[pl.ds(pl.multiple_of(kb * tk, tk), tk), :]
            for rows in rows_list:
                z_ref[slot, rows, :] = lax.dot_general(qbf_ref[rows, :], k, _NT, preferred_element_type=F32) * scale

        def add_values(kb, rows_list):
            v = v_ref[pl.ds(pl.multiple_of(kb * tk, tk), tk), :]
            for rows in rows_list:
                acc_ref[rows, :] += jnp.dot(w_ref[rows, :], v, preferred_element_type=F32)

        def weigh(slot, groups):
            for (rows, _), w in zip(groups, weights([z_ref[slot, rows, :] for rows, _ in groups], groups)):
                w_ref[rows, :] = w

        logits_into(0, n + nd - 1, rows_from(nd - 1))
        for i, u in enumerate(reversed(range(nd))):
            slot = i % 2
            if i > 0:
                add_values(n + u + 1, rows_from(u + 1))
            if u > 0:
                logits_into(1 - slot, n + u - 1, rows_from(u - 1))
            else:
                logits_into(1 - slot, jnp.maximum(n - 1, 0), all_rows)
            weigh(slot, [(rows, j == 0) for j, rows in enumerate(rows_from(u))])

        def half_step(j, slot):
            kb = n - 1 - j
            add_values(kb + 1, all_rows)
            logits_into(1 - slot, jnp.maximum(kb - 1, 0), all_rows)
            weigh(slot, [(rows, False) for rows in all_rows])

        def step(i, carry):
            half_step(2 * i, nd % 2)
            half_step(2 * i + 1, 1 - nd % 2)
            return carry

        lax.fori_loop(0, n // 2, step, 0)
        add_values(0, all_rows)
        o = acc_ref[...]
        o_ref[...] = o
        tl_ref[...] = jnp.broadcast_to(run_ref[...], (tq, LANES))
        gv = g_ref[...]
        y_ref[...] = (o * (gv * _sigmoid(gv))).astype(BF16)

    blk = lambda h, i: (i, h)
    return pl.pallas_call(
        body, name=name, grid=(nh, t // tq),
        in_specs=[pl.BlockSpec((tq, LANES), blk), pl.BlockSpec((tq, LANES), lambda h, i: (i, nh + h)),
                  pl.BlockSpec((t, LANES), lambda h, i: (0, h)), pl.BlockSpec((t, LANES), lambda h, i: (0, nh + h))],
        out_specs=[pl.BlockSpec((tq, LANES), blk)] * 3,
        out_shape=[jax.ShapeDtypeStruct((t, ds), BF16), jax.ShapeDtypeStruct((t, ds), F32),
                   jax.ShapeDtypeStruct((t, ds), F32)],
        scratch_shapes=[pltpu.VMEM((tq, LANES), BF16), pltpu.VMEM((tq, LANES), F32), pltpu.VMEM((tq, 1), F32),
                        pltpu.VMEM((2, tq, tk), F32), pltpu.VMEM((tq, tk), BF16)],
        compiler_params=_cparams(2),
    )(proj, proj, kv, kv)


def _sb_bwd(proj, kv, o_all, tl_all, dmix, name):
    t = proj.shape[0]
    ds = kv.shape[1] // 2
    nh = ds // LANES
    tq = _tile(t, SB_TQ)
    tk = _tile(tq, SB_TK)
    nd = tq // tk
    scale = LANES ** -0.5

    def body(q_ref, g_ref, k_ref, v_ref, o_ref, tl_ref, dy_ref, dq_ref, dg_ref, dk_ref, dv_ref,
             qbf_ref, dobf_ref, qt_ref, dot_ref, acc_ref, left_ref, rune_ref, z_ref, dw_ref, wp_ref, dzp_ref):
        qi = pl.program_id(1)

        @pl.when(qi == 0)
        def _():
            dk_ref[...] = jnp.zeros_like(dk_ref)
            dv_ref[...] = jnp.zeros_like(dv_ref)

        qbf_ref[...] = q_ref[...].astype(BF16)
        qt_ref[...] = q_ref[...].T.astype(BF16)
        gv = g_ref[...]
        sg = _sigmoid(gv)
        dy = dy_ref[...]
        do = dy * (gv * sg)
        dobf_ref[...] = do.astype(BF16)
        dot_ref[...] = do.T.astype(BF16)
        dg_ref[...] = (dy * o_ref[...] * (sg * (1.0 + gv * (1.0 - sg)))).astype(BF16)
        tri = _tri_ones("le", tk)
        acc_ref[...] = jnp.zeros_like(acc_ref)
        left_ref[...] = tl_ref[:, 0:1]
        rune_ref[...] = jnp.zeros_like(rune_ref)
        wp_ref[...] = jnp.zeros_like(wp_ref)
        dzp_ref[...] = jnp.zeros_like(dzp_ref)
        all_rows = [slice(s0, s0 + tk) for s0 in range(0, tq, tk)]

        def grads(zs, dws, groups):
            sps = [_sb_softplus(z, dg) for z, (_, dg) in zip(zs, groups)]
            cums = [_split_dot(sp, tri) for sp, _ in sps]
            ws, es, lbs = [], [], []
            for z, (rows, dg), (sp, mask), cum, dw in zip(zs, groups, sps, cums, dws):
                left = left_ref[rows, :]
                lb = z - sp
                w = jnp.exp(lb - (left - cum))
                if dg:
                    w = jnp.where(mask, w, 0.0)
                left_ref[rows, :] = left - cum[:, tk - 1:tk]
                ws.append(w.astype(BF16))
                es.append(dw * w)
                lbs.append(lb)
            cumes = [_split_dot(e, tri) for e in es]
            dzs = []
            for (rows, dg), (_, mask), lb, e, cume in zip(groups, sps, lbs, es, cumes):
                rune = rune_ref[rows, :]
                dz = (e - jnp.exp(lb) * (rune + cume)) * scale
                if dg:
                    dz = jnp.where(mask, dz, 0.0)
                rune_ref[rows, :] = rune + cume[:, tk - 1:tk]
                dzs.append(dz.astype(BF16))
            return dzs, ws

        n = qi * nd

        def rows_from(u):
            return [slice(s0, s0 + tk) for s0 in range(u * tk, tq, tk)]

        def logits_into(slot, kb, rows_list):
            k0 = pl.multiple_of(kb * tk, tk)
            k = k_ref[pl.ds(k0, tk), :]
            v = v_ref[pl.ds(k0, tk), :]
            for rows in rows_list:
                z_ref[slot, rows, :] = lax.dot_general(qbf_ref[rows, :], k, _NT, preferred_element_type=F32) * scale
                dw_ref[slot, rows, :] = lax.dot_general(dobf_ref[rows, :], v, _NT, preferred_element_type=F32)

        def apply_stored(kb, u):
            k0 = pl.multiple_of(kb * tk, tk)
            k = k_ref[pl.ds(k0, tk), :]
            for rows in rows_from(u):
                acc_ref[rows, :] += jnp.dot(dzp_ref[rows, :], k, preferred_element_type=F32)
            seen = slice(u * tk, tq)
            dk_ref[:, pl.ds(k0, tk)] += jnp.dot(qt_ref[:, seen], dzp_ref[seen, :], preferred_element_type=F32)
            dv_ref[:, pl.ds(k0, tk)] += jnp.dot(dot_ref[:, seen], wp_ref[seen, :], preferred_element_type=F32)

        def differentiate(slot, groups):
            dzs, ws = grads([z_ref[slot, rows, :] for rows, _ in groups], [dw_ref[slot, rows, :] for rows, _ in groups],
                            groups)
            for (rows, _), dz, w in zip(groups, dzs, ws):
                dzp_ref[rows, :] = dz
                wp_ref[rows, :] = w

        logits_into(0, 0, all_rows)

        def half_step(j, slot):
            apply_stored(jnp.maximum(j - 1, 0), 0)
            logits_into(1 - slot, j + 1, all_rows)
            differentiate(slot, [(rows, False) for rows in all_rows])

        def step(i, carry):
            half_step(2 * i, 0)
            half_step(2 * i + 1, 1)
            return carry

        lax.fori_loop(0, n // 2, step, 0)
        for u in range(nd):
            slot = u % 2
            if u == 0:
                apply_stored(jnp.maximum(n - 1, 0), 0)
            else:
                apply_stored(n + u - 1, u - 1)
            if u + 1 < nd:
                logits_into(1 - slot, n + u + 1, rows_from(u + 1))
            differentiate(slot, [(rows, j == 0) for j, rows in enumerate(rows_from(u))])
        apply_stored(n + nd - 1, nd - 1)
        dq_ref[...] = acc_ref[...].astype(BF16)

    blk = lambda h, i: (i, h)
    whole = lambda h, i: (0, h)
    return pl.pallas_call(
        body, name=name, grid=(nh, t // tq),
        in_specs=[pl.BlockSpec((tq, LANES), blk), pl.BlockSpec((tq, LANES), lambda h, i: (i, nh + h)),
                  pl.BlockSpec((t, LANES), whole), pl.BlockSpec((t, LANES), lambda h, i: (0, nh + h)),
                  pl.BlockSpec((tq, LANES), blk), pl.BlockSpec((tq, LANES), blk), pl.BlockSpec((tq, LANES), blk)],
        out_specs=[pl.BlockSpec((tq, LANES), blk), pl.BlockSpec((tq, LANES), blk),
                   pl.BlockSpec((LANES, t), lambda h, i: (h, 0)), pl.BlockSpec((LANES, t), lambda h, i: (h, 0))],
        out_shape=[jax.ShapeDtypeStruct((t, ds), BF16), jax.ShapeDtypeStruct((t, ds), BF16),
                   jax.ShapeDtypeStruct((ds, t), F32), jax.ShapeDtypeStruct((ds, t), F32)],
        scratch_shapes=[pltpu.VMEM((tq, LANES), BF16), pltpu.VMEM((tq, LANES), BF16),
                        pltpu.VMEM((LANES, tq), BF16), pltpu.VMEM((LANES, tq), BF16), pltpu.VMEM((tq, LANES), F32),
                        pltpu.VMEM((tq, 1), F32), pltpu.VMEM((tq, 1), F32),
                        pltpu.VMEM((2, tq, tk), F32), pltpu.VMEM((2, tq, tk), F32),
                        pltpu.VMEM((tq, tk), BF16), pltpu.VMEM((tq, tk), BF16)],
        compiler_params=_cparams(2),
    )(proj, proj, kv, kv, o_all, tl_all, dmix)


def _place():
    x, y, c = lax.axis_index("x"), lax.axis_index("y"), lax.axis_index("c")
    chips = [(1 - x, y), (x, 1 - y), (1 - x, 1 - y)]
    return x, y, c, chips


def _remote(src, dst, send_sems, recv_sems, k, to):
    return pltpu.make_async_remote_copy(src_ref=src, dst_ref=dst, send_sem=send_sems.at[k], recv_sem=recv_sems.at[k],
                                        device_id=to, device_id_type=MESH)


def _my_chip():
    return 2 * lax.axis_index("x") + lax.axis_index("y")


def _place_own(shard, name):
    r, w = shard.shape
    tr = _tile(r, FLAT_TR)

    def body(x_ref, o_ref):
        o_ref[...] = x_ref[...]

    return pl.pallas_call(
        body, name=name, out_shape=jax.ShapeDtypeStruct((N_CHIPS, r, w), shard.dtype), grid=(r // tr,),
        in_specs=[pl.BlockSpec((tr, w), lambda i: (i, 0))],
        out_specs=pl.BlockSpec((None, tr, w), lambda i: (_my_chip(), i, 0)),
        compiler_params=_cparams(1),
    )(shard)


def _chip_all_gather(shards, name):
    n = len(shards)

    def body(*refs):
        x_refs, out_refs, send_sems, recv_sems = refs[:n], refs[2 * n:3 * n], refs[3 * n], refs[3 * n + 1]
        x, y, c, chips = _place()
        me = 2 * x + y
        sibling = (x, y, 1 - c)

        def rows(t, core):
            rh = x_refs[t].shape[0] // 2
            return pl.ds(core * rh, rh)

        first = [_remote(x_refs[t].at[rows(t, c)], out_refs[t].at[me, rows(t, c)], send_sems, recv_sems, 6 * t + k,
                         (cx, cy, c)) for t in range(n) for k, (cx, cy) in enumerate(chips)]
        for cp in first:
            cp.start()
        passed = []
        for k, (cx, cy) in enumerate(chips):
            for t in range(n):
                got = out_refs[t].at[2 * cx + cy, rows(t, c)]
                _remote(got, got, send_sems, recv_sems, 6 * t + k, (cx, cy, c)).wait_recv()
                fwd = _remote(got, got, send_sems, recv_sems, 6 * t + 3 + k, sibling)
                fwd.start()
                passed.append(fwd)
        for k, (cx, cy) in enumerate(chips):
            for t in range(n):
                got = out_refs[t].at[2 * cx + cy, rows(t, 1 - c)]
                _remote(got, got, send_sems, recv_sems, 6 * t + 3 + k, sibling).wait_recv()
        for cp in first + passed:
            cp.wait_send()

    bufs = [_place_own(s, f"{name}_own{t}") for t, s in enumerate(shards)]
    return pl.pallas_call(
        body, name=name, in_specs=[ANY] * (2 * n), out_specs=[ANY] * n,
        out_shape=[jax.ShapeDtypeStruct((N_CHIPS,) + s.shape, s.dtype) for s in shards],
        input_output_aliases={n + t: t for t in range(n)},
        scratch_shapes=[pltpu.SemaphoreType.DMA((6 * n,)), pltpu.SemaphoreType.DMA((6 * n,))],
    )(*shards, *bufs)


def _sibling_take_half(ss, name):
    n = len(ss)

    def body(*refs):
        s_refs, a_refs, send_sems, recv_sems = refs[:n], refs[n:2 * n], refs[2 * n], refs[2 * n + 1]
        x, y, c, _ = _place()
        cps = []
        for t in range(n):
            rh = s_refs[t].shape[1] // 2
            cps.append(_remote(s_refs[t].at[:, pl.ds((1 - c) * rh, rh), :], a_refs[t], send_sems, recv_sems, t,
                               (x, y, 1 - c)))
        for cp in cps:
            cp.start()
        for cp in cps:
            cp.wait()

    return pl.pallas_call(
        body, name=name, in_specs=[ANY] * n, out_specs=[ANY] * n,
        out_shape=[jax.ShapeDtypeStruct((s.shape[0], s.shape[1] // 2, s.shape[2]), s.dtype) for s in ss],
        scratch_shapes=[pltpu.SemaphoreType.DMA((n,)), pltpu.SemaphoreType.DMA((n,))],
    )(*ss)


def _pair_sum(s, a, dtype, name):
    n, r, w = s.shape
    rh = r // 2
    tr = _tile(rh, FLAT_TR)
    nblk = rh // tr

    def body(s_ref, a_ref, o_ref):
        o_ref[...] = (s_ref[...] + a_ref[...]).astype(dtype)

    return pl.pallas_call(
        body, name=name, out_shape=jax.ShapeDtypeStruct((n, rh, w), dtype), grid=(n, nblk),
        in_specs=[pl.BlockSpec((None, tr, w), lambda k, i: (k, lax.axis_index("c") * nblk + i, 0)),
                  pl.BlockSpec((None, tr, w), lambda k, i: (k, i, 0))],
        out_specs=pl.BlockSpec((None, tr, w), lambda k, i: (k, i, 0)),
        compiler_params=_cparams(2),
    )(s, a)


def _chip_scatter(ps, name):
    n = len(ps)

    def body(*refs):
        p_refs, b_refs, send_sems, recv_sems = refs[:n], refs[n:2 * n], refs[2 * n], refs[2 * n + 1]
        x, y, c, chips = _place()
        me = 2 * x + y
        sends = [_remote(p_refs[t].at[2 * cx + cy], b_refs[t].at[me], send_sems, recv_sems, 3 * t + k, (cx, cy, c))
                 for t in range(n) for k, (cx, cy) in enumerate(chips)]
        for cp in sends:
            cp.start()
        for t in range(n):
            for k, (cx, cy) in enumerate(chips):
                got = b_refs[t].at[2 * cx + cy]
                _remote(got, got, send_sems, recv_sems, 3 * t + k, (cx, cy, c)).wait_recv()
        for cp in sends:
            cp.wait_send()

    return pl.pallas_call(
        body, name=name, in_specs=[ANY] * n, out_specs=[ANY] * n,
        out_shape=[jax.ShapeDtypeStruct(p.shape, p.dtype) for p in ps],
        scratch_shapes=[pltpu.SemaphoreType.DMA((3 * n,)), pltpu.SemaphoreType.DMA((3 * n,))],
    )(*ps)


def _chip_sum(p, b, name):
    n, rh, w = p.shape
    tr = _tile(rh, FLAT_TR)
    nblk = rh // tr

    def body(p_ref, b0_ref, b1_ref, b2_ref, b3_ref, o_ref):
        me = _my_chip()
        own = p_ref[...]
        t = [jnp.where(me == k, own, b_ref[...]).astype(F32) for k, b_ref in enumerate((b0_ref, b1_ref, b2_ref, b3_ref))]
        o_ref[...] = ((t[0] + t[1]) + t[2]) + t[3]

    def other(k):
        return lambda i: (jnp.where(_my_chip() == k, (k + 1) % N_CHIPS, k), i, 0)

    return pl.pallas_call(
        body, name=name, out_shape=jax.ShapeDtypeStruct((2 * rh, w), F32), grid=(nblk,),
        in_specs=[pl.BlockSpec((None, tr, w), lambda i: (_my_chip(), i, 0))]
        + [pl.BlockSpec((None, tr, w), other(k)) for k in range(N_CHIPS)],
        out_specs=pl.BlockSpec((tr, w), lambda i: (lax.axis_index("c") * nblk + i, 0)),
        compiler_params=_cparams(1),
    )(p, b, b, b, b)


def _sibling_join(gs, name):
    n = len(gs)

    def body(*refs):
        g_refs, send_sems, recv_sems = refs[n:2 * n], refs[2 * n], refs[2 * n + 1]
        x, y, c, _ = _place()
        cps = []
        for t in range(n):
            rh = g_refs[t].shape[0] // 2
            mine = g_refs[t].at[pl.ds(c * rh, rh)]
            cps.append(_remote(mine, mine, send_sems, recv_sems, t, (x, y, 1 - c)))
        for cp in cps:
            cp.start()
        for t in range(n):
            rh = g_refs[t].shape[0] // 2
            theirs = g_refs[t].at[pl.ds((1 - c) * rh, rh)]
            _remote(theirs, theirs, send_sems, recv_sems, t, (x, y, 1 - c)).wait_recv()
        for cp in cps:
            cp.wait_send()

    return pl.pallas_call(
        body, name=name, in_specs=[ANY] * n, out_specs=[ANY] * n,
        out_shape=[jax.ShapeDtypeStruct(g.shape, g.dtype) for g in gs],
        input_output_aliases={t: t for t in range(n)},
        scratch_shapes=[pltpu.SemaphoreType.DMA((n,)), pltpu.SemaphoreType.DMA((n,))],
    )(*gs)


def _adamw(g, w, m, v, name):
    r, wd = g.shape

    def body(g_ref, w_ref, m_ref, v_ref, d_ref, mo_ref, vo_ref):
        gv = g_ref[...]
        mn = ADAM_B1 * m_ref[...] + (1.0 - ADAM_B1) * gv
        vn = ADAM_B2 * v_ref[...] + (1.0 - ADAM_B2) * (gv * gv)
        m_hat = mn / (1.0 - ADAM_B1 ** ADAM_STEP)
        v_hat = vn / (1.0 - ADAM_B2 ** ADAM_STEP)
        d_ref[...] = -ADAM_LR * (m_hat / (jnp.sqrt(v_hat) + ADAM_EPS) + ADAM_WD * w_ref[...])
        mo_ref[...] = mn
        vo_ref[...] = vn

    tr = _tile(r, FLAT_TR)
    row = lambda i: (i, 0)
    spec = pl.BlockSpec((tr, wd), row)
    return pl.pallas_call(
        body, name=name, grid=(r // tr,), in_specs=[spec] * 4, out_specs=[spec] * 3,
        out_shape=[jax.ShapeDtypeStruct((r, wd), F32)] * 3,
        compiler_params=_cparams(1),
    )(g, w, m, v)


def _reduce_to_shards(ss, names):
    a = _sibling_take_half(ss, "grad_sibling_half")
    p = [_pair_sum(s, a_t, BF16 if n in BIG else F32, f"grad_pair_sum_{n}") for s, a_t, n in zip(ss, a, names)]
    b = _chip_scatter(p, "grad_chip_scatter")
    g = [_chip_sum(p_t, b_t, f"grad_chip_sum_{n}") for p_t, b_t, n in zip(p, b, names)]
    return _sibling_join(g, "grad_sibling_join")


WEIGHTS = ("mem_norm", "w_mem_kv", "norm_a", "w_in_a", "conv_w", "conv_b", "w_rec_gate", "b_rec_gate", "w_in_gate",
           "b_in_gate", "lru_lambda", "w_out_a", "kv_norm", "w_kv", "norm_b", "w_in_b", "w_out_b", "final_norm")
SHARD_DIM = {"mem_norm": None, "w_mem_kv": 1, "norm_a": 1, "w_in_a": 2, "conv_w": 2, "conv_b": 1, "w_rec_gate": None,
             "b_rec_gate": 1, "w_in_gate": None, "b_in_gate": 1, "lru_lambda": 1, "w_out_a": 1, "kv_norm": None,
             "w_kv": 1, "norm_b": None, "w_in_b": 2, "w_out_b": 1, "final_norm": None}
BIG = ("w_mem_kv", "w_in_a", "w_out_a", "w_kv", "w_in_b", "w_out_b")
SMALL = ("norm_a", "conv_w", "conv_b", "b_rec_gate", "b_in_gate", "lru_lambda")


def _pad_rows(flat, row_multiple):
    per = FLAT_W * row_multiple
    n = flat.shape[0]
    total = -(-n // per) * per
    return jnp.pad(flat, (0, total - n)).reshape(total // FLAT_W, FLAT_W)


def _flatten(parts, row_multiple):
    return _pad_rows(jnp.concatenate([p.reshape(-1) for p in parts]), row_multiple)


def _unflatten(flat2d, shapes):
    flat = flat2d.reshape(-1)
    out, off = [], 0
    for shp in shapes:
        n = 1
        for s in shp:
            n *= s
        out.append(flat[off:off + n].reshape(shp))
        off += n
    return out


def _gather_weights(local):
    packs = [local[n].astype(BF16).reshape(-1, local[n].shape[-1]) for n in BIG]
    small_shapes = [local[n].shape for n in SMALL]
    packs.append(_flatten([local[n] for n in SMALL], 16))
    gathered = _chip_all_gather(packs, "weights_all_gather")
    full = {}
    for n, g in zip(BIG, gathered):
        full[n] = jnp.concatenate([g[k].reshape(local[n].shape) for k in range(N_CHIPS)], axis=SHARD_DIM[n])
    per_chip = [_unflatten(gathered[-1][k], small_shapes) for k in range(N_CHIPS)]
    for i, n in enumerate(SMALL):
        full[n] = jnp.concatenate([per_chip[k][i] for k in range(N_CHIPS)], axis=SHARD_DIM[n])
    return full


def _piece(g, name, k):
    dim = SHARD_DIM[name]
    if dim is None:
        return g
    n = g.shape[dim] // N_CHIPS
    return lax.slice_in_dim(g, k * n, (k + 1) * n, axis=dim)


def _local_grads(x, mem, tgt, wts):
    t, d = x.shape
    depth = wts["w_mem_kv"].shape[0]
    n_a = wts["w_in_a"].shape[0]
    n_b = wts["w_in_b"].shape[0]
    nb = wts["w_rec_gate"].shape[1]
    dr = nb * LANES
    dm = wts["w_mem_kv"].shape[2] // 2
    row = lambda v: v.reshape(1, -1)

    wm_all = jnp.concatenate([wts["w_mem_kv"][l] for l in range(depth)], axis=1)
    memkv, memn_bf = _norm_matmul(mem, row(wts["mem_norm"]), wm_all, BF16, "mem_kv_proj")

    h = x
    saved = []
    vecs = []
    for l in range(n_a):
        proj, u_bf = _norm_matmul(h, row(wts["norm_a"][l]), wts["w_in_a"][l], F32, f"a{l}_in_proj")
        vec = jnp.concatenate([row(wts["conv_b"][l]), row(wts["b_rec_gate"][l]), row(wts["b_in_gate"][l]),
                               row(wts["lru_lambda"][l]), wts["conv_w"][l]], axis=0)
        vecs.append(vec)
        y_rnn, h_rnn, xc = _lru_fwd(proj, vec, wts["w_rec_gate"][l], wts["w_in_gate"][l], f"a{l}_lru_fwd")
        y_mem = _mem_attn_fwd(proj, memkv, l, dr, dm, f"a{l}_mem_fwd")
        mix = jnp.concatenate([y_rnn, y_mem], axis=1)
        h_next = _matmul_res(mix, wts["w_out_a"][l], h, f"a{l}_out_proj")
        saved.append((h, proj, u_bf, mix, h_rnn, xc))
        h = h_next

    h_kv = h
    kv, ukv_bf = _norm_matmul(h_kv, row(wts["kv_norm"]), wts["w_kv"], BF16, "kv_proj")

    for j in range(n_b):
        l = n_a + j
        proj, u_bf = _norm_matmul(h, row(wts["norm_b"][j]), wts["w_in_b"][j], F32, f"b{j}_in_proj")
        y_sb, o_sb, tl_sb = _sb_fwd(proj, kv, f"b{j}_sb_fwd")
        y_mem = _mem_attn_fwd(proj, memkv, l, dr, dm, f"b{j}_mem_fwd")
        mix = jnp.concatenate([y_sb, y_mem], axis=1)
        h_next = _matmul_res(mix, wts["w_out_b"][j], h, f"b{j}_out_proj")
        saved.append((h, proj, u_bf, mix, o_sb, tl_sb))
        h = h_next

    dh, d_final, loss = _final_loss_bwd(h, row(wts["final_norm"]), tgt, "final_loss_bwd")

    grads = {"final_norm": d_final.reshape(-1)}
    big = {}
    dmemkv = [None] * depth
    g_norm_b = [None] * n_b
    dks, dvs = [], []
    for j in reversed(range(n_b)):
        l = n_a + j
        h_in, proj, u_bf, mix, o_sb, tl_sb = saved[l]
        dmix = _matmul_nt(dh, wts["w_out_b"][j], f"b{j}_dmix")
        big["w_out_b"] = _dw_rows(mix, dh, j, n_b, big.get("w_out_b"), f"b{j}_dw_out")
        dq, dg, dk, dv = _sb_bwd(proj, kv, o_sb, tl_sb, dmix, f"b{j}_sb_bwd")
        dqm, dgm, dkm, dvm = _mem_attn_bwd(proj, memkv, dmix, l, dr, dm, f"b{j}_mem_bwd")
        dmemkv[l] = (dkm, dvm)
        dproj = jnp.concatenate([dq, dg, dqm, dgm], axis=1)
        du = _matmul_nt(dproj, wts["w_in_b"][j], f"b{j}_du")
        big["w_in_b"] = _dw_cols(u_bf, dproj, j, n_b, big.get("w_in_b"), f"b{j}_dw_in")
        dh, dgn = _rms_bwd(du, h_in, row(wts["norm_b"][j]), dh, f"b{j}_rms_bwd")
        g_norm_b[j] = dgn.reshape(-1)
        dks.append(dk)
        dvs.append(dv)
    assert n_b == 2
    dkv = jnp.concatenate([_add_cast(dks[0], dks[1], "dk_sum").T, _add_cast(dvs[0], dvs[1], "dv_sum").T], axis=1)
    du = _matmul_nt(dkv, wts["w_kv"], "kv_du")
    big["w_kv"] = _dw_cols(ukv_bf, dkv, 0, 1, None, "kv_dw")
    dh, dgn = _rms_bwd(du, h_kv, row(wts["kv_norm"]), dh, "kv_rms_bwd")
    grads["kv_norm"] = dgn.reshape(-1)

    g_norm_a = [None] * n_a
    g_wr, g_wi, g_vec = [None] * n_a, [None] * n_a, [None] * n_a
    for l in reversed(range(n_a)):
        h_in, proj, u_bf, mix, h_rnn, xc = saved[l]
        dmix = _matmul_nt(dh, wts["w_out_a"][l], f"a{l}_dmix")
        big["w_out_a"] = _dw_rows(mix, dh, l, n_a, big.get("w_out_a"), f"a{l}_dw_out")
        dx, dg, g_wr[l], g_wi[l], dvec = _lru_bwd(proj, xc, h_rnn, dmix, vecs[l], wts["w_rec_gate"][l],
                                                  wts["w_in_gate"][l], f"a{l}_lru_bwd")
        g_vec[l] = dvec.transpose(1, 0, 2).reshape(8, dr)
        dqm, dgm, dkm, dvm = _mem_attn_bwd(proj, memkv, dmix, l, dr, dm, f"a{l}_mem_bwd")
        dmemkv[l] = (dkm, dvm)
        dproj = jnp.concatenate([dx, dg, dqm, dgm], axis=1)
        du = _matmul_nt(dproj, wts["w_in_a"][l], f"a{l}_du")
        big["w_in_a"] = _dw_cols(u_bf, dproj, l, n_a, big.get("w_in_a"), f"a{l}_dw_in")
        dh, dgn = _rms_bwd(du, h_in, row(wts["norm_a"][l]), dh, f"a{l}_rms_bwd")
        g_norm_a[l] = dgn.reshape(-1)

    dmemkv_all = jnp.concatenate([jnp.concatenate(p, axis=1) for p in dmemkv], axis=1).astype(BF16)
    pk = d // N_CHIPS
    big["w_mem_kv"] = _matmul_tn(memn_bf, dmemkv_all, "mem_dw", pk, 2 * dm, (N_CHIPS, depth, pk, 2 * dm),
                                 (None, None, pk, 2 * dm), lambda i, j: (i, j, 0, 0))
    dmemn = _matmul_nt(dmemkv_all, wm_all, "mem_du")
    _, dgn = _rms_bwd(dmemn, mem, row(wts["mem_norm"]), jnp.zeros_like(mem), "mem_rms_bwd")
    grads["mem_norm"] = dgn.reshape(-1)
    grads["norm_a"] = jnp.stack(g_norm_a)
    grads["w_rec_gate"] = jnp.stack(g_wr)
    grads["w_in_gate"] = jnp.stack(g_wi)
    gv = jnp.stack(g_vec)
    grads["conv_b"], grads["b_rec_gate"], grads["b_in_gate"], grads["lru_lambda"] = gv[:, 0], gv[:, 1], gv[:, 2], gv[:, 3]
    grads["conv_w"] = gv[:, 4:8]
    grads["norm_b"] = jnp.stack(g_norm_b)
    big = {n: g.reshape(N_CHIPS, -1, g.shape[-1]) for n, g in big.items()}
    return loss, dh, grads, big


def kernel(x, mem, mem_norm, w_mem_kv, norm_a, w_in_a, conv_w, conv_b, w_rec_gate, b_rec_gate, w_in_gate, b_in_gate, lru_lambda, w_out_a, kv_norm, w_kv, norm_b, w_in_b, w_out_b, final_norm, loss_target, m_mem_norm, m_w_mem_kv, m_norm_a, m_w_in_a, m_conv_w, m_conv_b, m_w_rec_gate, m_b_rec_gate, m_w_in_gate, m_b_in_gate, m_lru_lambda, m_w_out_a, m_kv_norm, m_w_kv, m_norm_b, m_w_in_b, m_w_out_b, m_final_norm, v_mem_norm, v_w_mem_kv, v_norm_a, v_w_in_a, v_conv_w, v_conv_b, v_w_rec_gate, v_b_rec_gate, v_w_in_gate, v_b_in_gate, v_lru_lambda, v_w_out_a, v_kv_norm, v_w_kv, v_norm_b, v_w_in_b, v_w_out_b, v_final_norm):
    local = dict(mem_norm=mem_norm, w_mem_kv=w_mem_kv, norm_a=norm_a, w_in_a=w_in_a, conv_w=conv_w, conv_b=conv_b,
                 w_rec_gate=w_rec_gate, b_rec_gate=b_rec_gate, w_in_gate=w_in_gate, b_in_gate=b_in_gate,
                 lru_lambda=lru_lambda, w_out_a=w_out_a, kv_norm=kv_norm, w_kv=w_kv, norm_b=norm_b, w_in_b=w_in_b,
                 w_out_b=w_out_b, final_norm=final_norm)
    mom = dict(mem_norm=m_mem_norm, w_mem_kv=m_w_mem_kv, norm_a=m_norm_a, w_in_a=m_w_in_a, conv_w=m_conv_w,
               conv_b=m_conv_b, w_rec_gate=m_w_rec_gate, b_rec_gate=m_b_rec_gate, w_in_gate=m_w_in_gate,
               b_in_gate=m_b_in_gate, lru_lambda=m_lru_lambda, w_out_a=m_w_out_a, kv_norm=m_kv_norm, w_kv=m_w_kv,
               norm_b=m_norm_b, w_in_b=m_w_in_b, w_out_b=m_w_out_b, final_norm=m_final_norm)
    var = dict(mem_norm=v_mem_norm, w_mem_kv=v_w_mem_kv, norm_a=v_norm_a, w_in_a=v_w_in_a, conv_w=v_conv_w,
               conv_b=v_conv_b, w_rec_gate=v_w_rec_gate, b_rec_gate=v_b_rec_gate, w_in_gate=v_w_in_gate,
               b_in_gate=v_b_in_gate, lru_lambda=v_lru_lambda, w_out_a=v_w_out_a, kv_norm=v_kv_norm, w_kv=v_w_kv,
               norm_b=v_norm_b, w_in_b=v_w_in_b, w_out_b=v_w_out_b, final_norm=v_final_norm)

    wts = _gather_weights(local)
    for n in WEIGHTS:
        if SHARD_DIM[n] is None:
            wts[n] = local[n]
    wts["w_rec_gate"] = wts["w_rec_gate"].astype(BF16)
    wts["w_in_gate"] = wts["w_in_gate"].astype(BF16)

    loss, grad_x, grads, big = _local_grads(x[0], mem[0], loss_target[0], wts)

    rest = [n for n in WEIGHTS if n not in BIG]
    row_multiple = 2 * FLAT_TR
    s_rest = jnp.stack([_flatten([_piece(grads[n], n, k) for n in rest], row_multiple) for k in range(N_CHIPS)])
    reduced = _reduce_to_shards([big[n] for n in BIG] + [s_rest], list(BIG) + ["rest"])

    g_out, d_out, m_out, v_out = {}, {}, {}, {}
    for n, g in zip(BIG, reduced):
        shape = local[n].shape
        flat = lambda a: a.reshape(-1, shape[-1])
        d, mo, vo = _adamw(g, flat(local[n]), flat(mom[n]), flat(var[n]), f"adamw_{n}")
        g_out[n], d_out[n], m_out[n], v_out[n] = (a.reshape(shape) for a in (g, d, mo, vo))
    g_rest = reduced[-1]
    d_rest, m_rest, v_rest = _adamw(g_rest, *(_flatten([src[n] for n in rest], row_multiple) for src in (local, mom, var)),
                                    "adamw_rest")
    shapes = [local[n].shape for n in rest]
    for out, flat2d in ((g_out, g_rest), (d_out, d_rest), (m_out, m_rest), (v_out, v_rest)):
        out.update(zip(rest, _unflatten(flat2d, shapes)))

    total_loss = lax.psum(loss[0, 0], MESH_AXES)
    return (total_loss, grad_x[None], *[g_out[n] for n in WEIGHTS], *[d_out[n] for n in WEIGHTS],
            *[m_out[n] for n in WEIGHTS], *[v_out[n] for n in WEIGHTS])
```

```python
import functools

import jax
import jax.numpy as jnp
from jax import lax
from jax.experimental import pallas as pl
from jax.experimental.pallas import tpu as pltpu

F32 = jnp.float32
BF16 = jnp.bfloat16

RMS_EPS = 1e-6
LRU_C = 8.0
ADAM_LR = 0.001
ADAM_B1 = 0.9
ADAM_B2 = 0.999
ADAM_EPS = 1e-08
ADAM_WD = 0.01
ADAM_STEP = 10

LANES = 128
VMEM_LIMIT = 56 * 1024 * 1024
FLAT_W = 1024
FLAT_TR = 256
N_CHIPS = 4
MESH_AXES = ("x", "y", "c")

_NT = (((1,), (1,)), ((), ()))
_TN = (((0,), (0,)), ((), ()))
ANY = pl.BlockSpec(memory_space=pl.ANY)
HBM = pl.BlockSpec(memory_space=pltpu.HBM)
SEM = pl.BlockSpec(memory_space=pltpu.SEMAPHORE)
EFFECT = pltpu.SideEffectType.DATAFLOW_SIDE_EFFECTING
MESH = pl.DeviceIdType.MESH


def _cparams(n_axes):
    return pltpu.CompilerParams(dimension_semantics=("arbitrary",) * n_axes, vmem_limit_bytes=VMEM_LIMIT)


def _sigmoid(x):
    return 1.0 / (1.0 + jnp.exp(-x))


def _log1p_pos(e):
    return jnp.where(e < 1e-3, e * (1.0 - e * (0.5 - e * (1.0 / 3.0))), jnp.log(1.0 + e))


def _neg_expm1(x):
    small = -x * (1.0 + x * (0.5 + x * (1.0 / 6.0 + x * (1.0 / 24.0))))
    return jnp.where(x > -0.05, small, 1.0 - jnp.exp(x))


def _tile(n, want):
    if n <= want:
        return n
    t = want
    while n % t:
        t -= LANES
    assert t > 0, (n, want)
    return t


def _norm_matmul(x, g, w, out_dtype, name, after=None):
    m, k = x.shape
    n = w.shape[1]
    tm, tn = _tile(m, 1024), _tile(n, 1024)

    def body(x_ref, g_ref, w_ref, *rest):
        o_ref, u_ref = rest[-2:]

        @pl.when(pl.program_id(1) == 0)
        def _():
            xf = x_ref[...]
            r = lax.rsqrt(jnp.mean(xf * xf, axis=-1, keepdims=True) + RMS_EPS)
            u_ref[...] = ((xf * r) * g_ref[...]).astype(BF16)

        o_ref[...] = jnp.dot(u_ref[...], w_ref[...], preferred_element_type=F32).astype(o_ref.dtype)

    return pl.pallas_call(
        body, name=name, grid=(m // tm, n // tn),
        in_specs=[pl.BlockSpec((tm, k), lambda i, j: (i, 0)), pl.BlockSpec((1, k), lambda i, j: (0, 0)),
                  pl.BlockSpec((k, tn), lambda i, j: (0, j))] + ([] if after is None else [ANY]),
        out_specs=[pl.BlockSpec((tm, tn), lambda i, j: (i, j)), pl.BlockSpec((tm, k), lambda i, j: (i, 0))],
        out_shape=[jax.ShapeDtypeStruct((m, n), out_dtype), jax.ShapeDtypeStruct((m, k), BF16)],
        compiler_params=_cparams(2),
    )(x, g, w, *([] if after is None else [after]))


def _matmul_res(a, b, res, name):
    m, k = a.shape
    n = b.shape[1]
    tm, tn = _tile(m, 1024), _tile(n, 1024)

    def body(a_ref, b_ref, r_ref, o_ref):
        o_ref[...] = r_ref[...] + jnp.dot(a_ref[...], b_ref[...], preferred_element_type=F32)

    return pl.pallas_call(
        body, name=name, grid=(m // tm, n // tn),
        in_specs=[pl.BlockSpec((tm, k), lambda i, j: (i, 0)), pl.BlockSpec((k, tn), lambda i, j: (0, j)),
                  pl.BlockSpec((tm, tn), lambda i, j: (i, j))],
        out_specs=pl.BlockSpec((tm, tn), lambda i, j: (i, j)),
        out_shape=jax.ShapeDtypeStruct((m, n), F32),
        compiler_params=_cparams(2),
    )(a, b, res)


def _matmul_nt(a, b, name, after=None):
    m, n = a.shape
    k = b.shape[0]
    tm, tk = _tile(m, 1024), _tile(k, 512)

    def body(a_ref, b_ref, *rest):
        o_ref = rest[-1]
        o_ref[...] = lax.dot_general(a_ref[...].astype(BF16), b_ref[...], _NT, preferred_element_type=F32)

    in_specs = [pl.BlockSpec((tm, n), lambda i, j: (i, 0)), pl.BlockSpec((tk, n), lambda i, j: (j, 0))]
    args = [a, b]
    if after is not None:
        in_specs.append(ANY)
        args.append(after)
    return pl.pallas_call(
        body, name=name, grid=(m // tm, k // tk), in_specs=in_specs,
        out_specs=pl.BlockSpec((tm, tk), lambda i, j: (i, j)),
        out_shape=jax.ShapeDtypeStruct((m, k), F32),
        compiler_params=_cparams(2),
    )(*args)


def _matmul_tn(a, b, name, tk, tn, out_shape, out_block, out_index, into=None):
    m, k = a.shape
    n = b.shape[1]
    tm = _tile(m, 2048 if b.dtype == BF16 else 1024)

    def body(a_ref, b_ref, *rest):
        o_ref = rest[-1]
        part = lax.dot_general(a_ref[...].astype(BF16), b_ref[...].astype(BF16), _TN, preferred_element_type=F32)

        @pl.when(pl.program_id(2) == 0)
        def _():
            o_ref[...] = part

        @pl.when(pl.program_id(2) != 0)
        def _():
            o_ref[...] += part

    in_specs = [pl.BlockSpec((tm, tk), lambda i, j, s: (s, i)), pl.BlockSpec((tm, tn), lambda i, j, s: (s, j))]
    args = [a, b]
    if into is not None:
        in_specs.append(ANY)
        args.append(into)
    return pl.pallas_call(
        body, name=name, grid=(k // tk, n // tn, m // tm), in_specs=in_specs,
        out_specs=pl.BlockSpec(out_block, lambda i, j, s: out_index(i, j)),
        out_shape=jax.ShapeDtypeStruct(out_shape, F32),
        input_output_aliases={} if into is None else {2: 0},
        compiler_params=_cparams(3),
    )(*args)


def _dw_cols(a, b, layer, n_layers, into, name):
    k, n = a.shape[1], b.shape[1]
    pn = n // N_CHIPS
    tk = _tile(k, 512)
    return _matmul_tn(a, b, name, tk, pn, (N_CHIPS, n_layers, k, pn), (None, None, tk, pn),
                      lambda i, j: (j, layer, i, 0), into)


def _dw_rows(a, b, layer, n_layers, into, name):
    k, n = a.shape[1], b.shape[1]
    pk = k // N_CHIPS
    tn = _tile(n, 2048)
    return _matmul_tn(a, b, name, pk, tn, (N_CHIPS, n_layers, pk, n), (None, None, pk, tn),
                      lambda i, j: (i, layer, 0, j), into)


def _rms_bwd(du, h, g, dres, name):
    m, d = h.shape
    tm = _tile(m, 256)

    def body(du_ref, h_ref, g_ref, dres_ref, dx_ref, dg_ref):
        xf = h_ref[...]
        r = lax.rsqrt(jnp.mean(xf * xf, axis=-1, keepdims=True) + RMS_EPS)
        xhat = xf * r
        du_v = du_ref[...]
        dxn = du_v * g_ref[...]
        dx_ref[...] = dres_ref[...] + r * (dxn - xhat * jnp.mean(dxn * xhat, axis=-1, keepdims=True))
        part = jnp.sum(du_v * xhat, axis=0, keepdims=True)

        @pl.when(pl.program_id(0) == 0)
        def _():
            dg_ref[...] = part

        @pl.when(pl.program_id(0) != 0)
        def _():
            dg_ref[...] += part

    row = lambda i: (i, 0)
    return pl.pallas_call(
        body, name=name, grid=(m // tm,),
        in_specs=[pl.BlockSpec((tm, d), row), pl.BlockSpec((tm, d), row), pl.BlockSpec((1, d), lambda i: (0, 0)),
                  pl.BlockSpec((tm, d), row)],
        out_specs=[pl.BlockSpec((tm, d), row), pl.BlockSpec((1, d), lambda i: (0, 0))],
        out_shape=[jax.ShapeDtypeStruct((m, d), F32), jax.ShapeDtypeStruct((1, d), F32)],
        compiler_params=_cparams(1),
    )(du, h, g, dres)


def _final_loss_bwd(h, g, tgt, name):
    m, d = h.shape
    tm = _tile(m, 256)

    def body(h_ref, g_ref, t_ref, dx_ref, dg_ref, loss_ref):
        xf = h_ref[...]
        r = lax.rsqrt(jnp.mean(xf * xf, axis=-1, keepdims=True) + RMS_EPS)
        xhat = xf * r
        gv = g_ref[...]
        err = xhat * gv - t_ref[...]
        dy = err * (1.0 / d)
        dxn = dy * gv
        dx_ref[...] = r * (dxn - xhat * jnp.mean(dxn * xhat, axis=-1, keepdims=True))
        part = jnp.sum(dy * xhat, axis=0, keepdims=True)
        lpart = jnp.sum(jnp.sum(err * err, axis=0, keepdims=True), axis=1, keepdims=True) * (0.5 / d)

        @pl.when(pl.program_id(0) == 0)
        def _():
            dg_ref[...] = part
            loss_ref[...] = lpart

        @pl.when(pl.program_id(0) != 0)
        def _():
            dg_ref[...] += part
            loss_ref[...] += lpart

    row = lambda i: (i, 0)
    fixed = lambda i: (0, 0)
    return pl.pallas_call(
        body, name=name, grid=(m // tm,),
        in_specs=[pl.BlockSpec((tm, d), row), pl.BlockSpec((1, d), fixed), pl.BlockSpec((tm, d), row)],
        out_specs=[pl.BlockSpec((tm, d), row), pl.BlockSpec((1, d), fixed), pl.BlockSpec((1, 1), fixed)],
        out_shape=[jax.ShapeDtypeStruct((m, d), F32), jax.ShapeDtypeStruct((1, d), F32),
                   jax.ShapeDtypeStruct((1, 1), F32)],
        compiler_params=_cparams(1),
    )(h, g, tgt)


def _add_cast(a, b, name):
    m, n = a.shape
    tm, tn = _tile(m, 512), _tile(n, 2048)

    def body(a_ref, b_ref, o_ref):
        o_ref[...] = (a_ref[...] + b_ref[...]).astype(BF16)

    blk = lambda i, j: (i, j)
    return pl.pallas_call(
        body, name=name, grid=(m // tm, n // tn),
        in_specs=[pl.BlockSpec((tm, tn), blk), pl.BlockSpec((tm, tn), blk)],
        out_specs=pl.BlockSpec((tm, tn), blk),
        out_shape=jax.ShapeDtypeStruct((m, n), BF16),
        compiler_params=_cparams(2),
    )(a, b)


def _mem_attn_fwd(proj, memkv, layer, dr, dm, name):
    t = proj.shape[0]
    nm = memkv.shape[0]
    tm = _tile(t, 512)
    nh = dm // LANES
    scale = LANES ** -0.5
    qb = (2 * dr) // dm

    def body(q_ref, g_ref, k_ref, v_ref, y_ref):
        for hh in range(nh):
            sl = slice(hh * LANES, (hh + 1) * LANES)
            s = lax.dot_general(q_ref[:, sl].astype(BF16), k_ref[:, sl], _NT, preferred_element_type=F32) * scale
            p = jnp.exp(s - jnp.max(s, axis=-1, keepdims=True))
            p = p / jnp.sum(p, axis=-1, keepdims=True)
            o = jnp.dot(p.astype(BF16), v_ref[:, sl], preferred_element_type=F32)
            gv = g_ref[:, sl]
            y_ref[:, sl] = (o * (gv * _sigmoid(gv))).astype(BF16)

    return pl.pallas_call(
        body, name=name, grid=(t // tm,),
        in_specs=[pl.BlockSpec((tm, dm), lambda i: (i, qb)), pl.BlockSpec((tm, dm), lambda i: (i, qb + 1)),
                  pl.BlockSpec((nm, dm), lambda i: (0, 2 * layer)), pl.BlockSpec((nm, dm), lambda i: (0, 2 * layer + 1))],
        out_specs=pl.BlockSpec((tm, dm), lambda i: (i, 0)),
        out_shape=jax.ShapeDtypeStruct((t, dm), BF16),
        compiler_params=_cparams(1),
    )(proj, proj, memkv, memkv)


def _mem_attn_bwd(proj, memkv, dmix, layer, dr, dm, name):
    t = proj.shape[0]
    nm = memkv.shape[0]
    tm = _tile(t, 512)
    nh = dm // LANES
    scale = LANES ** -0.5
    qb = (2 * dr) // dm
    yb = dr // dm

    def body(q_ref, g_ref, k_ref, v_ref, dy_ref, dq_ref, dg_ref, dk_ref, dv_ref):
        @pl.when(pl.program_id(0) == 0)
        def _():
            dk_ref[...] = jnp.zeros_like(dk_ref)
            dv_ref[...] = jnp.zeros_like(dv_ref)

        for hh in range(nh):
            sl = slice(hh * LANES, (hh + 1) * LANES)
            q = q_ref[:, sl].astype(BF16)
            k = k_ref[:, sl]
            v = v_ref[:, sl]
            s = lax.dot_general(q, k, _NT, preferred_element_type=F32) * scale
            p = jnp.exp(s - jnp.max(s, axis=-1, keepdims=True))
            p = p / jnp.sum(p, axis=-1, keepdims=True)
            p_bf = p.astype(BF16)
            o = jnp.dot(p_bf, v, preferred_element_type=F32)
            gv = g_ref[:, sl]
            sg = _sigmoid(gv)
            dy = dy_ref[:, sl]
            do = dy * (gv * sg)
            dg_ref[:, sl] = (dy * o * (sg * (1.0 + gv * (1.0 - sg)))).astype(BF16)
            do_bf = do.astype(BF16)
            dv_ref[:, sl] += lax.dot_general(p_bf, do_bf, _TN, preferred_element_type=F32)
            dp = lax.dot_general(do_bf, v, _NT, preferred_element_type=F32)
            ds = (p * (dp - jnp.sum(dp * p, axis=-1, keepdims=True)) * scale).astype(BF16)
            dq_ref[:, sl] = jnp.dot(ds, k, preferred_element_type=F32).astype(BF16)
            dk_ref[:, sl] += lax.dot_general(ds, q, _TN, preferred_element_type=F32)

    fixed = lambda i: (0, 0)
    return pl.pallas_call(
        body, name=name, grid=(t // tm,),
        in_specs=[pl.BlockSpec((tm, dm), lambda i: (i, qb)), pl.BlockSpec((tm, dm), lambda i: (i, qb + 1)),
                  pl.BlockSpec((nm, dm), lambda i: (0, 2 * layer)), pl.BlockSpec((nm, dm), lambda i: (0, 2 * layer + 1)),
                  pl.BlockSpec((tm, dm), lambda i: (i, yb))],
        out_specs=[pl.BlockSpec((tm, dm), lambda i: (i, 0)), pl.BlockSpec((tm, dm), lambda i: (i, 0)),
                   pl.BlockSpec((nm, dm), fixed), pl.BlockSpec((nm, dm), fixed)],
        out_shape=[jax.ShapeDtypeStruct((t, dm), BF16), jax.ShapeDtypeStruct((t, dm), BF16),
                   jax.ShapeDtypeStruct((nm, dm), F32), jax.ShapeDtypeStruct((nm, dm), F32)],
        compiler_params=_cparams(1),
    )(proj, proj, memkv, memkv, dmix)


LRU_CHUNK = 256


def _lru_gates(xc, vec, wr_ref, wi_ref):
    r = _sigmoid(jnp.dot(xc.astype(BF16), wr_ref[...], preferred_element_type=F32) + vec[1:2])
    i = _sigmoid(jnp.dot(xc.astype(BF16), wi_ref[...], preferred_element_type=F32) + vec[2:3])
    lam = vec[3:4]
    cl = -LRU_C * (jnp.maximum(-lam, 0.0) + _log1p_pos(jnp.exp(-jnp.abs(lam))))
    la = cl * r
    a = jnp.exp(la)
    s2 = _neg_expm1(2.0 * la)
    return r, i, cl, a, s2


def _lru_fwd(proj, vec, wr, wi, name):
    t = proj.shape[0]
    nb = wr.shape[0]
    dr = nb * LANES
    c = _tile(t, LRU_CHUNK)

    def body(x_ref, g_ref, vec_ref, wr_ref, wi_ref, y_ref, h_ref, xc_ref, carry_ref, xprev_ref):
        @pl.when(pl.program_id(1) == 0)
        def _():
            carry_ref[...] = jnp.zeros_like(carry_ref)
            xprev_ref[...] = jnp.zeros_like(xprev_ref)

        x = x_ref[...]
        vec = vec_ref[...]
        rows = lax.broadcasted_iota(jnp.int32, (c, LANES), 0)
        xprev = xprev_ref[...]
        xc = vec[7:8] * x + vec[0:1]
        for k in range(1, 4):
            xs = jnp.where(rows < k, pltpu.roll(xprev, k, 0), pltpu.roll(x, k, 0))
            xc = xc + vec[7 - k:8 - k] * xs
        xprev_ref[...] = x
        xc_ref[...] = xc

        r, i, cl, a, s2 = _lru_gates(xc, vec, wr_ref, wi_ref)
        hh = jnp.sqrt(s2) * (i * xc)
        aa = a
        d = 1
        while d < c:
            keep = rows >= d
            hh = jnp.where(keep, aa * pltpu.roll(hh, d, 0) + hh, hh)
            aa = jnp.where(keep, aa * pltpu.roll(aa, d, 0), aa)
            d *= 2
        hfull = hh + aa * carry_ref[7:8, :]
        carry_ref[...] = hfull[c - 8:c, :]
        h_ref[...] = hfull
        gv = g_ref[...]
        y_ref[...] = (hfull * (gv * _sigmoid(gv))).astype(BF16)

    blk = lambda n, s: (s, n)
    return pl.pallas_call(
        body, name=name, grid=(nb, t // c),
        in_specs=[pl.BlockSpec((c, LANES), blk), pl.BlockSpec((c, LANES), lambda n, s: (s, nb + n)),
                  pl.BlockSpec((8, LANES), lambda n, s: (0, n)),
                  pl.BlockSpec((None, LANES, LANES), lambda n, s: (n, 0, 0)),
                  pl.BlockSpec((None, LANES, LANES), lambda n, s: (n, 0, 0))],
        out_specs=[pl.BlockSpec((c, LANES), blk)] * 3,
        out_shape=[jax.ShapeDtypeStruct((t, dr), BF16), jax.ShapeDtypeStruct((t, dr), F32),
                   jax.ShapeDtypeStruct((t, dr), F32)],
        scratch_shapes=[pltpu.VMEM((8, LANES), F32), pltpu.VMEM((c, LANES), F32)],
        compiler_params=_cparams(2),
    )(proj, proj, vec, wr, wi)


def _lru_bwd(proj, xc_all, h_all, dmix, vec, wr, wi, name):
    t = proj.shape[0]
    nb = wr.shape[0]
    dr = nb * LANES
    c = _tile(t, LRU_CHUNK)
    nc = t // c

    def body(x_ref, g_ref, xc_ref, h_ref, dy_ref, vec_ref, wr_ref, wi_ref,
             dx_ref, dg_ref, dwr_ref, dwi_ref, dvec_ref, qcarry_ref, dxc_next_ref):
        @pl.when(pl.program_id(1) == 0)
        def _():
            qcarry_ref[...] = jnp.zeros_like(qcarry_ref)
            dxc_next_ref[...] = jnp.zeros_like(dxc_next_ref)
            dwr_ref[...] = jnp.zeros_like(dwr_ref)
            dwi_ref[...] = jnp.zeros_like(dwi_ref)
            dvec_ref[...] = jnp.zeros_like(dvec_ref)

        x = x_ref[...]
        xc = xc_ref[...]
        h = h_ref[...]
        dy = dy_ref[...]
        gv = g_ref[...]
        vec = vec_ref[...]
        rows = lax.broadcasted_iota(jnp.int32, (c, LANES), 0)

        r, i, cl, a, s2 = _lru_gates(xc, vec, wr_ref, wi_ref)
        s = jnp.sqrt(s2)
        ixc = i * xc
        u = s * ixc
        sg = _sigmoid(gv)
        dh = dy * (gv * sg)
        dg_ref[...] = (dy * h * (sg * (1.0 + gv * (1.0 - sg)))).astype(BF16)

        aa = a
        qq = a * dh
        d = 1
        while d < c:
            keep = rows < c - d
            qq = jnp.where(keep, qq + aa * pltpu.roll(qq, c - d, 0), qq)
            aa = jnp.where(keep, aa * pltpu.roll(aa, c - d, 0), aa)
            d *= 2
        qin = qcarry_ref[0:1, :]
        qfull = qq + aa * qin
        gt = dh + jnp.where(rows == c - 1, qin, pltpu.roll(qfull, c - 1, 0))
        qcarry_ref[...] = qfull[0:8, :]

        dla = gt * (h - u) - gt * ixc * (a * a) / s
        dixc = gt * s
        di = dixc * xc
        dxc = dixc * i
        dzr = (dla * cl) * (r * (1.0 - r))
        dzi = di * (i * (1.0 - i))
        dzr_bf = dzr.astype(BF16)
        dzi_bf = dzi.astype(BF16)
        dxc = dxc + lax.dot_general(dzr_bf, wr_ref[...], _NT, preferred_element_type=F32)
        dxc = dxc + lax.dot_general(dzi_bf, wi_ref[...], _NT, preferred_element_type=F32)
        xc_bf = xc.astype(BF16)
        dwr_ref[...] += lax.dot_general(xc_bf, dzr_bf, _TN, preferred_element_type=F32)
        dwi_ref[...] += lax.dot_general(xc_bf, dzi_bf, _TN, preferred_element_type=F32)

        lam = vec[3:4]
        dlam = jnp.sum(dla * r, axis=0, keepdims=True) * (LRU_C * _sigmoid(-lam))
        colsum = lambda v: jnp.sum(v, axis=0, keepdims=True)
        dxn = dxc_next_ref[...]
        dx = vec[7:8] * dxc
        dtaps = [None] * 4
        dtaps[3] = colsum(x * dxc)
        for k in range(1, 4):
            sh = jnp.where(rows < c - k, pltpu.roll(dxc, c - k, 0), pltpu.roll(dxn, c - k, 0))
            dx = dx + vec[7 - k:8 - k] * sh
            dtaps[3 - k] = colsum(x * sh)
        dxc_next_ref[...] = dxc
        dx_ref[...] = dx.astype(BF16)
        dvec_ref[...] += jnp.concatenate([colsum(dxc), colsum(dzr), colsum(dzi), dlam] + dtaps, axis=0)

    rev = lambda n, s: (nc - 1 - s, n)
    sq = lambda n, s: (n, 0, 0)
    return pl.pallas_call(
        body, name=name, grid=(nb, nc),
        in_specs=[pl.BlockSpec((c, LANES), rev), pl.BlockSpec((c, LANES), lambda n, s: (nc - 1 - s, nb + n)),
                  pl.BlockSpec((c, LANES), rev), pl.BlockSpec((c, LANES), rev), pl.BlockSpec((c, LANES), rev),
                  pl.BlockSpec((8, LANES), lambda n, s: (0, n)),
                  pl.BlockSpec((None, LANES, LANES), sq), pl.BlockSpec((None, LANES, LANES), sq)],
        out_specs=[pl.BlockSpec((c, LANES), rev), pl.BlockSpec((c, LANES), rev),
                   pl.BlockSpec((None, LANES, LANES), sq), pl.BlockSpec((None, LANES, LANES), sq),
                   pl.BlockSpec((None, 8, LANES), sq)],
        out_shape=[jax.ShapeDtypeStruct((t, dr), BF16), jax.ShapeDtypeStruct((t, dr), BF16),
                   jax.ShapeDtypeStruct((nb, LANES, LANES), F32), jax.ShapeDtypeStruct((nb, LANES, LANES), F32),
                   jax.ShapeDtypeStruct((nb, 8, LANES), F32)],
        scratch_shapes=[pltpu.VMEM((8, LANES), F32), pltpu.VMEM((c, LANES), F32)],
        compiler_params=_cparams(2),
    )(proj, proj, xc_all, h_all, dmix, vec, wr, wi)


SB_TQ = 1024
SB_TK = 256


def _sb_softplus(z, diag):
    sp = jnp.maximum(z, 0.0) + jnp.log(1.0 + jnp.exp(-jnp.abs(z)))
    mask = None
    if diag:
        mask = lax.broadcasted_iota(jnp.int32, z.shape, 1) < lax.broadcasted_iota(jnp.int32, z.shape, 0)
        sp = jnp.where(mask, sp, 0.0)
    return sp, mask


def _split_dot(v, m):
    hi = v.astype(BF16)
    lo = (v - hi.astype(F32)).astype(BF16)
    return jnp.dot(hi, m, preferred_element_type=F32) + jnp.dot(lo, m, preferred_element_type=F32)


def _tri_ones(kind, tk):
    jj = lax.broadcasted_iota(jnp.int32, (tk, tk), 0)
    ss = lax.broadcasted_iota(jnp.int32, (tk, tk), 1)
    rel = {"ge": jj >= ss, "le": jj <= ss}[kind]
    return jnp.where(rel, 1.0, 0.0).astype(BF16)


def _sb_fwd(proj, kv, name):
    t = proj.shape[0]
    ds = kv.shape[1] // 2
    nh = ds // LANES
    tq = _tile(t, SB_TQ)
    tk = _tile(tq, SB_TK)
    nd = tq // tk
    assert nd % 2 == 0 or t == tq
    scale = LANES ** -0.5

    def body(q_ref, g_ref, k_ref, v_ref, y_ref, o_ref, tl_ref, qbf_ref, acc_ref, run_ref, z_ref, w_ref):
        qi = pl.program_id(1)
        qbf_ref[...] = q_ref[...].astype(BF16)
        tri = _tri_ones("ge", tk)
        acc_ref[...] = jnp.zeros_like(acc_ref)
        run_ref[...] = jnp.zeros_like(run_ref)
        all_rows = [slice(s0, s0 + tk) for s0 in range(0, tq, tk)]

        def weights(zs, groups):
            sps = [_sb_softplus(z, dg) for z, (_, dg) in zip(zs, groups)]
            cums = [_split_dot(sp, tri) for sp, _ in sps]
            ws = []
            for z, (rows, dg), (_, mask), cum in zip(zs, groups, sps, cums):
                run = run_ref[rows, :]
                w = jnp.exp(z - cum - run)
                if dg:
                    w = jnp.where(mask, w, 0.0)
                run_ref[rows, :] = run + cum[:, 0:1]
                ws.append(w.astype(BF16))
            return ws

        n = qi * nd

        def rows_from(u):
            return [slice(s0, s0 + tk) for s0 in range(u * tk, tq, tk)]

        def logits_into(slot, kb, rows_list):
            k = k_ref[pl.ds(pl.multiple_of(kb * tk, tk), tk), :]
            for rows in rows_list:
                z_ref[slot, rows, :] = lax.dot_general(qbf_ref[rows, :], k, _NT, preferred_element_type=F32) * scale

        def add_values(kb, rows_list):
            v = v_ref[pl.ds(pl.multiple_of(kb * tk, tk), tk), :]
            for rows in rows_list:
                acc_ref[rows, :] += jnp.dot(w_ref[rows, :], v, preferred_element_type=F32)

        def weigh(slot, groups):
            for (rows, _), w in zip(groups, weights([z_ref[slot, rows, :] for rows, _ in groups], groups)):
                w_ref[rows, :] = w

        logits_into(0, n + nd - 1, rows_from(nd - 1))
        for i, u in enumerate(reversed(range(nd))):
            slot = i % 2
            if i > 0:
                add_values(n + u + 1, rows_from(u + 1))
            if u > 0:
                logits_into(1 - slot, n + u - 1, rows_from(u - 1))
            else:
                logits_into(1 - slot, jnp.maximum(n - 1, 0), all_rows)
            weigh(slot, [(rows, j == 0) for j, rows in enumerate(rows_from(u))])

        def half_step(j, slot):
            kb = n - 1 - j
            add_values(kb + 1, all_rows)
            logits_into(1 - slot, jnp.maximum(kb - 1, 0), all_rows)
            weigh(slot, [(rows, False) for rows in all_rows])

        def step(i, carry):
            half_step(2 * i, nd % 2)
            half_step(2 * i + 1, 1 - nd % 2)
            return carry

        lax.fori_loop(0, n // 2, step, 0)
        add_values(0, all_rows)
        o = acc_ref[...]
        o_ref[...] = o
        tl_ref[...] = jnp.broadcast_to(run_ref[...], (tq, LANES))
        gv = g_ref[...]
        y_ref[...] = (o * (gv * _sigmoid(gv))).astype(BF16)

    blk = lambda h, i: (i, h)
    return pl.pallas_call(
        body, name=name, grid=(nh, t // tq),
        in_specs=[pl.BlockSpec((tq, LANES), blk), pl.BlockSpec((tq, LANES), lambda h, i: (i, nh + h)),
                  pl.BlockSpec((t, LANES), lambda h, i: (0, h)), pl.BlockSpec((t, LANES), lambda h, i: (0, nh + h))],
        out_specs=[pl.BlockSpec((tq, LANES), blk)] * 3,
        out_shape=[jax.ShapeDtypeStruct((t, ds), BF16), jax.ShapeDtypeStruct((t, ds), F32),
                   jax.ShapeDtypeStruct((t, ds), F32)],
        scratch_shapes=[pltpu.VMEM((tq, LANES), BF16), pltpu.VMEM((tq, LANES), F32), pltpu.VMEM((tq, 1), F32),
                        pltpu.VMEM((2, tq, tk), F32), pltpu.VMEM((tq, tk), BF16)],
        compiler_params=_cparams(2),
    )(proj, proj, kv, kv)


def _sb_bwd(proj, kv, o_all, tl_all, dmix, name):
    t = proj.shape[0]
    ds = kv.shape[1] // 2
    nh = ds // LANES
    tq = _tile(t, SB_TQ)
    tk = _tile(tq, SB_TK)
    nd = tq // tk
    scale = LANES ** -0.5

    def body(q_ref, g_ref, k_ref, v_ref, o_ref, tl_ref, dy_ref, dq_ref, dg_ref, dk_ref, dv_ref,
             qbf_ref, dobf_ref, qt_ref, dot_ref, acc_ref, left_ref, rune_ref, z_ref, dw_ref, wp_ref, dzp_ref):
        qi = pl.program_id(1)

        @pl.when(qi == 0)
        def _():
            dk_ref[...] = jnp.zeros_like(dk_ref)
            dv_ref[...] = jnp.zeros_like(dv_ref)

        qbf_ref[...] = q_ref[...].astype(BF16)
        qt_ref[...] = q_ref[...].T.astype(BF16)
        gv = g_ref[...]
        sg = _sigmoid(gv)
        dy = dy_ref[...]
        do = dy * (gv * sg)
        dobf_ref[...] = do.astype(BF16)
        dot_ref[...] = do.T.astype(BF16)
        dg_ref[...] = (dy * o_ref[...] * (sg * (1.0 + gv * (1.0 - sg)))).astype(BF16)
        tri = _tri_ones("le", tk)
        acc_ref[...] = jnp.zeros_like(acc_ref)
        left_ref[...] = tl_ref[:, 0:1]
        rune_ref[...] = jnp.zeros_like(rune_ref)
        wp_ref[...] = jnp.zeros_like(wp_ref)
        dzp_ref[...] = jnp.zeros_like(dzp_ref)
        all_rows = [slice(s0, s0 + tk) for s0 in range(0, tq, tk)]

        def grads(zs, dws, groups):
            sps = [_sb_softplus(z, dg) for z, (_, dg) in zip(zs, groups)]
            cums = [_split_dot(sp, tri) for sp, _ in sps]
            ws, es, lbs = [], [], []
            for z, (rows, dg), (sp, mask), cum, dw in zip(zs, groups, sps, cums, dws):
                left = left_ref[rows, :]
                lb = z - sp
                w = jnp.exp(lb - (left - cum))
                if dg:
                    w = jnp.where(mask, w, 0.0)
                left_ref[rows, :] = left - cum[:, tk - 1:tk]
                ws.append(w.astype(BF16))
                es.append(dw * w)
                lbs.append(lb)
            cumes = [_split_dot(e, tri) for e in es]
            dzs = []
            for (rows, dg), (_, mask), lb, e, cume in zip(groups, sps, lbs, es, cumes):
                rune = rune_ref[rows, :]
                dz = (e - jnp.exp(lb) * (rune + cume)) * scale
                if dg:
                    dz = jnp.where(mask, dz, 0.0)
                rune_ref[rows, :] = rune + cume[:, tk - 1:tk]
                dzs.append(dz.astype(BF16))
            return dzs, ws

        n = qi * nd

        def rows_from(u):
            return [slice(s0, s0 + tk) for s0 in range(u * tk, tq, tk)]

        def logits_into(slot, kb, rows_list):
            k0 = pl.multiple_of(kb * tk, tk)
            k = k_ref[pl.ds(k0, tk), :]
            v = v_ref[pl.ds(k0, tk), :]
            for rows in rows_list:
                z_ref[slot, rows, :] = lax.dot_general(qbf_ref[rows, :], k, _NT, preferred_element_type=F32) * scale
                dw_ref[slot, rows, :] = lax.dot_general(dobf_ref[rows, :], v, _NT, preferred_element_type=F32)

        def apply_stored(kb, u):
            k0 = pl.multiple_of(kb * tk, tk)
            k = k_ref[pl.ds(k0, tk), :]
            for rows in rows_from(u):
                acc_ref[rows, :] += jnp.dot(dzp_ref[rows, :], k, preferred_element_type=F32)
            seen = slice(u * tk, tq)
            dk_ref[:, pl.ds(k0, tk)] += jnp.dot(qt_ref[:, seen], dzp_ref[seen, :], preferred_element_type=F32)
            dv_ref[:, pl.ds(k0, tk)] += jnp.dot(dot_ref[:, seen], wp_ref[seen, :], preferred_element_type=F32)

        def differentiate(slot, groups):
            dzs, ws = grads([z_ref[slot, rows, :] for rows, _ in groups], [dw_ref[slot, rows, :] for rows, _ in groups],
                            groups)
            for (rows, _), dz, w in zip(groups, dzs, ws):
                dzp_ref[rows, :] = dz
                wp_ref[rows, :] = w

        logits_into(0, 0, all_rows)

        def half_step(j, slot):
            apply_stored(jnp.maximum(j - 1, 0), 0)
            logits_into(1 - slot, j + 1, all_rows)
            differentiate(slot, [(rows, False) for rows in all_rows])

        def step(i, carry):
            half_step(2 * i, 0)
            half_step(2 * i + 1, 1)
            return carry

        lax.fori_loop(0, n // 2, step, 0)
        for u in range(nd):
            slot = u % 2
            if u == 0:
                apply_stored(jnp.maximum(n - 1, 0), 0)
            else:
                apply_stored(n + u - 1, u - 1)
            if u + 1 < nd:
                logits_into(1 - slot, n + u + 1, rows_from(u + 1))
            differentiate(slot, [(rows, j == 0) for j, rows in enumerate(rows_from(u))])
        apply_stored(n + nd - 1, nd - 1)
        dq_ref[...] = acc_ref[...].astype(BF16)

    blk = lambda h, i: (i, h)
    whole = lambda h, i: (0, h)
    return pl.pallas_call(
        body, name=name, grid=(nh, t // tq),
        in_specs=[pl.BlockSpec((tq, LANES), blk), pl.BlockSpec((tq, LANES), lambda h, i: (i, nh + h)),
                  pl.BlockSpec((t, LANES), whole), pl.BlockSpec((t, LANES), lambda h, i: (0, nh + h)),
                  pl.BlockSpec((tq, LANES), blk), pl.BlockSpec((tq, LANES), blk), pl.BlockSpec((tq, LANES), blk)],
        out_specs=[pl.BlockSpec((tq, LANES), blk), pl.BlockSpec((tq, LANES), blk),
                   pl.BlockSpec((LANES, t), lambda h, i: (h, 0)), pl.BlockSpec((LANES, t), lambda h, i: (h, 0))],
        out_shape=[jax.ShapeDtypeStruct((t, ds), BF16), jax.ShapeDtypeStruct((t, ds), BF16),
                   jax.ShapeDtypeStruct((ds, t), F32), jax.ShapeDtypeStruct((ds, t), F32)],
        scratch_shapes=[pltpu.VMEM((tq, LANES), BF16), pltpu.VMEM((tq, LANES), BF16),
                        pltpu.VMEM((LANES, tq), BF16), pltpu.VMEM((LANES, tq), BF16), pltpu.VMEM((tq, LANES), F32),
                        pltpu.VMEM((tq, 1), F32), pltpu.VMEM((tq, 1), F32),
                        pltpu.VMEM((2, tq, tk), F32), pltpu.VMEM((2, tq, tk), F32),
                        pltpu.VMEM((tq, tk), BF16), pltpu.VMEM((tq, tk), BF16)],
        compiler_params=_cparams(2),
    )(proj, proj, kv, kv, o_all, tl_all, dmix)


def _place():
    x, y, c = lax.axis_index("x"), lax.axis_index("y"), lax.axis_index("c")
    chips = [(1 - x, y), (x, 1 - y), (1 - x, 1 - y)]
    return x, y, c, chips


def _remote(src, dst, send_sems, recv_sems, k, to):
    return pltpu.make_async_remote_copy(src_ref=src, dst_ref=dst, send_sem=send_sems.at[k], recv_sem=recv_sems.at[k],
                                        device_id=to, device_id_type=MESH)


def _my_chip():
    return 2 * lax.axis_index("x") + lax.axis_index("y")


def _place_own(shard, name):
    r, w = shard.shape
    tr = _tile(r, FLAT_TR)

    def body(x_ref, o_ref):
        o_ref[...] = x_ref[...]

    return pl.pallas_call(
        body, name=name, out_shape=jax.ShapeDtypeStruct((N_CHIPS, r, w), shard.dtype), grid=(r // tr,),
        in_specs=[pl.BlockSpec((tr, w), lambda i: (i, 0))],
        out_specs=pl.BlockSpec((None, tr, w), lambda i: (_my_chip(), i, 0)),
        compiler_params=_cparams(1),
    )(shard)


def _chip_all_gather(shards, name):
    n = len(shards)

    def body(*refs):
        x_refs, out_refs, send_sems, recv_sems = refs[:n], refs[2 * n:3 * n], refs[3 * n], refs[3 * n + 1]
        x, y, c, chips = _place()
        me = 2 * x + y
        sibling = (x, y, 1 - c)

        def rows(t, core):
            rh = x_refs[t].shape[0] // 2
            return pl.ds(core * rh, rh)

        first = [_remote(x_refs[t].at[rows(t, c)], out_refs[t].at[me, rows(t, c)], send_sems, recv_sems, 6 * t + k,
                         (cx, cy, c)) for t in range(n) for k, (cx, cy) in enumerate(chips)]
        for cp in first:
            cp.start()
        passed = []
        for k, (cx, cy) in enumerate(chips):
            for t in range(n):
                got = out_refs[t].at[2 * cx + cy, rows(t, c)]
                _remote(got, got, send_sems, recv_sems, 6 * t + k, (cx, cy, c)).wait_recv()
                fwd = _remote(got, got, send_sems, recv_sems, 6 * t + 3 + k, sibling)
                fwd.start()
                passed.append(fwd)
        for k, (cx, cy) in enumerate(chips):
            for t in range(n):
                got = out_refs[t].at[2 * cx + cy, rows(t, 1 - c)]
                _remote(got, got, send_sems, recv_sems, 6 * t + 3 + k, sibling).wait_recv()
        for cp in first + passed:
            cp.wait_send()

    bufs = [_place_own(s, f"{name}_own{t}") for t, s in enumerate(shards)]
    return pl.pallas_call(
        body, name=name, in_specs=[ANY] * (2 * n), out_specs=[ANY] * n,
        out_shape=[jax.ShapeDtypeStruct((N_CHIPS,) + s.shape, s.dtype) for s in shards],
        input_output_aliases={n + t: t for t in range(n)},
        scratch_shapes=[pltpu.SemaphoreType.DMA((6 * n,)), pltpu.SemaphoreType.DMA((6 * n,))],
    )(*shards, *bufs)


def _chip_gather_start(shards, name):
    n = len(shards)

    def body(*refs):
        x_refs, buf_refs, send_sems, recv_sems, token = refs[:n], refs[n:2 * n], refs[2 * n], refs[2 * n + 1], refs[-1]
        x, y, c, chips = _place()
        me = 2 * x + y
        for t in range(n):
            for k, (cx, cy) in enumerate(chips):
                _remote(x_refs[t], buf_refs[t].at[me], send_sems, recv_sems, 3 * t + k, (cx, cy, c)).start()
        token[...] = jnp.zeros_like(token)

    bufs = [_place_own(s, f"{name}_own{t}") for t, s in enumerate(shards)]
    hbm = [pltpu.HBM(a.shape, a.dtype) for a in list(shards) + bufs]
    outs = pl.pallas_call(
        body, name=name, in_specs=[HBM] * (2 * n),
        out_shape=(pltpu.SemaphoreType.DMA((3 * n,)), pltpu.SemaphoreType.DMA((3 * n,)), *hbm,
                   jax.ShapeDtypeStruct((8, LANES), F32)),
        out_specs=(SEM, SEM, *[HBM] * (2 * n), pl.BlockSpec(memory_space=pltpu.VMEM)),
        input_output_aliases={t: 2 + t for t in range(2 * n)},
        compiler_params=pltpu.CompilerParams(has_side_effects=EFFECT),
    )(*[pltpu.with_memory_space_constraint(a, pltpu.HBM) for a in list(shards) + bufs])
    return outs[0], outs[1], outs[2:2 + n], outs[2 + n:2 + 2 * n], outs[-1]


def _chip_gather_wait(send_sems, recv_sems, shards, bufs, after, name):
    n = len(shards)

    def body(*refs):
        x_refs, buf_refs, send, recv = refs[:n], refs[n:2 * n], refs[2 * n], refs[2 * n + 1]
        x, y, c, chips = _place()
        for t in range(n):
            for k, (cx, cy) in enumerate(chips):
                cp = _remote(x_refs[t], buf_refs[t].at[2 * cx + cy], send, recv, 3 * t + k, (cx, cy, c))
                cp.wait_send()
                cp.wait_recv()

    hbm = [pltpu.HBM(a.shape, a.dtype) for a in list(shards) + list(bufs)]
    outs = pl.pallas_call(
        body, name=name, in_specs=[HBM] * (2 * n) + [SEM, SEM, ANY],
        out_shape=tuple(hbm), out_specs=tuple([HBM] * (2 * n)),
        input_output_aliases={t: t for t in range(2 * n)},
        compiler_params=pltpu.CompilerParams(has_side_effects=EFFECT),
    )(*shards, *bufs, send_sems, recv_sems, after)
    return outs[n:]


def _sibling_take_half(ss, name):
    n = len(ss)

    def body(*refs):
        s_refs, a_refs, send_sems, recv_sems = refs[:n], refs[n:2 * n], refs[2 * n], refs[2 * n + 1]
        x, y, c, _ = _place()
        cps = []
        for t in range(n):
            rh = s_refs[t].shape[1] // 2
            cps.append(_remote(s_refs[t].at[:, pl.ds((1 - c) * rh, rh), :], a_refs[t], send_sems, recv_sems, t,
                               (x, y, 1 - c)))
        for cp in cps:
            cp.start()
        for cp in cps:
            cp.wait()

    return pl.pallas_call(
        body, name=name, in_specs=[ANY] * n, out_specs=[ANY] * n,
        out_shape=[jax.ShapeDtypeStruct((s.shape[0], s.shape[1] // 2, s.shape[2]), s.dtype) for s in ss],
        scratch_shapes=[pltpu.SemaphoreType.DMA((n,)), pltpu.SemaphoreType.DMA((n,))],
    )(*ss)


def _pair_sum(s, a, dtype, name):
    n, r, w = s.shape
    rh = r // 2
    tr = _tile(rh, FLAT_TR)
    nblk = rh // tr

    def body(s_ref, a_ref, o_ref):
        o_ref[...] = (s_ref[...] + a_ref[...]).astype(dtype)

    return pl.pallas_call(
        body, name=name, out_shape=jax.ShapeDtypeStruct((n, rh, w), dtype), grid=(n, nblk),
        in_specs=[pl.BlockSpec((None, tr, w), lambda k, i: (k, lax.axis_index("c") * nblk + i, 0)),
                  pl.BlockSpec((None, tr, w), lambda k, i: (k, i, 0))],
        out_specs=pl.BlockSpec((None, tr, w), lambda k, i: (k, i, 0)),
        compiler_params=_cparams(2),
    )(s, a)


def _chip_scatter(ps, name):
    n = len(ps)

    def body(*refs):
        p_refs, b_refs, send_sems, recv_sems = refs[:n], refs[n:2 * n], refs[2 * n], refs[2 * n + 1]
        x, y, c, chips = _place()
        me = 2 * x + y
        sends = [_remote(p_refs[t].at[2 * cx + cy], b_refs[t].at[me], send_sems, recv_sems, 3 * t + k, (cx, cy, c))
                 for t in range(n) for k, (cx, cy) in enumerate(chips)]
        for cp in sends:
            cp.start()
        for t in range(n):
            for k, (cx, cy) in enumerate(chips):
                got = b_refs[t].at[2 * cx + cy]
                _remote(got, got, send_sems, recv_sems, 3 * t + k, (cx, cy, c)).wait_recv()
        for cp in sends:
            cp.wait_send()

    return pl.pallas_call(
        body, name=name, in_specs=[ANY] * n, out_specs=[ANY] * n,
        out_shape=[jax.ShapeDtypeStruct(p.shape, p.dtype) for p in ps],
        scratch_shapes=[pltpu.SemaphoreType.DMA((3 * n,)), pltpu.SemaphoreType.DMA((3 * n,))],
    )(*ps)


def _chip_scatter_start(ps, name):
    n = len(ps)

    def body(*refs):
        p_refs, b_refs, send_sems, recv_sems, token = refs[:n], refs[n:2 * n], refs[2 * n], refs[2 * n + 1], refs[-1]
        x, y, c, chips = _place()
        me = 2 * x + y
        for t in range(n):
            for k, (cx, cy) in enumerate(chips):
                _remote(p_refs[t].at[2 * cx + cy], b_refs[t].at[me], send_sems, recv_sems, 3 * t + k, (cx, cy, c)).start()
        token[...] = jnp.zeros_like(token)

    hbm = [pltpu.HBM(p.shape, p.dtype) for p in ps]
    outs = pl.pallas_call(
        body, name=name, in_specs=[HBM] * (2 * n),
        out_shape=(pltpu.SemaphoreType.DMA((3 * n,)), pltpu.SemaphoreType.DMA((3 * n,)), *hbm, *hbm,
                   jax.ShapeDtypeStruct((8, LANES), F32)),
        out_specs=(SEM, SEM, *[HBM] * (2 * n), pl.BlockSpec(memory_space=pltpu.VMEM)),
        input_output_aliases={t: 2 + t for t in range(2 * n)},
        compiler_params=pltpu.CompilerParams(has_side_effects=EFFECT),
    )(*[pltpu.with_memory_space_constraint(p, pltpu.HBM) for p in ps],
      *[pltpu.with_memory_space_constraint(lax.empty(p.shape, p.dtype), pltpu.HBM) for p in ps])
    return outs[0], outs[1], outs[2:2 + n], outs[2 + n:2 + 2 * n], outs[-1]


def _chip_scatter_wait(send_sems, recv_sems, ps, bs, after, name):
    n = len(ps)

    def body(*refs):
        p_refs, b_refs, send, recv = refs[:n], refs[n:2 * n], refs[2 * n], refs[2 * n + 1]
        x, y, c, chips = _place()
        for t in range(n):
            for k, (cx, cy) in enumerate(chips):
                cp = _remote(p_refs[t].at[2 * cx + cy], b_refs[t].at[2 * cx + cy], send, recv, 3 * t + k, (cx, cy, c))
                cp.wait_send()
                cp.wait_recv()

    hbm = [pltpu.HBM(p.shape, p.dtype) for p in ps]
    outs = pl.pallas_call(
        body, name=name, in_specs=[HBM] * (2 * n) + [SEM, SEM, ANY],
        out_shape=(*hbm, *hbm), out_specs=tuple([HBM] * (2 * n)),
        input_output_aliases={t: t for t in range(2 * n)},
        compiler_params=pltpu.CompilerParams(has_side_effects=EFFECT),
    )(*ps, *bs, send_sems, recv_sems, after)
    return outs[n:]


def _chip_sum(p, b, name):
    n, rh, w = p.shape
    tr = _tile(rh, FLAT_TR)
    nblk = rh // tr

    def body(p_ref, b0_ref, b1_ref, b2_ref, b3_ref, o_ref):
        me = _my_chip()
        own = p_ref[...]
        t = [jnp.where(me == k, own, b_ref[...]).astype(F32) for k, b_ref in enumerate((b0_ref, b1_ref, b2_ref, b3_ref))]
        o_ref[...] = ((t[0] + t[1]) + t[2]) + t[3]

    def other(k):
        return lambda i: (jnp.where(_my_chip() == k, (k + 1) % N_CHIPS, k), i, 0)

    return pl.pallas_call(
        body, name=name, out_shape=jax.ShapeDtypeStruct((2 * rh, w), F32), grid=(nblk,),
        in_specs=[pl.BlockSpec((None, tr, w), lambda i: (_my_chip(), i, 0))]
        + [pl.BlockSpec((None, tr, w), other(k)) for k in range(N_CHIPS)],
        out_specs=pl.BlockSpec((tr, w), lambda i: (lax.axis_index("c") * nblk + i, 0)),
        compiler_params=_cparams(1),
    )(p, b, b, b, b)


def _sibling_join(gs, name):
    n = len(gs)

    def body(*refs):
        g_refs, send_sems, recv_sems = refs[n:2 * n], refs[2 * n], refs[2 * n + 1]
        x, y, c, _ = _place()
        cps = []
        for t in range(n):
            rh = g_refs[t].shape[0] // 2
            mine = g_refs[t].at[pl.ds(c * rh, rh)]
            cps.append(_remote(mine, mine, send_sems, recv_sems, t, (x, y, 1 - c)))
        for cp in cps:
            cp.start()
        for t in range(n):
            rh = g_refs[t].shape[0] // 2
            theirs = g_refs[t].at[pl.ds((1 - c) * rh, rh)]
            _remote(theirs, theirs, send_sems, recv_sems, t, (x, y, 1 - c)).wait_recv()
        for cp in cps:
            cp.wait_send()

    return pl.pallas_call(
        body, name=name, in_specs=[ANY] * n, out_specs=[ANY] * n,
        out_shape=[jax.ShapeDtypeStruct(g.shape, g.dtype) for g in gs],
        input_output_aliases={t: t for t in range(n)},
        scratch_shapes=[pltpu.SemaphoreType.DMA((n,)), pltpu.SemaphoreType.DMA((n,))],
    )(*gs)


def _adamw(g, w, m, v, name):
    r, wd = g.shape

    def body(g_ref, w_ref, m_ref, v_ref, d_ref, mo_ref, vo_ref):
        gv = g_ref[...]
        mn = ADAM_B1 * m_ref[...] + (1.0 - ADAM_B1) * gv
        vn = ADAM_B2 * v_ref[...] + (1.0 - ADAM_B2) * (gv * gv)
        m_hat = mn / (1.0 - ADAM_B1 ** ADAM_STEP)
        v_hat = vn / (1.0 - ADAM_B2 ** ADAM_STEP)
        d_ref[...] = -ADAM_LR * (m_hat / (jnp.sqrt(v_hat) + ADAM_EPS) + ADAM_WD * w_ref[...])
        mo_ref[...] = mn
        vo_ref[...] = vn

    tr = _tile(r, FLAT_TR)
    row = lambda i: (i, 0)
    spec = pl.BlockSpec((tr, wd), row)
    return pl.pallas_call(
        body, name=name, grid=(r // tr,), in_specs=[spec] * 4, out_specs=[spec] * 3,
        out_shape=[jax.ShapeDtypeStruct((r, wd), F32)] * 3,
        compiler_params=_cparams(1),
    )(g, w, m, v)


def _pair_sums(ss, names):
    a = _sibling_take_half(ss, "grad_sibling_half_" + names[0])
    return [_pair_sum(s, a_t, BF16 if n in BIG else F32, f"grad_pair_sum_{n}") for s, a_t, n in zip(ss, a, names)]


def _finish_reduce(ps, bs, names):
    g = [_chip_sum(p_t, b_t, f"grad_chip_sum_{n}") for p_t, b_t, n in zip(ps, bs, names)]
    return _sibling_join(g, "grad_sibling_join")


WEIGHTS = ("mem_norm", "w_mem_kv", "norm_a", "w_in_a", "conv_w", "conv_b", "w_rec_gate", "b_rec_gate", "w_in_gate",
           "b_in_gate", "lru_lambda", "w_out_a", "kv_norm", "w_kv", "norm_b", "w_in_b", "w_out_b", "final_norm")
SHARD_DIM = {"mem_norm": None, "w_mem_kv": 1, "norm_a": 1, "w_in_a": 2, "conv_w": 2, "conv_b": 1, "w_rec_gate": None,
             "b_rec_gate": 1, "w_in_gate": None, "b_in_gate": 1, "lru_lambda": 1, "w_out_a": 1, "kv_norm": None,
             "w_kv": 1, "norm_b": None, "w_in_b": 2, "w_out_b": 1, "final_norm": None}
BIG = ("w_mem_kv", "w_in_a", "w_out_a", "w_kv", "w_in_b", "w_out_b")
SMALL = ("norm_a", "conv_w", "conv_b", "b_rec_gate", "b_in_gate", "lru_lambda")


def _pad_rows(flat, row_multiple):
    per = FLAT_W * row_multiple
    n = flat.shape[0]
    total = -(-n // per) * per
    return jnp.pad(flat, (0, total - n)).reshape(total // FLAT_W, FLAT_W)


def _flatten(parts, row_multiple):
    return _pad_rows(jnp.concatenate([p.reshape(-1) for p in parts]), row_multiple)


def _unflatten(flat2d, shapes):
    flat = flat2d.reshape(-1)
    out, off = [], 0
    for shp in shapes:
        n = 1
        for s in shp:
            n *= s
        out.append(flat[off:off + n].reshape(shp))
        off += n
    return out


LATE = ("w_kv", "w_in_b", "w_out_b")


def _bf16_rows(w):
    return w.astype(BF16).reshape(-1, w.shape[-1])


def _whole(gathered, shape, dim):
    return jnp.concatenate([gathered[k].reshape(shape) for k in range(N_CHIPS)], axis=dim)


def _gather_weights(local):
    early = [n for n in BIG if n not in LATE]
    small_shapes = [local[n].shape for n in SMALL]
    send_sems, recv_sems, shards, bufs, token = _chip_gather_start([_bf16_rows(local[n]) for n in LATE],
                                                                   "late_weights_gather_start")
    gathered = _chip_all_gather([_bf16_rows(local[n]) for n in early] + [_flatten([local[n] for n in SMALL], 16)],
                                "weights_all_gather")
    full = {n: _whole(g, local[n].shape, SHARD_DIM[n]) for n, g in zip(early, gathered)}
    per_chip = [_unflatten(gathered[-1][k], small_shapes) for k in range(N_CHIPS)]
    for i, n in enumerate(SMALL):
        full[n] = jnp.concatenate([per_chip[k][i] for k in range(N_CHIPS)], axis=SHARD_DIM[n])

    def late(after):
        got = _chip_gather_wait(send_sems, recv_sems, shards, bufs, after, "late_weights_gather_wait")
        return {n: _whole(g, local[n].shape, SHARD_DIM[n]) for n, g in zip(LATE, got)}

    return full, token, late


def _piece(g, name, k):
    dim = SHARD_DIM[name]
    if dim is None:
        return g
    n = g.shape[dim] // N_CHIPS
    return lax.slice_in_dim(g, k * n, (k + 1) * n, axis=dim)


def _local_grads(x, mem, tgt, wts, token, late):
    t, d = x.shape
    depth = wts["w_mem_kv"].shape[0]
    n_a = wts["w_in_a"].shape[0]
    n_b = wts["norm_b"].shape[0]
    nb = wts["w_rec_gate"].shape[1]
    dr = nb * LANES
    dm = wts["w_mem_kv"].shape[2] // 2
    row = lambda v: v.reshape(1, -1)

    wm_all = jnp.concatenate([wts["w_mem_kv"][l] for l in range(depth)], axis=1)
    memkv, memn_bf = _norm_matmul(mem, row(wts["mem_norm"]), wm_all, BF16, "mem_kv_proj", after=token)

    h = x
    saved = []
    vecs = []
    for l in range(n_a):
        proj, u_bf = _norm_matmul(h, row(wts["norm_a"][l]), wts["w_in_a"][l], F32, f"a{l}_in_proj")
        vec = jnp.concatenate([row(wts["conv_b"][l]), row(wts["b_rec_gate"][l]), row(wts["b_in_gate"][l]),
                               row(wts["lru_lambda"][l]), wts["conv_w"][l]], axis=0)
        vecs.append(vec)
        y_rnn, h_rnn, xc = _lru_fwd(proj, vec, wts["w_rec_gate"][l], wts["w_in_gate"][l], f"a{l}_lru_fwd")
        y_mem = _mem_attn_fwd(proj, memkv, l, dr, dm, f"a{l}_mem_fwd")
        mix = jnp.concatenate([y_rnn, y_mem], axis=1)
        h_next = _matmul_res(mix, wts["w_out_a"][l], h, f"a{l}_out_proj")
        saved.append((h, proj, u_bf, mix, h_rnn, xc))
        h = h_next

    h_kv = h
    wts = {**wts, **late(h_kv)}
    kv, ukv_bf = _norm_matmul(h_kv, row(wts["kv_norm"]), wts["w_kv"], BF16, "kv_proj")

    for j in range(n_b):
        l = n_a + j
        proj, u_bf = _norm_matmul(h, row(wts["norm_b"][j]), wts["w_in_b"][j], F32, f"b{j}_in_proj")
        y_sb, o_sb, tl_sb = _sb_fwd(proj, kv, f"b{j}_sb_fwd")
        y_mem = _mem_attn_fwd(proj, memkv, l, dr, dm, f"b{j}_mem_fwd")
        mix = jnp.concatenate([y_sb, y_mem], axis=1)
        h_next = _matmul_res(mix, wts["w_out_b"][j], h, f"b{j}_out_proj")
        saved.append((h, proj, u_bf, mix, o_sb, tl_sb))
        h = h_next

    dh, d_final, loss = _final_loss_bwd(h, row(wts["final_norm"]), tgt, "final_loss_bwd")

    grads = {"final_norm": d_final.reshape(-1)}
    big = {}
    dmemkv = [None] * depth
    g_norm_b = [None] * n_b
    dks, dvs = [], []
    for j in reversed(range(n_b)):
        l = n_a + j
        h_in, proj, u_bf, mix, o_sb, tl_sb = saved[l]
        dmix = _matmul_nt(dh, wts["w_out_b"][j], f"b{j}_dmix")
        big["w_out_b"] = _dw_rows(mix, dh, j, n_b, big.get("w_out_b"), f"b{j}_dw_out")
        dq, dg, dk, dv = _sb_bwd(proj, kv, o_sb, tl_sb, dmix, f"b{j}_sb_bwd")
        dqm, dgm, dkm, dvm = _mem_attn_bwd(proj, memkv, dmix, l, dr, dm, f"b{j}_mem_bwd")
        dmemkv[l] = (dkm, dvm)
        dproj = jnp.concatenate([dq, dg, dqm, dgm], axis=1)
        du = _matmul_nt(dproj, wts["w_in_b"][j], f"b{j}_du")
        big["w_in_b"] = _dw_cols(u_bf, dproj, j, n_b, big.get("w_in_b"), f"b{j}_dw_in")
        dh, dgn = _rms_bwd(du, h_in, row(wts["norm_b"][j]), dh, f"b{j}_rms_bwd")
        g_norm_b[j] = dgn.reshape(-1)
        dks.append(dk)
        dvs.append(dv)
    assert n_b == 2
    dkv = jnp.concatenate([_add_cast(dks[0], dks[1], "dk_sum").T, _add_cast(dvs[0], dvs[1], "dv_sum").T], axis=1)
    du = _matmul_nt(dkv, wts["w_kv"], "kv_du")
    big["w_kv"] = _dw_cols(ukv_bf, dkv, 0, 1, None, "kv_dw")
    dh, dgn = _rms_bwd(du, h_kv, row(wts["kv_norm"]), dh, "kv_rms_bwd")
    grads["kv_norm"] = dgn.reshape(-1)

    early_names = [n for n in BIG if n in big]
    early_p = _pair_sums([big[n].reshape(N_CHIPS, -1, big[n].shape[-1]) for n in early_names], early_names)
    send_sems, recv_sems, early_p, early_b, token = _chip_scatter_start(early_p, "grad_chip_scatter_start")

    g_norm_a = [None] * n_a
    g_wr, g_wi, g_vec = [None] * n_a, [None] * n_a, [None] * n_a
    for l in reversed(range(n_a)):
        h_in, proj, u_bf, mix, h_rnn, xc = saved[l]
        dmix = _matmul_nt(dh, wts["w_out_a"][l], f"a{l}_dmix", after=token if l == n_a - 1 else None)
        big["w_out_a"] = _dw_rows(mix, dh, l, n_a, big.get("w_out_a"), f"a{l}_dw_out")
        dx, dg, g_wr[l], g_wi[l], dvec = _lru_bwd(proj, xc, h_rnn, dmix, vecs[l], wts["w_rec_gate"][l],
                                                  wts["w_in_gate"][l], f"a{l}_lru_bwd")
        g_vec[l] = dvec.transpose(1, 0, 2).reshape(8, dr)
        dqm, dgm, dkm, dvm = _mem_attn_bwd(proj, memkv, dmix, l, dr, dm, f"a{l}_mem_bwd")
        dmemkv[l] = (dkm, dvm)
        dproj = jnp.concatenate([dx, dg, dqm, dgm], axis=1)
        du = _matmul_nt(dproj, wts["w_in_a"][l], f"a{l}_du")
        big["w_in_a"] = _dw_cols(u_bf, dproj, l, n_a, big.get("w_in_a"), f"a{l}_dw_in")
        dh, dgn = _rms_bwd(du, h_in, row(wts["norm_a"][l]), dh, f"a{l}_rms_bwd")
        g_norm_a[l] = dgn.reshape(-1)

    dmemkv_all = jnp.concatenate([jnp.concatenate(p, axis=1) for p in dmemkv], axis=1).astype(BF16)
    pk = d // N_CHIPS
    big["w_mem_kv"] = _matmul_tn(memn_bf, dmemkv_all, "mem_dw", pk, 2 * dm, (N_CHIPS, depth, pk, 2 * dm),
                                 (None, None, pk, 2 * dm), lambda i, j: (i, j, 0, 0))
    dmemn = _matmul_nt(dmemkv_all, wm_all, "mem_du")
    _, dgn = _rms_bwd(dmemn, mem, row(wts["mem_norm"]), jnp.zeros_like(mem), "mem_rms_bwd")
    grads["mem_norm"] = dgn.reshape(-1)
    grads["norm_a"] = jnp.stack(g_norm_a)
    grads["w_rec_gate"] = jnp.stack(g_wr)
    grads["w_in_gate"] = jnp.stack(g_wi)
    gv = jnp.stack(g_vec)
    grads["conv_b"], grads["b_rec_gate"], grads["b_in_gate"], grads["lru_lambda"] = gv[:, 0], gv[:, 1], gv[:, 2], gv[:, 3]
    grads["conv_w"] = gv[:, 4:8]
    grads["norm_b"] = jnp.stack(g_norm_b)
    early_b = _chip_scatter_wait(send_sems, recv_sems, early_p, early_b, dh, "grad_chip_scatter_wait")
    early = (early_names, early_p, early_b)
    big = {n: g.reshape(N_CHIPS, -1, g.shape[-1]) for n, g in big.items() if n not in early_names}
    return loss, dh, grads, big, early


def kernel(x, mem, mem_norm, w_mem_kv, norm_a, w_in_a, conv_w, conv_b, w_rec_gate, b_rec_gate, w_in_gate, b_in_gate, lru_lambda, w_out_a, kv_norm, w_kv, norm_b, w_in_b, w_out_b, final_norm, loss_target, m_mem_norm, m_w_mem_kv, m_norm_a, m_w_in_a, m_conv_w, m_conv_b, m_w_rec_gate, m_b_rec_gate, m_w_in_gate, m_b_in_gate, m_lru_lambda, m_w_out_a, m_kv_norm, m_w_kv, m_norm_b, m_w_in_b, m_w_out_b, m_final_norm, v_mem_norm, v_w_mem_kv, v_norm_a, v_w_in_a, v_conv_w, v_conv_b, v_w_rec_gate, v_b_rec_gate, v_w_in_gate, v_b_in_gate, v_lru_lambda, v_w_out_a, v_kv_norm, v_w_kv, v_norm_b, v_w_in_b, v_w_out_b, v_final_norm):
    local = dict(mem_norm=mem_norm, w_mem_kv=w_mem_kv, norm_a=norm_a, w_in_a=w_in_a, conv_w=conv_w, conv_b=conv_b,
                 w_rec_gate=w_rec_gate, b_rec_gate=b_rec_gate, w_in_gate=w_in_gate, b_in_gate=b_in_gate,
                 lru_lambda=lru_lambda, w_out_a=w_out_a, kv_norm=kv_norm, w_kv=w_kv, norm_b=norm_b, w_in_b=w_in_b,
                 w_out_b=w_out_b, final_norm=final_norm)
    mom = dict(mem_norm=m_mem_norm, w_mem_kv=m_w_mem_kv, norm_a=m_norm_a, w_in_a=m_w_in_a, conv_w=m_conv_w,
               conv_b=m_conv_b, w_rec_gate=m_w_rec_gate, b_rec_gate=m_b_rec_gate, w_in_gate=m_w_in_gate,
               b_in_gate=m_b_in_gate, lru_lambda=m_lru_lambda, w_out_a=m_w_out_a, kv_norm=m_kv_norm, w_kv=m_w_kv,
               norm_b=m_norm_b, w_in_b=m_w_in_b, w_out_b=m_w_out_b, final_norm=m_final_norm)
    var = dict(mem_norm=v_mem_norm, w_mem_kv=v_w_mem_kv, norm_a=v_norm_a, w_in_a=v_w_in_a, conv_w=v_conv_w,
               conv_b=v_conv_b, w_rec_gate=v_w_rec_gate, b_rec_gate=v_b_rec_gate, w_in_gate=v_w_in_gate,
               b_in_gate=v_b_in_gate, lru_lambda=v_lru_lambda, w_out_a=v_w_out_a, kv_norm=v_kv_norm, w_kv=v_w_kv,
               norm_b=v_norm_b, w_in_b=v_w_in_b, w_out_b=v_w_out_b, final_norm=v_final_norm)

    wts, token, late = _gather_weights(local)
    for n in WEIGHTS:
        if SHARD_DIM[n] is None:
            wts[n] = local[n]
    wts["w_rec_gate"] = wts["w_rec_gate"].astype(BF16)
    wts["w_in_gate"] = wts["w_in_gate"].astype(BF16)

    loss, grad_x, grads, big, (early_names, early_p, early_b) = _local_grads(x[0], mem[0], loss_target[0], wts, token, late)

    rest = [n for n in WEIGHTS if n not in BIG]
    row_multiple = 2 * FLAT_TR
    s_rest = jnp.stack([_flatten([_piece(grads[n], n, k) for n in rest], row_multiple) for k in range(N_CHIPS)])
    late_names = [n for n in BIG if n in big] + ["rest"]
    late_p = _pair_sums([big[n] for n in late_names[:-1]] + [s_rest], late_names)
    late_b = _chip_scatter(late_p, "grad_chip_scatter")
    names = early_names + late_names
    reduced = dict(zip(names, _finish_reduce(list(early_p) + list(late_p), list(early_b) + list(late_b), names)))

    g_out, d_out, m_out, v_out = {}, {}, {}, {}
    for n in BIG:
        g = reduced[n]
        shape = local[n].shape
        flat = lambda a: a.reshape(-1, shape[-1])
        d, mo, vo = _adamw(g, flat(local[n]), flat(mom[n]), flat(var[n]), f"adamw_{n}")
        g_out[n], d_out[n], m_out[n], v_out[n] = (a.reshape(shape) for a in (g, d, mo, vo))
    g_rest = reduced["rest"]
    d_rest, m_rest, v_rest = _adamw(g_rest, *(_flatten([src[n] for n in rest], row_multiple) for src in (local, mom, var)),
                                    "adamw_rest")
    shapes = [local[n].shape for n in rest]
    for out, flat2d in ((g_out, g_rest), (d_out, d_rest), (m_out, m_rest), (v_out, v_rest)):
        out.update(zip(rest, _unflatten(flat2d, shapes)))

    total_loss = lax.psum(loss[0, 0], MESH_AXES)
    return (total_loss, grad_x[None], *[g_out[n] for n in WEIGHTS], *[d_out[n] for n in WEIGHTS],
            *[m_out[n] for n in WEIGHTS], *[v_out[n] for n in WEIGHTS])
```

```python
import functools

import jax
import jax.numpy as jnp
from jax import lax
from jax.experimental import pallas as pl
from jax.experimental.pallas import tpu as pltpu

F32 = jnp.float32
BF16 = jnp.bfloat16

RMS_EPS = 1e-6
LRU_C = 8.0
ADAM_LR = 0.001
ADAM_B1 = 0.9
ADAM_B2 = 0.999
ADAM_EPS = 1e-08
ADAM_WD = 0.01
ADAM_STEP = 10

LANES = 128
VMEM_LIMIT = 56 * 1024 * 1024
FLAT_W = 1024
FLAT_TR = 256
N_CHIPS = 4
MESH_AXES = ("x", "y", "c")

_NT = (((1,), (1,)), ((), ()))
_TN = (((0,), (0,)), ((), ()))
ANY = pl.BlockSpec(memory_space=pl.ANY)
HBM = pl.BlockSpec(memory_space=pltpu.HBM)
SEM = pl.BlockSpec(memory_space=pltpu.SEMAPHORE)
EFFECT = pltpu.SideEffectType.DATAFLOW_SIDE_EFFECTING
MESH = pl.DeviceIdType.MESH


def _cparams(n_axes):
    return pltpu.CompilerParams(dimension_semantics=("arbitrary",) * n_axes, vmem_limit_bytes=VMEM_LIMIT)


def _sigmoid(x):
    return 1.0 / (1.0 + jnp.exp(-x))


def _log1p_pos(e):
    return jnp.where(e < 1e-3, e * (1.0 - e * (0.5 - e * (1.0 / 3.0))), jnp.log(1.0 + e))


def _neg_expm1(x):
    small = -x * (1.0 + x * (0.5 + x * (1.0 / 6.0 + x * (1.0 / 24.0))))
    return jnp.where(x > -0.05, small, 1.0 - jnp.exp(x))


def _tile(n, want):
    if n <= want:
        return n
    t = want
    while n % t:
        t -= LANES
    assert t > 0, (n, want)
    return t


def _norm_matmul(x, g, w, out_dtype, name, after=None):
    m, k = x.shape
    n = w.shape[1]
    tm, tn = _tile(m, 1024), _tile(n, 1024)

    def body(x_ref, g_ref, w_ref, *rest):
        o_ref, u_ref = rest[-2:]

        @pl.when(pl.program_id(1) == 0)
        def _():
            xf = x_ref[...]
            r = lax.rsqrt(jnp.mean(xf * xf, axis=-1, keepdims=True) + RMS_EPS)
            u_ref[...] = ((xf * r) * g_ref[...]).astype(BF16)

        o_ref[...] = jnp.dot(u_ref[...], w_ref[...], preferred_element_type=F32).astype(o_ref.dtype)

    return pl.pallas_call(
        body, name=name, grid=(m // tm, n // tn),
        in_specs=[pl.BlockSpec((tm, k), lambda i, j: (i, 0)), pl.BlockSpec((1, k), lambda i, j: (0, 0)),
                  pl.BlockSpec((k, tn), lambda i, j: (0, j))] + ([] if after is None else [ANY]),
        out_specs=[pl.BlockSpec((tm, tn), lambda i, j: (i, j)), pl.BlockSpec((tm, k), lambda i, j: (i, 0))],
        out_shape=[jax.ShapeDtypeStruct((m, n), out_dtype), jax.ShapeDtypeStruct((m, k), BF16)],
        compiler_params=_cparams(2),
    )(x, g, w, *([] if after is None else [after]))


def _matmul_res(a, b, res, name):
    m, k = a.shape
    n = b.shape[1]
    tm, tn = _tile(m, 1024), _tile(n, 1024)

    def body(a_ref, b_ref, r_ref, o_ref):
        o_ref[...] = r_ref[...] + jnp.dot(a_ref[...], b_ref[...], preferred_element_type=F32)

    return pl.pallas_call(
        body, name=name, grid=(m // tm, n // tn),
        in_specs=[pl.BlockSpec((tm, k), lambda i, j: (i, 0)), pl.BlockSpec((k, tn), lambda i, j: (0, j)),
                  pl.BlockSpec((tm, tn), lambda i, j: (i, j))],
        out_specs=pl.BlockSpec((tm, tn), lambda i, j: (i, j)),
        out_shape=jax.ShapeDtypeStruct((m, n), F32),
        compiler_params=_cparams(2),
    )(a, b, res)


def _matmul_nt(a, b, name, after=None):
    m, n = a.shape
    k = b.shape[0]
    tm, tk = _tile(m, 1024), _tile(k, 512)

    def body(a_ref, b_ref, *rest):
        o_ref = rest[-1]
        o_ref[...] = lax.dot_general(a_ref[...].astype(BF16), b_ref[...], _NT, preferred_element_type=F32)

    in_specs = [pl.BlockSpec((tm, n), lambda i, j: (i, 0)), pl.BlockSpec((tk, n), lambda i, j: (j, 0))]
    args = [a, b]
    if after is not None:
        in_specs.append(ANY)
        args.append(after)
    return pl.pallas_call(
        body, name=name, grid=(m // tm, k // tk), in_specs=in_specs,
        out_specs=pl.BlockSpec((tm, tk), lambda i, j: (i, j)),
        out_shape=jax.ShapeDtypeStruct((m, k), F32),
        compiler_params=_cparams(2),
    )(*args)


def _matmul_tn(a, b, name, tk, tn, out_shape, out_block, out_index, into=None):
    m, k = a.shape
    n = b.shape[1]
    tm = _tile(m, 2048 if b.dtype == BF16 else 1024)

    def body(a_ref, b_ref, *rest):
        o_ref = rest[-1]
        part = lax.dot_general(a_ref[...].astype(BF16), b_ref[...].astype(BF16), _TN, preferred_element_type=F32)

        @pl.when(pl.program_id(2) == 0)
        def _():
            o_ref[...] = part

        @pl.when(pl.program_id(2) != 0)
        def _():
            o_ref[...] += part

    in_specs = [pl.BlockSpec((tm, tk), lambda i, j, s: (s, i)), pl.BlockSpec((tm, tn), lambda i, j, s: (s, j))]
    args = [a, b]
    if into is not None:
        in_specs.append(ANY)
        args.append(into)
    return pl.pallas_call(
        body, name=name, grid=(k // tk, n // tn, m // tm), in_specs=in_specs,
        out_specs=pl.BlockSpec(out_block, lambda i, j, s: out_index(i, j)),
        out_shape=jax.ShapeDtypeStruct(out_shape, F32),
        input_output_aliases={} if into is None else {2: 0},
        compiler_params=_cparams(3),
    )(*args)


def _dw_cols(a, b, layer, n_layers, into, name):
    k, n = a.shape[1], b.shape[1]
    pn = n // N_CHIPS
    tk = _tile(k, 512)
    return _matmul_tn(a, b, name, tk, pn, (N_CHIPS, n_layers, k, pn), (None, None, tk, pn),
                      lambda i, j: (j, layer, i, 0), into)


def _dw_rows(a, b, layer, n_layers, into, name):
    k, n = a.shape[1], b.shape[1]
    pk = k // N_CHIPS
    tn = _tile(n, 2048)
    return _matmul_tn(a, b, name, pk, tn, (N_CHIPS, n_layers, pk, n), (None, None, pk, tn),
                      lambda i, j: (i, layer, 0, j), into)


def _rms_bwd(du, h, g, dres, name):
    m, d = h.shape
    tm = _tile(m, 256)

    def body(du_ref, h_ref, g_ref, dres_ref, dx_ref, dg_ref):
        xf = h_ref[...]
        r = lax.rsqrt(jnp.mean(xf * xf, axis=-1, keepdims=True) + RMS_EPS)
        xhat = xf * r
        du_v = du_ref[...]
        dxn = du_v * g_ref[...]
        dx_ref[...] = dres_ref[...] + r * (dxn - xhat * jnp.mean(dxn * xhat, axis=-1, keepdims=True))
        part = jnp.sum(du_v * xhat, axis=0, keepdims=True)

        @pl.when(pl.program_id(0) == 0)
        def _():
            dg_ref[...] = part

        @pl.when(pl.program_id(0) != 0)
        def _():
            dg_ref[...] += part

    row = lambda i: (i, 0)
    return pl.pallas_call(
        body, name=name, grid=(m // tm,),
        in_specs=[pl.BlockSpec((tm, d), row), pl.BlockSpec((tm, d), row), pl.BlockSpec((1, d), lambda i: (0, 0)),
                  pl.BlockSpec((tm, d), row)],
        out_specs=[pl.BlockSpec((tm, d), row), pl.BlockSpec((1, d), lambda i: (0, 0))],
        out_shape=[jax.ShapeDtypeStruct((m, d), F32), jax.ShapeDtypeStruct((1, d), F32)],
        compiler_params=_cparams(1),
    )(du, h, g, dres)


def _final_loss_bwd(h, g, tgt, name):
    m, d = h.shape
    tm = _tile(m, 256)

    def body(h_ref, g_ref, t_ref, dx_ref, dg_ref, loss_ref):
        xf = h_ref[...]
        r = lax.rsqrt(jnp.mean(xf * xf, axis=-1, keepdims=True) + RMS_EPS)
        xhat = xf * r
        gv = g_ref[...]
        err = xhat * gv - t_ref[...]
        dy = err * (1.0 / d)
        dxn = dy * gv
        dx_ref[...] = r * (dxn - xhat * jnp.mean(dxn * xhat, axis=-1, keepdims=True))
        part = jnp.sum(dy * xhat, axis=0, keepdims=True)
        lpart = jnp.sum(jnp.sum(err * err, axis=0, keepdims=True), axis=1, keepdims=True) * (0.5 / d)

        @pl.when(pl.program_id(0) == 0)
        def _():
            dg_ref[...] = part
            loss_ref[...] = lpart

        @pl.when(pl.program_id(0) != 0)
        def _():
            dg_ref[...] += part
            loss_ref[...] += lpart

    row = lambda i: (i, 0)
    fixed = lambda i: (0, 0)
    return pl.pallas_call(
        body, name=name, grid=(m // tm,),
        in_specs=[pl.BlockSpec((tm, d), row), pl.BlockSpec((1, d), fixed), pl.BlockSpec((tm, d), row)],
        out_specs=[pl.BlockSpec((tm, d), row), pl.BlockSpec((1, d), fixed), pl.BlockSpec((1, 1), fixed)],
        out_shape=[jax.ShapeDtypeStruct((m, d), F32), jax.ShapeDtypeStruct((1, d), F32),
                   jax.ShapeDtypeStruct((1, 1), F32)],
        compiler_params=_cparams(1),
    )(h, g, tgt)


def _add_cast(a, b, name):
    m, n = a.shape
    tm, tn = _tile(m, 512), _tile(n, 2048)

    def body(a_ref, b_ref, o_ref):
        o_ref[...] = (a_ref[...] + b_ref[...]).astype(BF16)

    blk = lambda i, j: (i, j)
    return pl.pallas_call(
        body, name=name, grid=(m // tm, n // tn),
        in_specs=[pl.BlockSpec((tm, tn), blk), pl.BlockSpec((tm, tn), blk)],
        out_specs=pl.BlockSpec((tm, tn), blk),
        out_shape=jax.ShapeDtypeStruct((m, n), BF16),
        compiler_params=_cparams(2),
    )(a, b)


def _mem_attn_fwd(proj, memkv, layer, dr, dm, name):
    t = proj.shape[0]
    nm = memkv.shape[0]
    tm = _tile(t, 512)
    nh = dm // LANES
    scale = LANES ** -0.5
    qb = (2 * dr) // dm

    def body(q_ref, g_ref, k_ref, v_ref, y_ref):
        for hh in range(nh):
            sl = slice(hh * LANES, (hh + 1) * LANES)
            s = lax.dot_general(q_ref[:, sl].astype(BF16), k_ref[:, sl], _NT, preferred_element_type=F32) * scale
            p = jnp.exp(s - jnp.max(s, axis=-1, keepdims=True))
            p = p / jnp.sum(p, axis=-1, keepdims=True)
            o = jnp.dot(p.astype(BF16), v_ref[:, sl], preferred_element_type=F32)
            gv = g_ref[:, sl]
            y_ref[:, sl] = (o * (gv * _sigmoid(gv))).astype(BF16)

    return pl.pallas_call(
        body, name=name, grid=(t // tm,),
        in_specs=[pl.BlockSpec((tm, dm), lambda i: (i, qb)), pl.BlockSpec((tm, dm), lambda i: (i, qb + 1)),
                  pl.BlockSpec((nm, dm), lambda i: (0, 2 * layer)), pl.BlockSpec((nm, dm), lambda i: (0, 2 * layer + 1))],
        out_specs=pl.BlockSpec((tm, dm), lambda i: (i, 0)),
        out_shape=jax.ShapeDtypeStruct((t, dm), BF16),
        compiler_params=_cparams(1),
    )(proj, proj, memkv, memkv)


def _mem_attn_bwd(proj, memkv, dmix, layer, dr, dm, name):
    t = proj.shape[0]
    nm = memkv.shape[0]
    tm = _tile(t, 512)
    nh = dm // LANES
    scale = LANES ** -0.5
    qb = (2 * dr) // dm
    yb = dr // dm

    def body(q_ref, g_ref, k_ref, v_ref, dy_ref, dq_ref, dg_ref, dk_ref, dv_ref):
        @pl.when(pl.program_id(0) == 0)
        def _():
            dk_ref[...] = jnp.zeros_like(dk_ref)
            dv_ref[...] = jnp.zeros_like(dv_ref)

        for hh in range(nh):
            sl = slice(hh * LANES, (hh + 1) * LANES)
            q = q_ref[:, sl].astype(BF16)
            k = k_ref[:, sl]
            v = v_ref[:, sl]
            s = lax.dot_general(q, k, _NT, preferred_element_type=F32) * scale
            p = jnp.exp(s - jnp.max(s, axis=-1, keepdims=True))
            p = p / jnp.sum(p, axis=-1, keepdims=True)
            p_bf = p.astype(BF16)
            o = jnp.dot(p_bf, v, preferred_element_type=F32)
            gv = g_ref[:, sl]
            sg = _sigmoid(gv)
            dy = dy_ref[:, sl]
            do = dy * (gv * sg)
            dg_ref[:, sl] = (dy * o * (sg * (1.0 + gv * (1.0 - sg)))).astype(BF16)
            do_bf = do.astype(BF16)
            dv_ref[:, sl] += lax.dot_general(p_bf, do_bf, _TN, preferred_element_type=F32)
            dp = lax.dot_general(do_bf, v, _NT, preferred_element_type=F32)
            ds = (p * (dp - jnp.sum(dp * p, axis=-1, keepdims=True)) * scale).astype(BF16)
            dq_ref[:, sl] = jnp.dot(ds, k, preferred_element_type=F32).astype(BF16)
            dk_ref[:, sl] += lax.dot_general(ds, q, _TN, preferred_element_type=F32)

    fixed = lambda i: (0, 0)
    return pl.pallas_call(
        body, name=name, grid=(t // tm,),
        in_specs=[pl.BlockSpec((tm, dm), lambda i: (i, qb)), pl.BlockSpec((tm, dm), lambda i: (i, qb + 1)),
                  pl.BlockSpec((nm, dm), lambda i: (0, 2 * layer)), pl.BlockSpec((nm, dm), lambda i: (0, 2 * layer + 1)),
                  pl.BlockSpec((tm, dm), lambda i: (i, yb))],
        out_specs=[pl.BlockSpec((tm, dm), lambda i: (i, 0)), pl.BlockSpec((tm, dm), lambda i: (i, 0)),
                   pl.BlockSpec((nm, dm), fixed), pl.BlockSpec((nm, dm), fixed)],
        out_shape=[jax.ShapeDtypeStruct((t, dm), BF16), jax.ShapeDtypeStruct((t, dm), BF16),
                   jax.ShapeDtypeStruct((nm, dm), F32), jax.ShapeDtypeStruct((nm, dm), F32)],
        compiler_params=_cparams(1),
    )(proj, proj, memkv, memkv, dmix)


LRU_CHUNK = 256


def _lru_gates(xc, vec, wr_ref, wi_ref):
    r = _sigmoid(jnp.dot(xc.astype(BF16), wr_ref[...], preferred_element_type=F32) + vec[1:2])
    i = _sigmoid(jnp.dot(xc.astype(BF16), wi_ref[...], preferred_element_type=F32) + vec[2:3])
    lam = vec[3:4]
    cl = -LRU_C * (jnp.maximum(-lam, 0.0) + _log1p_pos(jnp.exp(-jnp.abs(lam))))
    la = cl * r
    a = jnp.exp(la)
    s2 = _neg_expm1(2.0 * la)
    return r, i, cl, a, s2


def _lru_fwd(proj, vec, wr, wi, name):
    t = proj.shape[0]
    nb = wr.shape[0]
    dr = nb * LANES
    c = _tile(t, LRU_CHUNK)

    def body(x_ref, g_ref, vec_ref, wr_ref, wi_ref, y_ref, h_ref, xc_ref, carry_ref, xprev_ref):
        @pl.when(pl.program_id(1) == 0)
        def _():
            carry_ref[...] = jnp.zeros_like(carry_ref)
            xprev_ref[...] = jnp.zeros_like(xprev_ref)

        x = x_ref[...]
        vec = vec_ref[...]
        rows = lax.broadcasted_iota(jnp.int32, (c, LANES), 0)
        xprev = xprev_ref[...]
        xc = vec[7:8] * x + vec[0:1]
        for k in range(1, 4):
            xs = jnp.where(rows < k, pltpu.roll(xprev, k, 0), pltpu.roll(x, k, 0))
            xc = xc + vec[7 - k:8 - k] * xs
        xprev_ref[...] = x
        xc_ref[...] = xc

        r, i, cl, a, s2 = _lru_gates(xc, vec, wr_ref, wi_ref)
        hh = jnp.sqrt(s2) * (i * xc)
        aa = a
        d = 1
        while d < c:
            keep = rows >= d
            hh = jnp.where(keep, aa * pltpu.roll(hh, d, 0) + hh, hh)
            aa = jnp.where(keep, aa * pltpu.roll(aa, d, 0), aa)
            d *= 2
        hfull = hh + aa * carry_ref[7:8, :]
        carry_ref[...] = hfull[c - 8:c, :]
        h_ref[...] = hfull
        gv = g_ref[...]
        y_ref[...] = (hfull * (gv * _sigmoid(gv))).astype(BF16)

    blk = lambda n, s: (s, n)
    return pl.pallas_call(
        body, name=name, grid=(nb, t // c),
        in_specs=[pl.BlockSpec((c, LANES), blk), pl.BlockSpec((c, LANES), lambda n, s: (s, nb + n)),
                  pl.BlockSpec((8, LANES), lambda n, s: (0, n)),
                  pl.BlockSpec((None, LANES, LANES), lambda n, s: (n, 0, 0)),
                  pl.BlockSpec((None, LANES, LANES), lambda n, s: (n, 0, 0))],
        out_specs=[pl.BlockSpec((c, LANES), blk)] * 3,
        out_shape=[jax.ShapeDtypeStruct((t, dr), BF16), jax.ShapeDtypeStruct((t, dr), F32),
                   jax.ShapeDtypeStruct((t, dr), F32)],
        scratch_shapes=[pltpu.VMEM((8, LANES), F32), pltpu.VMEM((c, LANES), F32)],
        compiler_params=_cparams(2),
    )(proj, proj, vec, wr, wi)


def _lru_bwd(proj, xc_all, h_all, dmix, vec, wr, wi, name):
    t = proj.shape[0]
    nb = wr.shape[0]
    dr = nb * LANES
    c = _tile(t, LRU_CHUNK)
    nc = t // c

    def body(x_ref, g_ref, xc_ref, h_ref, dy_ref, vec_ref, wr_ref, wi_ref,
             dx_ref, dg_ref, dwr_ref, dwi_ref, dvec_ref, qcarry_ref, dxc_next_ref):
        @pl.when(pl.program_id(1) == 0)
        def _():
            qcarry_ref[...] = jnp.zeros_like(qcarry_ref)
            dxc_next_ref[...] = jnp.zeros_like(dxc_next_ref)
            dwr_ref[...] = jnp.zeros_like(dwr_ref)
            dwi_ref[...] = jnp.zeros_like(dwi_ref)
            dvec_ref[...] = jnp.zeros_like(dvec_ref)

        x = x_ref[...]
        xc = xc_ref[...]
        h = h_ref[...]
        dy = dy_ref[...]
        gv = g_ref[...]
        vec = vec_ref[...]
        rows = lax.broadcasted_iota(jnp.int32, (c, LANES), 0)

        r, i, cl, a, s2 = _lru_gates(xc, vec, wr_ref, wi_ref)
        s = jnp.sqrt(s2)
        ixc = i * xc
        u = s * ixc
        sg = _sigmoid(gv)
        dh = dy * (gv * sg)
        dg_ref[...] = (dy * h * (sg * (1.0 + gv * (1.0 - sg)))).astype(BF16)

        aa = a
        qq = a * dh
        d = 1
        while d < c:
            keep = rows < c - d
            qq = jnp.where(keep, qq + aa * pltpu.roll(qq, c - d, 0), qq)
            aa = jnp.where(keep, aa * pltpu.roll(aa, c - d, 0), aa)
            d *= 2
        qin = qcarry_ref[0:1, :]
        qfull = qq + aa * qin
        gt = dh + jnp.where(rows == c - 1, qin, pltpu.roll(qfull, c - 1, 0))
        qcarry_ref[...] = qfull[0:8, :]

        dla = gt * (h - u) - gt * ixc * (a * a) / s
        dixc = gt * s
        di = dixc * xc
        dxc = dixc * i
        dzr = (dla * cl) * (r * (1.0 - r))
        dzi = di * (i * (1.0 - i))
        dzr_bf = dzr.astype(BF16)
        dzi_bf = dzi.astype(BF16)
        dxc = dxc + lax.dot_general(dzr_bf, wr_ref[...], _NT, preferred_element_type=F32)
        dxc = dxc + lax.dot_general(dzi_bf, wi_ref[...], _NT, preferred_element_type=F32)
        xc_bf = xc.astype(BF16)
        dwr_ref[...] += lax.dot_general(xc_bf, dzr_bf, _TN, preferred_element_type=F32)
        dwi_ref[...] += lax.dot_general(xc_bf, dzi_bf, _TN, preferred_element_type=F32)

        lam = vec[3:4]
        dlam = jnp.sum(dla * r, axis=0, keepdims=True) * (LRU_C * _sigmoid(-lam))
        colsum = lambda v: jnp.sum(v, axis=0, keepdims=True)
        dxn = dxc_next_ref[...]
        dx = vec[7:8] * dxc
        dtaps = [None] * 4
        dtaps[3] = colsum(x * dxc)
        for k in range(1, 4):
            sh = jnp.where(rows < c - k, pltpu.roll(dxc, c - k, 0), pltpu.roll(dxn, c - k, 0))
            dx = dx + vec[7 - k:8 - k] * sh
            dtaps[3 - k] = colsum(x * sh)
        dxc_next_ref[...] = dxc
        dx_ref[...] = dx.astype(BF16)
        dvec_ref[...] += jnp.concatenate([colsum(dxc), colsum(dzr), colsum(dzi), dlam] + dtaps, axis=0)

    rev = lambda n, s: (nc - 1 - s, n)
    sq = lambda n, s: (n, 0, 0)
    return pl.pallas_call(
        body, name=name, grid=(nb, nc),
        in_specs=[pl.BlockSpec((c, LANES), rev), pl.BlockSpec((c, LANES), lambda n, s: (nc - 1 - s, nb + n)),
                  pl.BlockSpec((c, LANES), rev), pl.BlockSpec((c, LANES), rev), pl.BlockSpec((c, LANES), rev),
                  pl.BlockSpec((8, LANES), lambda n, s: (0, n)),
                  pl.BlockSpec((None, LANES, LANES), sq), pl.BlockSpec((None, LANES, LANES), sq)],
        out_specs=[pl.BlockSpec((c, LANES), rev), pl.BlockSpec((c, LANES), rev),
                   pl.BlockSpec((None, LANES, LANES), sq), pl.BlockSpec((None, LANES, LANES), sq),
                   pl.BlockSpec((None, 8, LANES), sq)],
        out_shape=[jax.ShapeDtypeStruct((t, dr), BF16), jax.ShapeDtypeStruct((t, dr), BF16),
                   jax.ShapeDtypeStruct((nb, LANES, LANES), F32), jax.ShapeDtypeStruct((nb, LANES, LANES), F32),
                   jax.ShapeDtypeStruct((nb, 8, LANES), F32)],
        scratch_shapes=[pltpu.VMEM((8, LANES), F32), pltpu.VMEM((c, LANES), F32)],
        compiler_params=_cparams(2),
    )(proj, proj, xc_all, h_all, dmix, vec, wr, wi)


SB_TQ = 1024
SB_TK = 256


def _sb_softplus(z, diag):
    sp = jnp.maximum(z, 0.0) + jnp.log(1.0 + jnp.exp(-jnp.abs(z)))
    mask = None
    if diag:
        mask = lax.broadcasted_iota(jnp.int32, z.shape, 1) < lax.broadcasted_iota(jnp.int32, z.shape, 0)
        sp = jnp.where(mask, sp, 0.0)
    return sp, mask


def _split_dot(v, m):
    hi = v.astype(BF16)
    lo = (v - hi.astype(F32)).astype(BF16)
    return jnp.dot(hi, m, preferred_element_type=F32) + jnp.dot(lo, m, preferred_element_type=F32)


def _tri_ones(kind, tk):
    jj = lax.broadcasted_iota(jnp.int32, (tk, tk), 0)
    ss = lax.broadcasted_iota(jnp.int32, (tk, tk), 1)
    rel = {"ge": jj >= ss, "le": jj <= ss}[kind]
    return jnp.where(rel, 1.0, 0.0).astype(BF16)


def _sb_fwd(proj, kv, name):
    t = proj.shape[0]
    ds = kv.shape[1] // 2
    nh = ds // LANES
    tq = _tile(t, SB_TQ)
    tk = _tile(tq, SB_TK)
    nd = tq // tk
    assert nd % 2 == 0 or t == tq
    scale = LANES ** -0.5

    def body(q_ref, g_ref, k_ref, v_ref, y_ref, o_ref, tl_ref, qbf_ref, acc_ref, run_ref, z_ref, w_ref):
        qi = pl.program_id(1)
        qbf_ref[...] = q_ref[...].astype(BF16)
        tri = _tri_ones("ge", tk)
        acc_ref[...] = jnp.zeros_like(acc_ref)
        run_ref[...] = jnp.zeros_like(run_ref)
        all_rows = [slice(s0, s0 + tk) for s0 in range(0, tq, tk)]

        def weights(zs, groups):
            sps = [_sb_softplus(z, dg) for z, (_, dg) in zip(zs, groups)]
            cums = [_split_dot(sp, tri) for sp, _ in sps]
            ws = []
            for z, (rows, dg), (_, mask), cum in zip(zs, groups, sps, cums):
                run = run_ref[rows, :]
                w = jnp.exp(z - cum - run)
                if dg:
                    w = jnp.where(mask, w, 0.0)
                run_ref[rows, :] = run + cum[:, 0:1]
                ws.append(w.astype(BF16))
            return ws

        n = qi * nd

        def rows_from(u):
            return [slice(s0, s0 + tk) for s0 in range(u * tk, tq, tk)]

        def logits_into(slot, kb, rows_list):
            k = k_ref[pl.ds(pl.multiple_of(kb * tk, tk), tk), :]
            for rows in rows_list:
                z_ref[slot, rows, :] = lax.dot_general(qbf_ref[rows, :], k, _NT, preferred_element_type=F32) * scale

        def add_values(kb, rows_list):
            v = v_ref[pl.ds(pl.multiple_of(kb * tk, tk), tk), :]
            for rows in rows_list:
                acc_ref[rows, :] += jnp.dot(w_ref[rows, :], v, preferred_element_type=F32)

        def weigh(slot, groups):
            for (rows, _), w in zip(groups, weights([z_ref[slot, rows, :] for rows, _ in groups], groups)):
                w_ref[rows, :] = w

        logits_into(0, n + nd - 1, rows_from(nd - 1))
        for i, u in enumerate(reversed(range(nd))):
            slot = i % 2
            if i > 0:
                add_values(n + u + 1, rows_from(u + 1))
            if u > 0:
                logits_into(1 - slot, n + u - 1, rows_from(u - 1))
            else:
                logits_into(1 - slot, jnp.maximum(n - 1, 0), all_rows)
            weigh(slot, [(rows, j == 0) for j, rows in enumerate(rows_from(u))])

        def half_step(j, slot):
            kb = n - 1 - j
            add_values(kb + 1, all_rows)
            logits_into(1 - slot, jnp.maximum(kb - 1, 0), all_rows)
            weigh(slot, [(rows, False) for rows in all_rows])

        def step(i, carry):
            half_step(2 * i, nd % 2)
            half_step(2 * i + 1, 1 - nd % 2)
            return carry

        lax.fori_loop(0, n // 2, step, 0)
        add_values(0, all_rows)
        o = acc_ref[...]
        o_ref[...] = o
        tl_ref[...] = jnp.broadcast_to(run_ref[...], (tq, LANES))
        gv = g_ref[...]
        y_ref[...] = (o * (gv * _sigmoid(gv))).astype(BF16)

    blk = lambda h, i: (i, h)
    return pl.pallas_call(
        body, name=name, grid=(nh, t // tq),
        in_specs=[pl.BlockSpec((tq, LANES), blk), pl.BlockSpec((tq, LANES), lambda h, i: (i, nh + h)),
                  pl.BlockSpec((t, LANES), lambda h, i: (0, h)), pl.BlockSpec((t, LANES), lambda h, i: (0, nh + h))],
        out_specs=[pl.BlockSpec((tq, LANES), blk)] * 3,
        out_shape=[jax.ShapeDtypeStruct((t, ds), BF16), jax.ShapeDtypeStruct((t, ds), F32),
                   jax.ShapeDtypeStruct((t, ds), F32)],
        scratch_shapes=[pltpu.VMEM((tq, LANES), BF16), pltpu.VMEM((tq, LANES), F32), pltpu.VMEM((tq, 1), F32),
                        pltpu.VMEM((2, tq, tk), F32), pltpu.VMEM((tq, tk), BF16)],
        compiler_params=_cparams(2),
    )(proj, proj, kv, kv)


def _sb_bwd(proj, kv, o_all, tl_all, dmix, name):
    t = proj.shape[0]
    ds = kv.shape[1] // 2
    nh = ds // LANES
    tq = _tile(t, SB_TQ)
    tk = _tile(tq, SB_TK)
    nd = tq // tk
    scale = LANES ** -0.5

    def body(q_ref, g_ref, k_ref, v_ref, o_ref, tl_ref, dy_ref, dq_ref, dg_ref, dk_ref, dv_ref,
             qbf_ref, dobf_ref, qt_ref, dot_ref, acc_ref, left_ref, rune_ref, z_ref, dw_ref, wp_ref, dzp_ref):
        qi = pl.program_id(1)

        @pl.when(qi == 0)
        def _():
            dk_ref[...] = jnp.zeros_like(dk_ref)
            dv_ref[...] = jnp.zeros_like(dv_ref)

        qbf_ref[...] = q_ref[...].astype(BF16)
        qt_ref[...] = q_ref[...].T.astype(BF16)
        gv = g_ref[...]
        sg = _sigmoid(gv)
        dy = dy_ref[...]
        do = dy * (gv * sg)
        dobf_ref[...] = do.astype(BF16)
        dot_ref[...] = do.T.astype(BF16)
        dg_ref[...] = (dy * o_ref[...] * (sg * (1.0 + gv * (1.0 - sg)))).astype(BF16)
        tri = _tri_ones("le", tk)
        acc_ref[...] = jnp.zeros_like(acc_ref)
        left_ref[...] = tl_ref[:, 0:1]
        rune_ref[...] = jnp.zeros_like(rune_ref)
        wp_ref[...] = jnp.zeros_like(wp_ref)
        dzp_ref[...] = jnp.zeros_like(dzp_ref)
        all_rows = [slice(s0, s0 + tk) for s0 in range(0, tq, tk)]

        def grads(zs, dws, groups):
            sps = [_sb_softplus(z, dg) for z, (_, dg) in zip(zs, groups)]
            cums = [_split_dot(sp, tri) for sp, _ in sps]
            ws, es, lbs = [], [], []
            for z, (rows, dg), (sp, mask), cum, dw in zip(zs, groups, sps, cums, dws):
                left = left_ref[rows, :]
                lb = z - sp
                w = jnp.exp(lb - (left - cum))
                if dg:
                    w = jnp.where(mask, w, 0.0)
                left_ref[rows, :] = left - cum[:, tk - 1:tk]
                ws.append(w.astype(BF16))
                es.append(dw * w)
                lbs.append(lb)
            cumes = [_split_dot(e, tri) for e in es]
            dzs = []
            for (rows, dg), (_, mask), lb, e, cume in zip(groups, sps, lbs, es, cumes):
                rune = rune_ref[rows, :]
                dz = (e - jnp.exp(lb) * (rune + cume)) * scale
                if dg:
                    dz = jnp.where(mask, dz, 0.0)
                rune_ref[rows, :] = rune + cume[:, tk - 1:tk]
                dzs.append(dz.astype(BF16))
            return dzs, ws

        n = qi * nd

        def rows_from(u):
            return [slice(s0, s0 + tk) for s0 in range(u * tk, tq, tk)]

        def logits_into(slot, kb, rows_list):
            k0 = pl.multiple_of(kb * tk, tk)
            k = k_ref[pl.ds(k0, tk), :]
            v = v_ref[pl.ds(k0, tk), :]
            for rows in rows_list:
                z_ref[slot, rows, :] = lax.dot_general(qbf_ref[rows, :], k, _NT, preferred_element_type=F32) * scale
                dw_ref[slot, rows, :] = lax.dot_general(dobf_ref[rows, :], v, _NT, preferred_element_type=F32)

        def apply_stored(kb, u):
            k0 = pl.multiple_of(kb * tk, tk)
            k = k_ref[pl.ds(k0, tk), :]
            for rows in rows_from(u):
                acc_ref[rows, :] += jnp.dot(dzp_ref[rows, :], k, preferred_element_type=F32)
            seen = slice(u * tk, tq)
            dk_ref[:, pl.ds(k0, tk)] += jnp.dot(qt_ref[:, seen], dzp_ref[seen, :], preferred_element_type=F32)
            dv_ref[:, pl.ds(k0, tk)] += jnp.dot(dot_ref[:, seen], wp_ref[seen, :], preferred_element_type=F32)

        def differentiate(slot, groups):
            dzs, ws = grads([z_ref[slot, rows, :] for rows, _ in groups], [dw_ref[slot, rows, :] for rows, _ in groups],
                            groups)
            for (rows, _), dz, w in zip(groups, dzs, ws):
                dzp_ref[rows, :] = dz
                wp_ref[rows, :] = w

        logits_into(0, 0, all_rows)

        def half_step(j, slot):
            apply_stored(jnp.maximum(j - 1, 0), 0)
            logits_into(1 - slot, j + 1, all_rows)
            differentiate(slot, [(rows, False) for rows in all_rows])

        def step(i, carry):
            half_step(2 * i, 0)
            half_step(2 * i + 1, 1)
            return carry

        lax.fori_loop(0, n // 2, step, 0)
        for u in range(nd):
            slot = u % 2
            if u == 0:
                apply_stored(jnp.maximum(n - 1, 0), 0)
            else:
                apply_stored(n + u - 1, u - 1)
            if u + 1 < nd:
                logits_into(1 - slot, n + u + 1, rows_from(u + 1))
            differentiate(slot, [(rows, j == 0) for j, rows in enumerate(rows_from(u))])
        apply_stored(n + nd - 1, nd - 1)
        dq_ref[...] = acc_ref[...].astype(BF16)

    blk = lambda h, i: (i, h)
    whole = lambda h, i: (0, h)
    return pl.pallas_call(
        body, name=name, grid=(nh, t // tq),
        in_specs=[pl.BlockSpec((tq, LANES), blk), pl.BlockSpec((tq, LANES), lambda h, i: (i, nh + h)),
                  pl.BlockSpec((t, LANES), whole), pl.BlockSpec((t, LANES), lambda h, i: (0, nh + h)),
                  pl.BlockSpec((tq, LANES), blk), pl.BlockSpec((tq, LANES), blk), pl.BlockSpec((tq, LANES), blk)],
        out_specs=[pl.BlockSpec((tq, LANES), blk), pl.BlockSpec((tq, LANES), blk),
                   pl.BlockSpec((LANES, t), lambda h, i: (h, 0)), pl.BlockSpec((LANES, t), lambda h, i: (h, 0))],
        out_shape=[jax.ShapeDtypeStruct((t, ds), BF16), jax.ShapeDtypeStruct((t, ds), BF16),
                   jax.ShapeDtypeStruct((ds, t), F32), jax.ShapeDtypeStruct((ds, t), F32)],
        scratch_shapes=[pltpu.VMEM((tq, LANES), BF16), pltpu.VMEM((tq, LANES), BF16),
                        pltpu.VMEM((LANES, tq), BF16), pltpu.VMEM((LANES, tq), BF16), pltpu.VMEM((tq, LANES), F32),
                        pltpu.VMEM((tq, 1), F32), pltpu.VMEM((tq, 1), F32),
                        pltpu.VMEM((2, tq, tk), F32), pltpu.VMEM((2, tq, tk), F32),
                        pltpu.VMEM((tq, tk), BF16), pltpu.VMEM((tq, tk), BF16)],
        compiler_params=_cparams(2),
    )(proj, proj, kv, kv, o_all, tl_all, dmix)


def _place():
    x, y, c = lax.axis_index("x"), lax.axis_index("y"), lax.axis_index("c")
    chips = [(1 - x, y), (x, 1 - y), (1 - x, 1 - y)]
    return x, y, c, chips


def _remote(src, dst, send_sems, recv_sems, k, to):
    return pltpu.make_async_remote_copy(src_ref=src, dst_ref=dst, send_sem=send_sems.at[k], recv_sem=recv_sems.at[k],
                                        device_id=to, device_id_type=MESH)


def _my_chip():
    return 2 * lax.axis_index("x") + lax.axis_index("y")


def _place_own(shard, name):
    r, w = shard.shape
    tr = _tile(r, FLAT_TR)

    def body(x_ref, o_ref):
        o_ref[...] = x_ref[...]

    return pl.pallas_call(
        body, name=name, out_shape=jax.ShapeDtypeStruct((N_CHIPS, r, w), shard.dtype), grid=(r // tr,),
        in_specs=[pl.BlockSpec((tr, w), lambda i: (i, 0))],
        out_specs=pl.BlockSpec((None, tr, w), lambda i: (_my_chip(), i, 0)),
        compiler_params=_cparams(1),
    )(shard)


def _chip_all_gather(shards, name):
    n = len(shards)

    def body(*refs):
        x_refs, out_refs, send_sems, recv_sems = refs[:n], refs[2 * n:3 * n], refs[3 * n], refs[3 * n + 1]
        x, y, c, chips = _place()
        me = 2 * x + y
        sibling = (x, y, 1 - c)

        def rows(t, core):
            rh = x_refs[t].shape[0] // 2
            return pl.ds(core * rh, rh)

        first = [_remote(x_refs[t].at[rows(t, c)], out_refs[t].at[me, rows(t, c)], send_sems, recv_sems, 6 * t + k,
                         (cx, cy, c)) for t in range(n) for k, (cx, cy) in enumerate(chips)]
        for cp in first:
            cp.start()
        passed = []
        for k, (cx, cy) in enumerate(chips):
            for t in range(n):
                got = out_refs[t].at[2 * cx + cy, rows(t, c)]
                _remote(got, got, send_sems, recv_sems, 6 * t + k, (cx, cy, c)).wait_recv()
                fwd = _remote(got, got, send_sems, recv_sems, 6 * t + 3 + k, sibling)
                fwd.start()
                passed.append(fwd)
        for k, (cx, cy) in enumerate(chips):
            for t in range(n):
                got = out_refs[t].at[2 * cx + cy, rows(t, 1 - c)]
                _remote(got, got, send_sems, recv_sems, 6 * t + 3 + k, sibling).wait_recv()
        for cp in first + passed:
            cp.wait_send()

    bufs = [_place_own(s, f"{name}_own{t}") for t, s in enumerate(shards)]
    return pl.pallas_call(
        body, name=name, in_specs=[ANY] * (2 * n), out_specs=[ANY] * n,
        out_shape=[jax.ShapeDtypeStruct((N_CHIPS,) + s.shape, s.dtype) for s in shards],
        input_output_aliases={n + t: t for t in range(n)},
        scratch_shapes=[pltpu.SemaphoreType.DMA((6 * n,)), pltpu.SemaphoreType.DMA((6 * n,))],
    )(*shards, *bufs)


def _chip_gather_start(shards, name):
    n = len(shards)

    def body(*refs):
        x_refs, buf_refs, send_sems, recv_sems, token = refs[:n], refs[n:2 * n], refs[2 * n], refs[2 * n + 1], refs[-1]
        x, y, c, chips = _place()
        me = 2 * x + y
        for t in range(n):
            for k, (cx, cy) in enumerate(chips):
                _remote(x_refs[t], buf_refs[t].at[me], send_sems, recv_sems, 3 * t + k, (cx, cy, c)).start()
        token[...] = jnp.zeros_like(token)

    bufs = [_place_own(s, f"{name}_own{t}") for t, s in enumerate(shards)]
    hbm = [pltpu.HBM(a.shape, a.dtype) for a in list(shards) + bufs]
    outs = pl.pallas_call(
        body, name=name, in_specs=[HBM] * (2 * n),
        out_shape=(pltpu.SemaphoreType.DMA((3 * n,)), pltpu.SemaphoreType.DMA((3 * n,)), *hbm,
                   jax.ShapeDtypeStruct((8, LANES), F32)),
        out_specs=(SEM, SEM, *[HBM] * (2 * n), pl.BlockSpec(memory_space=pltpu.VMEM)),
        input_output_aliases={t: 2 + t for t in range(2 * n)},
        compiler_params=pltpu.CompilerParams(has_side_effects=EFFECT),
    )(*[pltpu.with_memory_space_constraint(a, pltpu.HBM) for a in list(shards) + bufs])
    return outs[0], outs[1], outs[2:2 + n], outs[2 + n:2 + 2 * n], outs[-1]


def _chip_gather_wait(send_sems, recv_sems, shards, bufs, after, name):
    n = len(shards)

    def body(*refs):
        x_refs, buf_refs, send, recv = refs[:n], refs[n:2 * n], refs[2 * n], refs[2 * n + 1]
        x, y, c, chips = _place()
        for t in range(n):
            for k, (cx, cy) in enumerate(chips):
                cp = _remote(x_refs[t], buf_refs[t].at[2 * cx + cy], send, recv, 3 * t + k, (cx, cy, c))
                cp.wait_send()
                cp.wait_recv()

    hbm = [pltpu.HBM(a.shape, a.dtype) for a in list(shards) + list(bufs)]
    outs = pl.pallas_call(
        body, name=name, in_specs=[HBM] * (2 * n) + [SEM, SEM, ANY],
        out_shape=tuple(hbm), out_specs=tuple([HBM] * (2 * n)),
        input_output_aliases={t: t for t in range(2 * n)},
        compiler_params=pltpu.CompilerParams(has_side_effects=EFFECT),
    )(*shards, *bufs, send_sems, recv_sems, after)
    return outs[n:]


def _sibling_take_half(ss, name):
    n = len(ss)

    def body(*refs):
        s_refs, a_refs, send_sems, recv_sems = refs[:n], refs[n:2 * n], refs[2 * n], refs[2 * n + 1]
        x, y, c, _ = _place()
        cps = []
        for t in range(n):
            rh = s_refs[t].shape[1] // 2
            cps.append(_remote(s_refs[t].at[:, pl.ds((1 - c) * rh, rh), :], a_refs[t], send_sems, recv_sems, t,
                               (x, y, 1 - c)))
        for cp in cps:
            cp.start()
        for cp in cps:
            cp.wait()

    return pl.pallas_call(
        body, name=name, in_specs=[ANY] * n, out_specs=[ANY] * n,
        out_shape=[jax.ShapeDtypeStruct((s.shape[0], s.shape[1] // 2, s.shape[2]), s.dtype) for s in ss],
        scratch_shapes=[pltpu.SemaphoreType.DMA((n,)), pltpu.SemaphoreType.DMA((n,))],
    )(*ss)


def _pair_sum(s, a, dtype, name):
    n, r, w = s.shape
    rh = r // 2
    tr = _tile(rh, FLAT_TR)
    nblk = rh // tr

    def body(s_ref, a_ref, o_ref):
        o_ref[...] = (s_ref[...] + a_ref[...]).astype(dtype)

    return pl.pallas_call(
        body, name=name, out_shape=jax.ShapeDtypeStruct((n, rh, w), dtype), grid=(n, nblk),
        in_specs=[pl.BlockSpec((None, tr, w), lambda k, i: (k, lax.axis_index("c") * nblk + i, 0)),
                  pl.BlockSpec((None, tr, w), lambda k, i: (k, i, 0))],
        out_specs=pl.BlockSpec((None, tr, w), lambda k, i: (k, i, 0)),
        compiler_params=_cparams(2),
    )(s, a)


def _chip_scatter_start(ps, name):
    n = len(ps)

    def body(*refs):
        p_refs, b_refs, send_sems, recv_sems, token = refs[:n], refs[n:2 * n], refs[2 * n], refs[2 * n + 1], refs[-1]
        x, y, c, chips = _place()
        me = 2 * x + y
        for t in range(n):
            for k, (cx, cy) in enumerate(chips):
                _remote(p_refs[t].at[2 * cx + cy], b_refs[t].at[me], send_sems, recv_sems, 3 * t + k, (cx, cy, c)).start()
        token[...] = jnp.zeros_like(token)

    hbm = [pltpu.HBM(p.shape, p.dtype) for p in ps]
    outs = pl.pallas_call(
        body, name=name, in_specs=[HBM] * (2 * n),
        out_shape=(pltpu.SemaphoreType.DMA((3 * n,)), pltpu.SemaphoreType.DMA((3 * n,)), *hbm, *hbm,
                   jax.ShapeDtypeStruct((8, LANES), F32)),
        out_specs=(SEM, SEM, *[HBM] * (2 * n), pl.BlockSpec(memory_space=pltpu.VMEM)),
        input_output_aliases={t: 2 + t for t in range(2 * n)},
        compiler_params=pltpu.CompilerParams(has_side_effects=EFFECT),
    )(*[pltpu.with_memory_space_constraint(p, pltpu.HBM) for p in ps],
      *[pltpu.with_memory_space_constraint(lax.empty(p.shape, p.dtype), pltpu.HBM) for p in ps])
    return outs[0], outs[1], outs[2:2 + n], outs[2 + n:2 + 2 * n], outs[-1]


def _chip_scatter_wait(send_sems, recv_sems, ps, bs, after, name):
    n = len(ps)

    def body(*refs):
        p_refs, b_refs, send, recv = refs[:n], refs[n:2 * n], refs[2 * n], refs[2 * n + 1]
        x, y, c, chips = _place()
        for t in range(n):
            for k, (cx, cy) in enumerate(chips):
                cp = _remote(p_refs[t].at[2 * cx + cy], b_refs[t].at[2 * cx + cy], send, recv, 3 * t + k, (cx, cy, c))
                cp.wait_send()
                cp.wait_recv()

    hbm = [pltpu.HBM(p.shape, p.dtype) for p in ps]
    outs = pl.pallas_call(
        body, name=name, in_specs=[HBM] * (2 * n) + [SEM, SEM, ANY],
        out_shape=(*hbm, *hbm), out_specs=tuple([HBM] * (2 * n)),
        input_output_aliases={t: t for t in range(2 * n)},
        compiler_params=pltpu.CompilerParams(has_side_effects=EFFECT),
    )(*ps, *bs, send_sems, recv_sems, after)
    return outs[n:]


def _chip_sum(p, b, name, after=None):
    n, rh, w = p.shape
    tr = _tile(rh, FLAT_TR)
    nblk = rh // tr

    def body(p_ref, b0_ref, b1_ref, b2_ref, b3_ref, *rest):
        o_ref = rest[-1]
        me = _my_chip()
        own = p_ref[...]
        t = [jnp.where(me == k, own, b_ref[...]).astype(F32) for k, b_ref in enumerate((b0_ref, b1_ref, b2_ref, b3_ref))]
        o_ref[...] = ((t[0] + t[1]) + t[2]) + t[3]

    def other(k):
        return lambda i: (jnp.where(_my_chip() == k, (k + 1) % N_CHIPS, k), i, 0)

    return pl.pallas_call(
        body, name=name, out_shape=jax.ShapeDtypeStruct((2 * rh, w), F32), grid=(nblk,),
        in_specs=[pl.BlockSpec((None, tr, w), lambda i: (_my_chip(), i, 0))]
        + [pl.BlockSpec((None, tr, w), other(k)) for k in range(N_CHIPS)] + ([] if after is None else [ANY]),
        out_specs=pl.BlockSpec((tr, w), lambda i: (lax.axis_index("c") * nblk + i, 0)),
        compiler_params=_cparams(1),
    )(p, b, b, b, b, *([] if after is None else [after]))


def _sibling_join(gs, name):
    n = len(gs)

    def body(*refs):
        g_refs, send_sems, recv_sems = refs[n:2 * n], refs[2 * n], refs[2 * n + 1]
        x, y, c, _ = _place()
        cps = []
        for t in range(n):
            rh = g_refs[t].shape[0] // 2
            mine = g_refs[t].at[pl.ds(c * rh, rh)]
            cps.append(_remote(mine, mine, send_sems, recv_sems, t, (x, y, 1 - c)))
        for cp in cps:
            cp.start()
        for t in range(n):
            rh = g_refs[t].shape[0] // 2
            theirs = g_refs[t].at[pl.ds((1 - c) * rh, rh)]
            _remote(theirs, theirs, send_sems, recv_sems, t, (x, y, 1 - c)).wait_recv()
        for cp in cps:
            cp.wait_send()

    return pl.pallas_call(
        body, name=name, in_specs=[ANY] * n, out_specs=[ANY] * n,
        out_shape=[jax.ShapeDtypeStruct(g.shape, g.dtype) for g in gs],
        input_output_aliases={t: t for t in range(n)},
        scratch_shapes=[pltpu.SemaphoreType.DMA((n,)), pltpu.SemaphoreType.DMA((n,))],
    )(*gs)


def _adamw(g, w, m, v, name):
    r, wd = g.shape

    def body(g_ref, w_ref, m_ref, v_ref, d_ref, mo_ref, vo_ref):
        gv = g_ref[...]
        mn = ADAM_B1 * m_ref[...] + (1.0 - ADAM_B1) * gv
        vn = ADAM_B2 * v_ref[...] + (1.0 - ADAM_B2) * (gv * gv)
        m_hat = mn / (1.0 - ADAM_B1 ** ADAM_STEP)
        v_hat = vn / (1.0 - ADAM_B2 ** ADAM_STEP)
        d_ref[...] = -ADAM_LR * (m_hat / (jnp.sqrt(v_hat) + ADAM_EPS) + ADAM_WD * w_ref[...])
        mo_ref[...] = mn
        vo_ref[...] = vn

    tr = _tile(r, FLAT_TR)
    row = lambda i: (i, 0)
    spec = pl.BlockSpec((tr, wd), row)
    return pl.pallas_call(
        body, name=name, grid=(r // tr,), in_specs=[spec] * 4, out_specs=[spec] * 3,
        out_shape=[jax.ShapeDtypeStruct((r, wd), F32)] * 3,
        compiler_params=_cparams(1),
    )(g, w, m, v)


def _pair_sums(ss, names):
    a = _sibling_take_half(ss, "grad_sibling_half_" + names[0])
    return [_pair_sum(s, a_t, BF16 if n in BIG else F32, f"grad_pair_sum_{n}") for s, a_t, n in zip(ss, a, names)]


def _finish_reduce(ps, bs, names, after=None):
    g = [_chip_sum(p_t, b_t, f"grad_chip_sum_{n}", after if i == 0 else None)
         for i, (p_t, b_t, n) in enumerate(zip(ps, bs, names))]
    return _sibling_join(g, "grad_sibling_join_" + names[0])


WEIGHTS = ("mem_norm", "w_mem_kv", "norm_a", "w_in_a", "conv_w", "conv_b", "w_rec_gate", "b_rec_gate", "w_in_gate",
           "b_in_gate", "lru_lambda", "w_out_a", "kv_norm", "w_kv", "norm_b", "w_in_b", "w_out_b", "final_norm")
SHARD_DIM = {"mem_norm": None, "w_mem_kv": 1, "norm_a": 1, "w_in_a": 2, "conv_w": 2, "conv_b": 1, "w_rec_gate": None,
             "b_rec_gate": 1, "w_in_gate": None, "b_in_gate": 1, "lru_lambda": 1, "w_out_a": 1, "kv_norm": None,
             "w_kv": 1, "norm_b": None, "w_in_b": 2, "w_out_b": 1, "final_norm": None}
BIG = ("w_mem_kv", "w_in_a", "w_out_a", "w_kv", "w_in_b", "w_out_b")
SMALL = ("norm_a", "conv_w", "conv_b", "b_rec_gate", "b_in_gate", "lru_lambda")


def _pad_rows(flat, row_multiple):
    per = FLAT_W * row_multiple
    n = flat.shape[0]
    total = -(-n // per) * per
    return jnp.pad(flat, (0, total - n)).reshape(total // FLAT_W, FLAT_W)


def _flatten(parts, row_multiple):
    return _pad_rows(jnp.concatenate([p.reshape(-1) for p in parts]), row_multiple)


def _unflatten(flat2d, shapes):
    flat = flat2d.reshape(-1)
    out, off = [], 0
    for shp in shapes:
        n = 1
        for s in shp:
            n *= s
        out.append(flat[off:off + n].reshape(shp))
        off += n
    return out


LATE = ("w_kv", "w_in_b", "w_out_b")


def _bf16_rows(w):
    return w.astype(BF16).reshape(-1, w.shape[-1])


def _whole(gathered, shape, dim):
    return jnp.concatenate([gathered[k].reshape(shape) for k in range(N_CHIPS)], axis=dim)


def _gather_weights(local):
    early = [n for n in BIG if n not in LATE]
    small_shapes = [local[n].shape for n in SMALL]
    send_sems, recv_sems, shards, bufs, token = _chip_gather_start([_bf16_rows(local[n]) for n in LATE],
                                                                   "late_weights_gather_start")
    gathered = _chip_all_gather([_bf16_rows(local[n]) for n in early] + [_flatten([local[n] for n in SMALL], 16)],
                                "weights_all_gather")
    full = {n: _whole(g, local[n].shape, SHARD_DIM[n]) for n, g in zip(early, gathered)}
    per_chip = [_unflatten(gathered[-1][k], small_shapes) for k in range(N_CHIPS)]
    for i, n in enumerate(SMALL):
        full[n] = jnp.concatenate([per_chip[k][i] for k in range(N_CHIPS)], axis=SHARD_DIM[n])

    def late(after):
        got = _chip_gather_wait(send_sems, recv_sems, shards, bufs, after, "late_weights_gather_wait")
        return {n: _whole(g, local[n].shape, SHARD_DIM[n]) for n, g in zip(LATE, got)}

    return full, token, late


def _piece(g, name, k):
    dim = SHARD_DIM[name]
    if dim is None:
        return g
    n = g.shape[dim] // N_CHIPS
    return lax.slice_in_dim(g, k * n, (k + 1) * n, axis=dim)


def _local_grads(x, mem, tgt, wts, token, late):
    t, d = x.shape
    depth = wts["w_mem_kv"].shape[0]
    n_a = wts["w_in_a"].shape[0]
    n_b = wts["norm_b"].shape[0]
    nb = wts["w_rec_gate"].shape[1]
    dr = nb * LANES
    dm = wts["w_mem_kv"].shape[2] // 2
    row = lambda v: v.reshape(1, -1)

    wm_all = jnp.concatenate([wts["w_mem_kv"][l] for l in range(depth)], axis=1)
    memkv, memn_bf = _norm_matmul(mem, row(wts["mem_norm"]), wm_all, BF16, "mem_kv_proj", after=token)

    h = x
    saved = []
    vecs = []
    for l in range(n_a):
        proj, u_bf = _norm_matmul(h, row(wts["norm_a"][l]), wts["w_in_a"][l], F32, f"a{l}_in_proj")
        vec = jnp.concatenate([row(wts["conv_b"][l]), row(wts["b_rec_gate"][l]), row(wts["b_in_gate"][l]),
                               row(wts["lru_lambda"][l]), wts["conv_w"][l]], axis=0)
        vecs.append(vec)
        y_rnn, h_rnn, xc = _lru_fwd(proj, vec, wts["w_rec_gate"][l], wts["w_in_gate"][l], f"a{l}_lru_fwd")
        y_mem = _mem_attn_fwd(proj, memkv, l, dr, dm, f"a{l}_mem_fwd")
        mix = jnp.concatenate([y_rnn, y_mem], axis=1)
        h_next = _matmul_res(mix, wts["w_out_a"][l], h, f"a{l}_out_proj")
        saved.append((h, proj, u_bf, mix, h_rnn, xc))
        h = h_next

    h_kv = h
    wts = {**wts, **late(h_kv)}
    kv, ukv_bf = _norm_matmul(h_kv, row(wts["kv_norm"]), wts["w_kv"], BF16, "kv_proj")

    for j in range(n_b):
        l = n_a + j
        proj, u_bf = _norm_matmul(h, row(wts["norm_b"][j]), wts["w_in_b"][j], F32, f"b{j}_in_proj")
        y_sb, o_sb, tl_sb = _sb_fwd(proj, kv, f"b{j}_sb_fwd")
        y_mem = _mem_attn_fwd(proj, memkv, l, dr, dm, f"b{j}_mem_fwd")
        mix = jnp.concatenate([y_sb, y_mem], axis=1)
        h_next = _matmul_res(mix, wts["w_out_b"][j], h, f"b{j}_out_proj")
        saved.append((h, proj, u_bf, mix, o_sb, tl_sb))
        h = h_next

    dh, d_final, loss = _final_loss_bwd(h, row(wts["final_norm"]), tgt, "final_loss_bwd")

    grads = {"final_norm": d_final.reshape(-1)}
    big = {}
    dmemkv = [None] * depth
    g_norm_b = [None] * n_b
    dks, dvs = [], []
    for j in reversed(range(n_b)):
        l = n_a + j
        h_in, proj, u_bf, mix, o_sb, tl_sb = saved[l]
        dmix = _matmul_nt(dh, wts["w_out_b"][j], f"b{j}_dmix")
        big["w_out_b"] = _dw_rows(mix, dh, j, n_b, big.get("w_out_b"), f"b{j}_dw_out")
        dq, dg, dk, dv = _sb_bwd(proj, kv, o_sb, tl_sb, dmix, f"b{j}_sb_bwd")
        dqm, dgm, dkm, dvm = _mem_attn_bwd(proj, memkv, dmix, l, dr, dm, f"b{j}_mem_bwd")
        dmemkv[l] = (dkm, dvm)
        dproj = jnp.concatenate([dq, dg, dqm, dgm], axis=1)
        du = _matmul_nt(dproj, wts["w_in_b"][j], f"b{j}_du")
        big["w_in_b"] = _dw_cols(u_bf, dproj, j, n_b, big.get("w_in_b"), f"b{j}_dw_in")
        dh, dgn = _rms_bwd(du, h_in, row(wts["norm_b"][j]), dh, f"b{j}_rms_bwd")
        g_norm_b[j] = dgn.reshape(-1)
        dks.append(dk)
        dvs.append(dv)
    assert n_b == 2
    dkv = jnp.concatenate([_add_cast(dks[0], dks[1], "dk_sum").T, _add_cast(dvs[0], dvs[1], "dv_sum").T], axis=1)
    du = _matmul_nt(dkv, wts["w_kv"], "kv_du")
    big["w_kv"] = _dw_cols(ukv_bf, dkv, 0, 1, None, "kv_dw")
    dh, dgn = _rms_bwd(du, h_kv, row(wts["kv_norm"]), dh, "kv_rms_bwd")
    grads["kv_norm"] = dgn.reshape(-1)

    early_names = [n for n in BIG if n in big]
    early_p = _pair_sums([big[n].reshape(N_CHIPS, -1, big[n].shape[-1]) for n in early_names], early_names)
    send_sems, recv_sems, early_p, early_b, token = _chip_scatter_start(early_p, "grad_chip_scatter_start")

    g_norm_a = [None] * n_a
    g_wr, g_wi, g_vec = [None] * n_a, [None] * n_a, [None] * n_a
    for l in reversed(range(n_a)):
        h_in, proj, u_bf, mix, h_rnn, xc = saved[l]
        dmix = _matmul_nt(dh, wts["w_out_a"][l], f"a{l}_dmix", after=token if l == n_a - 1 else None)
        big["w_out_a"] = _dw_rows(mix, dh, l, n_a, big.get("w_out_a"), f"a{l}_dw_out")
        dx, dg, g_wr[l], g_wi[l], dvec = _lru_bwd(proj, xc, h_rnn, dmix, vecs[l], wts["w_rec_gate"][l],
                                                  wts["w_in_gate"][l], f"a{l}_lru_bwd")
        g_vec[l] = dvec.transpose(1, 0, 2).reshape(8, dr)
        dqm, dgm, dkm, dvm = _mem_attn_bwd(proj, memkv, dmix, l, dr, dm, f"a{l}_mem_bwd")
        dmemkv[l] = (dkm, dvm)
        dproj = jnp.concatenate([dx, dg, dqm, dgm], axis=1)
        du = _matmul_nt(dproj, wts["w_in_a"][l], f"a{l}_du")
        big["w_in_a"] = _dw_cols(u_bf, dproj, l, n_a, big.get("w_in_a"), f"a{l}_dw_in")
        dh, dgn = _rms_bwd(du, h_in, row(wts["norm_a"][l]), dh, f"a{l}_rms_bwd")
        g_norm_a[l] = dgn.reshape(-1)

    dmemkv_all = jnp.concatenate([jnp.concatenate(p, axis=1) for p in dmemkv], axis=1).astype(BF16)
    pk = d // N_CHIPS
    big["w_mem_kv"] = _matmul_tn(memn_bf, dmemkv_all, "mem_dw", pk, 2 * dm, (N_CHIPS, depth, pk, 2 * dm),
                                 (None, None, pk, 2 * dm), lambda i, j: (i, j, 0, 0))
    dmemn = _matmul_nt(dmemkv_all, wm_all, "mem_du")
    _, dgn = _rms_bwd(dmemn, mem, row(wts["mem_norm"]), jnp.zeros_like(mem), "mem_rms_bwd")
    grads["mem_norm"] = dgn.reshape(-1)
    grads["norm_a"] = jnp.stack(g_norm_a)
    grads["w_rec_gate"] = jnp.stack(g_wr)
    grads["w_in_gate"] = jnp.stack(g_wi)
    gv = jnp.stack(g_vec)
    grads["conv_b"], grads["b_rec_gate"], grads["b_in_gate"], grads["lru_lambda"] = gv[:, 0], gv[:, 1], gv[:, 2], gv[:, 3]
    grads["conv_w"] = gv[:, 4:8]
    grads["norm_b"] = jnp.stack(g_norm_b)
    early_b = _chip_scatter_wait(send_sems, recv_sems, early_p, early_b, dh, "grad_chip_scatter_wait")
    early = (early_names, early_p, early_b)
    big = {n: g.reshape(N_CHIPS, -1, g.shape[-1]) for n, g in big.items() if n not in early_names}
    return loss, dh, grads, big, early


def kernel(x, mem, mem_norm, w_mem_kv, norm_a, w_in_a, conv_w, conv_b, w_rec_gate, b_rec_gate, w_in_gate, b_in_gate, lru_lambda, w_out_a, kv_norm, w_kv, norm_b, w_in_b, w_out_b, final_norm, loss_target, m_mem_norm, m_w_mem_kv, m_norm_a, m_w_in_a, m_conv_w, m_conv_b, m_w_rec_gate, m_b_rec_gate, m_w_in_gate, m_b_in_gate, m_lru_lambda, m_w_out_a, m_kv_norm, m_w_kv, m_norm_b, m_w_in_b, m_w_out_b, m_final_norm, v_mem_norm, v_w_mem_kv, v_norm_a, v_w_in_a, v_conv_w, v_conv_b, v_w_rec_gate, v_b_rec_gate, v_w_in_gate, v_b_in_gate, v_lru_lambda, v_w_out_a, v_kv_norm, v_w_kv, v_norm_b, v_w_in_b, v_w_out_b, v_final_norm):
    local = dict(mem_norm=mem_norm, w_mem_kv=w_mem_kv, norm_a=norm_a, w_in_a=w_in_a, conv_w=conv_w, conv_b=conv_b,
                 w_rec_gate=w_rec_gate, b_rec_gate=b_rec_gate, w_in_gate=w_in_gate, b_in_gate=b_in_gate,
                 lru_lambda=lru_lambda, w_out_a=w_out_a, kv_norm=kv_norm, w_kv=w_kv, norm_b=norm_b, w_in_b=w_in_b,
                 w_out_b=w_out_b, final_norm=final_norm)
    mom = dict(mem_norm=m_mem_norm, w_mem_kv=m_w_mem_kv, norm_a=m_norm_a, w_in_a=m_w_in_a, conv_w=m_conv_w,
               conv_b=m_conv_b, w_rec_gate=m_w_rec_gate, b_rec_gate=m_b_rec_gate, w_in_gate=m_w_in_gate,
               b_in_gate=m_b_in_gate, lru_lambda=m_lru_lambda, w_out_a=m_w_out_a, kv_norm=m_kv_norm, w_kv=m_w_kv,
               norm_b=m_norm_b, w_in_b=m_w_in_b, w_out_b=m_w_out_b, final_norm=m_final_norm)
    var = dict(mem_norm=v_mem_norm, w_mem_kv=v_w_mem_kv, norm_a=v_norm_a, w_in_a=v_w_in_a, conv_w=v_conv_w,
               conv_b=v_conv_b, w_rec_gate=v_w_rec_gate, b_rec_gate=v_b_rec_gate, w_in_gate=v_w_in_gate,
               b_in_gate=v_b_in_gate, lru_lambda=v_lru_lambda, w_out_a=v_w_out_a, kv_norm=v_kv_norm, w_kv=v_w_kv,
               norm_b=v_norm_b, w_in_b=v_w_in_b, w_out_b=v_w_out_b, final_norm=v_final_norm)

    wts, token, late = _gather_weights(local)
    for n in WEIGHTS:
        if SHARD_DIM[n] is None:
            wts[n] = local[n]
    wts["w_rec_gate"] = wts["w_rec_gate"].astype(BF16)
    wts["w_in_gate"] = wts["w_in_gate"].astype(BF16)

    loss, grad_x, grads, big, (early_names, early_p, early_b) = _local_grads(x[0], mem[0], loss_target[0], wts, token, late)

    rest = [n for n in WEIGHTS if n not in BIG]
    row_multiple = 2 * FLAT_TR
    s_rest = jnp.stack([_flatten([_piece(grads[n], n, k) for n in rest], row_multiple) for k in range(N_CHIPS)])
    late_names = [n for n in BIG if n in big] + ["rest"]
    late_p = _pair_sums([big[n] for n in late_names[:-1]] + [s_rest], late_names)
    send_sems, recv_sems, late_p, late_b, token = _chip_scatter_start(late_p, "grad_late_scatter_start")

    g_out, d_out, m_out, v_out = {}, {}, {}, {}

    def update(n, g):
        shape = local[n].shape
        flat = lambda a: a.reshape(-1, shape[-1])
        d, mo, vo = _adamw(g, flat(local[n]), flat(mom[n]), flat(var[n]), f"adamw_{n}")
        g_out[n], d_out[n], m_out[n], v_out[n] = (a.reshape(shape) for a in (g, d, mo, vo))
        return d

    for n, g in zip(early_names, _finish_reduce(early_p, early_b, early_names, after=token)):
        done = update(n, g)
    late_b = _chip_scatter_wait(send_sems, recv_sems, late_p, late_b, done, "grad_late_scatter_wait")
    reduced = dict(zip(late_names, _finish_reduce(late_p, late_b, late_names)))
    for n in late_names[:-1]:
        update(n, reduced[n])
    g_rest = reduced["rest"]
    d_rest, m_rest, v_rest = _adamw(g_rest, *(_flatten([src[n] for n in rest], row_multiple) for src in (local, mom, var)),
                                    "adamw_rest")
    shapes = [local[n].shape for n in rest]
    for out, flat2d in ((g_out, g_rest), (d_out, d_rest), (m_out, m_rest), (v_out, v_rest)):
        out.update(zip(rest, _unflatten(flat2d, shapes)))

    total_loss = lax.psum(loss[0, 0], MESH_AXES)
    return (total_loss, grad_x[None], *[g_out[n] for n in WEIGHTS], *[d_out[n] for n in WEIGHTS],
            *[m_out[n] for n in WEIGHTS], *[v_out[n] for n in WEIGHTS])
```

```python
import functools

import jax
import jax.numpy as jnp
from jax import lax
from jax.experimental import pallas as pl
from jax.experimental.pallas import tpu as pltpu

F32 = jnp.float32
BF16 = jnp.bfloat16

RMS_EPS = 1e-6
LRU_C = 8.0
ADAM_LR = 0.001
ADAM_B1 = 0.9
ADAM_B2 = 0.999
ADAM_EPS = 1e-08
ADAM_WD = 0.01
ADAM_STEP = 10

LANES = 128
VMEM_LIMIT = 56 * 1024 * 1024
FLAT_W = 1024
FLAT_TR = 256
N_CHIPS = 4
MESH_AXES = ("x", "y", "c")

_NT = (((1,), (1,)), ((), ()))
_TN = (((0,), (0,)), ((), ()))
ANY = pl.BlockSpec(memory_space=pl.ANY)
HBM = pl.BlockSpec(memory_space=pltpu.HBM)
SEM = pl.BlockSpec(memory_space=pltpu.SEMAPHORE)
EFFECT = pltpu.SideEffectType.DATAFLOW_SIDE_EFFECTING
MESH = pl.DeviceIdType.MESH


def _cparams(n_axes):
    return pltpu.CompilerParams(dimension_semantics=("arbitrary",) * n_axes, vmem_limit_bytes=VMEM_LIMIT)


def _sigmoid(x):
    return 1.0 / (1.0 + jnp.exp(-x))


def _log1p_pos(e):
    return jnp.where(e < 1e-3, e * (1.0 - e * (0.5 - e * (1.0 / 3.0))), jnp.log(1.0 + e))


def _neg_expm1(x):
    small = -x * (1.0 + x * (0.5 + x * (1.0 / 6.0 + x * (1.0 / 24.0))))
    return jnp.where(x > -0.05, small, 1.0 - jnp.exp(x))


def _tile(n, want):
    if n <= want:
        return n
    t = want
    while n % t:
        t -= LANES
    assert t > 0, (n, want)
    return t


def _norm_matmul(x, g, w, out_dtype, name, after=None):
    m, k = x.shape
    n = w.shape[1]
    tm, tn = _tile(m, 1024), _tile(n, 1024)

    def body(x_ref, g_ref, w_ref, *rest):
        o_ref, u_ref = rest[-2:]

        @pl.when(pl.program_id(1) == 0)
        def _():
            xf = x_ref[...]
            r = lax.rsqrt(jnp.mean(xf * xf, axis=-1, keepdims=True) + RMS_EPS)
            u_ref[...] = ((xf * r) * g_ref[...]).astype(BF16)

        o_ref[...] = jnp.dot(u_ref[...], w_ref[...], preferred_element_type=F32).astype(o_ref.dtype)

    return pl.pallas_call(
        body, name=name, grid=(m // tm, n // tn),
        in_specs=[pl.BlockSpec((tm, k), lambda i, j: (i, 0)), pl.BlockSpec((1, k), lambda i, j: (0, 0)),
                  pl.BlockSpec((k, tn), lambda i, j: (0, j))] + ([] if after is None else [ANY]),
        out_specs=[pl.BlockSpec((tm, tn), lambda i, j: (i, j)), pl.BlockSpec((tm, k), lambda i, j: (i, 0))],
        out_shape=[jax.ShapeDtypeStruct((m, n), out_dtype), jax.ShapeDtypeStruct((m, k), BF16)],
        compiler_params=_cparams(2),
    )(x, g, w, *([] if after is None else [after]))


def _matmul_res(a, b, res, name):
    m, k = a.shape
    n = b.shape[1]
    tm, tn = _tile(m, 1024), _tile(n, 1024)

    def body(a_ref, b_ref, r_ref, o_ref):
        o_ref[...] = r_ref[...] + jnp.dot(a_ref[...], b_ref[...], preferred_element_type=F32)

    return pl.pallas_call(
        body, name=name, grid=(m // tm, n // tn),
        in_specs=[pl.BlockSpec((tm, k), lambda i, j: (i, 0)), pl.BlockSpec((k, tn), lambda i, j: (0, j)),
                  pl.BlockSpec((tm, tn), lambda i, j: (i, j))],
        out_specs=pl.BlockSpec((tm, tn), lambda i, j: (i, j)),
        out_shape=jax.ShapeDtypeStruct((m, n), F32),
        compiler_params=_cparams(2),
    )(a, b, res)


def _matmul_nt(a, b, name, after=None):
    m, n = a.shape
    k = b.shape[0]
    tm, tk = _tile(m, 1024), _tile(k, 512)

    def body(a_ref, b_ref, *rest):
        o_ref = rest[-1]
        o_ref[...] = lax.dot_general(a_ref[...].astype(BF16), b_ref[...], _NT, preferred_element_type=F32)

    in_specs = [pl.BlockSpec((tm, n), lambda i, j: (i, 0)), pl.BlockSpec((tk, n), lambda i, j: (j, 0))]
    args = [a, b]
    if after is not None:
        in_specs.append(ANY)
        args.append(after)
    return pl.pallas_call(
        body, name=name, grid=(m // tm, k // tk), in_specs=in_specs,
        out_specs=pl.BlockSpec((tm, tk), lambda i, j: (i, j)),
        out_shape=jax.ShapeDtypeStruct((m, k), F32),
        compiler_params=_cparams(2),
    )(*args)


def _matmul_tn(a, b, name, tk, tn, out_shape, out_block, out_index, into=None):
    m, k = a.shape
    n = b.shape[1]
    tm = _tile(m, 2048 if b.dtype == BF16 else 1024)

    def body(a_ref, b_ref, *rest):
        o_ref = rest[-1]
        part = lax.dot_general(a_ref[...].astype(BF16), b_ref[...].astype(BF16), _TN, preferred_element_type=F32)

        @pl.when(pl.program_id(2) == 0)
        def _():
            o_ref[...] = part

        @pl.when(pl.program_id(2) != 0)
        def _():
            o_ref[...] += part

    in_specs = [pl.BlockSpec((tm, tk), lambda i, j, s: (s, i)), pl.BlockSpec((tm, tn), lambda i, j, s: (s, j))]
    args = [a, b]
    if into is not None:
        in_specs.append(ANY)
        args.append(into)
    return pl.pallas_call(
        body, name=name, grid=(k // tk, n // tn, m // tm), in_specs=in_specs,
        out_specs=pl.BlockSpec(out_block, lambda i, j, s: out_index(i, j)),
        out_shape=jax.ShapeDtypeStruct(out_shape, F32),
        input_output_aliases={} if into is None else {2: 0},
        compiler_params=_cparams(3),
    )(*args)


def _dw_cols(a, b, layer, n_layers, into, name):
    k, n = a.shape[1], b.shape[1]
    pn = n // N_CHIPS
    tk = _tile(k, 512)
    return _matmul_tn(a, b, name, tk, pn, (N_CHIPS, n_layers, k, pn), (None, None, tk, pn),
                      lambda i, j: (j, layer, i, 0), into)


def _dw_rows(a, b, layer, n_layers, into, name):
    k, n = a.shape[1], b.shape[1]
    pk = k // N_CHIPS
    tn = _tile(n, 2048)
    return _matmul_tn(a, b, name, pk, tn, (N_CHIPS, n_layers, pk, n), (None, None, pk, tn),
                      lambda i, j: (i, layer, 0, j), into)


def _rms_bwd(du, h, g, dres, name):
    m, d = h.shape
    tm = _tile(m, 256)

    def body(du_ref, h_ref, g_ref, dres_ref, dx_ref, dg_ref, dxb_ref):
        xf = h_ref[...]
        r = lax.rsqrt(jnp.mean(xf * xf, axis=-1, keepdims=True) + RMS_EPS)
        xhat = xf * r
        du_v = du_ref[...]
        dxn = du_v * g_ref[...]
        dx = dres_ref[...] + r * (dxn - xhat * jnp.mean(dxn * xhat, axis=-1, keepdims=True))
        dx_ref[...] = dx
        dxb_ref[...] = dx.astype(BF16)
        part = jnp.sum(du_v * xhat, axis=0, keepdims=True)

        @pl.when(pl.program_id(0) == 0)
        def _():
            dg_ref[...] = part

        @pl.when(pl.program_id(0) != 0)
        def _():
            dg_ref[...] += part

    row = lambda i: (i, 0)
    return pl.pallas_call(
        body, name=name, grid=(m // tm,),
        in_specs=[pl.BlockSpec((tm, d), row), pl.BlockSpec((tm, d), row), pl.BlockSpec((1, d), lambda i: (0, 0)),
                  pl.BlockSpec((tm, d), row)],
        out_specs=[pl.BlockSpec((tm, d), row), pl.BlockSpec((1, d), lambda i: (0, 0)), pl.BlockSpec((tm, d), row)],
        out_shape=[jax.ShapeDtypeStruct((m, d), F32), jax.ShapeDtypeStruct((1, d), F32),
                   jax.ShapeDtypeStruct((m, d), BF16)],
        compiler_params=_cparams(1),
    )(du, h, g, dres)


def _final_loss_bwd(h, g, tgt, name):
    m, d = h.shape
    tm = _tile(m, 256)

    def body(h_ref, g_ref, t_ref, dx_ref, dg_ref, loss_ref, dxb_ref):
        xf = h_ref[...]
        r = lax.rsqrt(jnp.mean(xf * xf, axis=-1, keepdims=True) + RMS_EPS)
        xhat = xf * r
        gv = g_ref[...]
        err = xhat * gv - t_ref[...]
        dy = err * (1.0 / d)
        dxn = dy * gv
        dx = r * (dxn - xhat * jnp.mean(dxn * xhat, axis=-1, keepdims=True))
        dx_ref[...] = dx
        dxb_ref[...] = dx.astype(BF16)
        part = jnp.sum(dy * xhat, axis=0, keepdims=True)
        lpart = jnp.sum(jnp.sum(err * err, axis=0, keepdims=True), axis=1, keepdims=True) * (0.5 / d)

        @pl.when(pl.program_id(0) == 0)
        def _():
            dg_ref[...] = part
            loss_ref[...] = lpart

        @pl.when(pl.program_id(0) != 0)
        def _():
            dg_ref[...] += part
            loss_ref[...] += lpart

    row = lambda i: (i, 0)
    fixed = lambda i: (0, 0)
    return pl.pallas_call(
        body, name=name, grid=(m // tm,),
        in_specs=[pl.BlockSpec((tm, d), row), pl.BlockSpec((1, d), fixed), pl.BlockSpec((tm, d), row)],
        out_specs=[pl.BlockSpec((tm, d), row), pl.BlockSpec((1, d), fixed), pl.BlockSpec((1, 1), fixed),
                   pl.BlockSpec((tm, d), row)],
        out_shape=[jax.ShapeDtypeStruct((m, d), F32), jax.ShapeDtypeStruct((1, d), F32),
                   jax.ShapeDtypeStruct((1, 1), F32), jax.ShapeDtypeStruct((m, d), BF16)],
        compiler_params=_cparams(1),
    )(h, g, tgt)


def _add_cast(a, b, name):
    m, n = a.shape
    tm, tn = _tile(m, 512), _tile(n, 2048)

    def body(a_ref, b_ref, o_ref):
        o_ref[...] = (a_ref[...] + b_ref[...]).astype(BF16)

    blk = lambda i, j: (i, j)
    return pl.pallas_call(
        body, name=name, grid=(m // tm, n // tn),
        in_specs=[pl.BlockSpec((tm, tn), blk), pl.BlockSpec((tm, tn), blk)],
        out_specs=pl.BlockSpec((tm, tn), blk),
        out_shape=jax.ShapeDtypeStruct((m, n), BF16),
        compiler_params=_cparams(2),
    )(a, b)


def _mem_attn_fwd(proj, memkv, layer, dr, dm, name):
    t = proj.shape[0]
    nm = memkv.shape[0]
    tm = _tile(t, 512)
    nh = dm // LANES
    scale = LANES ** -0.5
    qb = (2 * dr) // dm

    def body(q_ref, g_ref, k_ref, v_ref, y_ref):
        for hh in range(nh):
            sl = slice(hh * LANES, (hh + 1) * LANES)
            s = lax.dot_general(q_ref[:, sl].astype(BF16), k_ref[:, sl], _NT, preferred_element_type=F32) * scale
            p = jnp.exp(s - jnp.max(s, axis=-1, keepdims=True))
            p = p / jnp.sum(p, axis=-1, keepdims=True)
            o = jnp.dot(p.astype(BF16), v_ref[:, sl], preferred_element_type=F32)
            gv = g_ref[:, sl]
            y_ref[:, sl] = (o * (gv * _sigmoid(gv))).astype(BF16)

    return pl.pallas_call(
        body, name=name, grid=(t // tm,),
        in_specs=[pl.BlockSpec((tm, dm), lambda i: (i, qb)), pl.BlockSpec((tm, dm), lambda i: (i, qb + 1)),
                  pl.BlockSpec((nm, dm), lambda i: (0, 2 * layer)), pl.BlockSpec((nm, dm), lambda i: (0, 2 * layer + 1))],
        out_specs=pl.BlockSpec((tm, dm), lambda i: (i, 0)),
        out_shape=jax.ShapeDtypeStruct((t, dm), BF16),
        compiler_params=_cparams(1),
    )(proj, proj, memkv, memkv)


def _mem_attn_bwd(proj, memkv, dmix, layer, dr, dm, name):
    t = proj.shape[0]
    nm = memkv.shape[0]
    tm = _tile(t, 512)
    nh = dm // LANES
    scale = LANES ** -0.5
    qb = (2 * dr) // dm
    yb = dr // dm

    def body(q_ref, g_ref, k_ref, v_ref, dy_ref, dq_ref, dg_ref, dk_ref, dv_ref):
        @pl.when(pl.program_id(0) == 0)
        def _():
            dk_ref[...] = jnp.zeros_like(dk_ref)
            dv_ref[...] = jnp.zeros_like(dv_ref)

        for hh in range(nh):
            sl = slice(hh * LANES, (hh + 1) * LANES)
            q = q_ref[:, sl].astype(BF16)
            k = k_ref[:, sl]
            v = v_ref[:, sl]
            s = lax.dot_general(q, k, _NT, preferred_element_type=F32) * scale
            p = jnp.exp(s - jnp.max(s, axis=-1, keepdims=True))
            p = p / jnp.sum(p, axis=-1, keepdims=True)
            p_bf = p.astype(BF16)
            o = jnp.dot(p_bf, v, preferred_element_type=F32)
            gv = g_ref[:, sl]
            sg = _sigmoid(gv)
            dy = dy_ref[:, sl]
            do = dy * (gv * sg)
            dg_ref[:, sl] = (dy * o * (sg * (1.0 + gv * (1.0 - sg)))).astype(BF16)
            do_bf = do.astype(BF16)
            dv_ref[:, sl] += lax.dot_general(p_bf, do_bf, _TN, preferred_element_type=F32)
            dp = lax.dot_general(do_bf, v, _NT, preferred_element_type=F32)
            ds = (p * (dp - jnp.sum(dp * p, axis=-1, keepdims=True)) * scale).astype(BF16)
            dq_ref[:, sl] = jnp.dot(ds, k, preferred_element_type=F32).astype(BF16)
            dk_ref[:, sl] += lax.dot_general(ds, q, _TN, preferred_element_type=F32)

    fixed = lambda i: (0, 0)
    return pl.pallas_call(
        body, name=name, grid=(t // tm,),
        in_specs=[pl.BlockSpec((tm, dm), lambda i: (i, qb)), pl.BlockSpec((tm, dm), lambda i: (i, qb + 1)),
                  pl.BlockSpec((nm, dm), lambda i: (0, 2 * layer)), pl.BlockSpec((nm, dm), lambda i: (0, 2 * layer + 1)),
                  pl.BlockSpec((tm, dm), lambda i: (i, yb))],
        out_specs=[pl.BlockSpec((tm, dm), lambda i: (i, 0)), pl.BlockSpec((tm, dm), lambda i: (i, 0)),
                   pl.BlockSpec((nm, dm), fixed), pl.BlockSpec((nm, dm), fixed)],
        out_shape=[jax.ShapeDtypeStruct((t, dm), BF16), jax.ShapeDtypeStruct((t, dm), BF16),
                   jax.ShapeDtypeStruct((nm, dm), F32), jax.ShapeDtypeStruct((nm, dm), F32)],
        compiler_params=_cparams(1),
    )(proj, proj, memkv, memkv, dmix)


LRU_CHUNK = 256


def _lru_gates(xc, vec, wr_ref, wi_ref):
    r = _sigmoid(jnp.dot(xc.astype(BF16), wr_ref[...], preferred_element_type=F32) + vec[1:2])
    i = _sigmoid(jnp.dot(xc.astype(BF16), wi_ref[...], preferred_element_type=F32) + vec[2:3])
    lam = vec[3:4]
    cl = -LRU_C * (jnp.maximum(-lam, 0.0) + _log1p_pos(jnp.exp(-jnp.abs(lam))))
    la = cl * r
    a = jnp.exp(la)
    s2 = _neg_expm1(2.0 * la)
    return r, i, cl, a, s2


def _lru_fwd(proj, vec, wr, wi, name):
    t = proj.shape[0]
    nb = wr.shape[0]
    dr = nb * LANES
    c = _tile(t, LRU_CHUNK)

    def body(x_ref, g_ref, vec_ref, wr_ref, wi_ref, y_ref, h_ref, xc_ref, carry_ref, xprev_ref):
        @pl.when(pl.program_id(1) == 0)
        def _():
            carry_ref[...] = jnp.zeros_like(carry_ref)
            xprev_ref[...] = jnp.zeros_like(xprev_ref)

        x = x_ref[...]
        vec = vec_ref[...]
        rows = lax.broadcasted_iota(jnp.int32, (c, LANES), 0)
        xprev = xprev_ref[...]
        xc = vec[7:8] * x + vec[0:1]
        for k in range(1, 4):
            xs = jnp.where(rows < k, pltpu.roll(xprev, k, 0), pltpu.roll(x, k, 0))
            xc = xc + vec[7 - k:8 - k] * xs
        xprev_ref[...] = x
        xc_ref[...] = xc

        r, i, cl, a, s2 = _lru_gates(xc, vec, wr_ref, wi_ref)
        hh = jnp.sqrt(s2) * (i * xc)
        aa = a
        d = 1
        while d < c:
            keep = rows >= d
            hh = jnp.where(keep, aa * pltpu.roll(hh, d, 0) + hh, hh)
            aa = jnp.where(keep, aa * pltpu.roll(aa, d, 0), aa)
            d *= 2
        hfull = hh + aa * carry_ref[7:8, :]
        carry_ref[...] = hfull[c - 8:c, :]
        h_ref[...] = hfull
        gv = g_ref[...]
        y_ref[...] = (hfull * (gv * _sigmoid(gv))).astype(BF16)

    blk = lambda n, s: (s, n)
    return pl.pallas_call(
        body, name=name, grid=(nb, t // c),
        in_specs=[pl.BlockSpec((c, LANES), blk), pl.BlockSpec((c, LANES), lambda n, s: (s, nb + n)),
                  pl.BlockSpec((8, LANES), lambda n, s: (0, n)),
                  pl.BlockSpec((None, LANES, LANES), lambda n, s: (n, 0, 0)),
                  pl.BlockSpec((None, LANES, LANES), lambda n, s: (n, 0, 0))],
        out_specs=[pl.BlockSpec((c, LANES), blk)] * 3,
        out_shape=[jax.ShapeDtypeStruct((t, dr), BF16), jax.ShapeDtypeStruct((t, dr), F32),
                   jax.ShapeDtypeStruct((t, dr), F32)],
        scratch_shapes=[pltpu.VMEM((8, LANES), F32), pltpu.VMEM((c, LANES), F32)],
        compiler_params=_cparams(2),
    )(proj, proj, vec, wr, wi)


def _lru_bwd(proj, xc_all, h_all, dmix, vec, wr, wi, name):
    t = proj.shape[0]
    nb = wr.shape[0]
    dr = nb * LANES
    c = _tile(t, LRU_CHUNK)
    nc = t // c

    def body(x_ref, g_ref, xc_ref, h_ref, dy_ref, vec_ref, wr_ref, wi_ref,
             dx_ref, dg_ref, dwr_ref, dwi_ref, dvec_ref, qcarry_ref, dxc_next_ref):
        @pl.when(pl.program_id(1) == 0)
        def _():
            qcarry_ref[...] = jnp.zeros_like(qcarry_ref)
            dxc_next_ref[...] = jnp.zeros_like(dxc_next_ref)
            dwr_ref[...] = jnp.zeros_like(dwr_ref)
            dwi_ref[...] = jnp.zeros_like(dwi_ref)
            dvec_ref[...] = jnp.zeros_like(dvec_ref)

        x = x_ref[...]
        xc = xc_ref[...]
        h = h_ref[...]
        dy = dy_ref[...]
        gv = g_ref[...]
        vec = vec_ref[...]
        rows = lax.broadcasted_iota(jnp.int32, (c, LANES), 0)

        r, i, cl, a, s2 = _lru_gates(xc, vec, wr_ref, wi_ref)
        s = jnp.sqrt(s2)
        ixc = i * xc
        u = s * ixc
        sg = _sigmoid(gv)
        dh = dy * (gv * sg)
        dg_ref[...] = (dy * h * (sg * (1.0 + gv * (1.0 - sg)))).astype(BF16)

        aa = a
        qq = a * dh
        d = 1
        while d < c:
            keep = rows < c - d
            qq = jnp.where(keep, qq + aa * pltpu.roll(qq, c - d, 0), qq)
            aa = jnp.where(keep, aa * pltpu.roll(aa, c - d, 0), aa)
            d *= 2
        qin = qcarry_ref[0:1, :]
        qfull = qq + aa * qin
        gt = dh + jnp.where(rows == c - 1, qin, pltpu.roll(qfull, c - 1, 0))
        qcarry_ref[...] = qfull[0:8, :]

        dla = gt * (h - u) - gt * ixc * (a * a) / s
        dixc = gt * s
        di = dixc * xc
        dxc = dixc * i
        dzr = (dla * cl) * (r * (1.0 - r))
        dzi = di * (i * (1.0 - i))
        dzr_bf = dzr.astype(BF16)
        dzi_bf = dzi.astype(BF16)
        dxc = dxc + lax.dot_general(dzr_bf, wr_ref[...], _NT, preferred_element_type=F32)
        dxc = dxc + lax.dot_general(dzi_bf, wi_ref[...], _NT, preferred_element_type=F32)
        xc_bf = xc.astype(BF16)
        dwr_ref[...] += lax.dot_general(xc_bf, dzr_bf, _TN, preferred_element_type=F32)
        dwi_ref[...] += lax.dot_general(xc_bf, dzi_bf, _TN, preferred_element_type=F32)

        lam = vec[3:4]
        dlam = jnp.sum(dla * r, axis=0, keepdims=True) * (LRU_C * _sigmoid(-lam))
        colsum = lambda v: jnp.sum(v, axis=0, keepdims=True)
        dxn = dxc_next_ref[...]
        dx = vec[7:8] * dxc
        dtaps = [None] * 4
        dtaps[3] = colsum(x * dxc)
        for k in range(1, 4):
            sh = jnp.where(rows < c - k, pltpu.roll(dxc, c - k, 0), pltpu.roll(dxn, c - k, 0))
            dx = dx + vec[7 - k:8 - k] * sh
            dtaps[3 - k] = colsum(x * sh)
        dxc_next_ref[...] = dxc
        dx_ref[...] = dx.astype(BF16)
        dvec_ref[...] += jnp.concatenate([colsum(dxc), colsum(dzr), colsum(dzi), dlam] + dtaps, axis=0)

    rev = lambda n, s: (nc - 1 - s, n)
    sq = lambda n, s: (n, 0, 0)
    return pl.pallas_call(
        body, name=name, grid=(nb, nc),
        in_specs=[pl.BlockSpec((c, LANES), rev), pl.BlockSpec((c, LANES), lambda n, s: (nc - 1 - s, nb + n)),
                  pl.BlockSpec((c, LANES), rev), pl.BlockSpec((c, LANES), rev), pl.BlockSpec((c, LANES), rev),
                  pl.BlockSpec((8, LANES), lambda n, s: (0, n)),
                  pl.BlockSpec((None, LANES, LANES), sq), pl.BlockSpec((None, LANES, LANES), sq)],
        out_specs=[pl.BlockSpec((c, LANES), rev), pl.BlockSpec((c, LANES), rev),
                   pl.BlockSpec((None, LANES, LANES), sq), pl.BlockSpec((None, LANES, LANES), sq),
                   pl.BlockSpec((None, 8, LANES), sq)],
        out_shape=[jax.ShapeDtypeStruct((t, dr), BF16), jax.ShapeDtypeStruct((t, dr), BF16),
                   jax.ShapeDtypeStruct((nb, LANES, LANES), F32), jax.ShapeDtypeStruct((nb, LANES, LANES), F32),
                   jax.ShapeDtypeStruct((nb, 8, LANES), F32)],
        scratch_shapes=[pltpu.VMEM((8, LANES), F32), pltpu.VMEM((c, LANES), F32)],
        compiler_params=_cparams(2),
    )(proj, proj, xc_all, h_all, dmix, vec, wr, wi)


SB_TQ = 1024
SB_TK = 256


def _sb_softplus(z, diag):
    sp = jnp.maximum(z, 0.0) + jnp.log(1.0 + jnp.exp(-jnp.abs(z)))
    mask = None
    if diag:
        mask = lax.broadcasted_iota(jnp.int32, z.shape, 1) < lax.broadcasted_iota(jnp.int32, z.shape, 0)
        sp = jnp.where(mask, sp, 0.0)
    return sp, mask


def _split_dot(v, m):
    hi = v.astype(BF16)
    lo = (v - hi.astype(F32)).astype(BF16)
    return jnp.dot(hi, m, preferred_element_type=F32) + jnp.dot(lo, m, preferred_element_type=F32)


def _tri_ones(kind, tk):
    jj = lax.broadcasted_iota(jnp.int32, (tk, tk), 0)
    ss = lax.broadcasted_iota(jnp.int32, (tk, tk), 1)
    rel = {"ge": jj >= ss, "le": jj <= ss}[kind]
    return jnp.where(rel, 1.0, 0.0).astype(BF16)


def _sb_fwd(proj, kv, name):
    t = proj.shape[0]
    ds = kv.shape[1] // 2
    nh = ds // LANES
    tq = _tile(t, SB_TQ)
    tk = _tile(tq, SB_TK)
    nd = tq // tk
    assert nd % 2 == 0 or t == tq
    scale = LANES ** -0.5

    def body(q_ref, g_ref, k_ref, v_ref, y_ref, o_ref, tl_ref, qbf_ref, acc_ref, run_ref, z_ref, w_ref):
        qi = pl.program_id(1)
        qbf_ref[...] = q_ref[...].astype(BF16)
        tri = _tri_ones("ge", tk)
        acc_ref[...] = jnp.zeros_like(acc_ref)
        run_ref[...] = jnp.zeros_like(run_ref)
        all_rows = [slice(s0, s0 + tk) for s0 in range(0, tq, tk)]

        def weights(zs, groups):
            sps = [_sb_softplus(z, dg) for z, (_, dg) in zip(zs, groups)]
            cums = [_split_dot(sp, tri) for sp, _ in sps]
            ws = []
            for z, (rows, dg), (_, mask), cum in zip(zs, groups, sps, cums):
                run = run_ref[rows, :]
                w = jnp.exp(z - cum - run)
                if dg:
                    w = jnp.where(mask, w, 0.0)
                run_ref[rows, :] = run + cum[:, 0:1]
                ws.append(w.astype(BF16))
            return ws

        n = qi * nd

        def rows_from(u):
            return [slice(s0, s0 + tk) for s0 in range(u * tk, tq, tk)]

        def logits_into(slot, kb, rows_list):
            k = k_ref[pl.ds(pl.multiple_of(kb * tk, tk), tk), :]
            for rows in rows_list:
                z_ref[slot, rows, :] = lax.dot_general(qbf_ref[rows, :], k, _NT, preferred_element_type=F32) * scale

        def add_values(kb, rows_list):
            v = v_ref[pl.ds(pl.multiple_of(kb * tk, tk), tk), :]
            for rows in rows_list:
                acc_ref[rows, :] += jnp.dot(w_ref[rows, :], v, preferred_element_type=F32)

        def weigh(slot, groups):
            for (rows, _), w in zip(groups, weights([z_ref[slot, rows, :] for rows, _ in groups], groups)):
                w_ref[rows, :] = w

        logits_into(0, n + nd - 1, rows_from(nd - 1))
        for i, u in enumerate(reversed(range(nd))):
            slot = i % 2
            if i > 0:
                add_values(n + u + 1, rows_from(u + 1))
            if u > 0:
                logits_into(1 - slot, n + u - 1, rows_from(u - 1))
            else:
                logits_into(1 - slot, jnp.maximum(n - 1, 0), all_rows)
            weigh(slot, [(rows, j == 0) for j, rows in enumerate(rows_from(u))])

        def half_step(j, slot):
            kb = n - 1 - j
            add_values(kb + 1, all_rows)
            logits_into(1 - slot, jnp.maximum(kb - 1, 0), all_rows)
            weigh(slot, [(rows, False) for rows in all_rows])

        def step(i, carry):
            half_step(2 * i, nd % 2)
            half_step(2 * i + 1, 1 - nd % 2)
            return carry

        lax.fori_loop(0, n // 2, step, 0)
        add_values(0, all_rows)
        o = acc_ref[...]
        o_ref[...] = o
        tl_ref[...] = jnp.broadcast_to(run_ref[...], (tq, LANES))
        gv = g_ref[...]
        y_ref[...] = (o * (gv * _sigmoid(gv))).astype(BF16)

    blk = lambda h, i: (i, h)
    return pl.pallas_call(
        body, name=name, grid=(nh, t // tq),
        in_specs=[pl.BlockSpec((tq, LANES), blk), pl.BlockSpec((tq, LANES), lambda h, i: (i, nh + h)),
                  pl.BlockSpec((t, LANES), lambda h, i: (0, h)), pl.BlockSpec((t, LANES), lambda h, i: (0, nh + h))],
        out_specs=[pl.BlockSpec((tq, LANES), blk)] * 3,
        out_shape=[jax.ShapeDtypeStruct((t, ds), BF16), jax.ShapeDtypeStruct((t, ds), F32),
                   jax.ShapeDtypeStruct((t, ds), F32)],
        scratch_shapes=[pltpu.VMEM((tq, LANES), BF16), pltpu.VMEM((tq, LANES), F32), pltpu.VMEM((tq, 1), F32),
                        pltpu.VMEM((2, tq, tk), F32), pltpu.VMEM((tq, tk), BF16)],
        compiler_params=_cparams(2),
    )(proj, proj, kv, kv)


def _sb_bwd(proj, kv, o_all, tl_all, dmix, name):
    t = proj.shape[0]
    ds = kv.shape[1] // 2
    nh = ds // LANES
    tq = _tile(t, SB_TQ)
    tk = _tile(tq, SB_TK)
    nd = tq // tk
    scale = LANES ** -0.5

    def body(q_ref, g_ref, k_ref, v_ref, o_ref, tl_ref, dy_ref, dq_ref, dg_ref, dk_ref, dv_ref,
             qbf_ref, dobf_ref, qt_ref, dot_ref, acc_ref, left_ref, rune_ref, z_ref, dw_ref, wp_ref, dzp_ref):
        qi = pl.program_id(1)

        @pl.when(qi == 0)
        def _():
            dk_ref[...] = jnp.zeros_like(dk_ref)
            dv_ref[...] = jnp.zeros_like(dv_ref)

        qbf_ref[...] = q_ref[...].astype(BF16)
        qt_ref[...] = q_ref[...].T.astype(BF16)
        gv = g_ref[...]
        sg = _sigmoid(gv)
        dy = dy_ref[...]
        do = dy * (gv * sg)
        dobf_ref[...] = do.astype(BF16)
        dot_ref[...] = do.T.astype(BF16)
        dg_ref[...] = (dy * o_ref[...] * (sg * (1.0 + gv * (1.0 - sg)))).astype(BF16)
        tri = _tri_ones("le", tk)
        acc_ref[...] = jnp.zeros_like(acc_ref)
        left_ref[...] = tl_ref[:, 0:1]
        rune_ref[...] = jnp.zeros_like(rune_ref)
        wp_ref[...] = jnp.zeros_like(wp_ref)
        dzp_ref[...] = jnp.zeros_like(dzp_ref)
        all_rows = [slice(s0, s0 + tk) for s0 in range(0, tq, tk)]

        def grads(zs, dws, groups):
            sps = [_sb_softplus(z, dg) for z, (_, dg) in zip(zs, groups)]
            cums = [_split_dot(sp, tri) for sp, _ in sps]
            ws, es, lbs = [], [], []
            for z, (rows, dg), (sp, mask), cum, dw in zip(zs, groups, sps, cums, dws):
                left = left_ref[rows, :]
                lb = z - sp
                w = jnp.exp(lb - (left - cum))
                if dg:
                    w = jnp.where(mask, w, 0.0)
                left_ref[rows, :] = left - cum[:, tk - 1:tk]
                ws.append(w.astype(BF16))
                es.append(dw * w)
                lbs.append(lb)
            cumes = [_split_dot(e, tri) for e in es]
            dzs = []
            for (rows, dg), (_, mask), lb, e, cume in zip(groups, sps, lbs, es, cumes):
                rune = rune_ref[rows, :]
                dz = (e - jnp.exp(lb) * (rune + cume)) * scale
                if dg:
                    dz = jnp.where(mask, dz, 0.0)
                rune_ref[rows, :] = rune + cume[:, tk - 1:tk]
                dzs.append(dz.astype(BF16))
            return dzs, ws

        n = qi * nd

        def rows_from(u):
            return [slice(s0, s0 + tk) for s0 in range(u * tk, tq, tk)]

        def logits_into(slot, kb, rows_list):
            k0 = pl.multiple_of(kb * tk, tk)
            k = k_ref[pl.ds(k0, tk), :]
            v = v_ref[pl.ds(k0, tk), :]
            for rows in rows_list:
                z_ref[slot, rows, :] = lax.dot_general(qbf_ref[rows, :], k, _NT, preferred_element_type=F32) * scale
                dw_ref[slot, rows, :] = lax.dot_general(dobf_ref[rows, :], v, _NT, preferred_element_type=F32)

        def apply_stored(kb, u):
            k0 = pl.multiple_of(kb * tk, tk)
            k = k_ref[pl.ds(k0, tk), :]
            for rows in rows_from(u):
                acc_ref[rows, :] += jnp.dot(dzp_ref[rows, :], k, preferred_element_type=F32)
            seen = slice(u * tk, tq)
            dk_ref[:, pl.ds(k0, tk)] += jnp.dot(qt_ref[:, seen], dzp_ref[seen, :], preferred_element_type=F32)
            dv_ref[:, pl.ds(k0, tk)] += jnp.dot(dot_ref[:, seen], wp_ref[seen, :], preferred_element_type=F32)

        def differentiate(slot, groups):
            dzs, ws = grads([z_ref[slot, rows, :] for rows, _ in groups], [dw_ref[slot, rows, :] for rows, _ in groups],
                            groups)
            for (rows, _), dz, w in zip(groups, dzs, ws):
                dzp_ref[rows, :] = dz
                wp_ref[rows, :] = w

        logits_into(0, 0, all_rows)

        def half_step(j, slot):
            apply_stored(jnp.maximum(j - 1, 0), 0)
            logits_into(1 - slot, j + 1, all_rows)
            differentiate(slot, [(rows, False) for rows in all_rows])

        def step(i, carry):
            half_step(2 * i, 0)
            half_step(2 * i + 1, 1)
            return carry

        lax.fori_loop(0, n // 2, step, 0)
        for u in range(nd):
            slot = u % 2
            if u == 0:
                apply_stored(jnp.maximum(n - 1, 0), 0)
            else:
                apply_stored(n + u - 1, u - 1)
            if u + 1 < nd:
                logits_into(1 - slot, n + u + 1, rows_from(u + 1))
            differentiate(slot, [(rows, j == 0) for j, rows in enumerate(rows_from(u))])
        apply_stored(n + nd - 1, nd - 1)
        dq_ref[...] = acc_ref[...].astype(BF16)

    blk = lambda h, i: (i, h)
    whole = lambda h, i: (0, h)
    return pl.pallas_call(
        body, name=name, grid=(nh, t // tq),
        in_specs=[pl.BlockSpec((tq, LANES), blk), pl.BlockSpec((tq, LANES), lambda h, i: (i, nh + h)),
                  pl.BlockSpec((t, LANES), whole), pl.BlockSpec((t, LANES), lambda h, i: (0, nh + h)),
                  pl.BlockSpec((tq, LANES), blk), pl.BlockSpec((tq, LANES), blk), pl.BlockSpec((tq, LANES), blk)],
        out_specs=[pl.BlockSpec((tq, LANES), blk), pl.BlockSpec((tq, LANES), blk),
                   pl.BlockSpec((LANES, t), lambda h, i: (h, 0)), pl.BlockSpec((LANES, t), lambda h, i: (h, 0))],
        out_shape=[jax.ShapeDtypeStruct((t, ds), BF16), jax.ShapeDtypeStruct((t, ds), BF16),
                   jax.ShapeDtypeStruct((ds, t), F32), jax.ShapeDtypeStruct((ds, t), F32)],
        scratch_shapes=[pltpu.VMEM((tq, LANES), BF16), pltpu.VMEM((tq, LANES), BF16),
                        pltpu.VMEM((LANES, tq), BF16), pltpu.VMEM((LANES, tq), BF16), pltpu.VMEM((tq, LANES), F32),
                        pltpu.VMEM((tq, 1), F32), pltpu.VMEM((tq, 1), F32),
                        pltpu.VMEM((2, tq, tk), F32), pltpu.VMEM((2, tq, tk), F32),
                        pltpu.VMEM((tq, tk), BF16), pltpu.VMEM((tq, tk), BF16)],
        compiler_params=_cparams(2),
    )(proj, proj, kv, kv, o_all, tl_all, dmix)


def _place():
    x, y, c = lax.axis_index("x"), lax.axis_index("y"), lax.axis_index("c")
    chips = [(1 - x, y), (x, 1 - y), (1 - x, 1 - y)]
    return x, y, c, chips


def _remote(src, dst, send_sems, recv_sems, k, to):
    return pltpu.make_async_remote_copy(src_ref=src, dst_ref=dst, send_sem=send_sems.at[k], recv_sem=recv_sems.at[k],
                                        device_id=to, device_id_type=MESH)


def _my_chip():
    return 2 * lax.axis_index("x") + lax.axis_index("y")


def _place_own(shard, name):
    r, w = shard.shape
    tr = _tile(r, FLAT_TR)

    def body(x_ref, o_ref):
        o_ref[...] = x_ref[...]

    return pl.pallas_call(
        body, name=name, out_shape=jax.ShapeDtypeStruct((N_CHIPS, r, w), shard.dtype), grid=(r // tr,),
        in_specs=[pl.BlockSpec((tr, w), lambda i: (i, 0))],
        out_specs=pl.BlockSpec((None, tr, w), lambda i: (_my_chip(), i, 0)),
        compiler_params=_cparams(1),
    )(shard)


def _chip_all_gather(shards, name):
    n = len(shards)

    def body(*refs):
        x_refs, out_refs, send_sems, recv_sems = refs[:n], refs[2 * n:3 * n], refs[3 * n], refs[3 * n + 1]
        x, y, c, chips = _place()
        me = 2 * x + y
        sibling = (x, y, 1 - c)

        def rows(t, core):
            rh = x_refs[t].shape[0] // 2
            return pl.ds(core * rh, rh)

        first = [_remote(x_refs[t].at[rows(t, c)], out_refs[t].at[me, rows(t, c)], send_sems, recv_sems, 6 * t + k,
                         (cx, cy, c)) for t in range(n) for k, (cx, cy) in enumerate(chips)]
        for cp in first:
            cp.start()
        passed = []
        for k, (cx, cy) in enumerate(chips):
            for t in range(n):
                got = out_refs[t].at[2 * cx + cy, rows(t, c)]
                _remote(got, got, send_sems, recv_sems, 6 * t + k, (cx, cy, c)).wait_recv()
                fwd = _remote(got, got, send_sems, recv_sems, 6 * t + 3 + k, sibling)
                fwd.start()
                passed.append(fwd)
        for k, (cx, cy) in enumerate(chips):
            for t in range(n):
                got = out_refs[t].at[2 * cx + cy, rows(t, 1 - c)]
                _remote(got, got, send_sems, recv_sems, 6 * t + 3 + k, sibling).wait_recv()
        for cp in first + passed:
            cp.wait_send()

    bufs = [_place_own(s, f"{name}_own{t}") for t, s in enumerate(shards)]
    return pl.pallas_call(
        body, name=name, in_specs=[ANY] * (2 * n), out_specs=[ANY] * n,
        out_shape=[jax.ShapeDtypeStruct((N_CHIPS,) + s.shape, s.dtype) for s in shards],
        input_output_aliases={n + t: t for t in range(n)},
        scratch_shapes=[pltpu.SemaphoreType.DMA((6 * n,)), pltpu.SemaphoreType.DMA((6 * n,))],
    )(*shards, *bufs)


def _chip_gather_start(shards, name):
    n = len(shards)

    def body(*refs):
        x_refs, buf_refs, send_sems, recv_sems, token = refs[:n], refs[n:2 * n], refs[2 * n], refs[2 * n + 1], refs[-1]
        x, y, c, chips = _place()
        me = 2 * x + y
        for t in range(n):
            for k, (cx, cy) in enumerate(chips):
                _remote(x_refs[t], buf_refs[t].at[me], send_sems, recv_sems, 3 * t + k, (cx, cy, c)).start()
        token[...] = jnp.zeros_like(token)

    bufs = [_place_own(s, f"{name}_own{t}") for t, s in enumerate(shards)]
    hbm = [pltpu.HBM(a.shape, a.dtype) for a in list(shards) + bufs]
    outs = pl.pallas_call(
        body, name=name, in_specs=[HBM] * (2 * n),
        out_shape=(pltpu.SemaphoreType.DMA((3 * n,)), pltpu.SemaphoreType.DMA((3 * n,)), *hbm,
                   jax.ShapeDtypeStruct((8, LANES), F32)),
        out_specs=(SEM, SEM, *[HBM] * (2 * n), pl.BlockSpec(memory_space=pltpu.VMEM)),
        input_output_aliases={t: 2 + t for t in range(2 * n)},
        compiler_params=pltpu.CompilerParams(has_side_effects=EFFECT),
    )(*[pltpu.with_memory_space_constraint(a, pltpu.HBM) for a in list(shards) + bufs])
    return outs[0], outs[1], outs[2:2 + n], outs[2 + n:2 + 2 * n], outs[-1]


def _chip_gather_wait(send_sems, recv_sems, shards, bufs, after, name):
    n = len(shards)

    def body(*refs):
        x_refs, buf_refs, send, recv = refs[:n], refs[n:2 * n], refs[2 * n], refs[2 * n + 1]
        x, y, c, chips = _place()
        for t in range(n):
            for k, (cx, cy) in enumerate(chips):
                cp = _remote(x_refs[t], buf_refs[t].at[2 * cx + cy], send, recv, 3 * t + k, (cx, cy, c))
                cp.wait_send()
                cp.wait_recv()

    hbm = [pltpu.HBM(a.shape, a.dtype) for a in list(shards) + list(bufs)]
    outs = pl.pallas_call(
        body, name=name, in_specs=[HBM] * (2 * n) + [SEM, SEM, ANY],
        out_shape=tuple(hbm), out_specs=tuple([HBM] * (2 * n)),
        input_output_aliases={t: t for t in range(2 * n)},
        compiler_params=pltpu.CompilerParams(has_side_effects=EFFECT),
    )(*shards, *bufs, send_sems, recv_sems, after)
    return outs[n:]


def _sibling_take_half(ss, name):
    n = len(ss)

    def body(*refs):
        s_refs, a_refs, send_sems, recv_sems = refs[:n], refs[n:2 * n], refs[2 * n], refs[2 * n + 1]
        x, y, c, _ = _place()
        cps = []
        for t in range(n):
            rh = s_refs[t].shape[1] // 2
            cps.append(_remote(s_refs[t].at[:, pl.ds((1 - c) * rh, rh), :], a_refs[t], send_sems, recv_sems, t,
                               (x, y, 1 - c)))
        for cp in cps:
            cp.start()
        for cp in cps:
            cp.wait()

    return pl.pallas_call(
        body, name=name, in_specs=[ANY] * n, out_specs=[ANY] * n,
        out_shape=[jax.ShapeDtypeStruct((s.shape[0], s.shape[1] // 2, s.shape[2]), s.dtype) for s in ss],
        scratch_shapes=[pltpu.SemaphoreType.DMA((n,)), pltpu.SemaphoreType.DMA((n,))],
    )(*ss)


def _pair_sum(s, a, dtype, name):
    n, r, w = s.shape
    rh = r // 2
    tr = _tile(rh, FLAT_TR)
    nblk = rh // tr

    def body(s_ref, a_ref, o_ref):
        o_ref[...] = (s_ref[...] + a_ref[...]).astype(dtype)

    return pl.pallas_call(
        body, name=name, out_shape=jax.ShapeDtypeStruct((n, rh, w), dtype), grid=(n, nblk),
        in_specs=[pl.BlockSpec((None, tr, w), lambda k, i: (k, lax.axis_index("c") * nblk + i, 0)),
                  pl.BlockSpec((None, tr, w), lambda k, i: (k, i, 0))],
        out_specs=pl.BlockSpec((None, tr, w), lambda k, i: (k, i, 0)),
        compiler_params=_cparams(2),
    )(s, a)


def _chip_scatter_start(ps, name):
    n = len(ps)

    def body(*refs):
        p_refs, b_refs, send_sems, recv_sems, token = refs[:n], refs[n:2 * n], refs[2 * n], refs[2 * n + 1], refs[-1]
        x, y, c, chips = _place()
        me = 2 * x + y
        for t in range(n):
            for k, (cx, cy) in enumerate(chips):
                _remote(p_refs[t].at[2 * cx + cy], b_refs[t].at[me], send_sems, recv_sems, 3 * t + k, (cx, cy, c)).start()
        token[...] = jnp.zeros_like(token)

    hbm = [pltpu.HBM(p.shape, p.dtype) for p in ps]
    outs = pl.pallas_call(
        body, name=name, in_specs=[HBM] * (2 * n),
        out_shape=(pltpu.SemaphoreType.DMA((3 * n,)), pltpu.SemaphoreType.DMA((3 * n,)), *hbm, *hbm,
                   jax.ShapeDtypeStruct((8, LANES), F32)),
        out_specs=(SEM, SEM, *[HBM] * (2 * n), pl.BlockSpec(memory_space=pltpu.VMEM)),
        input_output_aliases={t: 2 + t for t in range(2 * n)},
        compiler_params=pltpu.CompilerParams(has_side_effects=EFFECT),
    )(*[pltpu.with_memory_space_constraint(p, pltpu.HBM) for p in ps],
      *[pltpu.with_memory_space_constraint(lax.empty(p.shape, p.dtype), pltpu.HBM) for p in ps])
    return outs[0], outs[1], outs[2:2 + n], outs[2 + n:2 + 2 * n], outs[-1]


def _chip_scatter_wait(send_sems, recv_sems, ps, bs, after, name):
    n = len(ps)

    def body(*refs):
        p_refs, b_refs, send, recv = refs[:n], refs[n:2 * n], refs[2 * n], refs[2 * n + 1]
        x, y, c, chips = _place()
        for t in range(n):
            for k, (cx, cy) in enumerate(chips):
                cp = _remote(p_refs[t].at[2 * cx + cy], b_refs[t].at[2 * cx + cy], send, recv, 3 * t + k, (cx, cy, c))
                cp.wait_send()
                cp.wait_recv()

    hbm = [pltpu.HBM(p.shape, p.dtype) for p in ps]
    outs = pl.pallas_call(
        body, name=name, in_specs=[HBM] * (2 * n) + [SEM, SEM, ANY],
        out_shape=(*hbm, *hbm), out_specs=tuple([HBM] * (2 * n)),
        input_output_aliases={t: t for t in range(2 * n)},
        compiler_params=pltpu.CompilerParams(has_side_effects=EFFECT),
    )(*ps, *bs, send_sems, recv_sems, after)
    return outs[n:]


def _chip_sum(p, b, name, after=None):
    n, rh, w = p.shape
    tr = _tile(rh, FLAT_TR)
    nblk = rh // tr

    def body(p_ref, b0_ref, b1_ref, b2_ref, b3_ref, *rest):
        o_ref = rest[-1]
        me = _my_chip()
        own = p_ref[...]
        t = [jnp.where(me == k, own, b_ref[...]).astype(F32) for k, b_ref in enumerate((b0_ref, b1_ref, b2_ref, b3_ref))]
        o_ref[...] = ((t[0] + t[1]) + t[2]) + t[3]

    def other(k):
        return lambda i: (jnp.where(_my_chip() == k, (k + 1) % N_CHIPS, k), i, 0)

    return pl.pallas_call(
        body, name=name, out_shape=jax.ShapeDtypeStruct((2 * rh, w), F32), grid=(nblk,),
        in_specs=[pl.BlockSpec((None, tr, w), lambda i: (_my_chip(), i, 0))]
        + [pl.BlockSpec((None, tr, w), other(k)) for k in range(N_CHIPS)] + ([] if after is None else [ANY]),
        out_specs=pl.BlockSpec((tr, w), lambda i: (lax.axis_index("c") * nblk + i, 0)),
        compiler_params=_cparams(1),
    )(p, b, b, b, b, *([] if after is None else [after]))


def _sibling_join(gs, name):
    n = len(gs)

    def body(*refs):
        g_refs, send_sems, recv_sems = refs[n:2 * n], refs[2 * n], refs[2 * n + 1]
        x, y, c, _ = _place()
        cps = []
        for t in range(n):
            rh = g_refs[t].shape[0] // 2
            mine = g_refs[t].at[pl.ds(c * rh, rh)]
            cps.append(_remote(mine, mine, send_sems, recv_sems, t, (x, y, 1 - c)))
        for cp in cps:
            cp.start()
        for t in range(n):
            rh = g_refs[t].shape[0] // 2
            theirs = g_refs[t].at[pl.ds((1 - c) * rh, rh)]
            _remote(theirs, theirs, send_sems, recv_sems, t, (x, y, 1 - c)).wait_recv()
        for cp in cps:
            cp.wait_send()

    return pl.pallas_call(
        body, name=name, in_specs=[ANY] * n, out_specs=[ANY] * n,
        out_shape=[jax.ShapeDtypeStruct(g.shape, g.dtype) for g in gs],
        input_output_aliases={t: t for t in range(n)},
        scratch_shapes=[pltpu.SemaphoreType.DMA((n,)), pltpu.SemaphoreType.DMA((n,))],
    )(*gs)


def _adamw(g, w, m, v, name):
    r, wd = g.shape

    def body(g_ref, w_ref, m_ref, v_ref, d_ref, mo_ref, vo_ref):
        gv = g_ref[...]
        mn = ADAM_B1 * m_ref[...] + (1.0 - ADAM_B1) * gv
        vn = ADAM_B2 * v_ref[...] + (1.0 - ADAM_B2) * (gv * gv)
        m_hat = mn / (1.0 - ADAM_B1 ** ADAM_STEP)
        v_hat = vn / (1.0 - ADAM_B2 ** ADAM_STEP)
        d_ref[...] = -ADAM_LR * (m_hat / (jnp.sqrt(v_hat) + ADAM_EPS) + ADAM_WD * w_ref[...])
        mo_ref[...] = mn
        vo_ref[...] = vn

    tr = _tile(r, FLAT_TR)
    row = lambda i: (i, 0)
    spec = pl.BlockSpec((tr, wd), row)
    return pl.pallas_call(
        body, name=name, grid=(r // tr,), in_specs=[spec] * 4, out_specs=[spec] * 3,
        out_shape=[jax.ShapeDtypeStruct((r, wd), F32)] * 3,
        compiler_params=_cparams(1),
    )(g, w, m, v)


def _pair_sums(ss, names):
    a = _sibling_take_half(ss, "grad_sibling_half_" + names[0])
    return [_pair_sum(s, a_t, BF16 if n in BIG else F32, f"grad_pair_sum_{n}") for s, a_t, n in zip(ss, a, names)]


def _finish_reduce(ps, bs, names, after=None):
    g = [_chip_sum(p_t, b_t, f"grad_chip_sum_{n}", after if i == 0 else None)
         for i, (p_t, b_t, n) in enumerate(zip(ps, bs, names))]
    return _sibling_join(g, "grad_sibling_join_" + names[0])


WEIGHTS = ("mem_norm", "w_mem_kv", "norm_a", "w_in_a", "conv_w", "conv_b", "w_rec_gate", "b_rec_gate", "w_in_gate",
           "b_in_gate", "lru_lambda", "w_out_a", "kv_norm", "w_kv", "norm_b", "w_in_b", "w_out_b", "final_norm")
SHARD_DIM = {"mem_norm": None, "w_mem_kv": 1, "norm_a": 1, "w_in_a": 2, "conv_w": 2, "conv_b": 1, "w_rec_gate": None,
             "b_rec_gate": 1, "w_in_gate": None, "b_in_gate": 1, "lru_lambda": 1, "w_out_a": 1, "kv_norm": None,
             "w_kv": 1, "norm_b": None, "w_in_b": 2, "w_out_b": 1, "final_norm": None}
BIG = ("w_mem_kv", "w_in_a", "w_out_a", "w_kv", "w_in_b", "w_out_b")
SMALL = ("norm_a", "conv_w", "conv_b", "b_rec_gate", "b_in_gate", "lru_lambda")


def _pad_rows(flat, row_multiple):
    per = FLAT_W * row_multiple
    n = flat.shape[0]
    total = -(-n // per) * per
    return jnp.pad(flat, (0, total - n)).reshape(total // FLAT_W, FLAT_W)


def _flatten(parts, row_multiple):
    return _pad_rows(jnp.concatenate([p.reshape(-1) for p in parts]), row_multiple)


def _unflatten(flat2d, shapes):
    flat = flat2d.reshape(-1)
    out, off = [], 0
    for shp in shapes:
        n = 1
        for s in shp:
            n *= s
        out.append(flat[off:off + n].reshape(shp))
        off += n
    return out


LATE = ("w_kv", "w_in_b", "w_out_b")


def _bf16_rows(w):
    return w.astype(BF16).reshape(-1, w.shape[-1])


def _whole(gathered, shape, dim):
    return jnp.concatenate([gathered[k].reshape(shape) for k in range(N_CHIPS)], axis=dim)


def _gather_weights(local):
    early = [n for n in BIG if n not in LATE]
    small_shapes = [local[n].shape for n in SMALL]
    send_sems, recv_sems, shards, bufs, token = _chip_gather_start([_bf16_rows(local[n]) for n in LATE],
                                                                   "late_weights_gather_start")
    gathered = _chip_all_gather([_bf16_rows(local[n]) for n in early] + [_flatten([local[n] for n in SMALL], 16)],
                                "weights_all_gather")
    full = {n: _whole(g, local[n].shape, SHARD_DIM[n]) for n, g in zip(early, gathered)}
    per_chip = [_unflatten(gathered[-1][k], small_shapes) for k in range(N_CHIPS)]
    for i, n in enumerate(SMALL):
        full[n] = jnp.concatenate([per_chip[k][i] for k in range(N_CHIPS)], axis=SHARD_DIM[n])

    def late(after):
        got = _chip_gather_wait(send_sems, recv_sems, shards, bufs, after, "late_weights_gather_wait")
        return {n: _whole(g, local[n].shape, SHARD_DIM[n]) for n, g in zip(LATE, got)}

    return full, token, late


def _piece(g, name, k):
    dim = SHARD_DIM[name]
    if dim is None:
        return g
    n = g.shape[dim] // N_CHIPS
    return lax.slice_in_dim(g, k * n, (k + 1) * n, axis=dim)


def _local_grads(x, mem, tgt, wts, token, late):
    t, d = x.shape
    depth = wts["w_mem_kv"].shape[0]
    n_a = wts["w_in_a"].shape[0]
    n_b = wts["norm_b"].shape[0]
    nb = wts["w_rec_gate"].shape[1]
    dr = nb * LANES
    dm = wts["w_mem_kv"].shape[2] // 2
    row = lambda v: v.reshape(1, -1)

    wm_all = jnp.concatenate([wts["w_mem_kv"][l] for l in range(depth)], axis=1)
    memkv, memn_bf = _norm_matmul(mem, row(wts["mem_norm"]), wm_all, BF16, "mem_kv_proj", after=token)

    h = x
    saved = []
    vecs = []
    for l in range(n_a):
        proj, u_bf = _norm_matmul(h, row(wts["norm_a"][l]), wts["w_in_a"][l], F32, f"a{l}_in_proj")
        vec = jnp.concatenate([row(wts["conv_b"][l]), row(wts["b_rec_gate"][l]), row(wts["b_in_gate"][l]),
                               row(wts["lru_lambda"][l]), wts["conv_w"][l]], axis=0)
        vecs.append(vec)
        y_rnn, h_rnn, xc = _lru_fwd(proj, vec, wts["w_rec_gate"][l], wts["w_in_gate"][l], f"a{l}_lru_fwd")
        y_mem = _mem_attn_fwd(proj, memkv, l, dr, dm, f"a{l}_mem_fwd")
        mix = jnp.concatenate([y_rnn, y_mem], axis=1)
        h_next = _matmul_res(mix, wts["w_out_a"][l], h, f"a{l}_out_proj")
        saved.append((h, proj, u_bf, mix, h_rnn, xc))
        h = h_next

    h_kv = h
    wts = {**wts, **late(h_kv)}
    kv, ukv_bf = _norm_matmul(h_kv, row(wts["kv_norm"]), wts["w_kv"], BF16, "kv_proj")

    for j in range(n_b):
        l = n_a + j
        proj, u_bf = _norm_matmul(h, row(wts["norm_b"][j]), wts["w_in_b"][j], F32, f"b{j}_in_proj")
        y_sb, o_sb, tl_sb = _sb_fwd(proj, kv, f"b{j}_sb_fwd")
        y_mem = _mem_attn_fwd(proj, memkv, l, dr, dm, f"b{j}_mem_fwd")
        mix = jnp.concatenate([y_sb, y_mem], axis=1)
        h_next = _matmul_res(mix, wts["w_out_b"][j], h, f"b{j}_out_proj")
        saved.append((h, proj, u_bf, mix, o_sb, tl_sb))
        h = h_next

    dh, d_final, loss, dh_bf = _final_loss_bwd(h, row(wts["final_norm"]), tgt, "final_loss_bwd")

    grads = {"final_norm": d_final.reshape(-1)}
    big = {}
    dmemkv = [None] * depth
    g_norm_b = [None] * n_b
    dks, dvs = [], []
    for j in reversed(range(n_b)):
        l = n_a + j
        h_in, proj, u_bf, mix, o_sb, tl_sb = saved[l]
        dmix = _matmul_nt(dh_bf, wts["w_out_b"][j], f"b{j}_dmix")
        big["w_out_b"] = _dw_rows(mix, dh_bf, j, n_b, big.get("w_out_b"), f"b{j}_dw_out")
        dq, dg, dk, dv = _sb_bwd(proj, kv, o_sb, tl_sb, dmix, f"b{j}_sb_bwd")
        dqm, dgm, dkm, dvm = _mem_attn_bwd(proj, memkv, dmix, l, dr, dm, f"b{j}_mem_bwd")
        dmemkv[l] = (dkm, dvm)
        dproj = jnp.concatenate([dq, dg, dqm, dgm], axis=1)
        du = _matmul_nt(dproj, wts["w_in_b"][j], f"b{j}_du")
        big["w_in_b"] = _dw_cols(u_bf, dproj, j, n_b, big.get("w_in_b"), f"b{j}_dw_in")
        dh, dgn, dh_bf = _rms_bwd(du, h_in, row(wts["norm_b"][j]), dh, f"b{j}_rms_bwd")
        g_norm_b[j] = dgn.reshape(-1)
        dks.append(dk)
        dvs.append(dv)
    assert n_b == 2
    dkv = jnp.concatenate([_add_cast(dks[0], dks[1], "dk_sum").T, _add_cast(dvs[0], dvs[1], "dv_sum").T], axis=1)
    du = _matmul_nt(dkv, wts["w_kv"], "kv_du")
    big["w_kv"] = _dw_cols(ukv_bf, dkv, 0, 1, None, "kv_dw")
    dh, dgn, dh_bf = _rms_bwd(du, h_kv, row(wts["kv_norm"]), dh, "kv_rms_bwd")
    grads["kv_norm"] = dgn.reshape(-1)

    early_names = [n for n in BIG if n in big]
    early_p = _pair_sums([big[n].reshape(N_CHIPS, -1, big[n].shape[-1]) for n in early_names], early_names)
    send_sems, recv_sems, early_p, early_b, token = _chip_scatter_start(early_p, "grad_chip_scatter_start")

    g_norm_a = [None] * n_a
    g_wr, g_wi, g_vec = [None] * n_a, [None] * n_a, [None] * n_a
    for l in reversed(range(n_a)):
        h_in, proj, u_bf, mix, h_rnn, xc = saved[l]
        dmix = _matmul_nt(dh_bf, wts["w_out_a"][l], f"a{l}_dmix", after=token if l == n_a - 1 else None)
        big["w_out_a"] = _dw_rows(mix, dh_bf, l, n_a, big.get("w_out_a"), f"a{l}_dw_out")
        dx, dg, g_wr[l], g_wi[l], dvec = _lru_bwd(proj, xc, h_rnn, dmix, vecs[l], wts["w_rec_gate"][l],
                                                  wts["w_in_gate"][l], f"a{l}_lru_bwd")
        g_vec[l] = dvec.transpose(1, 0, 2).reshape(8, dr)
        dqm, dgm, dkm, dvm = _mem_attn_bwd(proj, memkv, dmix, l, dr, dm, f"a{l}_mem_bwd")
        dmemkv[l] = (dkm, dvm)
        dproj = jnp.concatenate([dx, dg, dqm, dgm], axis=1)
        du = _matmul_nt(dproj, wts["w_in_a"][l], f"a{l}_du")
        big["w_in_a"] = _dw_cols(u_bf, dproj, l, n_a, big.get("w_in_a"), f"a{l}_dw_in")
        dh, dgn, dh_bf = _rms_bwd(du, h_in, row(wts["norm_a"][l]), dh, f"a{l}_rms_bwd")
        g_norm_a[l] = dgn.reshape(-1)

    dmemkv_all = jnp.concatenate([jnp.concatenate(p, axis=1) for p in dmemkv], axis=1).astype(BF16)
    pk = d // N_CHIPS
    big["w_mem_kv"] = _matmul_tn(memn_bf, dmemkv_all, "mem_dw", pk, 2 * dm, (N_CHIPS, depth, pk, 2 * dm),
                                 (None, None, pk, 2 * dm), lambda i, j: (i, j, 0, 0))
    dmemn = _matmul_nt(dmemkv_all, wm_all, "mem_du")
    _, dgn, _ = _rms_bwd(dmemn, mem, row(wts["mem_norm"]), jnp.zeros_like(mem), "mem_rms_bwd")
    grads["mem_norm"] = dgn.reshape(-1)
    grads["norm_a"] = jnp.stack(g_norm_a)
    grads["w_rec_gate"] = jnp.stack(g_wr)
    grads["w_in_gate"] = jnp.stack(g_wi)
    gv = jnp.stack(g_vec)
    grads["conv_b"], grads["b_rec_gate"], grads["b_in_gate"], grads["lru_lambda"] = gv[:, 0], gv[:, 1], gv[:, 2], gv[:, 3]
    grads["conv_w"] = gv[:, 4:8]
    grads["norm_b"] = jnp.stack(g_norm_b)
    early_b = _chip_scatter_wait(send_sems, recv_sems, early_p, early_b, dh, "grad_chip_scatter_wait")
    early = (early_names, early_p, early_b)
    big = {n: g.reshape(N_CHIPS, -1, g.shape[-1]) for n, g in big.items() if n not in early_names}
    return loss, dh, grads, big, early


def kernel(x, mem, mem_norm, w_mem_kv, norm_a, w_in_a, conv_w, conv_b, w_rec_gate, b_rec_gate, w_in_gate, b_in_gate, lru_lambda, w_out_a, kv_norm, w_kv, norm_b, w_in_b, w_out_b, final_norm, loss_target, m_mem_norm, m_w_mem_kv, m_norm_a, m_w_in_a, m_conv_w, m_conv_b, m_w_rec_gate, m_b_rec_gate, m_w_in_gate, m_b_in_gate, m_lru_lambda, m_w_out_a, m_kv_norm, m_w_kv, m_norm_b, m_w_in_b, m_w_out_b, m_final_norm, v_mem_norm, v_w_mem_kv, v_norm_a, v_w_in_a, v_conv_w, v_conv_b, v_w_rec_gate, v_b_rec_gate, v_w_in_gate, v_b_in_gate, v_lru_lambda, v_w_out_a, v_kv_norm, v_w_kv, v_norm_b, v_w_in_b, v_w_out_b, v_final_norm):
    local = dict(mem_norm=mem_norm, w_mem_kv=w_mem_kv, norm_a=norm_a, w_in_a=w_in_a, conv_w=conv_w, conv_b=conv_b,
                 w_rec_gate=w_rec_gate, b_rec_gate=b_rec_gate, w_in_gate=w_in_gate, b_in_gate=b_in_gate,
                 lru_lambda=lru_lambda, w_out_a=w_out_a, kv_norm=kv_norm, w_kv=w_kv, norm_b=norm_b, w_in_b=w_in_b,
                 w_out_b=w_out_b, final_norm=final_norm)
    mom = dict(mem_norm=m_mem_norm, w_mem_kv=m_w_mem_kv, norm_a=m_norm_a, w_in_a=m_w_in_a, conv_w=m_conv_w,
               conv_b=m_conv_b, w_rec_gate=m_w_rec_gate, b_rec_gate=m_b_rec_gate, w_in_gate=m_w_in_gate,
               b_in_gate=m_b_in_gate, lru_lambda=m_lru_lambda, w_out_a=m_w_out_a, kv_norm=m_kv_norm, w_kv=m_w_kv,
               norm_b=m_norm_b, w_in_b=m_w_in_b, w_out_b=m_w_out_b, final_norm=m_final_norm)
    var = dict(mem_norm=v_mem_norm, w_mem_kv=v_w_mem_kv, norm_a=v_norm_a, w_in_a=v_w_in_a, conv_w=v_conv_w,
               conv_b=v_conv_b, w_rec_gate=v_w_rec_gate, b_rec_gate=v_b_rec_gate, w_in_gate=v_w_in_gate,
               b_in_gate=v_b_in_gate, lru_lambda=v_lru_lambda, w_out_a=v_w_out_a, kv_norm=v_kv_norm, w_kv=v_w_kv,
               norm_b=v_norm_b, w_in_b=v_w_in_b, w_out_b=v_w_out_b, final_norm=v_final_norm)

    wts, token, late = _gather_weights(local)
    for n in WEIGHTS:
        if SHARD_DIM[n] is None:
            wts[n] = local[n]
    wts["w_rec_gate"] = wts["w_rec_gate"].astype(BF16)
    wts["w_in_gate"] = wts["w_in_gate"].astype(BF16)

    loss, grad_x, grads, big, (early_names, early_p, early_b) = _local_grads(x[0], mem[0], loss_target[0], wts, token, late)

    rest = [n for n in WEIGHTS if n not in BIG]
    row_multiple = 2 * FLAT_TR
    s_rest = jnp.stack([_flatten([_piece(grads[n], n, k) for n in rest], row_multiple) for k in range(N_CHIPS)])
    late_names = [n for n in BIG if n in big] + ["rest"]
    late_p = _pair_sums([big[n] for n in late_names[:-1]] + [s_rest], late_names)
    send_sems, recv_sems, late_p, late_b, token = _chip_scatter_start(late_p, "grad_late_scatter_start")

    g_out, d_out, m_out, v_out = {}, {}, {}, {}

    def update(n, g):
        shape = local[n].shape
        flat = lambda a: a.reshape(-1, shape[-1])
        d, mo, vo = _adamw(g, flat(local[n]), flat(mom[n]), flat(var[n]), f"adamw_{n}")
        g_out[n], d_out[n], m_out[n], v_out[n] = (a.reshape(shape) for a in (g, d, mo, vo))
        return d

    for n, g in zip(early_names, _finish_reduce(early_p, early_b, early_names, after=token)):
        done = update(n, g)
    late_b = _chip_scatter_wait(send_sems, recv_sems, late_p, late_b, done, "grad_late_scatter_wait")
    reduced = dict(zip(late_names, _finish_reduce(late_p, late_b, late_names)))
    for n in late_names[:-1]:
        update(n, reduced[n])
    g_rest = reduced["rest"]
    d_rest, m_rest, v_rest = _adamw(g_rest, *(_flatten([src[n] for n in rest], row_multiple) for src in (local, mom, var)),
                                    "adamw_rest")
    shapes = [local[n].shape for n in rest]
    for out, flat2d in ((g_out, g_rest), (d_out, d_rest), (m_out, m_rest), (v_out, v_rest)):
        out.update(zip(rest, _unflatten(flat2d, shapes)))

    total_loss = lax.psum(loss[0, 0], MESH_AXES)
    return (total_loss, grad_x[None], *[g_out[n] for n in WEIGHTS], *[d_out[n] for n in WEIGHTS],
            *[m_out[n] for n in WEIGHTS], *[v_out[n] for n in WEIGHTS])
```

```python
import functools

import jax
import jax.numpy as jnp
from jax import lax
from jax.experimental import pallas as pl
from jax.experimental.pallas import tpu as pltpu

F32 = jnp.float32
BF16 = jnp.bfloat16

RMS_EPS = 1e-6
LRU_C = 8.0
ADAM_LR = 0.001
ADAM_B1 = 0.9
ADAM_B2 = 0.999
ADAM_EPS = 1e-08
ADAM_WD = 0.01
ADAM_STEP = 10

LANES = 128
VMEM_LIMIT = 56 * 1024 * 1024
FLAT_W = 1024
FLAT_TR = 256
N_CHIPS = 4
MESH_AXES = ("x", "y", "c")

_NT = (((1,), (1,)), ((), ()))
_TN = (((0,), (0,)), ((), ()))
ANY = pl.BlockSpec(memory_space=pl.ANY)
HBM = pl.BlockSpec(memory_space=pltpu.HBM)
SEM = pl.BlockSpec(memory_space=pltpu.SEMAPHORE)
EFFECT = pltpu.SideEffectType.DATAFLOW_SIDE_EFFECTING
MESH = pl.DeviceIdType.MESH


def _cparams(n_axes):
    return pltpu.CompilerParams(dimension_semantics=("arbitrary",) * n_axes, vmem_limit_bytes=VMEM_LIMIT)


def _sigmoid(x):
    return 1.0 / (1.0 + jnp.exp(-x))


def _log1p_pos(e):
    return jnp.where(e < 1e-3, e * (1.0 - e * (0.5 - e * (1.0 / 3.0))), jnp.log(1.0 + e))


def _neg_expm1(x):
    small = -x * (1.0 + x * (0.5 + x * (1.0 / 6.0 + x * (1.0 / 24.0))))
    return jnp.where(x > -0.05, small, 1.0 - jnp.exp(x))


def _tile(n, want):
    if n <= want:
        return n
    t = want
    while n % t:
        t -= LANES
    assert t > 0, (n, want)
    return t


def _norm_matmul(x, g, w, out_dtype, name, after=None):
    m, k = x.shape
    n = w.shape[1]
    tm, tn = _tile(m, 1024), _tile(n, 1024)

    def body(x_ref, g_ref, w_ref, *rest):
        o_ref, u_ref = rest[-2:]

        @pl.when(pl.program_id(1) == 0)
        def _():
            xf = x_ref[...]
            r = lax.rsqrt(jnp.mean(xf * xf, axis=-1, keepdims=True) + RMS_EPS)
            u_ref[...] = ((xf * r) * g_ref[...]).astype(BF16)

        o_ref[...] = jnp.dot(u_ref[...], w_ref[...], preferred_element_type=F32).astype(o_ref.dtype)

    return pl.pallas_call(
        body, name=name, grid=(m // tm, n // tn),
        in_specs=[pl.BlockSpec((tm, k), lambda i, j: (i, 0)), pl.BlockSpec((1, k), lambda i, j: (0, 0)),
                  pl.BlockSpec((k, tn), lambda i, j: (0, j))] + ([] if after is None else [ANY]),
        out_specs=[pl.BlockSpec((tm, tn), lambda i, j: (i, j)), pl.BlockSpec((tm, k), lambda i, j: (i, 0))],
        out_shape=[jax.ShapeDtypeStruct((m, n), out_dtype), jax.ShapeDtypeStruct((m, k), BF16)],
        compiler_params=_cparams(2),
    )(x, g, w, *([] if after is None else [after]))


def _matmul_res(a, b, res, name):
    m, k = a.shape
    n = b.shape[1]
    tm, tn = _tile(m, 1024), _tile(n, 1024)

    def body(a_ref, b_ref, r_ref, o_ref):
        o_ref[...] = r_ref[...] + jnp.dot(a_ref[...], b_ref[...], preferred_element_type=F32)

    return pl.pallas_call(
        body, name=name, grid=(m // tm, n // tn),
        in_specs=[pl.BlockSpec((tm, k), lambda i, j: (i, 0)), pl.BlockSpec((k, tn), lambda i, j: (0, j)),
                  pl.BlockSpec((tm, tn), lambda i, j: (i, j))],
        out_specs=pl.BlockSpec((tm, tn), lambda i, j: (i, j)),
        out_shape=jax.ShapeDtypeStruct((m, n), F32),
        compiler_params=_cparams(2),
    )(a, b, res)


def _matmul_nt(a, b, name, after=None):
    m, n = a.shape
    k = b.shape[0]
    tm, tk = _tile(m, 1024), _tile(k, 512)

    def body(a_ref, b_ref, *rest):
        o_ref = rest[-1]
        o_ref[...] = lax.dot_general(a_ref[...].astype(BF16), b_ref[...], _NT, preferred_element_type=F32)

    in_specs = [pl.BlockSpec((tm, n), lambda i, j: (i, 0)), pl.BlockSpec((tk, n), lambda i, j: (j, 0))]
    args = [a, b]
    if after is not None:
        in_specs.append(ANY)
        args.append(after)
    return pl.pallas_call(
        body, name=name, grid=(m // tm, k // tk), in_specs=in_specs,
        out_specs=pl.BlockSpec((tm, tk), lambda i, j: (i, j)),
        out_shape=jax.ShapeDtypeStruct((m, k), F32),
        compiler_params=_cparams(2),
    )(*args)


def _matmul_tn(a, b, name, tk, tn, out_shape, out_block, out_index, into=None):
    m, k = a.shape
    n = b.shape[1]
    tm = _tile(m, 2048 if b.dtype == BF16 else 1024)

    def body(a_ref, b_ref, *rest):
        o_ref = rest[-1]
        part = lax.dot_general(a_ref[...].astype(BF16), b_ref[...].astype(BF16), _TN, preferred_element_type=F32)

        @pl.when(pl.program_id(2) == 0)
        def _():
            o_ref[...] = part

        @pl.when(pl.program_id(2) != 0)
        def _():
            o_ref[...] += part

    in_specs = [pl.BlockSpec((tm, tk), lambda i, j, s: (s, i)), pl.BlockSpec((tm, tn), lambda i, j, s: (s, j))]
    args = [a, b]
    if into is not None:
        in_specs.append(ANY)
        args.append(into)
    return pl.pallas_call(
        body, name=name, grid=(k // tk, n // tn, m // tm), in_specs=in_specs,
        out_specs=pl.BlockSpec(out_block, lambda i, j, s: out_index(i, j)),
        out_shape=jax.ShapeDtypeStruct(out_shape, F32),
        input_output_aliases={} if into is None else {2: 0},
        compiler_params=_cparams(3),
    )(*args)


def _dw_cols(a, b, layer, n_layers, into, name):
    k, n = a.shape[1], b.shape[1]
    pn = n // N_CHIPS
    tk = _tile(k, 512)
    return _matmul_tn(a, b, name, tk, pn, (N_CHIPS, n_layers, k, pn), (None, None, tk, pn),
                      lambda i, j: (j, layer, i, 0), into)


def _dw_rows(a, b, layer, n_layers, into, name):
    k, n = a.shape[1], b.shape[1]
    pk = k // N_CHIPS
    tn = _tile(n, 2048)
    return _matmul_tn(a, b, name, pk, tn, (N_CHIPS, n_layers, pk, n), (None, None, pk, tn),
                      lambda i, j: (i, layer, 0, j), into)


def _rms_bwd(du, h, g, dres, name):
    m, d = h.shape
    tm = _tile(m, 256)

    def body(du_ref, h_ref, g_ref, dres_ref, dx_ref, dg_ref, dxb_ref):
        xf = h_ref[...]
        r = lax.rsqrt(jnp.mean(xf * xf, axis=-1, keepdims=True) + RMS_EPS)
        xhat = xf * r
        du_v = du_ref[...]
        dxn = du_v * g_ref[...]
        dx = dres_ref[...] + r * (dxn - xhat * jnp.mean(dxn * xhat, axis=-1, keepdims=True))
        dx_ref[...] = dx
        dxb_ref[...] = dx.astype(BF16)
        part = jnp.sum(du_v * xhat, axis=0, keepdims=True)

        @pl.when(pl.program_id(0) == 0)
        def _():
            dg_ref[...] = part

        @pl.when(pl.program_id(0) != 0)
        def _():
            dg_ref[...] += part

    row = lambda i: (i, 0)
    return pl.pallas_call(
        body, name=name, grid=(m // tm,),
        in_specs=[pl.BlockSpec((tm, d), row), pl.BlockSpec((tm, d), row), pl.BlockSpec((1, d), lambda i: (0, 0)),
                  pl.BlockSpec((tm, d), row)],
        out_specs=[pl.BlockSpec((tm, d), row), pl.BlockSpec((1, d), lambda i: (0, 0)), pl.BlockSpec((tm, d), row)],
        out_shape=[jax.ShapeDtypeStruct((m, d), F32), jax.ShapeDtypeStruct((1, d), F32),
                   jax.ShapeDtypeStruct((m, d), BF16)],
        compiler_params=_cparams(1),
    )(du, h, g, dres)


def _final_loss_bwd(h, g, tgt, name):
    m, d = h.shape
    tm = _tile(m, 256)

    def body(h_ref, g_ref, t_ref, dx_ref, dg_ref, loss_ref, dxb_ref):
        xf = h_ref[...]
        r = lax.rsqrt(jnp.mean(xf * xf, axis=-1, keepdims=True) + RMS_EPS)
        xhat = xf * r
        gv = g_ref[...]
        err = xhat * gv - t_ref[...]
        dy = err * (1.0 / d)
        dxn = dy * gv
        dx = r * (dxn - xhat * jnp.mean(dxn * xhat, axis=-1, keepdims=True))
        dx_ref[...] = dx
        dxb_ref[...] = dx.astype(BF16)
        part = jnp.sum(dy * xhat, axis=0, keepdims=True)
        lpart = jnp.sum(jnp.sum(err * err, axis=0, keepdims=True), axis=1, keepdims=True) * (0.5 / d)

        @pl.when(pl.program_id(0) == 0)
        def _():
            dg_ref[...] = part
            loss_ref[...] = lpart

        @pl.when(pl.program_id(0) != 0)
        def _():
            dg_ref[...] += part
            loss_ref[...] += lpart

    row = lambda i: (i, 0)
    fixed = lambda i: (0, 0)
    return pl.pallas_call(
        body, name=name, grid=(m // tm,),
        in_specs=[pl.BlockSpec((tm, d), row), pl.BlockSpec((1, d), fixed), pl.BlockSpec((tm, d), row)],
        out_specs=[pl.BlockSpec((tm, d), row), pl.BlockSpec((1, d), fixed), pl.BlockSpec((1, 1), fixed),
                   pl.BlockSpec((tm, d), row)],
        out_shape=[jax.ShapeDtypeStruct((m, d), F32), jax.ShapeDtypeStruct((1, d), F32),
                   jax.ShapeDtypeStruct((1, 1), F32), jax.ShapeDtypeStruct((m, d), BF16)],
        compiler_params=_cparams(1),
    )(h, g, tgt)


def _add_cast(a, b, name):
    m, n = a.shape
    tm, tn = _tile(m, 512), _tile(n, 2048)

    def body(a_ref, b_ref, o_ref):
        o_ref[...] = (a_ref[...] + b_ref[...]).astype(BF16)

    blk = lambda i, j: (i, j)
    return pl.pallas_call(
        body, name=name, grid=(m // tm, n // tn),
        in_specs=[pl.BlockSpec((tm, tn), blk), pl.BlockSpec((tm, tn), blk)],
        out_specs=pl.BlockSpec((tm, tn), blk),
        out_shape=jax.ShapeDtypeStruct((m, n), BF16),
        compiler_params=_cparams(2),
    )(a, b)


def _mem_attn_fwd(proj, memkv, layer, dr, dm, name):
    t = proj.shape[0]
    nm = memkv.shape[0]
    tm = _tile(t, 512)
    nh = dm // LANES
    scale = LANES ** -0.5
    qb = (2 * dr) // dm

    def body(q_ref, g_ref, k_ref, v_ref, y_ref):
        for hh in range(nh):
            sl = slice(hh * LANES, (hh + 1) * LANES)
            s = lax.dot_general(q_ref[:, sl].astype(BF16), k_ref[:, sl], _NT, preferred_element_type=F32) * scale
            p = jnp.exp(s - jnp.max(s, axis=-1, keepdims=True))
            p = p / jnp.sum(p, axis=-1, keepdims=True)
            o = jnp.dot(p.astype(BF16), v_ref[:, sl], preferred_element_type=F32)
            gv = g_ref[:, sl]
            y_ref[:, sl] = (o * (gv * _sigmoid(gv))).astype(BF16)

    return pl.pallas_call(
        body, name=name, grid=(t // tm,),
        in_specs=[pl.BlockSpec((tm, dm), lambda i: (i, qb)), pl.BlockSpec((tm, dm), lambda i: (i, qb + 1)),
                  pl.BlockSpec((nm, dm), lambda i: (0, 2 * layer)), pl.BlockSpec((nm, dm), lambda i: (0, 2 * layer + 1))],
        out_specs=pl.BlockSpec((tm, dm), lambda i: (i, 0)),
        out_shape=jax.ShapeDtypeStruct((t, dm), BF16),
        compiler_params=_cparams(1),
    )(proj, proj, memkv, memkv)


def _mem_attn_bwd(proj, memkv, dmix, layer, dr, dm, name):
    t = proj.shape[0]
    nm = memkv.shape[0]
    tm = _tile(t, 512)
    nh = dm // LANES
    scale = LANES ** -0.5
    qb = (2 * dr) // dm
    yb = dr // dm

    def body(q_ref, g_ref, k_ref, v_ref, dy_ref, dq_ref, dg_ref, dk_ref, dv_ref):
        @pl.when(pl.program_id(0) == 0)
        def _():
            dk_ref[...] = jnp.zeros_like(dk_ref)
            dv_ref[...] = jnp.zeros_like(dv_ref)

        for hh in range(nh):
            sl = slice(hh * LANES, (hh + 1) * LANES)
            q = q_ref[:, sl].astype(BF16)
            k = k_ref[:, sl]
            v = v_ref[:, sl]
            s = lax.dot_general(q, k, _NT, preferred_element_type=F32) * scale
            p = jnp.exp(s - jnp.max(s, axis=-1, keepdims=True))
            p = p / jnp.sum(p, axis=-1, keepdims=True)
            p_bf = p.astype(BF16)
            o = jnp.dot(p_bf, v, preferred_element_type=F32)
            gv = g_ref[:, sl]
            sg = _sigmoid(gv)
            dy = dy_ref[:, sl]
            do = dy * (gv * sg)
            dg_ref[:, sl] = (dy * o * (sg * (1.0 + gv * (1.0 - sg)))).astype(BF16)
            do_bf = do.astype(BF16)
            dv_ref[:, sl] += lax.dot_general(p_bf, do_bf, _TN, preferred_element_type=F32)
            dp = lax.dot_general(do_bf, v, _NT, preferred_element_type=F32)
            ds = (p * (dp - jnp.sum(dp * p, axis=-1, keepdims=True)) * scale).astype(BF16)
            dq_ref[:, sl] = jnp.dot(ds, k, preferred_element_type=F32).astype(BF16)
            dk_ref[:, sl] += lax.dot_general(ds, q, _TN, preferred_element_type=F32)

    fixed = lambda i: (0, 0)
    return pl.pallas_call(
        body, name=name, grid=(t // tm,),
        in_specs=[pl.BlockSpec((tm, dm), lambda i: (i, qb)), pl.BlockSpec((tm, dm), lambda i: (i, qb + 1)),
                  pl.BlockSpec((nm, dm), lambda i: (0, 2 * layer)), pl.BlockSpec((nm, dm), lambda i: (0, 2 * layer + 1)),
                  pl.BlockSpec((tm, dm), lambda i: (i, yb))],
        out_specs=[pl.BlockSpec((tm, dm), lambda i: (i, 0)), pl.BlockSpec((tm, dm), lambda i: (i, 0)),
                   pl.BlockSpec((nm, dm), fixed), pl.BlockSpec((nm, dm), fixed)],
        out_shape=[jax.ShapeDtypeStruct((t, dm), BF16), jax.ShapeDtypeStruct((t, dm), BF16),
                   jax.ShapeDtypeStruct((nm, dm), F32), jax.ShapeDtypeStruct((nm, dm), F32)],
        compiler_params=_cparams(1),
    )(proj, proj, memkv, memkv, dmix)


LRU_CHUNK = 256


def _lru_gates(xc, vec, wr_ref, wi_ref):
    r = _sigmoid(jnp.dot(xc.astype(BF16), wr_ref[...], preferred_element_type=F32) + vec[1:2])
    i = _sigmoid(jnp.dot(xc.astype(BF16), wi_ref[...], preferred_element_type=F32) + vec[2:3])
    lam = vec[3:4]
    cl = -LRU_C * (jnp.maximum(-lam, 0.0) + _log1p_pos(jnp.exp(-jnp.abs(lam))))
    la = cl * r
    a = jnp.exp(la)
    s2 = _neg_expm1(2.0 * la)
    return r, i, cl, a, s2


def _lru_fwd(proj, vec, wr, wi, name):
    t = proj.shape[0]
    nb = wr.shape[0]
    dr = nb * LANES
    c = _tile(t, LRU_CHUNK)

    def body(x_ref, g_ref, vec_ref, wr_ref, wi_ref, y_ref, h_ref, xc_ref, carry_ref, xprev_ref):
        @pl.when(pl.program_id(1) == 0)
        def _():
            carry_ref[...] = jnp.zeros_like(carry_ref)
            xprev_ref[...] = jnp.zeros_like(xprev_ref)

        x = x_ref[...]
        vec = vec_ref[...]
        rows = lax.broadcasted_iota(jnp.int32, (c, LANES), 0)
        xprev = xprev_ref[...]
        xc = vec[7:8] * x + vec[0:1]
        for k in range(1, 4):
            xs = jnp.where(rows < k, pltpu.roll(xprev, k, 0), pltpu.roll(x, k, 0))
            xc = xc + vec[7 - k:8 - k] * xs
        xprev_ref[...] = x
        xc_ref[...] = xc

        r, i, cl, a, s2 = _lru_gates(xc, vec, wr_ref, wi_ref)
        hh = jnp.sqrt(s2) * (i * xc)
        aa = a
        d = 1
        while d < c:
            keep = rows >= d
            hh = jnp.where(keep, aa * pltpu.roll(hh, d, 0) + hh, hh)
            aa = jnp.where(keep, aa * pltpu.roll(aa, d, 0), aa)
            d *= 2
        hfull = hh + aa * carry_ref[7:8, :]
        carry_ref[...] = hfull[c - 8:c, :]
        h_ref[...] = hfull
        gv = g_ref[...]
        y_ref[...] = (hfull * (gv * _sigmoid(gv))).astype(BF16)

    blk = lambda n, s: (s, n)
    return pl.pallas_call(
        body, name=name, grid=(nb, t // c),
        in_specs=[pl.BlockSpec((c, LANES), blk), pl.BlockSpec((c, LANES), lambda n, s: (s, nb + n)),
                  pl.BlockSpec((8, LANES), lambda n, s: (0, n)),
                  pl.BlockSpec((None, LANES, LANES), lambda n, s: (n, 0, 0)),
                  pl.BlockSpec((None, LANES, LANES), lambda n, s: (n, 0, 0))],
        out_specs=[pl.BlockSpec((c, LANES), blk)] * 3,
        out_shape=[jax.ShapeDtypeStruct((t, dr), BF16), jax.ShapeDtypeStruct((t, dr), F32),
                   jax.ShapeDtypeStruct((t, dr), F32)],
        scratch_shapes=[pltpu.VMEM((8, LANES), F32), pltpu.VMEM((c, LANES), F32)],
        compiler_params=_cparams(2),
    )(proj, proj, vec, wr, wi)


def _lru_bwd(proj, xc_all, h_all, dmix, vec, wr, wi, name):
    t = proj.shape[0]
    nb = wr.shape[0]
    dr = nb * LANES
    c = _tile(t, LRU_CHUNK)
    nc = t // c

    def body(x_ref, g_ref, xc_ref, h_ref, dy_ref, vec_ref, wr_ref, wi_ref,
             dx_ref, dg_ref, dwr_ref, dwi_ref, dvec_ref, qcarry_ref, dxc_next_ref):
        @pl.when(pl.program_id(1) == 0)
        def _():
            qcarry_ref[...] = jnp.zeros_like(qcarry_ref)
            dxc_next_ref[...] = jnp.zeros_like(dxc_next_ref)
            dwr_ref[...] = jnp.zeros_like(dwr_ref)
            dwi_ref[...] = jnp.zeros_like(dwi_ref)
            dvec_ref[...] = jnp.zeros_like(dvec_ref)

        x = x_ref[...]
        xc = xc_ref[...]
        h = h_ref[...]
        dy = dy_ref[...]
        gv = g_ref[...]
        vec = vec_ref[...]
        rows = lax.broadcasted_iota(jnp.int32, (c, LANES), 0)

        r, i, cl, a, s2 = _lru_gates(xc, vec, wr_ref, wi_ref)
        s = jnp.sqrt(s2)
        ixc = i * xc
        u = s * ixc
        sg = _sigmoid(gv)
        dh = dy * (gv * sg)
        dg_ref[...] = (dy * h * (sg * (1.0 + gv * (1.0 - sg)))).astype(BF16)

        aa = a
        qq = a * dh
        d = 1
        while d < c:
            keep = rows < c - d
            qq = jnp.where(keep, qq + aa * pltpu.roll(qq, c - d, 0), qq)
            aa = jnp.where(keep, aa * pltpu.roll(aa, c - d, 0), aa)
            d *= 2
        qin = qcarry_ref[0:1, :]
        qfull = qq + aa * qin
        gt = dh + jnp.where(rows == c - 1, qin, pltpu.roll(qfull, c - 1, 0))
        qcarry_ref[...] = qfull[0:8, :]

        dla = gt * (h - u) - gt * ixc * (a * a) / s
        dixc = gt * s
        di = dixc * xc
        dxc = dixc * i
        dzr = (dla * cl) * (r * (1.0 - r))
        dzi = di * (i * (1.0 - i))
        dzr_bf = dzr.astype(BF16)
        dzi_bf = dzi.astype(BF16)
        dxc = dxc + lax.dot_general(dzr_bf, wr_ref[...], _NT, preferred_element_type=F32)
        dxc = dxc + lax.dot_general(dzi_bf, wi_ref[...], _NT, preferred_element_type=F32)
        xc_bf = xc.astype(BF16)
        dwr_ref[...] += lax.dot_general(xc_bf, dzr_bf, _TN, preferred_element_type=F32)
        dwi_ref[...] += lax.dot_general(xc_bf, dzi_bf, _TN, preferred_element_type=F32)

        lam = vec[3:4]
        dlam = jnp.sum(dla * r, axis=0, keepdims=True) * (LRU_C * _sigmoid(-lam))
        colsum = lambda v: jnp.sum(v, axis=0, keepdims=True)
        dxn = dxc_next_ref[...]
        dx = vec[7:8] * dxc
        dtaps = [None] * 4
        dtaps[3] = colsum(x * dxc)
        for k in range(1, 4):
            sh = jnp.where(rows < c - k, pltpu.roll(dxc, c - k, 0), pltpu.roll(dxn, c - k, 0))
            dx = dx + vec[7 - k:8 - k] * sh
            dtaps[3 - k] = colsum(x * sh)
        dxc_next_ref[...] = dxc
        dx_ref[...] = dx.astype(BF16)
        dvec_ref[...] += jnp.concatenate([colsum(dxc), colsum(dzr), colsum(dzi), dlam] + dtaps, axis=0)

    rev = lambda n, s: (nc - 1 - s, n)
    sq = lambda n, s: (n, 0, 0)
    return pl.pallas_call(
        body, name=name, grid=(nb, nc),
        in_specs=[pl.BlockSpec((c, LANES), rev), pl.BlockSpec((c, LANES), lambda n, s: (nc - 1 - s, nb + n)),
                  pl.BlockSpec((c, LANES), rev), pl.BlockSpec((c, LANES), rev), pl.BlockSpec((c, LANES), rev),
                  pl.BlockSpec((8, LANES), lambda n, s: (0, n)),
                  pl.BlockSpec((None, LANES, LANES), sq), pl.BlockSpec((None, LANES, LANES), sq)],
        out_specs=[pl.BlockSpec((c, LANES), rev), pl.BlockSpec((c, LANES), rev),
                   pl.BlockSpec((None, LANES, LANES), sq), pl.BlockSpec((None, LANES, LANES), sq),
                   pl.BlockSpec((None, 8, LANES), sq)],
        out_shape=[jax.ShapeDtypeStruct((t, dr), BF16), jax.ShapeDtypeStruct((t, dr), BF16),
                   jax.ShapeDtypeStruct((nb, LANES, LANES), F32), jax.ShapeDtypeStruct((nb, LANES, LANES), F32),
                   jax.ShapeDtypeStruct((nb, 8, LANES), F32)],
        scratch_shapes=[pltpu.VMEM((8, LANES), F32), pltpu.VMEM((c, LANES), F32)],
        compiler_params=_cparams(2),
    )(proj, proj, xc_all, h_all, dmix, vec, wr, wi)


SB_TQ = 1024
SB_TK = 256


def _sb_softplus(z, diag):
    sp = jnp.maximum(z, 0.0) + jnp.log(1.0 + jnp.exp(-jnp.abs(z)))
    mask = None
    if diag:
        mask = lax.broadcasted_iota(jnp.int32, z.shape, 1) < lax.broadcasted_iota(jnp.int32, z.shape, 0)
        sp = jnp.where(mask, sp, 0.0)
    return sp, mask


def _split_dot(v, m):
    hi = v.astype(BF16)
    lo = (v - hi.astype(F32)).astype(BF16)
    return jnp.dot(hi, m, preferred_element_type=F32) + jnp.dot(lo, m, preferred_element_type=F32)


def _tri_ones(kind, tk):
    jj = lax.broadcasted_iota(jnp.int32, (tk, tk), 0)
    ss = lax.broadcasted_iota(jnp.int32, (tk, tk), 1)
    rel = {"ge": jj >= ss, "le": jj <= ss}[kind]
    return jnp.where(rel, 1.0, 0.0).astype(BF16)


def _sb_fwd(proj, kv, name):
    t = proj.shape[0]
    ds = kv.shape[1] // 2
    nh = ds // LANES
    tq = _tile(t, SB_TQ)
    tk = _tile(tq, SB_TK)
    nd = tq // tk
    assert nd % 2 == 0 or t == tq
    scale = LANES ** -0.5

    def body(q_ref, g_ref, k_ref, v_ref, y_ref, o_ref, tl_ref, qbf_ref, acc_ref, run_ref, z_ref, w_ref):
        qi = pl.program_id(1)
        qbf_ref[...] = q_ref[...].astype(BF16)
        tri = _tri_ones("ge", tk)
        acc_ref[...] = jnp.zeros_like(acc_ref)
        run_ref[...] = jnp.zeros_like(run_ref)
        all_rows = [slice(s0, s0 + tk) for s0 in range(0, tq, tk)]

        def weights(zs, groups):
            sps = [_sb_softplus(z, dg) for z, (_, dg) in zip(zs, groups)]
            cums = [_split_dot(sp, tri) for sp, _ in sps]
            ws = []
            for z, (rows, dg), (_, mask), cum in zip(zs, groups, sps, cums):
                run = run_ref[rows, :]
                w = jnp.exp(z - cum - run)
                if dg:
                    w = jnp.where(mask, w, 0.0)
                run_ref[rows, :] = run + cum[:, 0:1]
                ws.append(w.astype(BF16))
            return ws

        n = qi * nd

        def rows_from(u):
            return [slice(s0, s0 + tk) for s0 in range(u * tk, tq, tk)]

        def logits_into(slot, kb, rows_list):
            k = k_ref[pl.ds(pl.multiple_of(kb * tk, tk), tk), :]
            seen = slice(rows_list[0].start, tq)
            z_ref[slot, seen, :] = lax.dot_general(qbf_ref[seen, :], k, _NT, preferred_element_type=F32) * scale

        def add_values(kb, rows_list):
            v = v_ref[pl.ds(pl.multiple_of(kb * tk, tk), tk), :]
            seen = slice(rows_list[0].start, tq)
            acc_ref[seen, :] += jnp.dot(w_ref[seen, :], v, preferred_element_type=F32)

        def weigh(slot, groups):
            for (rows, _), w in zip(groups, weights([z_ref[slot, rows, :] for rows, _ in groups], groups)):
                w_ref[rows, :] = w

        logits_into(0, n + nd - 1, rows_from(nd - 1))
        for i, u in enumerate(reversed(range(nd))):
            slot = i % 2
            if i > 0:
                add_values(n + u + 1, rows_from(u + 1))
            if u > 0:
                logits_into(1 - slot, n + u - 1, rows_from(u - 1))
            else:
                logits_into(1 - slot, jnp.maximum(n - 1, 0), all_rows)
            weigh(slot, [(rows, j == 0) for j, rows in enumerate(rows_from(u))])

        def half_step(j, slot):
            kb = n - 1 - j
            add_values(kb + 1, all_rows)
            logits_into(1 - slot, jnp.maximum(kb - 1, 0), all_rows)
            weigh(slot, [(rows, False) for rows in all_rows])

        def step(i, carry):
            half_step(2 * i, nd % 2)
            half_step(2 * i + 1, 1 - nd % 2)
            return carry

        lax.fori_loop(0, n // 2, step, 0)
        add_values(0, all_rows)
        o = acc_ref[...]
        o_ref[...] = o
        tl_ref[...] = jnp.broadcast_to(run_ref[...], (tq, LANES))
        gv = g_ref[...]
        y_ref[...] = (o * (gv * _sigmoid(gv))).astype(BF16)

    blk = lambda h, i: (i, h)
    return pl.pallas_call(
        body, name=name, grid=(nh, t // tq),
        in_specs=[pl.BlockSpec((tq, LANES), blk), pl.BlockSpec((tq, LANES), lambda h, i: (i, nh + h)),
                  pl.BlockSpec((t, LANES), lambda h, i: (0, h)), pl.BlockSpec((t, LANES), lambda h, i: (0, nh + h))],
        out_specs=[pl.BlockSpec((tq, LANES), blk)] * 3,
        out_shape=[jax.ShapeDtypeStruct((t, ds), BF16), jax.ShapeDtypeStruct((t, ds), F32),
                   jax.ShapeDtypeStruct((t, ds), F32)],
        scratch_shapes=[pltpu.VMEM((tq, LANES), BF16), pltpu.VMEM((tq, LANES), F32), pltpu.VMEM((tq, 1), F32),
                        pltpu.VMEM((2, tq, tk), F32), pltpu.VMEM((tq, tk), BF16)],
        compiler_params=_cparams(2),
    )(proj, proj, kv, kv)


def _sb_bwd(proj, kv, o_all, tl_all, dmix, name):
    t = proj.shape[0]
    ds = kv.shape[1] // 2
    nh = ds // LANES
    tq = _tile(t, SB_TQ)
    tk = _tile(tq, SB_TK)
    nd = tq // tk
    scale = LANES ** -0.5

    def body(q_ref, g_ref, k_ref, v_ref, o_ref, tl_ref, dy_ref, dq_ref, dg_ref, dk_ref, dv_ref,
             qbf_ref, dobf_ref, qt_ref, dot_ref, acc_ref, left_ref, rune_ref, z_ref, dw_ref, wp_ref, dzp_ref):
        qi = pl.program_id(1)

        @pl.when(qi == 0)
        def _():
            dk_ref[...] = jnp.zeros_like(dk_ref)
            dv_ref[...] = jnp.zeros_like(dv_ref)

        qbf_ref[...] = q_ref[...].astype(BF16)
        qt_ref[...] = q_ref[...].T.astype(BF16)
        gv = g_ref[...]
        sg = _sigmoid(gv)
        dy = dy_ref[...]
        do = dy * (gv * sg)
        dobf_ref[...] = do.astype(BF16)
        dot_ref[...] = do.T.astype(BF16)
        dg_ref[...] = (dy * o_ref[...] * (sg * (1.0 + gv * (1.0 - sg)))).astype(BF16)
        tri = _tri_ones("le", tk)
        acc_ref[...] = jnp.zeros_like(acc_ref)
        left_ref[...] = tl_ref[:, 0:1]
        rune_ref[...] = jnp.zeros_like(rune_ref)
        wp_ref[...] = jnp.zeros_like(wp_ref)
        dzp_ref[...] = jnp.zeros_like(dzp_ref)
        all_rows = [slice(s0, s0 + tk) for s0 in range(0, tq, tk)]

        def grads(zs, dws, groups):
            sps = [_sb_softplus(z, dg) for z, (_, dg) in zip(zs, groups)]
            cums = [_split_dot(sp, tri) for sp, _ in sps]
            ws, es, lbs = [], [], []
            for z, (rows, dg), (sp, mask), cum, dw in zip(zs, groups, sps, cums, dws):
                left = left_ref[rows, :]
                lb = z - sp
                w = jnp.exp(lb - (left - cum))
                if dg:
                    w = jnp.where(mask, w, 0.0)
                left_ref[rows, :] = left - cum[:, tk - 1:tk]
                ws.append(w.astype(BF16))
                es.append(dw * w)
                lbs.append(lb)
            cumes = [_split_dot(e, tri) for e in es]
            dzs = []
            for (rows, dg), (_, mask), lb, e, cume in zip(groups, sps, lbs, es, cumes):
                rune = rune_ref[rows, :]
                dz = (e - jnp.exp(lb) * (rune + cume)) * scale
                if dg:
                    dz = jnp.where(mask, dz, 0.0)
                rune_ref[rows, :] = rune + cume[:, tk - 1:tk]
                dzs.append(dz.astype(BF16))
            return dzs, ws

        n = qi * nd

        def rows_from(u):
            return [slice(s0, s0 + tk) for s0 in range(u * tk, tq, tk)]

        def logits_into(slot, kb, rows_list):
            k0 = pl.multiple_of(kb * tk, tk)
            k = k_ref[pl.ds(k0, tk), :]
            v = v_ref[pl.ds(k0, tk), :]
            seen = slice(rows_list[0].start, tq)
            z_ref[slot, seen, :] = lax.dot_general(qbf_ref[seen, :], k, _NT, preferred_element_type=F32) * scale
            dw_ref[slot, seen, :] = lax.dot_general(dobf_ref[seen, :], v, _NT, preferred_element_type=F32)

        def apply_stored(kb, u):
            k0 = pl.multiple_of(kb * tk, tk)
            k = k_ref[pl.ds(k0, tk), :]
            seen = slice(u * tk, tq)
            acc_ref[seen, :] += jnp.dot(dzp_ref[seen, :], k, preferred_element_type=F32)
            dk_ref[:, pl.ds(k0, tk)] += jnp.dot(qt_ref[:, seen], dzp_ref[seen, :], preferred_element_type=F32)
            dv_ref[:, pl.ds(k0, tk)] += jnp.dot(dot_ref[:, seen], wp_ref[seen, :], preferred_element_type=F32)

        def differentiate(slot, groups):
            dzs, ws = grads([z_ref[slot, rows, :] for rows, _ in groups], [dw_ref[slot, rows, :] for rows, _ in groups],
                            groups)
            for (rows, _), dz, w in zip(groups, dzs, ws):
                dzp_ref[rows, :] = dz
                wp_ref[rows, :] = w

        logits_into(0, 0, all_rows)

        def half_step(j, slot):
            apply_stored(jnp.maximum(j - 1, 0), 0)
            logits_into(1 - slot, j + 1, all_rows)
            differentiate(slot, [(rows, False) for rows in all_rows])

        def step(i, carry):
            half_step(2 * i, 0)
            half_step(2 * i + 1, 1)
            return carry

        lax.fori_loop(0, n // 2, step, 0)
        for u in range(nd):
            slot = u % 2
            if u == 0:
                apply_stored(jnp.maximum(n - 1, 0), 0)
            else:
                apply_stored(n + u - 1, u - 1)
            if u + 1 < nd:
                logits_into(1 - slot, n + u + 1, rows_from(u + 1))
            differentiate(slot, [(rows, j == 0) for j, rows in enumerate(rows_from(u))])
        apply_stored(n + nd - 1, nd - 1)
        dq_ref[...] = acc_ref[...].astype(BF16)

    blk = lambda h, i: (i, h)
    whole = lambda h, i: (0, h)
    return pl.pallas_call(
        body, name=name, grid=(nh, t // tq),
        in_specs=[pl.BlockSpec((tq, LANES), blk), pl.BlockSpec((tq, LANES), lambda h, i: (i, nh + h)),
                  pl.BlockSpec((t, LANES), whole), pl.BlockSpec((t, LANES), lambda h, i: (0, nh + h)),
                  pl.BlockSpec((tq, LANES), blk), pl.BlockSpec((tq, LANES), blk), pl.BlockSpec((tq, LANES), blk)],
        out_specs=[pl.BlockSpec((tq, LANES), blk), pl.BlockSpec((tq, LANES), blk),
                   pl.BlockSpec((LANES, t), lambda h, i: (h, 0)), pl.BlockSpec((LANES, t), lambda h, i: (h, 0))],
        out_shape=[jax.ShapeDtypeStruct((t, ds), BF16), jax.ShapeDtypeStruct((t, ds), BF16),
                   jax.ShapeDtypeStruct((ds, t), F32), jax.ShapeDtypeStruct((ds, t), F32)],
        scratch_shapes=[pltpu.VMEM((tq, LANES), BF16), pltpu.VMEM((tq, LANES), BF16),
                        pltpu.VMEM((LANES, tq), BF16), pltpu.VMEM((LANES, tq), BF16), pltpu.VMEM((tq, LANES), F32),
                        pltpu.VMEM((tq, 1), F32), pltpu.VMEM((tq, 1), F32),
                        pltpu.VMEM((2, tq, tk), F32), pltpu.VMEM((2, tq, tk), F32),
                        pltpu.VMEM((tq, tk), BF16), pltpu.VMEM((tq, tk), BF16)],
        compiler_params=_cparams(2),
    )(proj, proj, kv, kv, o_all, tl_all, dmix)


def _place():
    x, y, c = lax.axis_index("x"), lax.axis_index("y"), lax.axis_index("c")
    chips = [(1 - x, y), (x, 1 - y), (1 - x, 1 - y)]
    return x, y, c, chips


def _remote(src, dst, send_sems, recv_sems, k, to):
    return pltpu.make_async_remote_copy(src_ref=src, dst_ref=dst, send_sem=send_sems.at[k], recv_sem=recv_sems.at[k],
                                        device_id=to, device_id_type=MESH)


def _my_chip():
    return 2 * lax.axis_index("x") + lax.axis_index("y")


def _place_own(shard, name):
    r, w = shard.shape
    tr = _tile(r, FLAT_TR)

    def body(x_ref, o_ref):
        o_ref[...] = x_ref[...]

    return pl.pallas_call(
        body, name=name, out_shape=jax.ShapeDtypeStruct((N_CHIPS, r, w), shard.dtype), grid=(r // tr,),
        in_specs=[pl.BlockSpec((tr, w), lambda i: (i, 0))],
        out_specs=pl.BlockSpec((None, tr, w), lambda i: (_my_chip(), i, 0)),
        compiler_params=_cparams(1),
    )(shard)


def _chip_all_gather(shards, name):
    n = len(shards)

    def body(*refs):
        x_refs, out_refs, send_sems, recv_sems = refs[:n], refs[2 * n:3 * n], refs[3 * n], refs[3 * n + 1]
        x, y, c, chips = _place()
        me = 2 * x + y
        sibling = (x, y, 1 - c)

        def rows(t, core):
            rh = x_refs[t].shape[0] // 2
            return pl.ds(core * rh, rh)

        first = [_remote(x_refs[t].at[rows(t, c)], out_refs[t].at[me, rows(t, c)], send_sems, recv_sems, 6 * t + k,
                         (cx, cy, c)) for t in range(n) for k, (cx, cy) in enumerate(chips)]
        for cp in first:
            cp.start()
        passed = []
        for k, (cx, cy) in enumerate(chips):
            for t in range(n):
                got = out_refs[t].at[2 * cx + cy, rows(t, c)]
                _remote(got, got, send_sems, recv_sems, 6 * t + k, (cx, cy, c)).wait_recv()
                fwd = _remote(got, got, send_sems, recv_sems, 6 * t + 3 + k, sibling)
                fwd.start()
                passed.append(fwd)
        for k, (cx, cy) in enumerate(chips):
            for t in range(n):
                got = out_refs[t].at[2 * cx + cy, rows(t, 1 - c)]
                _remote(got, got, send_sems, recv_sems, 6 * t + 3 + k, sibling).wait_recv()
        for cp in first + passed:
            cp.wait_send()

    bufs = [_place_own(s, f"{name}_own{t}") for t, s in enumerate(shards)]
    return pl.pallas_call(
        body, name=name, in_specs=[ANY] * (2 * n), out_specs=[ANY] * n,
        out_shape=[jax.ShapeDtypeStruct((N_CHIPS,) + s.shape, s.dtype) for s in shards],
        input_output_aliases={n + t: t for t in range(n)},
        scratch_shapes=[pltpu.SemaphoreType.DMA((6 * n,)), pltpu.SemaphoreType.DMA((6 * n,))],
    )(*shards, *bufs)


def _chip_gather_start(shards, name):
    n = len(shards)

    def body(*refs):
        x_refs, buf_refs, send_sems, recv_sems, token = refs[:n], refs[n:2 * n], refs[2 * n], refs[2 * n + 1], refs[-1]
        x, y, c, chips = _place()
        me = 2 * x + y
        for t in range(n):
            for k, (cx, cy) in enumerate(chips):
                _remote(x_refs[t], buf_refs[t].at[me], send_sems, recv_sems, 3 * t + k, (cx, cy, c)).start()
        token[...] = jnp.zeros_like(token)

    bufs = [_place_own(s, f"{name}_own{t}") for t, s in enumerate(shards)]
    hbm = [pltpu.HBM(a.shape, a.dtype) for a in list(shards) + bufs]
    outs = pl.pallas_call(
        body, name=name, in_specs=[HBM] * (2 * n),
        out_shape=(pltpu.SemaphoreType.DMA((3 * n,)), pltpu.SemaphoreType.DMA((3 * n,)), *hbm,
                   jax.ShapeDtypeStruct((8, LANES), F32)),
        out_specs=(SEM, SEM, *[HBM] * (2 * n), pl.BlockSpec(memory_space=pltpu.VMEM)),
        input_output_aliases={t: 2 + t for t in range(2 * n)},
        compiler_params=pltpu.CompilerParams(has_side_effects=EFFECT),
    )(*[pltpu.with_memory_space_constraint(a, pltpu.HBM) for a in list(shards) + bufs])
    return outs[0], outs[1], outs[2:2 + n], outs[2 + n:2 + 2 * n], outs[-1]


def _chip_gather_wait(send_sems, recv_sems, shards, bufs, after, name):
    n = len(shards)

    def body(*refs):
        x_refs, buf_refs, send, recv = refs[:n], refs[n:2 * n], refs[2 * n], refs[2 * n + 1]
        x, y, c, chips = _place()
        for t in range(n):
            for k, (cx, cy) in enumerate(chips):
                cp = _remote(x_refs[t], buf_refs[t].at[2 * cx + cy], send, recv, 3 * t + k, (cx, cy, c))
                cp.wait_send()
                cp.wait_recv()

    hbm = [pltpu.HBM(a.shape, a.dtype) for a in list(shards) + list(bufs)]
    outs = pl.pallas_call(
        body, name=name, in_specs=[HBM] * (2 * n) + [SEM, SEM, ANY],
        out_shape=tuple(hbm), out_specs=tuple([HBM] * (2 * n)),
        input_output_aliases={t: t for t in range(2 * n)},
        compiler_params=pltpu.CompilerParams(has_side_effects=EFFECT),
    )(*shards, *bufs, send_sems, recv_sems, after)
    return outs[n:]


def _sibling_take_half(ss, name):
    n = len(ss)

    def body(*refs):
        s_refs, a_refs, send_sems, recv_sems = refs[:n], refs[n:2 * n], refs[2 * n], refs[2 * n + 1]
        x, y, c, _ = _place()
        cps = []
        for t in range(n):
            rh = s_refs[t].shape[1] // 2
            cps.append(_remote(s_refs[t].at[:, pl.ds((1 - c) * rh, rh), :], a_refs[t], send_sems, recv_sems, t,
                               (x, y, 1 - c)))
        for cp in cps:
            cp.start()
        for cp in cps:
            cp.wait()

    return pl.pallas_call(
        body, name=name, in_specs=[ANY] * n, out_specs=[ANY] * n,
        out_shape=[jax.ShapeDtypeStruct((s.shape[0], s.shape[1] // 2, s.shape[2]), s.dtype) for s in ss],
        scratch_shapes=[pltpu.SemaphoreType.DMA((n,)), pltpu.SemaphoreType.DMA((n,))],
    )(*ss)


def _pair_sum(s, a, dtype, name):
    n, r, w = s.shape
    rh = r // 2
    tr = _tile(rh, FLAT_TR)
    nblk = rh // tr

    def body(s_ref, a_ref, o_ref):
        o_ref[...] = (s_ref[...] + a_ref[...]).astype(dtype)

    return pl.pallas_call(
        body, name=name, out_shape=jax.ShapeDtypeStruct((n, rh, w), dtype), grid=(n, nblk),
        in_specs=[pl.BlockSpec((None, tr, w), lambda k, i: (k, lax.axis_index("c") * nblk + i, 0)),
                  pl.BlockSpec((None, tr, w), lambda k, i: (k, i, 0))],
        out_specs=pl.BlockSpec((None, tr, w), lambda k, i: (k, i, 0)),
        compiler_params=_cparams(2),
    )(s, a)


def _chip_scatter_start(ps, name):
    n = len(ps)

    def body(*refs):
        p_refs, b_refs, send_sems, recv_sems, token = refs[:n], refs[n:2 * n], refs[2 * n], refs[2 * n + 1], refs[-1]
        x, y, c, chips = _place()
        me = 2 * x + y
        for t in range(n):
            for k, (cx, cy) in enumerate(chips):
                _remote(p_refs[t].at[2 * cx + cy], b_refs[t].at[me], send_sems, recv_sems, 3 * t + k, (cx, cy, c)).start()
        token[...] = jnp.zeros_like(token)

    hbm = [pltpu.HBM(p.shape, p.dtype) for p in ps]
    outs = pl.pallas_call(
        body, name=name, in_specs=[HBM] * (2 * n),
        out_shape=(pltpu.SemaphoreType.DMA((3 * n,)), pltpu.SemaphoreType.DMA((3 * n,)), *hbm, *hbm,
                   jax.ShapeDtypeStruct((8, LANES), F32)),
        out_specs=(SEM, SEM, *[HBM] * (2 * n), pl.BlockSpec(memory_space=pltpu.VMEM)),
        input_output_aliases={t: 2 + t for t in range(2 * n)},
        compiler_params=pltpu.CompilerParams(has_side_effects=EFFECT),
    )(*[pltpu.with_memory_space_constraint(p, pltpu.HBM) for p in ps],
      *[pltpu.with_memory_space_constraint(lax.empty(p.shape, p.dtype), pltpu.HBM) for p in ps])
    return outs[0], outs[1], outs[2:2 + n], outs[2 + n:2 + 2 * n], outs[-1]


def _chip_scatter_wait(send_sems, recv_sems, ps, bs, after, name):
    n = len(ps)

    def body(*refs):
        p_refs, b_refs, send, recv = refs[:n], refs[n:2 * n], refs[2 * n], refs[2 * n + 1]
        x, y, c, chips = _place()
        for t in range(n):
            for k, (cx, cy) in enumerate(chips):
                cp = _remote(p_refs[t].at[2 * cx + cy], b_refs[t].at[2 * cx + cy], send, recv, 3 * t + k, (cx, cy, c))
                cp.wait_send()
                cp.wait_recv()

    hbm = [pltpu.HBM(p.shape, p.dtype) for p in ps]
    outs = pl.pallas_call(
        body, name=name, in_specs=[HBM] * (2 * n) + [SEM, SEM, ANY],
        out_shape=(*hbm, *hbm), out_specs=tuple([HBM] * (2 * n)),
        input_output_aliases={t: t for t in range(2 * n)},
        compiler_params=pltpu.CompilerParams(has_side_effects=EFFECT),
    )(*ps, *bs, send_sems, recv_sems, after)
    return outs[n:]


def _chip_sum(p, b, name, after=None):
    n, rh, w = p.shape
    tr = _tile(rh, FLAT_TR)
    nblk = rh // tr

    def body(p_ref, b0_ref, b1_ref, b2_ref, b3_ref, *rest):
        o_ref = rest[-1]
        me = _my_chip()
        own = p_ref[...]
        t = [jnp.where(me == k, own, b_ref[...]).astype(F32) for k, b_ref in enumerate((b0_ref, b1_ref, b2_ref, b3_ref))]
        o_ref[...] = ((t[0] + t[1]) + t[2]) + t[3]

    def other(k):
        return lambda i: (jnp.where(_my_chip() == k, (k + 1) % N_CHIPS, k), i, 0)

    return pl.pallas_call(
        body, name=name, out_shape=jax.ShapeDtypeStruct((2 * rh, w), F32), grid=(nblk,),
        in_specs=[pl.BlockSpec((None, tr, w), lambda i: (_my_chip(), i, 0))]
        + [pl.BlockSpec((None, tr, w), other(k)) for k in range(N_CHIPS)] + ([] if after is None else [ANY]),
        out_specs=pl.BlockSpec((tr, w), lambda i: (lax.axis_index("c") * nblk + i, 0)),
        compiler_params=_cparams(1),
    )(p, b, b, b, b, *([] if after is None else [after]))


def _sibling_join(gs, name):
    n = len(gs)

    def body(*refs):
        g_refs, send_sems, recv_sems = refs[n:2 * n], refs[2 * n], refs[2 * n + 1]
        x, y, c, _ = _place()
        cps = []
        for t in range(n):
            rh = g_refs[t].shape[0] // 2
            mine = g_refs[t].at[pl.ds(c * rh, rh)]
            cps.append(_remote(mine, mine, send_sems, recv_sems, t, (x, y, 1 - c)))
        for cp in cps:
            cp.start()
        for t in range(n):
            rh = g_refs[t].shape[0] // 2
            theirs = g_refs[t].at[pl.ds((1 - c) * rh, rh)]
            _remote(theirs, theirs, send_sems, recv_sems, t, (x, y, 1 - c)).wait_recv()
        for cp in cps:
            cp.wait_send()

    return pl.pallas_call(
        body, name=name, in_specs=[ANY] * n, out_specs=[ANY] * n,
        out_shape=[jax.ShapeDtypeStruct(g.shape, g.dtype) for g in gs],
        input_output_aliases={t: t for t in range(n)},
        scratch_shapes=[pltpu.SemaphoreType.DMA((n,)), pltpu.SemaphoreType.DMA((n,))],
    )(*gs)


def _adamw(g, w, m, v, name):
    r, wd = g.shape

    def body(g_ref, w_ref, m_ref, v_ref, d_ref, mo_ref, vo_ref):
        gv = g_ref[...]
        mn = ADAM_B1 * m_ref[...] + (1.0 - ADAM_B1) * gv
        vn = ADAM_B2 * v_ref[...] + (1.0 - ADAM_B2) * (gv * gv)
        m_hat = mn / (1.0 - ADAM_B1 ** ADAM_STEP)
        v_hat = vn / (1.0 - ADAM_B2 ** ADAM_STEP)
        d_ref[...] = -ADAM_LR * (m_hat / (jnp.sqrt(v_hat) + ADAM_EPS) + ADAM_WD * w_ref[...])
        mo_ref[...] = mn
        vo_ref[...] = vn

    tr = _tile(r, FLAT_TR)
    row = lambda i: (i, 0)
    spec = pl.BlockSpec((tr, wd), row)
    return pl.pallas_call(
        body, name=name, grid=(r // tr,), in_specs=[spec] * 4, out_specs=[spec] * 3,
        out_shape=[jax.ShapeDtypeStruct((r, wd), F32)] * 3,
        compiler_params=_cparams(1),
    )(g, w, m, v)


def _pair_sums(ss, names):
    a = _sibling_take_half(ss, "grad_sibling_half_" + names[0])
    return [_pair_sum(s, a_t, BF16 if n in BIG else F32, f"grad_pair_sum_{n}") for s, a_t, n in zip(ss, a, names)]


def _finish_reduce(ps, bs, names, after=None):
    g = [_chip_sum(p_t, b_t, f"grad_chip_sum_{n}", after if i == 0 else None)
         for i, (p_t, b_t, n) in enumerate(zip(ps, bs, names))]
    return _sibling_join(g, "grad_sibling_join_" + names[0])


WEIGHTS = ("mem_norm", "w_mem_kv", "norm_a", "w_in_a", "conv_w", "conv_b", "w_rec_gate", "b_rec_gate", "w_in_gate",
           "b_in_gate", "lru_lambda", "w_out_a", "kv_norm", "w_kv", "norm_b", "w_in_b", "w_out_b", "final_norm")
SHARD_DIM = {"mem_norm": None, "w_mem_kv": 1, "norm_a": 1, "w_in_a": 2, "conv_w": 2, "conv_b": 1, "w_rec_gate": None,
             "b_rec_gate": 1, "w_in_gate": None, "b_in_gate": 1, "lru_lambda": 1, "w_out_a": 1, "kv_norm": None,
             "w_kv": 1, "norm_b": None, "w_in_b": 2, "w_out_b": 1, "final_norm": None}
BIG = ("w_mem_kv", "w_in_a", "w_out_a", "w_kv", "w_in_b", "w_out_b")
SMALL = ("norm_a", "conv_w", "conv_b", "b_rec_gate", "b_in_gate", "lru_lambda")


def _pad_rows(flat, row_multiple):
    per = FLAT_W * row_multiple
    n = flat.shape[0]
    total = -(-n // per) * per
    return jnp.pad(flat, (0, total - n)).reshape(total // FLAT_W, FLAT_W)


def _flatten(parts, row_multiple):
    return _pad_rows(jnp.concatenate([p.reshape(-1) for p in parts]), row_multiple)


def _unflatten(flat2d, shapes):
    flat = flat2d.reshape(-1)
    out, off = [], 0
    for shp in shapes:
        n = 1
        for s in shp:
            n *= s
        out.append(flat[off:off + n].reshape(shp))
        off += n
    return out


LATE = ("w_kv", "w_in_b", "w_out_b")


def _bf16_rows(w):
    return w.astype(BF16).reshape(-1, w.shape[-1])


def _whole(gathered, shape, dim):
    return jnp.concatenate([gathered[k].reshape(shape) for k in range(N_CHIPS)], axis=dim)


def _gather_weights(local):
    early = [n for n in BIG if n not in LATE]
    small_shapes = [local[n].shape for n in SMALL]
    send_sems, recv_sems, shards, bufs, token = _chip_gather_start([_bf16_rows(local[n]) for n in LATE],
                                                                   "late_weights_gather_start")
    gathered = _chip_all_gather([_bf16_rows(local[n]) for n in early] + [_flatten([local[n] for n in SMALL], 16)],
                                "weights_all_gather")
    full = {n: _whole(g, local[n].shape, SHARD_DIM[n]) for n, g in zip(early, gathered)}
    per_chip = [_unflatten(gathered[-1][k], small_shapes) for k in range(N_CHIPS)]
    for i, n in enumerate(SMALL):
        full[n] = jnp.concatenate([per_chip[k][i] for k in range(N_CHIPS)], axis=SHARD_DIM[n])

    def late(after):
        got = _chip_gather_wait(send_sems, recv_sems, shards, bufs, after, "late_weights_gather_wait")
        return {n: _whole(g, local[n].shape, SHARD_DIM[n]) for n, g in zip(LATE, got)}

    return full, token, late


def _piece(g, name, k):
    dim = SHARD_DIM[name]
    if dim is None:
        return g
    n = g.shape[dim] // N_CHIPS
    return lax.slice_in_dim(g, k * n, (k + 1) * n, axis=dim)


def _local_grads(x, mem, tgt, wts, token, late):
    t, d = x.shape
    depth = wts["w_mem_kv"].shape[0]
    n_a = wts["w_in_a"].shape[0]
    n_b = wts["norm_b"].shape[0]
    nb = wts["w_rec_gate"].shape[1]
    dr = nb * LANES
    dm = wts["w_mem_kv"].shape[2] // 2
    row = lambda v: v.reshape(1, -1)

    wm_all = jnp.concatenate([wts["w_mem_kv"][l] for l in range(depth)], axis=1)
    memkv, memn_bf = _norm_matmul(mem, row(wts["mem_norm"]), wm_all, BF16, "mem_kv_proj", after=token)

    h = x
    saved = []
    vecs = []
    for l in range(n_a):
        proj, u_bf = _norm_matmul(h, row(wts["norm_a"][l]), wts["w_in_a"][l], F32, f"a{l}_in_proj")
        vec = jnp.concatenate([row(wts["conv_b"][l]), row(wts["b_rec_gate"][l]), row(wts["b_in_gate"][l]),
                               row(wts["lru_lambda"][l]), wts["conv_w"][l]], axis=0)
        vecs.append(vec)
        y_rnn, h_rnn, xc = _lru_fwd(proj, vec, wts["w_rec_gate"][l], wts["w_in_gate"][l], f"a{l}_lru_fwd")
        y_mem = _mem_attn_fwd(proj, memkv, l, dr, dm, f"a{l}_mem_fwd")
        mix = jnp.concatenate([y_rnn, y_mem], axis=1)
        h_next = _matmul_res(mix, wts["w_out_a"][l], h, f"a{l}_out_proj")
        saved.append((h, proj, u_bf, mix, h_rnn, xc))
        h = h_next

    h_kv = h
    wts = {**wts, **late(h_kv)}
    kv, ukv_bf = _norm_matmul(h_kv, row(wts["kv_norm"]), wts["w_kv"], BF16, "kv_proj")

    for j in range(n_b):
        l = n_a + j
        proj, u_bf = _norm_matmul(h, row(wts["norm_b"][j]), wts["w_in_b"][j], F32, f"b{j}_in_proj")
        y_sb, o_sb, tl_sb = _sb_fwd(proj, kv, f"b{j}_sb_fwd")
        y_mem = _mem_attn_fwd(proj, memkv, l, dr, dm, f"b{j}_mem_fwd")
        mix = jnp.concatenate([y_sb, y_mem], axis=1)
        h_next = _matmul_res(mix, wts["w_out_b"][j], h, f"b{j}_out_proj")
        saved.append((h, proj, u_bf, mix, o_sb, tl_sb))
        h = h_next

    dh, d_final, loss, dh_bf = _final_loss_bwd(h, row(wts["final_norm"]), tgt, "final_loss_bwd")

    grads = {"final_norm": d_final.reshape(-1)}
    big = {}
    dmemkv = [None] * depth
    g_norm_b = [None] * n_b
    dks, dvs = [], []
    for j in reversed(range(n_b)):
        l = n_a + j
        h_in, proj, u_bf, mix, o_sb, tl_sb = saved[l]
        dmix = _matmul_nt(dh_bf, wts["w_out_b"][j], f"b{j}_dmix")
        big["w_out_b"] = _dw_rows(mix, dh_bf, j, n_b, big.get("w_out_b"), f"b{j}_dw_out")
        dq, dg, dk, dv = _sb_bwd(proj, kv, o_sb, tl_sb, dmix, f"b{j}_sb_bwd")
        dqm, dgm, dkm, dvm = _mem_attn_bwd(proj, memkv, dmix, l, dr, dm, f"b{j}_mem_bwd")
        dmemkv[l] = (dkm, dvm)
        dproj = jnp.concatenate([dq, dg, dqm, dgm], axis=1)
        du = _matmul_nt(dproj, wts["w_in_b"][j], f"b{j}_du")
        big["w_in_b"] = _dw_cols(u_bf, dproj, j, n_b, big.get("w_in_b"), f"b{j}_dw_in")
        dh, dgn, dh_bf = _rms_bwd(du, h_in, row(wts["norm_b"][j]), dh, f"b{j}_rms_bwd")
        g_norm_b[j] = dgn.reshape(-1)
        dks.append(dk)
        dvs.append(dv)
    assert n_b == 2
    dkv = jnp.concatenate([_add_cast(dks[0], dks[1], "dk_sum").T, _add_cast(dvs[0], dvs[1], "dv_sum").T], axis=1)
    du = _matmul_nt(dkv, wts["w_kv"], "kv_du")
    big["w_kv"] = _dw_cols(ukv_bf, dkv, 0, 1, None, "kv_dw")
    dh, dgn, dh_bf = _rms_bwd(du, h_kv, row(wts["kv_norm"]), dh, "kv_rms_bwd")
    grads["kv_norm"] = dgn.reshape(-1)

    early_names = [n for n in BIG if n in big]
    early_p = _pair_sums([big[n].reshape(N_CHIPS, -1, big[n].shape[-1]) for n in early_names], early_names)
    send_sems, recv_sems, early_p, early_b, token = _chip_scatter_start(early_p, "grad_chip_scatter_start")

    g_norm_a = [None] * n_a
    g_wr, g_wi, g_vec = [None] * n_a, [None] * n_a, [None] * n_a
    for l in reversed(range(n_a)):
        h_in, proj, u_bf, mix, h_rnn, xc = saved[l]
        dmix = _matmul_nt(dh_bf, wts["w_out_a"][l], f"a{l}_dmix", after=token if l == n_a - 1 else None)
        big["w_out_a"] = _dw_rows(mix, dh_bf, l, n_a, big.get("w_out_a"), f"a{l}_dw_out")
        dx, dg, g_wr[l], g_wi[l], dvec = _lru_bwd(proj, xc, h_rnn, dmix, vecs[l], wts["w_rec_gate"][l],
                                                  wts["w_in_gate"][l], f"a{l}_lru_bwd")
        g_vec[l] = dvec.transpose(1, 0, 2).reshape(8, dr)
        dqm, dgm, dkm, dvm = _mem_attn_bwd(proj, memkv, dmix, l, dr, dm, f"a{l}_mem_bwd")
        dmemkv[l] = (dkm, dvm)
        dproj = jnp.concatenate([dx, dg, dqm, dgm], axis=1)
        du = _matmul_nt(dproj, wts["w_in_a"][l], f"a{l}_du")
        big["w_in_a"] = _dw_cols(u_bf, dproj, l, n_a, big.get("w_in_a"), f"a{l}_dw_in")
        dh, dgn, dh_bf = _rms_bwd(du, h_in, row(wts["norm_a"][l]), dh, f"a{l}_rms_bwd")
        g_norm_a[l] = dgn.reshape(-1)

    dmemkv_all = jnp.concatenate([jnp.concatenate(p, axis=1) for p in dmemkv], axis=1).astype(BF16)
    pk = d // N_CHIPS
    big["w_mem_kv"] = _matmul_tn(memn_bf, dmemkv_all, "mem_dw", pk, 2 * dm, (N_CHIPS, depth, pk, 2 * dm),
                                 (None, None, pk, 2 * dm), lambda i, j: (i, j, 0, 0))
    dmemn = _matmul_nt(dmemkv_all, wm_all, "mem_du")
    _, dgn, _ = _rms_bwd(dmemn, mem, row(wts["mem_norm"]), jnp.zeros_like(mem), "mem_rms_bwd")
    grads["mem_norm"] = dgn.reshape(-1)
    grads["norm_a"] = jnp.stack(g_norm_a)
    grads["w_rec_gate"] = jnp.stack(g_wr)
    grads["w_in_gate"] = jnp.stack(g_wi)
    gv = jnp.stack(g_vec)
    grads["conv_b"], grads["b_rec_gate"], grads["b_in_gate"], grads["lru_lambda"] = gv[:, 0], gv[:, 1], gv[:, 2], gv[:, 3]
    grads["conv_w"] = gv[:, 4:8]
    grads["norm_b"] = jnp.stack(g_norm_b)
    early_b = _chip_scatter_wait(send_sems, recv_sems, early_p, early_b, dh, "grad_chip_scatter_wait")
    early = (early_names, early_p, early_b)
    big = {n: g.reshape(N_CHIPS, -1, g.shape[-1]) for n, g in big.items() if n not in early_names}
    return loss, dh, grads, big, early


def kernel(x, mem, mem_norm, w_mem_kv, norm_a, w_in_a, conv_w, conv_b, w_rec_gate, b_rec_gate, w_in_gate, b_in_gate, lru_lambda, w_out_a, kv_norm, w_kv, norm_b, w_in_b, w_out_b, final_norm, loss_target, m_mem_norm, m_w_mem_kv, m_norm_a, m_w_in_a, m_conv_w, m_conv_b, m_w_rec_gate, m_b_rec_gate, m_w_in_gate, m_b_in_gate, m_lru_lambda, m_w_out_a, m_kv_norm, m_w_kv, m_norm_b, m_w_in_b, m_w_out_b, m_final_norm, v_mem_norm, v_w_mem_kv, v_norm_a, v_w_in_a, v_conv_w, v_conv_b, v_w_rec_gate, v_b_rec_gate, v_w_in_gate, v_b_in_gate, v_lru_lambda, v_w_out_a, v_kv_norm, v_w_kv, v_norm_b, v_w_in_b, v_w_out_b, v_final_norm):
    local = dict(mem_norm=mem_norm, w_mem_kv=w_mem_kv, norm_a=norm_a, w_in_a=w_in_a, conv_w=conv_w, conv_b=conv_b,
                 w_rec_gate=w_rec_gate, b_rec_gate=b_rec_gate, w_in_gate=w_in_gate, b_in_gate=b_in_gate,
                 lru_lambda=lru_lambda, w_out_a=w_out_a, kv_norm=kv_norm, w_kv=w_kv, norm_b=norm_b, w_in_b=w_in_b,
                 w_out_b=w_out_b, final_norm=final_norm)
    mom = dict(mem_norm=m_mem_norm, w_mem_kv=m_w_mem_kv, norm_a=m_norm_a, w_in_a=m_w_in_a, conv_w=m_conv_w,
               conv_b=m_conv_b, w_rec_gate=m_w_rec_gate, b_rec_gate=m_b_rec_gate, w_in_gate=m_w_in_gate,
               b_in_gate=m_b_in_gate, lru_lambda=m_lru_lambda, w_out_a=m_w_out_a, kv_norm=m_kv_norm, w_kv=m_w_kv,
               norm_b=m_norm_b, w_in_b=m_w_in_b, w_out_b=m_w_out_b, final_norm=m_final_norm)
    var = dict(mem_norm=v_mem_norm, w_mem_kv=v_w_mem_kv, norm_a=v_norm_a, w_in_a=v_w_in_a, conv_w=v_conv_w,
               conv_b=v_conv_b, w_rec_gate=v_w_rec_gate, b_rec_gate=v_b_rec_gate, w_in_gate=v_w_in_gate,
               b_in_gate=v_b_in_gate, lru_lambda=v_lru_lambda, w_out_a=v_w_out_a, kv_norm=v_kv_norm, w_kv=v_w_kv,
               norm_b=v_norm_b, w_in_b=v_w_in_b, w_out_b=v_w_out_b, final_norm=v_final_norm)

    wts, token, late = _gather_weights(local)
    for n in WEIGHTS:
        if SHARD_DIM[n] is None:
            wts[n] = local[n]
    wts["w_rec_gate"] = wts["w_rec_gate"].astype(BF16)
    wts["w_in_gate"] = wts["w_in_gate"].astype(BF16)

    loss, grad_x, grads, big, (early_names, early_p, early_b) = _local_grads(x[0], mem[0], loss_target[0], wts, token, late)

    rest = [n for n in WEIGHTS if n not in BIG]
    row_multiple = 2 * FLAT_TR
    s_rest = jnp.stack([_flatten([_piece(grads[n], n, k) for n in rest], row_multiple) for k in range(N_CHIPS)])
    late_names = [n for n in BIG if n in big] + ["rest"]
    late_p = _pair_sums([big[n] for n in late_names[:-1]] + [s_rest], late_names)
    send_sems, recv_sems, late_p, late_b, token = _chip_scatter_start(late_p, "grad_late_scatter_start")

    g_out, d_out, m_out, v_out = {}, {}, {}, {}

    def update(n, g):
        shape = local[n].shape
        flat = lambda a: a.reshape(-1, shape[-1])
        d, mo, vo = _adamw(g, flat(local[n]), flat(mom[n]), flat(var[n]), f"adamw_{n}")
        g_out[n], d_out[n], m_out[n], v_out[n] = (a.reshape(shape) for a in (g, d, mo, vo))
        return d

    for n, g in zip(early_names, _finish_reduce(early_p, early_b, early_names, after=token)):
        done = update(n, g)
    late_b = _chip_scatter_wait(send_sems, recv_sems, late_p, late_b, done, "grad_late_scatter_wait")
    reduced = dict(zip(late_names, _finish_reduce(late_p, late_b, late_names)))
    for n in late_names[:-1]:
        update(n, reduced[n])
    g_rest = reduced["rest"]
    d_rest, m_rest, v_rest = _adamw(g_rest, *(_flatten([src[n] for n in rest], row_multiple) for src in (local, mom, var)),
                                    "adamw_rest")
    shapes = [local[n].shape for n in rest]
    for out, flat2d in ((g_out, g_rest), (d_out, d_rest), (m_out, m_rest), (v_out, v_rest)):
        out.update(zip(rest, _unflatten(flat2d, shapes)))

    total_loss = lax.psum(loss[0, 0], MESH_AXES)
    return (total_loss, grad_x[None], *[g_out[n] for n in WEIGHTS], *[d_out[n] for n in WEIGHTS],
            *[m_out[n] for n in WEIGHTS], *[v_out[n] for n in WEIGHTS])
```

```python
import functools

import jax
import jax.numpy as jnp
from jax import lax
from jax.experimental import pallas as pl
from jax.experimental.pallas import tpu as pltpu

F32 = jnp.float32
BF16 = jnp.bfloat16

RMS_EPS = 1e-6
LRU_C = 8.0
ADAM_LR = 0.001
ADAM_B1 = 0.9
ADAM_B2 = 0.999
ADAM_EPS = 1e-08
ADAM_WD = 0.01
ADAM_STEP = 10

LANES = 128
VMEM_LIMIT = 56 * 1024 * 1024
FLAT_W = 1024
FLAT_TR = 256
N_CHIPS = 4
MESH_AXES = ("x", "y", "c")

_NT = (((1,), (1,)), ((), ()))
_TN = (((0,), (0,)), ((), ()))
ANY = pl.BlockSpec(memory_space=pl.ANY)
HBM = pl.BlockSpec(memory_space=pltpu.HBM)
SEM = pl.BlockSpec(memory_space=pltpu.SEMAPHORE)
EFFECT = pltpu.SideEffectType.DATAFLOW_SIDE_EFFECTING
MESH = pl.DeviceIdType.MESH


def _cparams(n_axes):
    return pltpu.CompilerParams(dimension_semantics=("arbitrary",) * n_axes, vmem_limit_bytes=VMEM_LIMIT)


def _sigmoid(x):
    return 1.0 / (1.0 + jnp.exp(-x))


def _log1p_pos(e):
    return jnp.where(e < 1e-3, e * (1.0 - e * (0.5 - e * (1.0 / 3.0))), jnp.log(1.0 + e))


def _neg_expm1(x):
    small = -x * (1.0 + x * (0.5 + x * (1.0 / 6.0 + x * (1.0 / 24.0))))
    return jnp.where(x > -0.05, small, 1.0 - jnp.exp(x))


def _tile(n, want):
    if n <= want:
        return n
    t = want
    while n % t:
        t -= LANES
    assert t > 0, (n, want)
    return t


def _norm_matmul(x, g, w, out_dtype, name, after=None):
    m, k = x.shape
    n = w.shape[1]
    tm, tn = _tile(m, 1024), _tile(n, 1024)

    def body(x_ref, g_ref, w_ref, *rest):
        o_ref, u_ref = rest[-2:]

        @pl.when(pl.program_id(1) == 0)
        def _():
            xf = x_ref[...]
            r = lax.rsqrt(jnp.mean(xf * xf, axis=-1, keepdims=True) + RMS_EPS)
            u_ref[...] = ((xf * r) * g_ref[...]).astype(BF16)

        o_ref[...] = jnp.dot(u_ref[...], w_ref[...], preferred_element_type=F32).astype(o_ref.dtype)

    return pl.pallas_call(
        body, name=name, grid=(m // tm, n // tn),
        in_specs=[pl.BlockSpec((tm, k), lambda i, j: (i, 0)), pl.BlockSpec((1, k), lambda i, j: (0, 0)),
                  pl.BlockSpec((k, tn), lambda i, j: (0, j))] + ([] if after is None else [ANY]),
        out_specs=[pl.BlockSpec((tm, tn), lambda i, j: (i, j)), pl.BlockSpec((tm, k), lambda i, j: (i, 0))],
        out_shape=[jax.ShapeDtypeStruct((m, n), out_dtype), jax.ShapeDtypeStruct((m, k), BF16)],
        compiler_params=_cparams(2),
    )(x, g, w, *([] if after is None else [after]))


def _matmul_res(a, b, res, name):
    m, k = a.shape
    n = b.shape[1]
    tm, tn = _tile(m, 1024), _tile(n, 1024)

    def body(a_ref, b_ref, r_ref, o_ref):
        o_ref[...] = r_ref[...] + jnp.dot(a_ref[...], b_ref[...], preferred_element_type=F32)

    return pl.pallas_call(
        body, name=name, grid=(m // tm, n // tn),
        in_specs=[pl.BlockSpec((tm, k), lambda i, j: (i, 0)), pl.BlockSpec((k, tn), lambda i, j: (0, j)),
                  pl.BlockSpec((tm, tn), lambda i, j: (i, j))],
        out_specs=pl.BlockSpec((tm, tn), lambda i, j: (i, j)),
        out_shape=jax.ShapeDtypeStruct((m, n), F32),
        compiler_params=_cparams(2),
    )(a, b, res)


def _matmul_nt(a, b, name, after=None):
    m, n = a.shape
    k = b.shape[0]
    tm, tk = _tile(m, 1024), _tile(k, 512)

    def body(a_ref, b_ref, *rest):
        o_ref = rest[-1]
        o_ref[...] = lax.dot_general(a_ref[...].astype(BF16), b_ref[...], _NT, preferred_element_type=F32)

    in_specs = [pl.BlockSpec((tm, n), lambda i, j: (i, 0)), pl.BlockSpec((tk, n), lambda i, j: (j, 0))]
    args = [a, b]
    if after is not None:
        in_specs.append(ANY)
        args.append(after)
    return pl.pallas_call(
        body, name=name, grid=(m // tm, k // tk), in_specs=in_specs,
        out_specs=pl.BlockSpec((tm, tk), lambda i, j: (i, j)),
        out_shape=jax.ShapeDtypeStruct((m, k), F32),
        compiler_params=_cparams(2),
    )(*args)


def _matmul_tn(a, b, name, tk, tn, out_shape, out_block, out_index, into=None):
    m, k = a.shape
    n = b.shape[1]
    tm = _tile(m, 2048 if b.dtype == BF16 else 1024)

    def body(a_ref, b_ref, *rest):
        o_ref = rest[-1]
        part = lax.dot_general(a_ref[...].astype(BF16), b_ref[...].astype(BF16), _TN, preferred_element_type=F32)

        @pl.when(pl.program_id(2) == 0)
        def _():
            o_ref[...] = part

        @pl.when(pl.program_id(2) != 0)
        def _():
            o_ref[...] += part

    in_specs = [pl.BlockSpec((tm, tk), lambda i, j, s: (s, i)), pl.BlockSpec((tm, tn), lambda i, j, s: (s, j))]
    args = [a, b]
    if into is not None:
        in_specs.append(ANY)
        args.append(into)
    return pl.pallas_call(
        body, name=name, grid=(k // tk, n // tn, m // tm), in_specs=in_specs,
        out_specs=pl.BlockSpec(out_block, lambda i, j, s: out_index(i, j)),
        out_shape=jax.ShapeDtypeStruct(out_shape, F32),
        input_output_aliases={} if into is None else {2: 0},
        compiler_params=_cparams(3),
    )(*args)


def _dw_cols(a, b, layer, n_layers, into, name):
    k, n = a.shape[1], b.shape[1]
    pn = n // N_CHIPS
    tk = _tile(k, 512)
    return _matmul_tn(a, b, name, tk, pn, (N_CHIPS, n_layers, k, pn), (None, None, tk, pn),
                      lambda i, j: (j, layer, i, 0), into)


def _dw_rows(a, b, layer, n_layers, into, name):
    k, n = a.shape[1], b.shape[1]
    pk = k // N_CHIPS
    tn = _tile(n, 2048)
    return _matmul_tn(a, b, name, pk, tn, (N_CHIPS, n_layers, pk, n), (None, None, pk, tn),
                      lambda i, j: (i, layer, 0, j), into)


def _rms_bwd(du, h, g, dres, name):
    m, d = h.shape
    tm = _tile(m, 256)

    def body(du_ref, h_ref, g_ref, dres_ref, dx_ref, dg_ref, dxb_ref):
        xf = h_ref[...]
        r = lax.rsqrt(jnp.mean(xf * xf, axis=-1, keepdims=True) + RMS_EPS)
        xhat = xf * r
        du_v = du_ref[...]
        dxn = du_v * g_ref[...]
        dx = dres_ref[...] + r * (dxn - xhat * jnp.mean(dxn * xhat, axis=-1, keepdims=True))
        dx_ref[...] = dx
        dxb_ref[...] = dx.astype(BF16)
        part = jnp.sum(du_v * xhat, axis=0, keepdims=True)

        @pl.when(pl.program_id(0) == 0)
        def _():
            dg_ref[...] = part

        @pl.when(pl.program_id(0) != 0)
        def _():
            dg_ref[...] += part

    row = lambda i: (i, 0)
    return pl.pallas_call(
        body, name=name, grid=(m // tm,),
        in_specs=[pl.BlockSpec((tm, d), row), pl.BlockSpec((tm, d), row), pl.BlockSpec((1, d), lambda i: (0, 0)),
                  pl.BlockSpec((tm, d), row)],
        out_specs=[pl.BlockSpec((tm, d), row), pl.BlockSpec((1, d), lambda i: (0, 0)), pl.BlockSpec((tm, d), row)],
        out_shape=[jax.ShapeDtypeStruct((m, d), F32), jax.ShapeDtypeStruct((1, d), F32),
                   jax.ShapeDtypeStruct((m, d), BF16)],
        compiler_params=_cparams(1),
    )(du, h, g, dres)


def _final_loss_bwd(h, g, tgt, name):
    m, d = h.shape
    tm = _tile(m, 256)

    def body(h_ref, g_ref, t_ref, dx_ref, dg_ref, loss_ref, dxb_ref):
        xf = h_ref[...]
        r = lax.rsqrt(jnp.mean(xf * xf, axis=-1, keepdims=True) + RMS_EPS)
        xhat = xf * r
        gv = g_ref[...]
        err = xhat * gv - t_ref[...]
        dy = err * (1.0 / d)
        dxn = dy * gv
        dx = r * (dxn - xhat * jnp.mean(dxn * xhat, axis=-1, keepdims=True))
        dx_ref[...] = dx
        dxb_ref[...] = dx.astype(BF16)
        part = jnp.sum(dy * xhat, axis=0, keepdims=True)
        lpart = jnp.sum(jnp.sum(err * err, axis=0, keepdims=True), axis=1, keepdims=True) * (0.5 / d)

        @pl.when(pl.program_id(0) == 0)
        def _():
            dg_ref[...] = part
            loss_ref[...] = lpart

        @pl.when(pl.program_id(0) != 0)
        def _():
            dg_ref[...] += part
            loss_ref[...] += lpart

    row = lambda i: (i, 0)
    fixed = lambda i: (0, 0)
    return pl.pallas_call(
        body, name=name, grid=(m // tm,),
        in_specs=[pl.BlockSpec((tm, d), row), pl.BlockSpec((1, d), fixed), pl.BlockSpec((tm, d), row)],
        out_specs=[pl.BlockSpec((tm, d), row), pl.BlockSpec((1, d), fixed), pl.BlockSpec((1, 1), fixed),
                   pl.BlockSpec((tm, d), row)],
        out_shape=[jax.ShapeDtypeStruct((m, d), F32), jax.ShapeDtypeStruct((1, d), F32),
                   jax.ShapeDtypeStruct((1, 1), F32), jax.ShapeDtypeStruct((m, d), BF16)],
        compiler_params=_cparams(1),
    )(h, g, tgt)


def _add_cast(a, b, name):
    m, n = a.shape
    tm, tn = _tile(m, 512), _tile(n, 2048)

    def body(a_ref, b_ref, o_ref):
        o_ref[...] = (a_ref[...] + b_ref[...]).astype(BF16)

    blk = lambda i, j: (i, j)
    return pl.pallas_call(
        body, name=name, grid=(m // tm, n // tn),
        in_specs=[pl.BlockSpec((tm, tn), blk), pl.BlockSpec((tm, tn), blk)],
        out_specs=pl.BlockSpec((tm, tn), blk),
        out_shape=jax.ShapeDtypeStruct((m, n), BF16),
        compiler_params=_cparams(2),
    )(a, b)


def _mem_attn_fwd(proj, memkv, layer, dr, dm, name):
    t = proj.shape[0]
    nm = memkv.shape[0]
    tm = _tile(t, 512)
    nh = dm // LANES
    scale = LANES ** -0.5
    qb = (2 * dr) // dm

    def body(q_ref, g_ref, k_ref, v_ref, y_ref):
        heads = [slice(hh * LANES, (hh + 1) * LANES) for hh in range(nh)]
        ss = [lax.dot_general(q_ref[:, sl].astype(BF16), k_ref[:, sl], _NT, preferred_element_type=F32) * scale
              for sl in heads]
        ps = [jnp.exp(s - jnp.max(s, axis=-1, keepdims=True)) for s in ss]
        ps = [p / jnp.sum(p, axis=-1, keepdims=True) for p in ps]
        os = [jnp.dot(p.astype(BF16), v_ref[:, sl], preferred_element_type=F32) for p, sl in zip(ps, heads)]
        for o, sl in zip(os, heads):
            gv = g_ref[:, sl]
            y_ref[:, sl] = (o * (gv * _sigmoid(gv))).astype(BF16)

    return pl.pallas_call(
        body, name=name, grid=(t // tm,),
        in_specs=[pl.BlockSpec((tm, dm), lambda i: (i, qb)), pl.BlockSpec((tm, dm), lambda i: (i, qb + 1)),
                  pl.BlockSpec((nm, dm), lambda i: (0, 2 * layer)), pl.BlockSpec((nm, dm), lambda i: (0, 2 * layer + 1))],
        out_specs=pl.BlockSpec((tm, dm), lambda i: (i, 0)),
        out_shape=jax.ShapeDtypeStruct((t, dm), BF16),
        compiler_params=_cparams(1),
    )(proj, proj, memkv, memkv)


def _mem_attn_bwd(proj, memkv, dmix, layer, dr, dm, name):
    t = proj.shape[0]
    nm = memkv.shape[0]
    tm = _tile(t, 512)
    nh = dm // LANES
    scale = LANES ** -0.5
    qb = (2 * dr) // dm
    yb = dr // dm

    def body(q_ref, g_ref, k_ref, v_ref, dy_ref, dq_ref, dg_ref, dk_ref, dv_ref):
        @pl.when(pl.program_id(0) == 0)
        def _():
            dk_ref[...] = jnp.zeros_like(dk_ref)
            dv_ref[...] = jnp.zeros_like(dv_ref)

        heads = [slice(hh * LANES, (hh + 1) * LANES) for hh in range(nh)]
        qs = [q_ref[:, sl].astype(BF16) for sl in heads]
        ss = [lax.dot_general(q, k_ref[:, sl], _NT, preferred_element_type=F32) * scale for q, sl in zip(qs, heads)]
        dos = []
        for sl in heads:
            gv = g_ref[:, sl]
            dos.append((dy_ref[:, sl] * (gv * _sigmoid(gv))).astype(BF16))
        dps = [lax.dot_general(do_bf, v_ref[:, sl], _NT, preferred_element_type=F32) for do_bf, sl in zip(dos, heads)]
        ps = [jnp.exp(s - jnp.max(s, axis=-1, keepdims=True)) for s in ss]
        ps = [p / jnp.sum(p, axis=-1, keepdims=True) for p in ps]
        pbs = [p.astype(BF16) for p in ps]
        os = [jnp.dot(p_bf, v_ref[:, sl], preferred_element_type=F32) for p_bf, sl in zip(pbs, heads)]
        dss = [(p * (dp - jnp.sum(dp * p, axis=-1, keepdims=True)) * scale).astype(BF16) for p, dp in zip(ps, dps)]
        for sl, q, o, p_bf, do_bf, ds in zip(heads, qs, os, pbs, dos, dss):
            gv = g_ref[:, sl]
            sg = _sigmoid(gv)
            dg_ref[:, sl] = (dy_ref[:, sl] * o * (sg * (1.0 + gv * (1.0 - sg)))).astype(BF16)
            dv_ref[:, sl] += lax.dot_general(p_bf, do_bf, _TN, preferred_element_type=F32)
            dq_ref[:, sl] = jnp.dot(ds, k_ref[:, sl], preferred_element_type=F32).astype(BF16)
            dk_ref[:, sl] += lax.dot_general(ds, q, _TN, preferred_element_type=F32)

    fixed = lambda i: (0, 0)
    return pl.pallas_call(
        body, name=name, grid=(t // tm,),
        in_specs=[pl.BlockSpec((tm, dm), lambda i: (i, qb)), pl.BlockSpec((tm, dm), lambda i: (i, qb + 1)),
                  pl.BlockSpec((nm, dm), lambda i: (0, 2 * layer)), pl.BlockSpec((nm, dm), lambda i: (0, 2 * layer + 1)),
                  pl.BlockSpec((tm, dm), lambda i: (i, yb))],
        out_specs=[pl.BlockSpec((tm, dm), lambda i: (i, 0)), pl.BlockSpec((tm, dm), lambda i: (i, 0)),
                   pl.BlockSpec((nm, dm), fixed), pl.BlockSpec((nm, dm), fixed)],
        out_shape=[jax.ShapeDtypeStruct((t, dm), BF16), jax.ShapeDtypeStruct((t, dm), BF16),
                   jax.ShapeDtypeStruct((nm, dm), F32), jax.ShapeDtypeStruct((nm, dm), F32)],
        compiler_params=_cparams(1),
    )(proj, proj, memkv, memkv, dmix)


LRU_CHUNK = 256


def _lru_gates(xc, vec, wr_ref, wi_ref):
    r = _sigmoid(jnp.dot(xc.astype(BF16), wr_ref[...], preferred_element_type=F32) + vec[1:2])
    i = _sigmoid(jnp.dot(xc.astype(BF16), wi_ref[...], preferred_element_type=F32) + vec[2:3])
    lam = vec[3:4]
    cl = -LRU_C * (jnp.maximum(-lam, 0.0) + _log1p_pos(jnp.exp(-jnp.abs(lam))))
    la = cl * r
    a = jnp.exp(la)
    s2 = _neg_expm1(2.0 * la)
    return r, i, cl, a, s2


def _lru_fwd(proj, vec, wr, wi, name):
    t = proj.shape[0]
    nb = wr.shape[0]
    dr = nb * LANES
    c = _tile(t, LRU_CHUNK)

    def body(x_ref, g_ref, vec_ref, wr_ref, wi_ref, y_ref, h_ref, xc_ref, carry_ref, xprev_ref):
        @pl.when(pl.program_id(1) == 0)
        def _():
            carry_ref[...] = jnp.zeros_like(carry_ref)
            xprev_ref[...] = jnp.zeros_like(xprev_ref)

        x = x_ref[...]
        vec = vec_ref[...]
        rows = lax.broadcasted_iota(jnp.int32, (c, LANES), 0)
        xprev = xprev_ref[...]
        xc = vec[7:8] * x + vec[0:1]
        for k in range(1, 4):
            xs = jnp.where(rows < k, pltpu.roll(xprev, k, 0), pltpu.roll(x, k, 0))
            xc = xc + vec[7 - k:8 - k] * xs
        xprev_ref[...] = x
        xc_ref[...] = xc

        r, i, cl, a, s2 = _lru_gates(xc, vec, wr_ref, wi_ref)
        hh = jnp.sqrt(s2) * (i * xc)
        aa = a
        d = 1
        while d < c:
            keep = rows >= d
            hh = jnp.where(keep, aa * pltpu.roll(hh, d, 0) + hh, hh)
            aa = jnp.where(keep, aa * pltpu.roll(aa, d, 0), aa)
            d *= 2
        hfull = hh + aa * carry_ref[7:8, :]
        carry_ref[...] = hfull[c - 8:c, :]
        h_ref[...] = hfull
        gv = g_ref[...]
        y_ref[...] = (hfull * (gv * _sigmoid(gv))).astype(BF16)

    blk = lambda n, s: (s, n)
    return pl.pallas_call(
        body, name=name, grid=(nb, t // c),
        in_specs=[pl.BlockSpec((c, LANES), blk), pl.BlockSpec((c, LANES), lambda n, s: (s, nb + n)),
                  pl.BlockSpec((8, LANES), lambda n, s: (0, n)),
                  pl.BlockSpec((None, LANES, LANES), lambda n, s: (n, 0, 0)),
                  pl.BlockSpec((None, LANES, LANES), lambda n, s: (n, 0, 0))],
        out_specs=[pl.BlockSpec((c, LANES), blk)] * 3,
        out_shape=[jax.ShapeDtypeStruct((t, dr), BF16), jax.ShapeDtypeStruct((t, dr), F32),
                   jax.ShapeDtypeStruct((t, dr), F32)],
        scratch_shapes=[pltpu.VMEM((8, LANES), F32), pltpu.VMEM((c, LANES), F32)],
        compiler_params=_cparams(2),
    )(proj, proj, vec, wr, wi)


def _lru_bwd(proj, xc_all, h_all, dmix, vec, wr, wi, name):
    t = proj.shape[0]
    nb = wr.shape[0]
    dr = nb * LANES
    c = _tile(t, LRU_CHUNK)
    nc = t // c

    def body(x_ref, g_ref, xc_ref, h_ref, dy_ref, vec_ref, wr_ref, wi_ref,
             dx_ref, dg_ref, dwr_ref, dwi_ref, dvec_ref, qcarry_ref, dxc_next_ref):
        @pl.when(pl.program_id(1) == 0)
        def _():
            qcarry_ref[...] = jnp.zeros_like(qcarry_ref)
            dxc_next_ref[...] = jnp.zeros_like(dxc_next_ref)
            dwr_ref[...] = jnp.zeros_like(dwr_ref)
            dwi_ref[...] = jnp.zeros_like(dwi_ref)
            dvec_ref[...] = jnp.zeros_like(dvec_ref)

        x = x_ref[...]
        xc = xc_ref[...]
        h = h_ref[...]
        dy = dy_ref[...]
        gv = g_ref[...]
        vec = vec_ref[...]
        rows = lax.broadcasted_iota(jnp.int32, (c, LANES), 0)

        r, i, cl, a, s2 = _lru_gates(xc, vec, wr_ref, wi_ref)
        s = jnp.sqrt(s2)
        ixc = i * xc
        u = s * ixc
        sg = _sigmoid(gv)
        dh = dy * (gv * sg)
        dg_ref[...] = (dy * h * (sg * (1.0 + gv * (1.0 - sg)))).astype(BF16)

        aa = a
        qq = a * dh
        d = 1
        while d < c:
            keep = rows < c - d
            qq = jnp.where(keep, qq + aa * pltpu.roll(qq, c - d, 0), qq)
            aa = jnp.where(keep, aa * pltpu.roll(aa, c - d, 0), aa)
            d *= 2
        qin = qcarry_ref[0:1, :]
        qfull = qq + aa * qin
        gt = dh + jnp.where(rows == c - 1, qin, pltpu.roll(qfull, c - 1, 0))
        qcarry_ref[...] = qfull[0:8, :]

        dla = gt * (h - u) - gt * ixc * (a * a) / s
        dixc = gt * s
        di = dixc * xc
        dxc = dixc * i
        dzr = (dla * cl) * (r * (1.0 - r))
        dzi = di * (i * (1.0 - i))
        dzr_bf = dzr.astype(BF16)
        dzi_bf = dzi.astype(BF16)
        dxc = dxc + lax.dot_general(dzr_bf, wr_ref[...], _NT, preferred_element_type=F32)
        dxc = dxc + lax.dot_general(dzi_bf, wi_ref[...], _NT, preferred_element_type=F32)
        xc_bf = xc.astype(BF16)
        dwr_ref[...] += lax.dot_general(xc_bf, dzr_bf, _TN, preferred_element_type=F32)
        dwi_ref[...] += lax.dot_general(xc_bf, dzi_bf, _TN, preferred_element_type=F32)

        lam = vec[3:4]
        dlam = jnp.sum(dla * r, axis=0, keepdims=True) * (LRU_C * _sigmoid(-lam))
        colsum = lambda v: jnp.sum(v, axis=0, keepdims=True)
        dxn = dxc_next_ref[...]
        dx = vec[7:8] * dxc
        dtaps = [None] * 4
        dtaps[3] = colsum(x * dxc)
        for k in range(1, 4):
            sh = jnp.where(rows < c - k, pltpu.roll(dxc, c - k, 0), pltpu.roll(dxn, c - k, 0))
            dx = dx + vec[7 - k:8 - k] * sh
            dtaps[3 - k] = colsum(x * sh)
        dxc_next_ref[...] = dxc
        dx_ref[...] = dx.astype(BF16)
        dvec_ref[...] += jnp.concatenate([colsum(dxc), colsum(dzr), colsum(dzi), dlam] + dtaps, axis=0)

    rev = lambda n, s: (nc - 1 - s, n)
    sq = lambda n, s: (n, 0, 0)
    return pl.pallas_call(
        body, name=name, grid=(nb, nc),
        in_specs=[pl.BlockSpec((c, LANES), rev), pl.BlockSpec((c, LANES), lambda n, s: (nc - 1 - s, nb + n)),
                  pl.BlockSpec((c, LANES), rev), pl.BlockSpec((c, LANES), rev), pl.BlockSpec((c, LANES), rev),
                  pl.BlockSpec((8, LANES), lambda n, s: (0, n)),
                  pl.BlockSpec((None, LANES, LANES), sq), pl.BlockSpec((None, LANES, LANES), sq)],
        out_specs=[pl.BlockSpec((c, LANES), rev), pl.BlockSpec((c, LANES), rev),
                   pl.BlockSpec((None, LANES, LANES), sq), pl.BlockSpec((None, LANES, LANES), sq),
                   pl.BlockSpec((None, 8, LANES), sq)],
        out_shape=[jax.ShapeDtypeStruct((t, dr), BF16), jax.ShapeDtypeStruct((t, dr), BF16),
                   jax.ShapeDtypeStruct((nb, LANES, LANES), F32), jax.ShapeDtypeStruct((nb, LANES, LANES), F32),
                   jax.ShapeDtypeStruct((nb, 8, LANES), F32)],
        scratch_shapes=[pltpu.VMEM((8, LANES), F32), pltpu.VMEM((c, LANES), F32)],
        compiler_params=_cparams(2),
    )(proj, proj, xc_all, h_all, dmix, vec, wr, wi)


SB_TQ = 1024
SB_TK = 256


def _sb_softplus(z, diag):
    sp = jnp.maximum(z, 0.0) + jnp.log(1.0 + jnp.exp(-jnp.abs(z)))
    mask = None
    if diag:
        mask = lax.broadcasted_iota(jnp.int32, z.shape, 1) < lax.broadcasted_iota(jnp.int32, z.shape, 0)
        sp = jnp.where(mask, sp, 0.0)
    return sp, mask


def _split_dot(v, m):
    hi = v.astype(BF16)
    lo = (v - hi.astype(F32)).astype(BF16)
    return jnp.dot(hi, m, preferred_element_type=F32) + jnp.dot(lo, m, preferred_element_type=F32)


def _tri_ones(kind, tk):
    jj = lax.broadcasted_iota(jnp.int32, (tk, tk), 0)
    ss = lax.broadcasted_iota(jnp.int32, (tk, tk), 1)
    rel = {"ge": jj >= ss, "le": jj <= ss}[kind]
    return jnp.where(rel, 1.0, 0.0).astype(BF16)


def _sb_fwd(proj, kv, name):
    t = proj.shape[0]
    ds = kv.shape[1] // 2
    nh = ds // LANES
    tq = _tile(t, SB_TQ)
    tk = _tile(tq, SB_TK)
    nd = tq // tk
    assert nd % 2 == 0 or t == tq
    scale = LANES ** -0.5

    def body(q_ref, g_ref, k_ref, v_ref, y_ref, o_ref, tl_ref, qbf_ref, acc_ref, run_ref, z_ref, w_ref):
        qi = pl.program_id(1)
        qbf_ref[...] = q_ref[...].astype(BF16)
        tri = _tri_ones("ge", tk)
        acc_ref[...] = jnp.zeros_like(acc_ref)
        run_ref[...] = jnp.zeros_like(run_ref)
        all_rows = [slice(s0, s0 + tk) for s0 in range(0, tq, tk)]

        def weights(zs, groups):
            sps = [_sb_softplus(z, dg) for z, (_, dg) in zip(zs, groups)]
            cums = [_split_dot(sp, tri) for sp, _ in sps]
            ws = []
            for z, (rows, dg), (_, mask), cum in zip(zs, groups, sps, cums):
                run = run_ref[rows, :]
                w = jnp.exp(z - cum - run)
                if dg:
                    w = jnp.where(mask, w, 0.0)
                run_ref[rows, :] = run + cum[:, 0:1]
                ws.append(w.astype(BF16))
            return ws

        n = qi * nd

        def rows_from(u):
            return [slice(s0, s0 + tk) for s0 in range(u * tk, tq, tk)]

        def logits_into(slot, kb, rows_list):
            k = k_ref[pl.ds(pl.multiple_of(kb * tk, tk), tk), :]
            seen = slice(rows_list[0].start, tq)
            z_ref[slot, seen, :] = lax.dot_general(qbf_ref[seen, :], k, _NT, preferred_element_type=F32) * scale

        def add_values(kb, rows_list):
            v = v_ref[pl.ds(pl.multiple_of(kb * tk, tk), tk), :]
            seen = slice(rows_list[0].start, tq)
            acc_ref[seen, :] += jnp.dot(w_ref[seen, :], v, preferred_element_type=F32)

        def weigh(slot, groups):
            for (rows, _), w in zip(groups, weights([z_ref[slot, rows, :] for rows, _ in groups], groups)):
                w_ref[rows, :] = w

        logits_into(0, n + nd - 1, rows_from(nd - 1))
        for i, u in enumerate(reversed(range(nd))):
            slot = i % 2
            if i > 0:
                add_values(n + u + 1, rows_from(u + 1))
            if u > 0:
                logits_into(1 - slot, n + u - 1, rows_from(u - 1))
            else:
                logits_into(1 - slot, jnp.maximum(n - 1, 0), all_rows)
            weigh(slot, [(rows, j == 0) for j, rows in enumerate(rows_from(u))])

        def half_step(j, slot):
            kb = n - 1 - j
            add_values(kb + 1, all_rows)
            logits_into(1 - slot, jnp.maximum(kb - 1, 0), all_rows)
            weigh(slot, [(rows, False) for rows in all_rows])

        def step(i, carry):
            half_step(2 * i, nd % 2)
            half_step(2 * i + 1, 1 - nd % 2)
            return carry

        lax.fori_loop(0, n // 2, step, 0)
        add_values(0, all_rows)
        o = acc_ref[...]
        o_ref[...] = o
        tl_ref[...] = jnp.broadcast_to(run_ref[...], (tq, LANES))
        gv = g_ref[...]
        y_ref[...] = (o * (gv * _sigmoid(gv))).astype(BF16)

    blk = lambda h, i: (i, h)
    return pl.pallas_call(
        body, name=name, grid=(nh, t // tq),
        in_specs=[pl.BlockSpec((tq, LANES), blk), pl.BlockSpec((tq, LANES), lambda h, i: (i, nh + h)),
                  pl.BlockSpec((t, LANES), lambda h, i: (0, h)), pl.BlockSpec((t, LANES), lambda h, i: (0, nh + h))],
        out_specs=[pl.BlockSpec((tq, LANES), blk)] * 3,
        out_shape=[jax.ShapeDtypeStruct((t, ds), BF16), jax.ShapeDtypeStruct((t, ds), F32),
                   jax.ShapeDtypeStruct((t, ds), F32)],
        scratch_shapes=[pltpu.VMEM((tq, LANES), BF16), pltpu.VMEM((tq, LANES), F32), pltpu.VMEM((tq, 1), F32),
                        pltpu.VMEM((2, tq, tk), F32), pltpu.VMEM((tq, tk), BF16)],
        compiler_params=_cparams(2),
    )(proj, proj, kv, kv)


def _sb_bwd(proj, kv, o_all, tl_all, dmix, name):
    t = proj.shape[0]
    ds = kv.shape[1] // 2
    nh = ds // LANES
    tq = _tile(t, SB_TQ)
    tk = _tile(tq, SB_TK)
    nd = tq // tk
    scale = LANES ** -0.5

    def body(q_ref, g_ref, k_ref, v_ref, o_ref, tl_ref, dy_ref, dq_ref, dg_ref, dk_ref, dv_ref,
             qbf_ref, dobf_ref, qt_ref, dot_ref, acc_ref, left_ref, rune_ref, z_ref, dw_ref, wp_ref, dzp_ref):
        qi = pl.program_id(1)

        @pl.when(qi == 0)
        def _():
            dk_ref[...] = jnp.zeros_like(dk_ref)
            dv_ref[...] = jnp.zeros_like(dv_ref)

        qbf_ref[...] = q_ref[...].astype(BF16)
        qt_ref[...] = q_ref[...].T.astype(BF16)
        gv = g_ref[...]
        sg = _sigmoid(gv)
        dy = dy_ref[...]
        do = dy * (gv * sg)
        dobf_ref[...] = do.astype(BF16)
        dot_ref[...] = do.T.astype(BF16)
        dg_ref[...] = (dy * o_ref[...] * (sg * (1.0 + gv * (1.0 - sg)))).astype(BF16)
        tri = _tri_ones("le", tk)
        acc_ref[...] = jnp.zeros_like(acc_ref)
        left_ref[...] = tl_ref[:, 0:1]
        rune_ref[...] = jnp.zeros_like(rune_ref)
        wp_ref[...] = jnp.zeros_like(wp_ref)
        dzp_ref[...] = jnp.zeros_like(dzp_ref)
        all_rows = [slice(s0, s0 + tk) for s0 in range(0, tq, tk)]

        def grads(zs, dws, groups):
            sps = [_sb_softplus(z, dg) for z, (_, dg) in zip(zs, groups)]
            cums = [_split_dot(sp, tri) for sp, _ in sps]
            ws, es, lbs = [], [], []
            for z, (rows, dg), (sp, mask), cum, dw in zip(zs, groups, sps, cums, dws):
                left = left_ref[rows, :]
                lb = z - sp
                w = jnp.exp(lb - (left - cum))
                if dg:
                    w = jnp.where(mask, w, 0.0)
                left_ref[rows, :] = left - cum[:, tk - 1:tk]
                ws.append(w.astype(BF16))
                es.append(dw * w)
                lbs.append(lb)
            cumes = [_split_dot(e, tri) for e in es]
            dzs = []
            for (rows, dg), (_, mask), lb, e, cume in zip(groups, sps, lbs, es, cumes):
                rune = rune_ref[rows, :]
                dz = (e - jnp.exp(lb) * (rune + cume)) * scale
                if dg:
                    dz = jnp.where(mask, dz, 0.0)
                rune_ref[rows, :] = rune + cume[:, tk - 1:tk]
                dzs.append(dz.astype(BF16))
            return dzs, ws

        n = qi * nd

        def rows_from(u):
            return [slice(s0, s0 + tk) for s0 in range(u * tk, tq, tk)]

        def logits_into(slot, kb, rows_list):
            k0 = pl.multiple_of(kb * tk, tk)
            k = k_ref[pl.ds(k0, tk), :]
            v = v_ref[pl.ds(k0, tk), :]
            seen = slice(rows_list[0].start, tq)
            z_ref[slot, seen, :] = lax.dot_general(qbf_ref[seen, :], k, _NT, preferred_element_type=F32) * scale
            dw_ref[slot, seen, :] = lax.dot_general(dobf_ref[seen, :], v, _NT, preferred_element_type=F32)

        def apply_stored(kb, u):
            k0 = pl.multiple_of(kb * tk, tk)
            k = k_ref[pl.ds(k0, tk), :]
            seen = slice(u * tk, tq)
            acc_ref[seen, :] += jnp.dot(dzp_ref[seen, :], k, preferred_element_type=F32)
            dk_ref[:, pl.ds(k0, tk)] += jnp.dot(qt_ref[:, seen], dzp_ref[seen, :], preferred_element_type=F32)
            dv_ref[:, pl.ds(k0, tk)] += jnp.dot(dot_ref[:, seen], wp_ref[seen, :], preferred_element_type=F32)

        def differentiate(slot, groups):
            dzs, ws = grads([z_ref[slot, rows, :] for rows, _ in groups], [dw_ref[slot, rows, :] for rows, _ in groups],
                            groups)
            for (rows, _), dz, w in zip(groups, dzs, ws):
                dzp_ref[rows, :] = dz
                wp_ref[rows, :] = w

        logits_into(0, 0, all_rows)

        def half_step(j, slot):
            apply_stored(jnp.maximum(j - 1, 0), 0)
            logits_into(1 - slot, j + 1, all_rows)
            differentiate(slot, [(rows, False) for rows in all_rows])

        def step(i, carry):
            half_step(2 * i, 0)
            half_step(2 * i + 1, 1)
            return carry

        lax.fori_loop(0, n // 2, step, 0)
        for u in range(nd):
            slot = u % 2
            if u == 0:
                apply_stored(jnp.maximum(n - 1, 0), 0)
            else:
                apply_stored(n + u - 1, u - 1)
            if u + 1 < nd:
                logits_into(1 - slot, n + u + 1, rows_from(u + 1))
            differentiate(slot, [(rows, j == 0) for j, rows in enumerate(rows_from(u))])
        apply_stored(n + nd - 1, nd - 1)
        dq_ref[...] = acc_ref[...].astype(BF16)

    blk = lambda h, i: (i, h)
    whole = lambda h, i: (0, h)
    return pl.pallas_call(
        body, name=name, grid=(nh, t // tq),
        in_specs=[pl.BlockSpec((tq, LANES), blk), pl.BlockSpec((tq, LANES), lambda h, i: (i, nh + h)),
                  pl.BlockSpec((t, LANES), whole), pl.BlockSpec((t, LANES), lambda h, i: (0, nh + h)),
                  pl.BlockSpec((tq, LANES), blk), pl.BlockSpec((tq, LANES), blk), pl.BlockSpec((tq, LANES), blk)],
        out_specs=[pl.BlockSpec((tq, LANES), blk), pl.BlockSpec((tq, LANES), blk),
                   pl.BlockSpec((LANES, t), lambda h, i: (h, 0)), pl.BlockSpec((LANES, t), lambda h, i: (h, 0))],
        out_shape=[jax.ShapeDtypeStruct((t, ds), BF16), jax.ShapeDtypeStruct((t, ds), BF16),
                   jax.ShapeDtypeStruct((ds, t), F32), jax.ShapeDtypeStruct((ds, t), F32)],
        scratch_shapes=[pltpu.VMEM((tq, LANES), BF16), pltpu.VMEM((tq, LANES), BF16),
                        pltpu.VMEM((LANES, tq), BF16), pltpu.VMEM((LANES, tq), BF16), pltpu.VMEM((tq, LANES), F32),
                        pltpu.VMEM((tq, 1), F32), pltpu.VMEM((tq, 1), F32),
                        pltpu.VMEM((2, tq, tk), F32), pltpu.VMEM((2, tq, tk), F32),
                        pltpu.VMEM((tq, tk), BF16), pltpu.VMEM((tq, tk), BF16)],
        compiler_params=_cparams(2),
    )(proj, proj, kv, kv, o_all, tl_all, dmix)


def _place():
    x, y, c = lax.axis_index("x"), lax.axis_index("y"), lax.axis_index("c")
    chips = [(1 - x, y), (x, 1 - y), (1 - x, 1 - y)]
    return x, y, c, chips


def _remote(src, dst, send_sems, recv_sems, k, to):
    return pltpu.make_async_remote_copy(src_ref=src, dst_ref=dst, send_sem=send_sems.at[k], recv_sem=recv_sems.at[k],
                                        device_id=to, device_id_type=MESH)


def _my_chip():
    return 2 * lax.axis_index("x") + lax.axis_index("y")


def _place_own(shard, name):
    r, w = shard.shape
    tr = _tile(r, FLAT_TR)

    def body(x_ref, o_ref):
        o_ref[...] = x_ref[...]

    return pl.pallas_call(
        body, name=name, out_shape=jax.ShapeDtypeStruct((N_CHIPS, r, w), shard.dtype), grid=(r // tr,),
        in_specs=[pl.BlockSpec((tr, w), lambda i: (i, 0))],
        out_specs=pl.BlockSpec((None, tr, w), lambda i: (_my_chip(), i, 0)),
        compiler_params=_cparams(1),
    )(shard)


def _chip_all_gather(shards, name):
    n = len(shards)

    def body(*refs):
        x_refs, out_refs, send_sems, recv_sems = refs[:n], refs[2 * n:3 * n], refs[3 * n], refs[3 * n + 1]
        x, y, c, chips = _place()
        me = 2 * x + y
        sibling = (x, y, 1 - c)

        def rows(t, core):
            rh = x_refs[t].shape[0] // 2
            return pl.ds(core * rh, rh)

        first = [_remote(x_refs[t].at[rows(t, c)], out_refs[t].at[me, rows(t, c)], send_sems, recv_sems, 6 * t + k,
                         (cx, cy, c)) for t in range(n) for k, (cx, cy) in enumerate(chips)]
        for cp in first:
            cp.start()
        passed = []
        for k, (cx, cy) in enumerate(chips):
            for t in range(n):
                got = out_refs[t].at[2 * cx + cy, rows(t, c)]
                _remote(got, got, send_sems, recv_sems, 6 * t + k, (cx, cy, c)).wait_recv()
                fwd = _remote(got, got, send_sems, recv_sems, 6 * t + 3 + k, sibling)
                fwd.start()
                passed.append(fwd)
        for k, (cx, cy) in enumerate(chips):
            for t in range(n):
                got = out_refs[t].at[2 * cx + cy, rows(t, 1 - c)]
                _remote(got, got, send_sems, recv_sems, 6 * t + 3 + k, sibling).wait_recv()
        for cp in first + passed:
            cp.wait_send()

    bufs = [_place_own(s, f"{name}_own{t}") for t, s in enumerate(shards)]
    return pl.pallas_call(
        body, name=name, in_specs=[ANY] * (2 * n), out_specs=[ANY] * n,
        out_shape=[jax.ShapeDtypeStruct((N_CHIPS,) + s.shape, s.dtype) for s in shards],
        input_output_aliases={n + t: t for t in range(n)},
        scratch_shapes=[pltpu.SemaphoreType.DMA((6 * n,)), pltpu.SemaphoreType.DMA((6 * n,))],
    )(*shards, *bufs)


def _chip_gather_start(shards, name):
    n = len(shards)

    def body(*refs):
        x_refs, buf_refs, send_sems, recv_sems, token = refs[:n], refs[n:2 * n], refs[2 * n], refs[2 * n + 1], refs[-1]
        x, y, c, chips = _place()
        me = 2 * x + y
        for t in range(n):
            for k, (cx, cy) in enumerate(chips):
                _remote(x_refs[t], buf_refs[t].at[me], send_sems, recv_sems, 3 * t + k, (cx, cy, c)).start()
        token[...] = jnp.zeros_like(token)

    bufs = [_place_own(s, f"{name}_own{t}") for t, s in enumerate(shards)]
    hbm = [pltpu.HBM(a.shape, a.dtype) for a in list(shards) + bufs]
    outs = pl.pallas_call(
        body, name=name, in_specs=[HBM] * (2 * n),
        out_shape=(pltpu.SemaphoreType.DMA((3 * n,)), pltpu.SemaphoreType.DMA((3 * n,)), *hbm,
                   jax.ShapeDtypeStruct((8, LANES), F32)),
        out_specs=(SEM, SEM, *[HBM] * (2 * n), pl.BlockSpec(memory_space=pltpu.VMEM)),
        input_output_aliases={t: 2 + t for t in range(2 * n)},
        compiler_params=pltpu.CompilerParams(has_side_effects=EFFECT),
    )(*[pltpu.with_memory_space_constraint(a, pltpu.HBM) for a in list(shards) + bufs])
    return outs[0], outs[1], outs[2:2 + n], outs[2 + n:2 + 2 * n], outs[-1]


def _chip_gather_wait(send_sems, recv_sems, shards, bufs, after, name):
    n = len(shards)

    def body(*refs):
        x_refs, buf_refs, send, recv = refs[:n], refs[n:2 * n], refs[2 * n], refs[2 * n + 1]
        x, y, c, chips = _place()
        for t in range(n):
            for k, (cx, cy) in enumerate(chips):
                cp = _remote(x_refs[t], buf_refs[t].at[2 * cx + cy], send, recv, 3 * t + k, (cx, cy, c))
                cp.wait_send()
                cp.wait_recv()

    hbm = [pltpu.HBM(a.shape, a.dtype) for a in list(shards) + list(bufs)]
    outs = pl.pallas_call(
        body, name=name, in_specs=[HBM] * (2 * n) + [SEM, SEM, ANY],
        out_shape=tuple(hbm), out_specs=tuple([HBM] * (2 * n)),
        input_output_aliases={t: t for t in range(2 * n)},
        compiler_params=pltpu.CompilerParams(has_side_effects=EFFECT),
    )(*shards, *bufs, send_sems, recv_sems, after)
    return outs[n:]


def _sibling_take_half(ss, name):
    n = len(ss)

    def body(*refs):
        s_refs, a_refs, send_sems, recv_sems = refs[:n], refs[n:2 * n], refs[2 * n], refs[2 * n + 1]
        x, y, c, _ = _place()
        cps = []
        for t in range(n):
            rh = s_refs[t].shape[1] // 2
            cps.append(_remote(s_refs[t].at[:, pl.ds((1 - c) * rh, rh), :], a_refs[t], send_sems, recv_sems, t,
                               (x, y, 1 - c)))
        for cp in cps:
            cp.start()
        for cp in cps:
            cp.wait()

    return pl.pallas_call(
        body, name=name, in_specs=[ANY] * n, out_specs=[ANY] * n,
        out_shape=[jax.ShapeDtypeStruct((s.shape[0], s.shape[1] // 2, s.shape[2]), s.dtype) for s in ss],
        scratch_shapes=[pltpu.SemaphoreType.DMA((n,)), pltpu.SemaphoreType.DMA((n,))],
    )(*ss)


def _pair_sum(s, a, dtype, name):
    n, r, w = s.shape
    rh = r // 2
    tr = _tile(rh, FLAT_TR)
    nblk = rh // tr

    def body(s_ref, a_ref, o_ref):
        o_ref[...] = (s_ref[...] + a_ref[...]).astype(dtype)

    return pl.pallas_call(
        body, name=name, out_shape=jax.ShapeDtypeStruct((n, rh, w), dtype), grid=(n, nblk),
        in_specs=[pl.BlockSpec((None, tr, w), lambda k, i: (k, lax.axis_index("c") * nblk + i, 0)),
                  pl.BlockSpec((None, tr, w), lambda k, i: (k, i, 0))],
        out_specs=pl.BlockSpec((None, tr, w), lambda k, i: (k, i, 0)),
        compiler_params=_cparams(2),
    )(s, a)


def _chip_scatter_start(ps, name):
    n = len(ps)

    def body(*refs):
        p_refs, b_refs, send_sems, recv_sems, token = refs[:n], refs[n:2 * n], refs[2 * n], refs[2 * n + 1], refs[-1]
        x, y, c, chips = _place()
        me = 2 * x + y
        for t in range(n):
            for k, (cx, cy) in enumerate(chips):
                _remote(p_refs[t].at[2 * cx + cy], b_refs[t].at[me], send_sems, recv_sems, 3 * t + k, (cx, cy, c)).start()
        token[...] = jnp.zeros_like(token)

    hbm = [pltpu.HBM(p.shape, p.dtype) for p in ps]
    outs = pl.pallas_call(
        body, name=name, in_specs=[HBM] * (2 * n),
        out_shape=(pltpu.SemaphoreType.DMA((3 * n,)), pltpu.SemaphoreType.DMA((3 * n,)), *hbm, *hbm,
                   jax.ShapeDtypeStruct((8, LANES), F32)),
        out_specs=(SEM, SEM, *[HBM] * (2 * n), pl.BlockSpec(memory_space=pltpu.VMEM)),
        input_output_aliases={t: 2 + t for t in range(2 * n)},
        compiler_params=pltpu.CompilerParams(has_side_effects=EFFECT),
    )(*[pltpu.with_memory_space_constraint(p, pltpu.HBM) for p in ps],
      *[pltpu.with_memory_space_constraint(lax.empty(p.shape, p.dtype), pltpu.HBM) for p in ps])
    return outs[0], outs[1], outs[2:2 + n], outs[2 + n:2 + 2 * n], outs[-1]


def _chip_scatter_wait(send_sems, recv_sems, ps, bs, after, name):
    n = len(ps)

    def body(*refs):
        p_refs, b_refs, send, recv = refs[:n], refs[n:2 * n], refs[2 * n], refs[2 * n + 1]
        x, y, c, chips = _place()
        for t in range(n):
            for k, (cx, cy) in enumerate(chips):
                cp = _remote(p_refs[t].at[2 * cx + cy], b_refs[t].at[2 * cx + cy], send, recv, 3 * t + k, (cx, cy, c))
                cp.wait_send()
                cp.wait_recv()

    hbm = [pltpu.HBM(p.shape, p.dtype) for p in ps]
    outs = pl.pallas_call(
        body, name=name, in_specs=[HBM] * (2 * n) + [SEM, SEM, ANY],
        out_shape=(*hbm, *hbm), out_specs=tuple([HBM] * (2 * n)),
        input_output_aliases={t: t for t in range(2 * n)},
        compiler_params=pltpu.CompilerParams(has_side_effects=EFFECT),
    )(*ps, *bs, send_sems, recv_sems, after)
    return outs[n:]


def _chip_sum(p, b, name, after=None):
    n, rh, w = p.shape
    tr = _tile(rh, FLAT_TR)
    nblk = rh // tr

    def body(p_ref, b0_ref, b1_ref, b2_ref, b3_ref, *rest):
        o_ref = rest[-1]
        me = _my_chip()
        own = p_ref[...]
        t = [jnp.where(me == k, own, b_ref[...]).astype(F32) for k, b_ref in enumerate((b0_ref, b1_ref, b2_ref, b3_ref))]
        o_ref[...] = ((t[0] + t[1]) + t[2]) + t[3]

    def other(k):
        return lambda i: (jnp.where(_my_chip() == k, (k + 1) % N_CHIPS, k), i, 0)

    return pl.pallas_call(
        body, name=name, out_shape=jax.ShapeDtypeStruct((2 * rh, w), F32), grid=(nblk,),
        in_specs=[pl.BlockSpec((None, tr, w), lambda i: (_my_chip(), i, 0))]
        + [pl.BlockSpec((None, tr, w), other(k)) for k in range(N_CHIPS)] + ([] if after is None else [ANY]),
        out_specs=pl.BlockSpec((tr, w), lambda i: (lax.axis_index("c") * nblk + i, 0)),
        compiler_params=_cparams(1),
    )(p, b, b, b, b, *([] if after is None else [after]))


def _sibling_join(gs, name):
    n = len(gs)

    def body(*refs):
        g_refs, send_sems, recv_sems = refs[n:2 * n], refs[2 * n], refs[2 * n + 1]
        x, y, c, _ = _place()
        cps = []
        for t in range(n):
            rh = g_refs[t].shape[0] // 2
            mine = g_refs[t].at[pl.ds(c * rh, rh)]
            cps.append(_remote(mine, mine, send_sems, recv_sems, t, (x, y, 1 - c)))
        for cp in cps:
            cp.start()
        for t in range(n):
            rh = g_refs[t].shape[0] // 2
            theirs = g_refs[t].at[pl.ds((1 - c) * rh, rh)]
            _remote(theirs, theirs, send_sems, recv_sems, t, (x, y, 1 - c)).wait_recv()
        for cp in cps:
            cp.wait_send()

    return pl.pallas_call(
        body, name=name, in_specs=[ANY] * n, out_specs=[ANY] * n,
        out_shape=[jax.ShapeDtypeStruct(g.shape, g.dtype) for g in gs],
        input_output_aliases={t: t for t in range(n)},
        scratch_shapes=[pltpu.SemaphoreType.DMA((n,)), pltpu.SemaphoreType.DMA((n,))],
    )(*gs)


def _adamw(g, w, m, v, name):
    r, wd = g.shape

    def body(g_ref, w_ref, m_ref, v_ref, d_ref, mo_ref, vo_ref):
        gv = g_ref[...]
        mn = ADAM_B1 * m_ref[...] + (1.0 - ADAM_B1) * gv
        vn = ADAM_B2 * v_ref[...] + (1.0 - ADAM_B2) * (gv * gv)
        m_hat = mn / (1.0 - ADAM_B1 ** ADAM_STEP)
        v_hat = vn / (1.0 - ADAM_B2 ** ADAM_STEP)
        d_ref[...] = -ADAM_LR * (m_hat / (jnp.sqrt(v_hat) + ADAM_EPS) + ADAM_WD * w_ref[...])
        mo_ref[...] = mn
        vo_ref[...] = vn

    tr = _tile(r, FLAT_TR)
    row = lambda i: (i, 0)
    spec = pl.BlockSpec((tr, wd), row)
    return pl.pallas_call(
        body, name=name, grid=(r // tr,), in_specs=[spec] * 4, out_specs=[spec] * 3,
        out_shape=[jax.ShapeDtypeStruct((r, wd), F32)] * 3,
        compiler_params=_cparams(1),
    )(g, w, m, v)


def _pair_sums(ss, names):
    a = _sibling_take_half(ss, "grad_sibling_half_" + names[0])
    return [_pair_sum(s, a_t, BF16 if n in BIG else F32, f"grad_pair_sum_{n}") for s, a_t, n in zip(ss, a, names)]


def _finish_reduce(ps, bs, names, after=None):
    g = [_chip_sum(p_t, b_t, f"grad_chip_sum_{n}", after if i == 0 else None)
         for i, (p_t, b_t, n) in enumerate(zip(ps, bs, names))]
    return _sibling_join(g, "grad_sibling_join_" + names[0])


WEIGHTS = ("mem_norm", "w_mem_kv", "norm_a", "w_in_a", "conv_w", "conv_b", "w_rec_gate", "b_rec_gate", "w_in_gate",
           "b_in_gate", "lru_lambda", "w_out_a", "kv_norm", "w_kv", "norm_b", "w_in_b", "w_out_b", "final_norm")
SHARD_DIM = {"mem_norm": None, "w_mem_kv": 1, "norm_a": 1, "w_in_a": 2, "conv_w": 2, "conv_b": 1, "w_rec_gate": None,
             "b_rec_gate": 1, "w_in_gate": None, "b_in_gate": 1, "lru_lambda": 1, "w_out_a": 1, "kv_norm": None,
             "w_kv": 1, "norm_b": None, "w_in_b": 2, "w_out_b": 1, "final_norm": None}
BIG = ("w_mem_kv", "w_in_a", "w_out_a", "w_kv", "w_in_b", "w_out_b")
SMALL = ("norm_a", "conv_w", "conv_b", "b_rec_gate", "b_in_gate", "lru_lambda")


def _pad_rows(flat, row_multiple):
    per = FLAT_W * row_multiple
    n = flat.shape[0]
    total = -(-n // per) * per
    return jnp.pad(flat, (0, total - n)).reshape(total // FLAT_W, FLAT_W)


def _flatten(parts, row_multiple):
    return _pad_rows(jnp.concatenate([p.reshape(-1) for p in parts]), row_multiple)


def _unflatten(flat2d, shapes):
    flat = flat2d.reshape(-1)
    out, off = [], 0
    for shp in shapes:
        n = 1
        for s in shp:
            n *= s
        out.append(flat[off:off + n].reshape(shp))
        off += n
    return out


LATE = ("w_kv", "w_in_b", "w_out_b")


def _bf16_rows(w):
    return w.astype(BF16).reshape(-1, w.shape[-1])


def _whole(gathered, shape, dim):
    return jnp.concatenate([gathered[k].reshape(shape) for k in range(N_CHIPS)], axis=dim)


def _gather_weights(local):
    early = [n for n in BIG if n not in LATE]
    small_shapes = [local[n].shape for n in SMALL]
    send_sems, recv_sems, shards, bufs, token = _chip_gather_start([_bf16_rows(local[n]) for n in LATE],
                                                                   "late_weights_gather_start")
    gathered = _chip_all_gather([_bf16_rows(local[n]) for n in early] + [_flatten([local[n] for n in SMALL], 16)],
                                "weights_all_gather")
    full = {n: _whole(g, local[n].shape, SHARD_DIM[n]) for n, g in zip(early, gathered)}
    per_chip = [_unflatten(gathered[-1][k], small_shapes) for k in range(N_CHIPS)]
    for i, n in enumerate(SMALL):
        full[n] = jnp.concatenate([per_chip[k][i] for k in range(N_CHIPS)], axis=SHARD_DIM[n])

    def late(after):
        got = _chip_gather_wait(send_sems, recv_sems, shards, bufs, after, "late_weights_gather_wait")
        return {n: _whole(g, local[n].shape, SHARD_DIM[n]) for n, g in zip(LATE, got)}

    return full, token, late


def _piece(g, name, k):
    dim = SHARD_DIM[name]
    if dim is None:
        return g
    n = g.shape[dim] // N_CHIPS
    return lax.slice_in_dim(g, k * n, (k + 1) * n, axis=dim)


def _local_grads(x, mem, tgt, wts, token, late):
    t, d = x.shape
    depth = wts["w_mem_kv"].shape[0]
    n_a = wts["w_in_a"].shape[0]
    n_b = wts["norm_b"].shape[0]
    nb = wts["w_rec_gate"].shape[1]
    dr = nb * LANES
    dm = wts["w_mem_kv"].shape[2] // 2
    row = lambda v: v.reshape(1, -1)

    wm_all = jnp.concatenate([wts["w_mem_kv"][l] for l in range(depth)], axis=1)
    memkv, memn_bf = _norm_matmul(mem, row(wts["mem_norm"]), wm_all, BF16, "mem_kv_proj", after=token)

    h = x
    saved = []
    vecs = []
    for l in range(n_a):
        proj, u_bf = _norm_matmul(h, row(wts["norm_a"][l]), wts["w_in_a"][l], F32, f"a{l}_in_proj")
        vec = jnp.concatenate([row(wts["conv_b"][l]), row(wts["b_rec_gate"][l]), row(wts["b_in_gate"][l]),
                               row(wts["lru_lambda"][l]), wts["conv_w"][l]], axis=0)
        vecs.append(vec)
        y_rnn, h_rnn, xc = _lru_fwd(proj, vec, wts["w_rec_gate"][l], wts["w_in_gate"][l], f"a{l}_lru_fwd")
        y_mem = _mem_attn_fwd(proj, memkv, l, dr, dm, f"a{l}_mem_fwd")
        mix = jnp.concatenate([y_rnn, y_mem], axis=1)
        h_next = _matmul_res(mix, wts["w_out_a"][l], h, f"a{l}_out_proj")
        saved.append((h, proj, u_bf, mix, h_rnn, xc))
        h = h_next

    h_kv = h
    wts = {**wts, **late(h_kv)}
    kv, ukv_bf = _norm_matmul(h_kv, row(wts["kv_norm"]), wts["w_kv"], BF16, "kv_proj")

    for j in range(n_b):
        l = n_a + j
        proj, u_bf = _norm_matmul(h, row(wts["norm_b"][j]), wts["w_in_b"][j], F32, f"b{j}_in_proj")
        y_sb, o_sb, tl_sb = _sb_fwd(proj, kv, f"b{j}_sb_fwd")
        y_mem = _mem_attn_fwd(proj, memkv, l, dr, dm, f"b{j}_mem_fwd")
        mix = jnp.concatenate([y_sb, y_mem], axis=1)
        h_next = _matmul_res(mix, wts["w_out_b"][j], h, f"b{j}_out_proj")
        saved.append((h, proj, u_bf, mix, o_sb, tl_sb))
        h = h_next

    dh, d_final, loss, dh_bf = _final_loss_bwd(h, row(wts["final_norm"]), tgt, "final_loss_bwd")

    grads = {"final_norm": d_final.reshape(-1)}
    big = {}
    dmemkv = [None] * depth
    g_norm_b = [None] * n_b
    dks, dvs = [], []
    for j in reversed(range(n_b)):
        l = n_a + j
        h_in, proj, u_bf, mix, o_sb, tl_sb = saved[l]
        dmix = _matmul_nt(dh_bf, wts["w_out_b"][j], f"b{j}_dmix")
        big["w_out_b"] = _dw_rows(mix, dh_bf, j, n_b, big.get("w_out_b"), f"b{j}_dw_out")
        dq, dg, dk, dv = _sb_bwd(proj, kv, o_sb, tl_sb, dmix, f"b{j}_sb_bwd")
        dqm, dgm, dkm, dvm = _mem_attn_bwd(proj, memkv, dmix, l, dr, dm, f"b{j}_mem_bwd")
        dmemkv[l] = (dkm, dvm)
        dproj = jnp.concatenate([dq, dg, dqm, dgm], axis=1)
        du = _matmul_nt(dproj, wts["w_in_b"][j], f"b{j}_du")
        big["w_in_b"] = _dw_cols(u_bf, dproj, j, n_b, big.get("w_in_b"), f"b{j}_dw_in")
        dh, dgn, dh_bf = _rms_bwd(du, h_in, row(wts["norm_b"][j]), dh, f"b{j}_rms_bwd")
        g_norm_b[j] = dgn.reshape(-1)
        dks.append(dk)
        dvs.append(dv)
    assert n_b == 2
    dkv = jnp.concatenate([_add_cast(dks[0], dks[1], "dk_sum").T, _add_cast(dvs[0], dvs[1], "dv_sum").T], axis=1)
    du = _matmul_nt(dkv, wts["w_kv"], "kv_du")
    big["w_kv"] = _dw_cols(ukv_bf, dkv, 0, 1, None, "kv_dw")
    dh, dgn, dh_bf = _rms_bwd(du, h_kv, row(wts["kv_norm"]), dh, "kv_rms_bwd")
    grads["kv_norm"] = dgn.reshape(-1)

    early_names = [n for n in BIG if n in big]
    early_p = _pair_sums([big[n].reshape(N_CHIPS, -1, big[n].shape[-1]) for n in early_names], early_names)
    send_sems, recv_sems, early_p, early_b, token = _chip_scatter_start(early_p, "grad_chip_scatter_start")

    g_norm_a = [None] * n_a
    g_wr, g_wi, g_vec = [None] * n_a, [None] * n_a, [None] * n_a
    for l in reversed(range(n_a)):
        h_in, proj, u_bf, mix, h_rnn, xc = saved[l]
        dmix = _matmul_nt(dh_bf, wts["w_out_a"][l], f"a{l}_dmix", after=token if l == n_a - 1 else None)
        big["w_out_a"] = _dw_rows(mix, dh_bf, l, n_a, big.get("w_out_a"), f"a{l}_dw_out")
        dx, dg, g_wr[l], g_wi[l], dvec = _lru_bwd(proj, xc, h_rnn, dmix, vecs[l], wts["w_rec_gate"][l],
                                                  wts["w_in_gate"][l], f"a{l}_lru_bwd")
        g_vec[l] = dvec.transpose(1, 0, 2).reshape(8, dr)
        dqm, dgm, dkm, dvm = _mem_attn_bwd(proj, memkv, dmix, l, dr, dm, f"a{l}_mem_bwd")
        dmemkv[l] = (dkm, dvm)
        dproj = jnp.concatenate([dx, dg, dqm, dgm], axis=1)
        du = _matmul_nt(dproj, wts["w_in_a"][l], f"a{l}_du")
        big["w_in_a"] = _dw_cols(u_bf, dproj, l, n_a, big.get("w_in_a"), f"a{l}_dw_in")
        dh, dgn, dh_bf = _rms_bwd(du, h_in, row(wts["norm_a"][l]), dh, f"a{l}_rms_bwd")
        g_norm_a[l] = dgn.reshape(-1)

    dmemkv_all = jnp.concatenate([jnp.concatenate(p, axis=1) for p in dmemkv], axis=1).astype(BF16)
    pk = d // N_CHIPS
    big["w_mem_kv"] = _matmul_tn(memn_bf, dmemkv_all, "mem_dw", pk, 2 * dm, (N_CHIPS, depth, pk, 2 * dm),
                                 (None, None, pk, 2 * dm), lambda i, j: (i, j, 0, 0))
    dmemn = _matmul_nt(dmemkv_all, wm_all, "mem_du")
    _, dgn, _ = _rms_bwd(dmemn, mem, row(wts["mem_norm"]), jnp.zeros_like(mem), "mem_rms_bwd")
    grads["mem_norm"] = dgn.reshape(-1)
    grads["norm_a"] = jnp.stack(g_norm_a)
    grads["w_rec_gate"] = jnp.stack(g_wr)
    grads["w_in_gate"] = jnp.stack(g_wi)
    gv = jnp.stack(g_vec)
    grads["conv_b"], grads["b_rec_gate"], grads["b_in_gate"], grads["lru_lambda"] = gv[:, 0], gv[:, 1], gv[:, 2], gv[:, 3]
    grads["conv_w"] = gv[:, 4:8]
    grads["norm_b"] = jnp.stack(g_norm_b)
    early_b = _chip_scatter_wait(send_sems, recv_sems, early_p, early_b, dh, "grad_chip_scatter_wait")
    early = (early_names, early_p, early_b)
    big = {n: g.reshape(N_CHIPS, -1, g.shape[-1]) for n, g in big.items() if n not in early_names}
    return loss, dh, grads, big, early


def kernel(x, mem, mem_norm, w_mem_kv, norm_a, w_in_a, conv_w, conv_b, w_rec_gate, b_rec_gate, w_in_gate, b_in_gate, lru_lambda, w_out_a, kv_norm, w_kv, norm_b, w_in_b, w_out_b, final_norm, loss_target, m_mem_norm, m_w_mem_kv, m_norm_a, m_w_in_a, m_conv_w, m_conv_b, m_w_rec_gate, m_b_rec_gate, m_w_in_gate, m_b_in_gate, m_lru_lambda, m_w_out_a, m_kv_norm, m_w_kv, m_norm_b, m_w_in_b, m_w_out_b, m_final_norm, v_mem_norm, v_w_mem_kv, v_norm_a, v_w_in_a, v_conv_w, v_conv_b, v_w_rec_gate, v_b_rec_gate, v_w_in_gate, v_b_in_gate, v_lru_lambda, v_w_out_a, v_kv_norm, v_w_kv, v_norm_b, v_w_in_b, v_w_out_b, v_final_norm):
    local = dict(mem_norm=mem_norm, w_mem_kv=w_mem_kv, norm_a=norm_a, w_in_a=w_in_a, conv_w=conv_w, conv_b=conv_b,
                 w_rec_gate=w_rec_gate, b_rec_gate=b_rec_gate, w_in_gate=w_in_gate, b_in_gate=b_in_gate,
                 lru_lambda=lru_lambda, w_out_a=w_out_a, kv_norm=kv_norm, w_kv=w_kv, norm_b=norm_b, w_in_b=w_in_b,
                 w_out_b=w_out_b, final_norm=final_norm)
    mom = dict(mem_norm=m_mem_norm, w_mem_kv=m_w_mem_kv, norm_a=m_norm_a, w_in_a=m_w_in_a, conv_w=m_conv_w,
               conv_b=m_conv_b, w_rec_gate=m_w_rec_gate, b_rec_gate=m_b_rec_gate, w_in_gate=m_w_in_gate,
               b_in_gate=m_b_in_gate, lru_lambda=m_lru_lambda, w_out_a=m_w_out_a, kv_norm=m_kv_norm, w_kv=m_w_kv,
               norm_b=m_norm_b, w_in_b=m_w_in_b, w_out_b=m_w_out_b, final_norm=m_final_norm)
    var = dict(mem_norm=v_mem_norm, w_mem_kv=v_w_mem_kv, norm_a=v_norm_a, w_in_a=v_w_in_a, conv_w=v_conv_w,
               conv_b=v_conv_b, w_rec_gate=v_w_rec_gate, b_rec_gate=v_b_rec_gate, w_in_gate=v_w_in_gate,
               b_in_gate=v_b_in_gate, lru_lambda=v_lru_lambda, w_out_a=v_w_out_a, kv_norm=v_kv_norm, w_kv=v_w_kv,
               norm_b=v_norm_b, w_in_b=v_w_in_b, w_out_b=v_w_out_b, final_norm=v_final_norm)

    wts, token, late = _gather_weights(local)
    for n in WEIGHTS:
        if SHARD_DIM[n] is None:
            wts[n] = local[n]
    wts["w_rec_gate"] = wts["w_rec_gate"].astype(BF16)
    wts["w_in_gate"] = wts["w_in_gate"].astype(BF16)

    loss, grad_x, grads, big, (early_names, early_p, early_b) = _local_grads(x[0], mem[0], loss_target[0], wts, token, late)

    rest = [n for n in WEIGHTS if n not in BIG]
    row_multiple = 2 * FLAT_TR
    s_rest = jnp.stack([_flatten([_piece(grads[n], n, k) for n in rest], row_multiple) for k in range(N_CHIPS)])
    late_names = [n for n in BIG if n in big] + ["rest"]
    late_p = _pair_sums([big[n] for n in late_names[:-1]] + [s_rest], late_names)
    send_sems, recv_sems, late_p, late_b, token = _chip_scatter_start(late_p, "grad_late_scatter_start")

    g_out, d_out, m_out, v_out = {}, {}, {}, {}

    def update(n, g):
        shape = local[n].shape
        flat = lambda a: a.reshape(-1, shape[-1])
        d, mo, vo = _adamw(g, flat(local[n]), flat(mom[n]), flat(var[n]), f"adamw_{n}")
        g_out[n], d_out[n], m_out[n], v_out[n] = (a.reshape(shape) for a in (g, d, mo, vo))
        return d

    for n, g in zip(early_names, _finish_reduce(early_p, early_b, early_names, after=token)):
        done = update(n, g)
    late_b = _chip_scatter_wait(send_sems, recv_sems, late_p, late_b, done, "grad_late_scatter_wait")
    reduced = dict(zip(late_names, _finish_reduce(late_p, late_b, late_names)))
    for n in late_names[:-1]:
        update(n, reduced[n])
    g_rest = reduced["rest"]
    d_rest, m_rest, v_rest = _adamw(g_rest, *(_flatten([src[n] for n in rest], row_multiple) for src in (local, mom, var)),
                                    "adamw_rest")
    shapes = [local[n].shape for n in rest]
    for out, flat2d in ((g_out, g_rest), (d_out, d_rest), (m_out, m_rest), (v_out, v_rest)):
        out.update(zip(rest, _unflatten(flat2d, shapes)))

    total_loss = lax.psum(loss[0, 0], MESH_AXES)
    return (total_loss, grad_x[None], *[g_out[n] for n in WEIGHTS], *[d_out[n] for n in WEIGHTS],
            *[m_out[n] for n in WEIGHTS], *[v_out[n] for n in WEIGHTS])
```

```python
import functools

import jax
import jax.numpy as jnp
from jax import lax
from jax.experimental import pallas as pl
from jax.experimental.pallas import tpu as pltpu

F32 = jnp.float32
BF16 = jnp.bfloat16

RMS_EPS = 1e-6
LRU_C = 8.0
ADAM_LR = 0.001
ADAM_B1 = 0.9
ADAM_B2 = 0.999
ADAM_EPS = 1e-08
ADAM_WD = 0.01
ADAM_STEP = 10

LANES = 128
VMEM_LIMIT = 56 * 1024 * 1024
FLAT_W = 1024
FLAT_TR = 256
N_CHIPS = 4
MESH_AXES = ("x", "y", "c")

_NT = (((1,), (1,)), ((), ()))
_TN = (((0,), (0,)), ((), ()))
ANY = pl.BlockSpec(memory_space=pl.ANY)
HBM = pl.BlockSpec(memory_space=pltpu.HBM)
SEM = pl.BlockSpec(memory_space=pltpu.SEMAPHORE)
EFFECT = pltpu.SideEffectType.DATAFLOW_SIDE_EFFECTING
MESH = pl.DeviceIdType.MESH


def _cparams(n_axes):
    return pltpu.CompilerParams(dimension_semantics=("arbitrary",) * n_axes, vmem_limit_bytes=VMEM_LIMIT)


def _sigmoid(x):
    return 1.0 / (1.0 + jnp.exp(-x))


def _log1p_pos(e):
    return jnp.where(e < 1e-3, e * (1.0 - e * (0.5 - e * (1.0 / 3.0))), jnp.log(1.0 + e))


def _neg_expm1(x):
    small = -x * (1.0 + x * (0.5 + x * (1.0 / 6.0 + x * (1.0 / 24.0))))
    return jnp.where(x > -0.05, small, 1.0 - jnp.exp(x))


def _tile(n, want):
    if n <= want:
        return n
    t = want
    while n % t:
        t -= LANES
    assert t > 0, (n, want)
    return t


def _norm_matmul(x, g, w, out_dtype, name, after=None):
    m, k = x.shape
    n = w.shape[1]
    tm, tn = _tile(m, 1024), _tile(n, 1024)

    def body(x_ref, g_ref, w_ref, *rest):
        o_ref, u_ref = rest[-2:]

        @pl.when(pl.program_id(1) == 0)
        def _():
            xf = x_ref[...]
            r = lax.rsqrt(jnp.mean(xf * xf, axis=-1, keepdims=True) + RMS_EPS)
            u_ref[...] = ((xf * r) * g_ref[...]).astype(BF16)

        o_ref[...] = jnp.dot(u_ref[...], w_ref[...], preferred_element_type=F32).astype(o_ref.dtype)

    return pl.pallas_call(
        body, name=name, grid=(m // tm, n // tn),
        in_specs=[pl.BlockSpec((tm, k), lambda i, j: (i, 0)), pl.BlockSpec((1, k), lambda i, j: (0, 0)),
                  pl.BlockSpec((k, tn), lambda i, j: (0, j))] + ([] if after is None else [ANY]),
        out_specs=[pl.BlockSpec((tm, tn), lambda i, j: (i, j)), pl.BlockSpec((tm, k), lambda i, j: (i, 0))],
        out_shape=[jax.ShapeDtypeStruct((m, n), out_dtype), jax.ShapeDtypeStruct((m, k), BF16)],
        compiler_params=_cparams(2),
    )(x, g, w, *([] if after is None else [after]))


def _matmul_res(a, b, res, name):
    m, k = a.shape
    n = b.shape[1]
    tm, tn = _tile(m, 1024), _tile(n, 1024)

    def body(a_ref, b_ref, r_ref, o_ref):
        o_ref[...] = r_ref[...] + jnp.dot(a_ref[...], b_ref[...], preferred_element_type=F32)

    return pl.pallas_call(
        body, name=name, grid=(m // tm, n // tn),
        in_specs=[pl.BlockSpec((tm, k), lambda i, j: (i, 0)), pl.BlockSpec((k, tn), lambda i, j: (0, j)),
                  pl.BlockSpec((tm, tn), lambda i, j: (i, j))],
        out_specs=pl.BlockSpec((tm, tn), lambda i, j: (i, j)),
        out_shape=jax.ShapeDtypeStruct((m, n), F32),
        compiler_params=_cparams(2),
    )(a, b, res)


def _matmul_nt(a, b, name, after=None):
    m, n = a.shape
    k = b.shape[0]
    tm, tk = _tile(m, 1024), _tile(k, 512)

    def body(a_ref, b_ref, *rest):
        o_ref = rest[-1]
        o_ref[...] = lax.dot_general(a_ref[...].astype(BF16), b_ref[...], _NT, preferred_element_type=F32)

    in_specs = [pl.BlockSpec((tm, n), lambda i, j: (i, 0)), pl.BlockSpec((tk, n), lambda i, j: (j, 0))]
    args = [a, b]
    if after is not None:
        in_specs.append(ANY)
        args.append(after)
    return pl.pallas_call(
        body, name=name, grid=(m // tm, k // tk), in_specs=in_specs,
        out_specs=pl.BlockSpec((tm, tk), lambda i, j: (i, j)),
        out_shape=jax.ShapeDtypeStruct((m, k), F32),
        compiler_params=_cparams(2),
    )(*args)


def _matmul_tn(a, b, name, tk, tn, out_shape, out_block, out_index, into=None):
    m, k = a.shape
    n = b.shape[1]
    tm = _tile(m, 2048 if b.dtype == BF16 else 1024)

    def body(a_ref, b_ref, *rest):
        o_ref = rest[-1]
        part = lax.dot_general(a_ref[...].astype(BF16), b_ref[...].astype(BF16), _TN, preferred_element_type=F32)

        @pl.when(pl.program_id(2) == 0)
        def _():
            o_ref[...] = part

        @pl.when(pl.program_id(2) != 0)
        def _():
            o_ref[...] += part

    in_specs = [pl.BlockSpec((tm, tk), lambda i, j, s: (s, i)), pl.BlockSpec((tm, tn), lambda i, j, s: (s, j))]
    args = [a, b]
    if into is not None:
        in_specs.append(ANY)
        args.append(into)
    return pl.pallas_call(
        body, name=name, grid=(k // tk, n // tn, m // tm), in_specs=in_specs,
        out_specs=pl.BlockSpec(out_block, lambda i, j, s: out_index(i, j)),
        out_shape=jax.ShapeDtypeStruct(out_shape, F32),
        input_output_aliases={} if into is None else {2: 0},
        compiler_params=_cparams(3),
    )(*args)


def _dw_cols(a, b, layer, n_layers, into, name):
    k, n = a.shape[1], b.shape[1]
    pn = n // N_CHIPS
    tk = _tile(k, 512)
    return _matmul_tn(a, b, name, tk, pn, (N_CHIPS, n_layers, k, pn), (None, None, tk, pn),
                      lambda i, j: (j, layer, i, 0), into)


def _dw_rows(a, b, layer, n_layers, into, name):
    k, n = a.shape[1], b.shape[1]
    pk = k // N_CHIPS
    tn = _tile(n, 2048)
    return _matmul_tn(a, b, name, pk, tn, (N_CHIPS, n_layers, pk, n), (None, None, pk, tn),
                      lambda i, j: (i, layer, 0, j), into)


def _rms_bwd(du, h, g, dres, name):
    m, d = h.shape
    tm = _tile(m, 256)

    def body(du_ref, h_ref, g_ref, dres_ref, dx_ref, dg_ref, dxb_ref):
        xf = h_ref[...]
        r = lax.rsqrt(jnp.mean(xf * xf, axis=-1, keepdims=True) + RMS_EPS)
        xhat = xf * r
        du_v = du_ref[...]
        dxn = du_v * g_ref[...]
        dx = dres_ref[...] + r * (dxn - xhat * jnp.mean(dxn * xhat, axis=-1, keepdims=True))
        dx_ref[...] = dx
        dxb_ref[...] = dx.astype(BF16)
        part = jnp.sum(du_v * xhat, axis=0, keepdims=True)

        @pl.when(pl.program_id(0) == 0)
        def _():
            dg_ref[...] = part

        @pl.when(pl.program_id(0) != 0)
        def _():
            dg_ref[...] += part

    row = lambda i: (i, 0)
    return pl.pallas_call(
        body, name=name, grid=(m // tm,),
        in_specs=[pl.BlockSpec((tm, d), row), pl.BlockSpec((tm, d), row), pl.BlockSpec((1, d), lambda i: (0, 0)),
                  pl.BlockSpec((tm, d), row)],
        out_specs=[pl.BlockSpec((tm, d), row), pl.BlockSpec((1, d), lambda i: (0, 0)), pl.BlockSpec((tm, d), row)],
        out_shape=[jax.ShapeDtypeStruct((m, d), F32), jax.ShapeDtypeStruct((1, d), F32),
                   jax.ShapeDtypeStruct((m, d), BF16)],
        compiler_params=_cparams(1),
    )(du, h, g, dres)


def _final_loss_bwd(h, g, tgt, name):
    m, d = h.shape
    tm = _tile(m, 256)

    def body(h_ref, g_ref, t_ref, dx_ref, dg_ref, loss_ref, dxb_ref):
        xf = h_ref[...]
        r = lax.rsqrt(jnp.mean(xf * xf, axis=-1, keepdims=True) + RMS_EPS)
        xhat = xf * r
        gv = g_ref[...]
        err = xhat * gv - t_ref[...]
        dy = err * (1.0 / d)
        dxn = dy * gv
        dx = r * (dxn - xhat * jnp.mean(dxn * xhat, axis=-1, keepdims=True))
        dx_ref[...] = dx
        dxb_ref[...] = dx.astype(BF16)
        part = jnp.sum(dy * xhat, axis=0, keepdims=True)
        lpart = jnp.sum(jnp.sum(err * err, axis=0, keepdims=True), axis=1, keepdims=True) * (0.5 / d)

        @pl.when(pl.program_id(0) == 0)
        def _():
            dg_ref[...] = part
            loss_ref[...] = lpart

        @pl.when(pl.program_id(0) != 0)
        def _():
            dg_ref[...] += part
            loss_ref[...] += lpart

    row = lambda i: (i, 0)
    fixed = lambda i: (0, 0)
    return pl.pallas_call(
        body, name=name, grid=(m // tm,),
        in_specs=[pl.BlockSpec((tm, d), row), pl.BlockSpec((1, d), fixed), pl.BlockSpec((tm, d), row)],
        out_specs=[pl.BlockSpec((tm, d), row), pl.BlockSpec((1, d), fixed), pl.BlockSpec((1, 1), fixed),
                   pl.BlockSpec((tm, d), row)],
        out_shape=[jax.ShapeDtypeStruct((m, d), F32), jax.ShapeDtypeStruct((1, d), F32),
                   jax.ShapeDtypeStruct((1, 1), F32), jax.ShapeDtypeStruct((m, d), BF16)],
        compiler_params=_cparams(1),
    )(h, g, tgt)


def _add_cast(a, b, name):
    m, n = a.shape
    tm, tn = _tile(m, 512), _tile(n, 2048)

    def body(a_ref, b_ref, o_ref):
        o_ref[...] = (a_ref[...] + b_ref[...]).astype(BF16)

    blk = lambda i, j: (i, j)
    return pl.pallas_call(
        body, name=name, grid=(m // tm, n // tn),
        in_specs=[pl.BlockSpec((tm, tn), blk), pl.BlockSpec((tm, tn), blk)],
        out_specs=pl.BlockSpec((tm, tn), blk),
        out_shape=jax.ShapeDtypeStruct((m, n), BF16),
        compiler_params=_cparams(2),
    )(a, b)


def _mem_attn_fwd(proj, memkv, layer, dr, dm, name):
    t = proj.shape[0]
    nm = memkv.shape[0]
    tm = _tile(t, 512)
    nh = dm // LANES
    scale = LANES ** -0.5
    qb = (2 * dr) // dm

    def body(q_ref, g_ref, k_ref, v_ref, y_ref):
        heads = [slice(hh * LANES, (hh + 1) * LANES) for hh in range(nh)]
        ss = [lax.dot_general(q_ref[:, sl].astype(BF16), k_ref[:, sl], _NT, preferred_element_type=F32) * scale
              for sl in heads]
        ps = [jnp.exp(s - jnp.max(s, axis=-1, keepdims=True)) for s in ss]
        ps = [p / jnp.sum(p, axis=-1, keepdims=True) for p in ps]
        os = [jnp.dot(p.astype(BF16), v_ref[:, sl], preferred_element_type=F32) for p, sl in zip(ps, heads)]
        for o, sl in zip(os, heads):
            gv = g_ref[:, sl]
            y_ref[:, sl] = (o * (gv * _sigmoid(gv))).astype(BF16)

    return pl.pallas_call(
        body, name=name, grid=(t // tm,),
        in_specs=[pl.BlockSpec((tm, dm), lambda i: (i, qb)), pl.BlockSpec((tm, dm), lambda i: (i, qb + 1)),
                  pl.BlockSpec((nm, dm), lambda i: (0, 2 * layer)), pl.BlockSpec((nm, dm), lambda i: (0, 2 * layer + 1))],
        out_specs=pl.BlockSpec((tm, dm), lambda i: (i, 0)),
        out_shape=jax.ShapeDtypeStruct((t, dm), BF16),
        compiler_params=_cparams(1),
    )(proj, proj, memkv, memkv)


def _mem_attn_bwd(proj, memkv, dmix, layer, dr, dm, name):
    t = proj.shape[0]
    nm = memkv.shape[0]
    tm = _tile(t, 512)
    nh = dm // LANES
    scale = LANES ** -0.5
    qb = (2 * dr) // dm
    yb = dr // dm

    def body(q_ref, g_ref, k_ref, v_ref, dy_ref, dq_ref, dg_ref, dk_ref, dv_ref):
        @pl.when(pl.program_id(0) == 0)
        def _():
            dk_ref[...] = jnp.zeros_like(dk_ref)
            dv_ref[...] = jnp.zeros_like(dv_ref)

        heads = [slice(hh * LANES, (hh + 1) * LANES) for hh in range(nh)]
        qs = [q_ref[:, sl].astype(BF16) for sl in heads]
        ss = [lax.dot_general(q, k_ref[:, sl], _NT, preferred_element_type=F32) * scale for q, sl in zip(qs, heads)]
        dos = []
        for sl in heads:
            gv = g_ref[:, sl]
            dos.append((dy_ref[:, sl] * (gv * _sigmoid(gv))).astype(BF16))
        dps = [lax.dot_general(do_bf, v_ref[:, sl], _NT, preferred_element_type=F32) for do_bf, sl in zip(dos, heads)]
        ps = [jnp.exp(s - jnp.max(s, axis=-1, keepdims=True)) for s in ss]
        ps = [p / jnp.sum(p, axis=-1, keepdims=True) for p in ps]
        pbs = [p.astype(BF16) for p in ps]
        os = [jnp.dot(p_bf, v_ref[:, sl], preferred_element_type=F32) for p_bf, sl in zip(pbs, heads)]
        dss = [(p * (dp - jnp.sum(dp * p, axis=-1, keepdims=True)) * scale).astype(BF16) for p, dp in zip(ps, dps)]
        for sl, q, o, p_bf, do_bf, ds in zip(heads, qs, os, pbs, dos, dss):
            gv = g_ref[:, sl]
            sg = _sigmoid(gv)
            dg_ref[:, sl] = (dy_ref[:, sl] * o * (sg * (1.0 + gv * (1.0 - sg)))).astype(BF16)
            dv_ref[:, sl] += lax.dot_general(p_bf, do_bf, _TN, preferred_element_type=F32)
            dq_ref[:, sl] = jnp.dot(ds, k_ref[:, sl], preferred_element_type=F32).astype(BF16)
            dk_ref[:, sl] += lax.dot_general(ds, q, _TN, preferred_element_type=F32)

    fixed = lambda i: (0, 0)
    return pl.pallas_call(
        body, name=name, grid=(t // tm,),
        in_specs=[pl.BlockSpec((tm, dm), lambda i: (i, qb)), pl.BlockSpec((tm, dm), lambda i: (i, qb + 1)),
                  pl.BlockSpec((nm, dm), lambda i: (0, 2 * layer)), pl.BlockSpec((nm, dm), lambda i: (0, 2 * layer + 1)),
                  pl.BlockSpec((tm, dm), lambda i: (i, yb))],
        out_specs=[pl.BlockSpec((tm, dm), lambda i: (i, 0)), pl.BlockSpec((tm, dm), lambda i: (i, 0)),
                   pl.BlockSpec((nm, dm), fixed), pl.BlockSpec((nm, dm), fixed)],
        out_shape=[jax.ShapeDtypeStruct((t, dm), BF16), jax.ShapeDtypeStruct((t, dm), BF16),
                   jax.ShapeDtypeStruct((nm, dm), F32), jax.ShapeDtypeStruct((nm, dm), F32)],
        compiler_params=_cparams(1),
    )(proj, proj, memkv, memkv, dmix)


LRU_CHUNK = 256


def _lru_gates(xc, vec, wr_ref, wi_ref):
    r = _sigmoid(jnp.dot(xc.astype(BF16), wr_ref[...], preferred_element_type=F32) + vec[1:2])
    i = _sigmoid(jnp.dot(xc.astype(BF16), wi_ref[...], preferred_element_type=F32) + vec[2:3])
    lam = vec[3:4]
    cl = -LRU_C * (jnp.maximum(-lam, 0.0) + _log1p_pos(jnp.exp(-jnp.abs(lam))))
    la = cl * r
    a = jnp.exp(la)
    s2 = _neg_expm1(2.0 * la)
    return r, i, cl, a, s2


def _lru_fwd(proj, vec, wr, wi, name):
    t = proj.shape[0]
    nb = wr.shape[0]
    dr = nb * LANES
    c = _tile(t, LRU_CHUNK)
    per = 2 if nb % 2 == 0 else 1
    w = per * LANES
    ng = nb // per

    def body(x_ref, g_ref, vec_ref, wr_ref, wi_ref, y_ref, h_ref, xc_ref, carry_ref, xprev_ref):
        @pl.when(pl.program_id(1) == 0)
        def _():
            carry_ref[...] = jnp.zeros_like(carry_ref)
            xprev_ref[...] = jnp.zeros_like(xprev_ref)

        rows = lax.broadcasted_iota(jnp.int32, (c, LANES), 0)
        for j in range(per):
            lanes = slice(j * LANES, (j + 1) * LANES)
            x = x_ref[:, lanes]
            vec = vec_ref[:, lanes]
            xprev = xprev_ref[:, lanes]
            xc = vec[7:8] * x + vec[0:1]
            for k in range(1, 4):
                xs = jnp.where(rows < k, pltpu.roll(xprev, k, 0), pltpu.roll(x, k, 0))
                xc = xc + vec[7 - k:8 - k] * xs
            xprev_ref[:, lanes] = x
            xc_ref[:, lanes] = xc

            r, i, cl, a, s2 = _lru_gates(xc, vec, wr_ref.at[j], wi_ref.at[j])
            hh = jnp.sqrt(s2) * (i * xc)
            aa = a
            d = 1
            while d < c:
                keep = rows >= d
                hh = jnp.where(keep, aa * pltpu.roll(hh, d, 0) + hh, hh)
                aa = jnp.where(keep, aa * pltpu.roll(aa, d, 0), aa)
                d *= 2
            hfull = hh + aa * carry_ref[7:8, lanes]
            carry_ref[:, lanes] = hfull[c - 8:c, :]
            h_ref[:, lanes] = hfull
            gv = g_ref[:, lanes]
            y_ref[:, lanes] = (hfull * (gv * _sigmoid(gv))).astype(BF16)

    blk = lambda n, s: (s, n)
    return pl.pallas_call(
        body, name=name, grid=(ng, t // c),
        in_specs=[pl.BlockSpec((c, w), blk), pl.BlockSpec((c, w), lambda n, s: (s, ng + n)),
                  pl.BlockSpec((8, w), lambda n, s: (0, n)),
                  pl.BlockSpec((per, LANES, LANES), lambda n, s: (n, 0, 0)),
                  pl.BlockSpec((per, LANES, LANES), lambda n, s: (n, 0, 0))],
        out_specs=[pl.BlockSpec((c, w), blk)] * 3,
        out_shape=[jax.ShapeDtypeStruct((t, dr), BF16), jax.ShapeDtypeStruct((t, dr), F32),
                   jax.ShapeDtypeStruct((t, dr), F32)],
        scratch_shapes=[pltpu.VMEM((8, w), F32), pltpu.VMEM((c, w), F32)],
        compiler_params=_cparams(2),
    )(proj, proj, vec, wr, wi)


def _lru_bwd(proj, xc_all, h_all, dmix, vec, wr, wi, name):
    t = proj.shape[0]
    nb = wr.shape[0]
    dr = nb * LANES
    c = _tile(t, LRU_CHUNK)
    nc = t // c
    per = 2 if nb % 2 == 0 else 1
    w = per * LANES
    ng = nb // per

    def body(x_ref, g_ref, xc_ref, h_ref, dy_ref, vec_ref, wr_ref, wi_ref,
             dx_ref, dg_ref, dwr_ref, dwi_ref, dvec_ref, qcarry_ref, dxc_next_ref):
        @pl.when(pl.program_id(1) == 0)
        def _():
            qcarry_ref[...] = jnp.zeros_like(qcarry_ref)
            dxc_next_ref[...] = jnp.zeros_like(dxc_next_ref)
            dwr_ref[...] = jnp.zeros_like(dwr_ref)
            dwi_ref[...] = jnp.zeros_like(dwi_ref)
            dvec_ref[...] = jnp.zeros_like(dvec_ref)

        rows = lax.broadcasted_iota(jnp.int32, (c, LANES), 0)
        for j in range(per):
            lanes = slice(j * LANES, (j + 1) * LANES)
            x = x_ref[:, lanes]
            xc = xc_ref[:, lanes]
            h = h_ref[:, lanes]
            dy = dy_ref[:, lanes]
            gv = g_ref[:, lanes]
            vec = vec_ref[:, lanes]

            r, i, cl, a, s2 = _lru_gates(xc, vec, wr_ref.at[j], wi_ref.at[j])
            s = jnp.sqrt(s2)
            ixc = i * xc
            u = s * ixc
            sg = _sigmoid(gv)
            dh = dy * (gv * sg)
            dg_ref[:, lanes] = (dy * h * (sg * (1.0 + gv * (1.0 - sg)))).astype(BF16)

            aa = a
            qq = a * dh
            d = 1
            while d < c:
                keep = rows < c - d
                qq = jnp.where(keep, qq + aa * pltpu.roll(qq, c - d, 0), qq)
                aa = jnp.where(keep, aa * pltpu.roll(aa, c - d, 0), aa)
                d *= 2
            qin = qcarry_ref[0:1, lanes]
            qfull = qq + aa * qin
            gt = dh + jnp.where(rows == c - 1, qin, pltpu.roll(qfull, c - 1, 0))
            qcarry_ref[:, lanes] = qfull[0:8, :]

            dla = gt * (h - u) - gt * ixc * (a * a) / s
            dixc = gt * s
            di = dixc * xc
            dxc = dixc * i
            dzr = (dla * cl) * (r * (1.0 - r))
            dzi = di * (i * (1.0 - i))
            dzr_bf = dzr.astype(BF16)
            dzi_bf = dzi.astype(BF16)
            dxc = dxc + lax.dot_general(dzr_bf, wr_ref[j], _NT, preferred_element_type=F32)
            dxc = dxc + lax.dot_general(dzi_bf, wi_ref[j], _NT, preferred_element_type=F32)
            xc_bf = xc.astype(BF16)
            dwr_ref[j] += lax.dot_general(xc_bf, dzr_bf, _TN, preferred_element_type=F32)
            dwi_ref[j] += lax.dot_general(xc_bf, dzi_bf, _TN, preferred_element_type=F32)

            lam = vec[3:4]
            dlam = jnp.sum(dla * r, axis=0, keepdims=True) * (LRU_C * _sigmoid(-lam))
            colsum = lambda v: jnp.sum(v, axis=0, keepdims=True)
            dxn = dxc_next_ref[:, lanes]
            dx = vec[7:8] * dxc
            dtaps = [None] * 4
            dtaps[3] = colsum(x * dxc)
            for k in range(1, 4):
                sh = jnp.where(rows < c - k, pltpu.roll(dxc, c - k, 0), pltpu.roll(dxn, c - k, 0))
                dx = dx + vec[7 - k:8 - k] * sh
                dtaps[3 - k] = colsum(x * sh)
            dxc_next_ref[:, lanes] = dxc
            dx_ref[:, lanes] = dx.astype(BF16)
            dvec_ref[j] += jnp.concatenate([colsum(dxc), colsum(dzr), colsum(dzi), dlam] + dtaps, axis=0)

    rev = lambda n, s: (nc - 1 - s, n)
    sq = lambda n, s: (n, 0, 0)
    return pl.pallas_call(
        body, name=name, grid=(ng, nc),
        in_specs=[pl.BlockSpec((c, w), rev), pl.BlockSpec((c, w), lambda n, s: (nc - 1 - s, ng + n)),
                  pl.BlockSpec((c, w), rev), pl.BlockSpec((c, w), rev), pl.BlockSpec((c, w), rev),
                  pl.BlockSpec((8, w), lambda n, s: (0, n)),
                  pl.BlockSpec((per, LANES, LANES), sq), pl.BlockSpec((per, LANES, LANES), sq)],
        out_specs=[pl.BlockSpec((c, w), rev), pl.BlockSpec((c, w), rev),
                   pl.BlockSpec((per, LANES, LANES), sq), pl.BlockSpec((per, LANES, LANES), sq),
                   pl.BlockSpec((per, 8, LANES), sq)],
        out_shape=[jax.ShapeDtypeStruct((t, dr), BF16), jax.ShapeDtypeStruct((t, dr), BF16),
                   jax.ShapeDtypeStruct((nb, LANES, LANES), F32), jax.ShapeDtypeStruct((nb, LANES, LANES), F32),
                   jax.ShapeDtypeStruct((nb, 8, LANES), F32)],
        scratch_shapes=[pltpu.VMEM((8, w), F32), pltpu.VMEM((c, w), F32)],
        compiler_params=_cparams(2),
    )(proj, proj, xc_all, h_all, dmix, vec, wr, wi)


SB_TQ = 1024
SB_TK = 256


def _sb_softplus(z, diag):
    sp = jnp.maximum(z, 0.0) + jnp.log(1.0 + jnp.exp(-jnp.abs(z)))
    mask = None
    if diag:
        mask = lax.broadcasted_iota(jnp.int32, z.shape, 1) < lax.broadcasted_iota(jnp.int32, z.shape, 0)
        sp = jnp.where(mask, sp, 0.0)
    return sp, mask


def _split_dot(v, m):
    hi = v.astype(BF16)
    lo = (v - hi.astype(F32)).astype(BF16)
    return jnp.dot(hi, m, preferred_element_type=F32) + jnp.dot(lo, m, preferred_element_type=F32)


def _tri_ones(kind, tk):
    jj = lax.broadcasted_iota(jnp.int32, (tk, tk), 0)
    ss = lax.broadcasted_iota(jnp.int32, (tk, tk), 1)
    rel = {"ge": jj >= ss, "le": jj <= ss}[kind]
    return jnp.where(rel, 1.0, 0.0).astype(BF16)


def _sb_fwd(proj, kv, name):
    t = proj.shape[0]
    ds = kv.shape[1] // 2
    nh = ds // LANES
    tq = _tile(t, SB_TQ)
    tk = _tile(tq, SB_TK)
    nd = tq // tk
    assert nd % 2 == 0 or t == tq
    scale = LANES ** -0.5

    def body(q_ref, g_ref, k_ref, v_ref, y_ref, o_ref, tl_ref, qbf_ref, acc_ref, run_ref, z_ref, w_ref):
        qi = pl.program_id(1)
        qbf_ref[...] = q_ref[...].astype(BF16)
        tri = _tri_ones("ge", tk)
        acc_ref[...] = jnp.zeros_like(acc_ref)
        run_ref[...] = jnp.zeros_like(run_ref)
        all_rows = [slice(s0, s0 + tk) for s0 in range(0, tq, tk)]

        def weights(zs, groups):
            sps = [_sb_softplus(z, dg) for z, (_, dg) in zip(zs, groups)]
            cums = [_split_dot(sp, tri) for sp, _ in sps]
            ws = []
            for z, (rows, dg), (_, mask), cum in zip(zs, groups, sps, cums):
                run = run_ref[rows, :]
                w = jnp.exp(z - cum - run)
                if dg:
                    w = jnp.where(mask, w, 0.0)
                run_ref[rows, :] = run + cum[:, 0:1]
                ws.append(w.astype(BF16))
            return ws

        n = qi * nd

        def rows_from(u):
            return [slice(s0, s0 + tk) for s0 in range(u * tk, tq, tk)]

        def logits_into(slot, kb, rows_list):
            k = k_ref[pl.ds(pl.multiple_of(kb * tk, tk), tk), :]
            seen = slice(rows_list[0].start, tq)
            z_ref[slot, seen, :] = lax.dot_general(qbf_ref[seen, :], k, _NT, preferred_element_type=F32) * scale

        def add_values(kb, rows_list):
            v = v_ref[pl.ds(pl.multiple_of(kb * tk, tk), tk), :]
            seen = slice(rows_list[0].start, tq)
            acc_ref[seen, :] += jnp.dot(w_ref[seen, :], v, preferred_element_type=F32)

        def weigh(slot, groups):
            for (rows, _), w in zip(groups, weights([z_ref[slot, rows, :] for rows, _ in groups], groups)):
                w_ref[rows, :] = w

        logits_into(0, n + nd - 1, rows_from(nd - 1))
        for i, u in enumerate(reversed(range(nd))):
            slot = i % 2
            if i > 0:
                add_values(n + u + 1, rows_from(u + 1))
            if u > 0:
                logits_into(1 - slot, n + u - 1, rows_from(u - 1))
            else:
                logits_into(1 - slot, jnp.maximum(n - 1, 0), all_rows)
            weigh(slot, [(rows, j == 0) for j, rows in enumerate(rows_from(u))])

        def half_step(j, slot):
            kb = n - 1 - j
            add_values(kb + 1, all_rows)
            logits_into(1 - slot, jnp.maximum(kb - 1, 0), all_rows)
            weigh(slot, [(rows, False) for rows in all_rows])

        def step(i, carry):
            half_step(2 * i, nd % 2)
            half_step(2 * i + 1, 1 - nd % 2)
            return carry

        lax.fori_loop(0, n // 2, step, 0)
        add_values(0, all_rows)
        o = acc_ref[...]
        o_ref[...] = o
        tl_ref[...] = jnp.broadcast_to(run_ref[...], (tq, LANES))
        gv = g_ref[...]
        y_ref[...] = (o * (gv * _sigmoid(gv))).astype(BF16)

    blk = lambda h, i: (i, h)
    return pl.pallas_call(
        body, name=name, grid=(nh, t // tq),
        in_specs=[pl.BlockSpec((tq, LANES), blk), pl.BlockSpec((tq, LANES), lambda h, i: (i, nh + h)),
                  pl.BlockSpec((t, LANES), lambda h, i: (0, h)), pl.BlockSpec((t, LANES), lambda h, i: (0, nh + h))],
        out_specs=[pl.BlockSpec((tq, LANES), blk)] * 3,
        out_shape=[jax.ShapeDtypeStruct((t, ds), BF16), jax.ShapeDtypeStruct((t, ds), F32),
                   jax.ShapeDtypeStruct((t, ds), F32)],
        scratch_shapes=[pltpu.VMEM((tq, LANES), BF16), pltpu.VMEM((tq, LANES), F32), pltpu.VMEM((tq, 1), F32),
                        pltpu.VMEM((2, tq, tk), F32), pltpu.VMEM((tq, tk), BF16)],
        compiler_params=_cparams(2),
    )(proj, proj, kv, kv)


def _sb_bwd(proj, kv, o_all, tl_all, dmix, name):
    t = proj.shape[0]
    ds = kv.shape[1] // 2
    nh = ds // LANES
    tq = _tile(t, SB_TQ)
    tk = _tile(tq, SB_TK)
    nd = tq // tk
    scale = LANES ** -0.5

    def body(q_ref, g_ref, k_ref, v_ref, o_ref, tl_ref, dy_ref, dq_ref, dg_ref, dk_ref, dv_ref,
             qbf_ref, dobf_ref, qt_ref, dot_ref, acc_ref, left_ref, rune_ref, z_ref, dw_ref, wp_ref, dzp_ref):
        qi = pl.program_id(1)

        @pl.when(qi == 0)
        def _():
            dk_ref[...] = jnp.zeros_like(dk_ref)
            dv_ref[...] = jnp.zeros_like(dv_ref)

        qbf_ref[...] = q_ref[...].astype(BF16)
        qt_ref[...] = q_ref[...].T.astype(BF16)
        gv = g_ref[...]
        sg = _sigmoid(gv)
        dy = dy_ref[...]
        do = dy * (gv * sg)
        dobf_ref[...] = do.astype(BF16)
        dot_ref[...] = do.T.astype(BF16)
        dg_ref[...] = (dy * o_ref[...] * (sg * (1.0 + gv * (1.0 - sg)))).astype(BF16)
        tri = _tri_ones("le", tk)
        acc_ref[...] = jnp.zeros_like(acc_ref)
        left_ref[...] = tl_ref[:, 0:1]
        rune_ref[...] = jnp.zeros_like(rune_ref)
        wp_ref[...] = jnp.zeros_like(wp_ref)
        dzp_ref[...] = jnp.zeros_like(dzp_ref)
        all_rows = [slice(s0, s0 + tk) for s0 in range(0, tq, tk)]

        def grads(zs, dws, groups):
            sps = [_sb_softplus(z, dg) for z, (_, dg) in zip(zs, groups)]
            cums = [_split_dot(sp, tri) for sp, _ in sps]
            ws, es, lbs = [], [], []
            for z, (rows, dg), (sp, mask), cum, dw in zip(zs, groups, sps, cums, dws):
                left = left_ref[rows, :]
                lb = z - sp
                w = jnp.exp(lb - (left - cum))
                if dg:
                    w = jnp.where(mask, w, 0.0)
                left_ref[rows, :] = left - cum[:, tk - 1:tk]
                ws.append(w.astype(BF16))
                es.append(dw * w)
                lbs.append(lb)
            cumes = [_split_dot(e, tri) for e in es]
            dzs = []
            for (rows, dg), (_, mask), lb, e, cume in zip(groups, sps, lbs, es, cumes):
                rune = rune_ref[rows, :]
                dz = (e - jnp.exp(lb) * (rune + cume)) * scale
                if dg:
                    dz = jnp.where(mask, dz, 0.0)
                rune_ref[rows, :] = rune + cume[:, tk - 1:tk]
                dzs.append(dz.astype(BF16))
            return dzs, ws

        n = qi * nd

        def rows_from(u):
            return [slice(s0, s0 + tk) for s0 in range(u * tk, tq, tk)]

        def logits_into(slot, kb, rows_list):
            k0 = pl.multiple_of(kb * tk, tk)
            k = k_ref[pl.ds(k0, tk), :]
            v = v_ref[pl.ds(k0, tk), :]
            seen = slice(rows_list[0].start, tq)
            z_ref[slot, seen, :] = lax.dot_general(qbf_ref[seen, :], k, _NT, preferred_element_type=F32) * scale
            dw_ref[slot, seen, :] = lax.dot_general(dobf_ref[seen, :], v, _NT, preferred_element_type=F32)

        def apply_stored(kb, u):
            k0 = pl.multiple_of(kb * tk, tk)
            k = k_ref[pl.ds(k0, tk), :]
            seen = slice(u * tk, tq)
            acc_ref[seen, :] += jnp.dot(dzp_ref[seen, :], k, preferred_element_type=F32)
            dk_ref[:, pl.ds(k0, tk)] += jnp.dot(qt_ref[:, seen], dzp_ref[seen, :], preferred_element_type=F32)
            dv_ref[:, pl.ds(k0, tk)] += jnp.dot(dot_ref[:, seen], wp_ref[seen, :], preferred_element_type=F32)

        def differentiate(slot, groups):
            dzs, ws = grads([z_ref[slot, rows, :] for rows, _ in groups], [dw_ref[slot, rows, :] for rows, _ in groups],
                            groups)
            for (rows, _), dz, w in zip(groups, dzs, ws):
                dzp_ref[rows, :] = dz
                wp_ref[rows, :] = w

        logits_into(0, 0, all_rows)

        def half_step(j, slot):
            apply_stored(jnp.maximum(j - 1, 0), 0)
            logits_into(1 - slot, j + 1, all_rows)
            differentiate(slot, [(rows, False) for rows in all_rows])

        def step(i, carry):
            half_step(2 * i, 0)
            half_step(2 * i + 1, 1)
            return carry

        lax.fori_loop(0, n // 2, step, 0)
        for u in range(nd):
            slot = u % 2
            if u == 0:
                apply_stored(jnp.maximum(n - 1, 0), 0)
            else:
                apply_stored(n + u - 1, u - 1)
            if u + 1 < nd:
                logits_into(1 - slot, n + u + 1, rows_from(u + 1))
            differentiate(slot, [(rows, j == 0) for j, rows in enumerate(rows_from(u))])
        apply_stored(n + nd - 1, nd - 1)
        dq_ref[...] = acc_ref[...].astype(BF16)

    blk = lambda h, i: (i, h)
    whole = lambda h, i: (0, h)
    return pl.pallas_call(
        body, name=name, grid=(nh, t // tq),
        in_specs=[pl.BlockSpec((tq, LANES), blk), pl.BlockSpec((tq, LANES), lambda h, i: (i, nh + h)),
                  pl.BlockSpec((t, LANES), whole), pl.BlockSpec((t, LANES), lambda h, i: (0, nh + h)),
                  pl.BlockSpec((tq, LANES), blk), pl.BlockSpec((tq, LANES), blk), pl.BlockSpec((tq, LANES), blk)],
        out_specs=[pl.BlockSpec((tq, LANES), blk), pl.BlockSpec((tq, LANES), blk),
                   pl.BlockSpec((LANES, t), lambda h, i: (h, 0)), pl.BlockSpec((LANES, t), lambda h, i: (h, 0))],
        out_shape=[jax.ShapeDtypeStruct((t, ds), BF16), jax.ShapeDtypeStruct((t, ds), BF16),
                   jax.ShapeDtypeStruct((ds, t), F32), jax.ShapeDtypeStruct((ds, t), F32)],
        scratch_shapes=[pltpu.VMEM((tq, LANES), BF16), pltpu.VMEM((tq, LANES), BF16),
                        pltpu.VMEM((LANES, tq), BF16), pltpu.VMEM((LANES, tq), BF16), pltpu.VMEM((tq, LANES), F32),
                        pltpu.VMEM((tq, 1), F32), pltpu.VMEM((tq, 1), F32),
                        pltpu.VMEM((2, tq, tk), F32), pltpu.VMEM((2, tq, tk), F32),
                        pltpu.VMEM((tq, tk), BF16), pltpu.VMEM((tq, tk), BF16)],
        compiler_params=_cparams(2),
    )(proj, proj, kv, kv, o_all, tl_all, dmix)


def _place():
    x, y, c = lax.axis_index("x"), lax.axis_index("y"), lax.axis_index("c")
    chips = [(1 - x, y), (x, 1 - y), (1 - x, 1 - y)]
    return x, y, c, chips


def _remote(src, dst, send_sems, recv_sems, k, to):
    return pltpu.make_async_remote_copy(src_ref=src, dst_ref=dst, send_sem=send_sems.at[k], recv_sem=recv_sems.at[k],
                                        device_id=to, device_id_type=MESH)


def _my_chip():
    return 2 * lax.axis_index("x") + lax.axis_index("y")


def _place_own(shard, name):
    r, w = shard.shape
    tr = _tile(r, FLAT_TR)

    def body(x_ref, o_ref):
        o_ref[...] = x_ref[...]

    return pl.pallas_call(
        body, name=name, out_shape=jax.ShapeDtypeStruct((N_CHIPS, r, w), shard.dtype), grid=(r // tr,),
        in_specs=[pl.BlockSpec((tr, w), lambda i: (i, 0))],
        out_specs=pl.BlockSpec((None, tr, w), lambda i: (_my_chip(), i, 0)),
        compiler_params=_cparams(1),
    )(shard)


def _chip_all_gather(shards, name):
    n = len(shards)

    def body(*refs):
        x_refs, out_refs, send_sems, recv_sems = refs[:n], refs[2 * n:3 * n], refs[3 * n], refs[3 * n + 1]
        x, y, c, chips = _place()
        me = 2 * x + y
        sibling = (x, y, 1 - c)

        def rows(t, core):
            rh = x_refs[t].shape[0] // 2
            return pl.ds(core * rh, rh)

        first = [_remote(x_refs[t].at[rows(t, c)], out_refs[t].at[me, rows(t, c)], send_sems, recv_sems, 6 * t + k,
                         (cx, cy, c)) for t in range(n) for k, (cx, cy) in enumerate(chips)]
        for cp in first:
            cp.start()
        passed = []
        for k, (cx, cy) in enumerate(chips):
            for t in range(n):
                got = out_refs[t].at[2 * cx + cy, rows(t, c)]
                _remote(got, got, send_sems, recv_sems, 6 * t + k, (cx, cy, c)).wait_recv()
                fwd = _remote(got, got, send_sems, recv_sems, 6 * t + 3 + k, sibling)
                fwd.start()
                passed.append(fwd)
        for k, (cx, cy) in enumerate(chips):
            for t in range(n):
                got = out_refs[t].at[2 * cx + cy, rows(t, 1 - c)]
                _remote(got, got, send_sems, recv_sems, 6 * t + 3 + k, sibling).wait_recv()
        for cp in first + passed:
            cp.wait_send()

    bufs = [_place_own(s, f"{name}_own{t}") for t, s in enumerate(shards)]
    return pl.pallas_call(
        body, name=name, in_specs=[ANY] * (2 * n), out_specs=[ANY] * n,
        out_shape=[jax.ShapeDtypeStruct((N_CHIPS,) + s.shape, s.dtype) for s in shards],
        input_output_aliases={n + t: t for t in range(n)},
        scratch_shapes=[pltpu.SemaphoreType.DMA((6 * n,)), pltpu.SemaphoreType.DMA((6 * n,))],
    )(*shards, *bufs)


def _chip_gather_start(shards, name):
    n = len(shards)

    def body(*refs):
        x_refs, buf_refs, send_sems, recv_sems, token = refs[:n], refs[n:2 * n], refs[2 * n], refs[2 * n + 1], refs[-1]
        x, y, c, chips = _place()
        me = 2 * x + y
        for t in range(n):
            for k, (cx, cy) in enumerate(chips):
                _remote(x_refs[t], buf_refs[t].at[me], send_sems, recv_sems, 3 * t + k, (cx, cy, c)).start()
        token[...] = jnp.zeros_like(token)

    bufs = [_place_own(s, f"{name}_own{t}") for t, s in enumerate(shards)]
    hbm = [pltpu.HBM(a.shape, a.dtype) for a in list(shards) + bufs]
    outs = pl.pallas_call(
        body, name=name, in_specs=[HBM] * (2 * n),
        out_shape=(pltpu.SemaphoreType.DMA((3 * n,)), pltpu.SemaphoreType.DMA((3 * n,)), *hbm,
                   jax.ShapeDtypeStruct((8, LANES), F32)),
        out_specs=(SEM, SEM, *[HBM] * (2 * n), pl.BlockSpec(memory_space=pltpu.VMEM)),
        input_output_aliases={t: 2 + t for t in range(2 * n)},
        compiler_params=pltpu.CompilerParams(has_side_effects=EFFECT),
    )(*[pltpu.with_memory_space_constraint(a, pltpu.HBM) for a in list(shards) + bufs])
    return outs[0], outs[1], outs[2:2 + n], outs[2 + n:2 + 2 * n], outs[-1]


def _chip_gather_wait(send_sems, recv_sems, shards, bufs, after, name):
    n = len(shards)

    def body(*refs):
        x_refs, buf_refs, send, recv = refs[:n], refs[n:2 * n], refs[2 * n], refs[2 * n + 1]
        x, y, c, chips = _place()
        for t in range(n):
            for k, (cx, cy) in enumerate(chips):
                cp = _remote(x_refs[t], buf_refs[t].at[2 * cx + cy], send, recv, 3 * t + k, (cx, cy, c))
                cp.wait_send()
                cp.wait_recv()

    hbm = [pltpu.HBM(a.shape, a.dtype) for a in list(shards) + list(bufs)]
    outs = pl.pallas_call(
        body, name=name, in_specs=[HBM] * (2 * n) + [SEM, SEM, ANY],
        out_shape=tuple(hbm), out_specs=tuple([HBM] * (2 * n)),
        input_output_aliases={t: t for t in range(2 * n)},
        compiler_params=pltpu.CompilerParams(has_side_effects=EFFECT),
    )(*shards, *bufs, send_sems, recv_sems, after)
    return outs[n:]


def _sibling_take_half(ss, name):
    n = len(ss)

    def body(*refs):
        s_refs, a_refs, send_sems, recv_sems = refs[:n], refs[n:2 * n], refs[2 * n], refs[2 * n + 1]
        x, y, c, _ = _place()
        cps = []
        for t in range(n):
            rh = s_refs[t].shape[1] // 2
            cps.append(_remote(s_refs[t].at[:, pl.ds((1 - c) * rh, rh), :], a_refs[t], send_sems, recv_sems, t,
                               (x, y, 1 - c)))
        for cp in cps:
            cp.start()
        for cp in cps:
            cp.wait()

    return pl.pallas_call(
        body, name=name, in_specs=[ANY] * n, out_specs=[ANY] * n,
        out_shape=[jax.ShapeDtypeStruct((s.shape[0], s.shape[1] // 2, s.shape[2]), s.dtype) for s in ss],
        scratch_shapes=[pltpu.SemaphoreType.DMA((n,)), pltpu.SemaphoreType.DMA((n,))],
    )(*ss)


def _pair_sum(s, a, dtype, name):
    n, r, w = s.shape
    rh = r // 2
    tr = _tile(rh, FLAT_TR)
    nblk = rh // tr

    def body(s_ref, a_ref, o_ref):
        o_ref[...] = (s_ref[...] + a_ref[...]).astype(dtype)

    return pl.pallas_call(
        body, name=name, out_shape=jax.ShapeDtypeStruct((n, rh, w), dtype), grid=(n, nblk),
        in_specs=[pl.BlockSpec((None, tr, w), lambda k, i: (k, lax.axis_index("c") * nblk + i, 0)),
                  pl.BlockSpec((None, tr, w), lambda k, i: (k, i, 0))],
        out_specs=pl.BlockSpec((None, tr, w), lambda k, i: (k, i, 0)),
        compiler_params=_cparams(2),
    )(s, a)


def _chip_scatter_start(ps, name):
    n = len(ps)

    def body(*refs):
        p_refs, b_refs, send_sems, recv_sems, token = refs[:n], refs[n:2 * n], refs[2 * n], refs[2 * n + 1], refs[-1]
        x, y, c, chips = _place()
        me = 2 * x + y
        for t in range(n):
            for k, (cx, cy) in enumerate(chips):
                _remote(p_refs[t].at[2 * cx + cy], b_refs[t].at[me], send_sems, recv_sems, 3 * t + k, (cx, cy, c)).start()
        token[...] = jnp.zeros_like(token)

    hbm = [pltpu.HBM(p.shape, p.dtype) for p in ps]
    outs = pl.pallas_call(
        body, name=name, in_specs=[HBM] * (2 * n),
        out_shape=(pltpu.SemaphoreType.DMA((3 * n,)), pltpu.SemaphoreType.DMA((3 * n,)), *hbm, *hbm,
                   jax.ShapeDtypeStruct((8, LANES), F32)),
        out_specs=(SEM, SEM, *[HBM] * (2 * n), pl.BlockSpec(memory_space=pltpu.VMEM)),
        input_output_aliases={t: 2 + t for t in range(2 * n)},
        compiler_params=pltpu.CompilerParams(has_side_effects=EFFECT),
    )(*[pltpu.with_memory_space_constraint(p, pltpu.HBM) for p in ps],
      *[pltpu.with_memory_space_constraint(lax.empty(p.shape, p.dtype), pltpu.HBM) for p in ps])
    return outs[0], outs[1], outs[2:2 + n], outs[2 + n:2 + 2 * n], outs[-1]


def _chip_scatter_wait(send_sems, recv_sems, ps, bs, after, name):
    n = len(ps)

    def body(*refs):
        p_refs, b_refs, send, recv = refs[:n], refs[n:2 * n], refs[2 * n], refs[2 * n + 1]
        x, y, c, chips = _place()
        for t in range(n):
            for k, (cx, cy) in enumerate(chips):
                cp = _remote(p_refs[t].at[2 * cx + cy], b_refs[t].at[2 * cx + cy], send, recv, 3 * t + k, (cx, cy, c))
                cp.wait_send()
                cp.wait_recv()

    hbm = [pltpu.HBM(p.shape, p.dtype) for p in ps]
    outs = pl.pallas_call(
        body, name=name, in_specs=[HBM] * (2 * n) + [SEM, SEM, ANY],
        out_shape=(*hbm, *hbm), out_specs=tuple([HBM] * (2 * n)),
        input_output_aliases={t: t for t in range(2 * n)},
        compiler_params=pltpu.CompilerParams(has_side_effects=EFFECT),
    )(*ps, *bs, send_sems, recv_sems, after)
    return outs[n:]


def _chip_sum(p, b, name, after=None):
    n, rh, w = p.shape
    tr = _tile(rh, FLAT_TR)
    nblk = rh // tr

    def body(p_ref, b0_ref, b1_ref, b2_ref, b3_ref, *rest):
        o_ref = rest[-1]
        me = _my_chip()
        own = p_ref[...]
        t = [jnp.where(me == k, own, b_ref[...]).astype(F32) for k, b_ref in enumerate((b0_ref, b1_ref, b2_ref, b3_ref))]
        o_ref[...] = ((t[0] + t[1]) + t[2]) + t[3]

    def other(k):
        return lambda i: (jnp.where(_my_chip() == k, (k + 1) % N_CHIPS, k), i, 0)

    return pl.pallas_call(
        body, name=name, out_shape=jax.ShapeDtypeStruct((2 * rh, w), F32), grid=(nblk,),
        in_specs=[pl.BlockSpec((None, tr, w), lambda i: (_my_chip(), i, 0))]
        + [pl.BlockSpec((None, tr, w), other(k)) for k in range(N_CHIPS)] + ([] if after is None else [ANY]),
        out_specs=pl.BlockSpec((tr, w), lambda i: (lax.axis_index("c") * nblk + i, 0)),
        compiler_params=_cparams(1),
    )(p, b, b, b, b, *([] if after is None else [after]))


def _sibling_join(gs, name):
    n = len(gs)

    def body(*refs):
        g_refs, send_sems, recv_sems = refs[n:2 * n], refs[2 * n], refs[2 * n + 1]
        x, y, c, _ = _place()
        cps = []
        for t in range(n):
            rh = g_refs[t].shape[0] // 2
            mine = g_refs[t].at[pl.ds(c * rh, rh)]
            cps.append(_remote(mine, mine, send_sems, recv_sems, t, (x, y, 1 - c)))
        for cp in cps:
            cp.start()
        for t in range(n):
            rh = g_refs[t].shape[0] // 2
            theirs = g_refs[t].at[pl.ds((1 - c) * rh, rh)]
            _remote(theirs, theirs, send_sems, recv_sems, t, (x, y, 1 - c)).wait_recv()
        for cp in cps:
            cp.wait_send()

    return pl.pallas_call(
        body, name=name, in_specs=[ANY] * n, out_specs=[ANY] * n,
        out_shape=[jax.ShapeDtypeStruct(g.shape, g.dtype) for g in gs],
        input_output_aliases={t: t for t in range(n)},
        scratch_shapes=[pltpu.SemaphoreType.DMA((n,)), pltpu.SemaphoreType.DMA((n,))],
    )(*gs)


def _adamw(g, w, m, v, name):
    r, wd = g.shape

    def body(g_ref, w_ref, m_ref, v_ref, d_ref, mo_ref, vo_ref):
        gv = g_ref[...]
        mn = ADAM_B1 * m_ref[...] + (1.0 - ADAM_B1) * gv
        vn = ADAM_B2 * v_ref[...] + (1.0 - ADAM_B2) * (gv * gv)
        m_hat = mn / (1.0 - ADAM_B1 ** ADAM_STEP)
        v_hat = vn / (1.0 - ADAM_B2 ** ADAM_STEP)
        d_ref[...] = -ADAM_LR * (m_hat / (jnp.sqrt(v_hat) + ADAM_EPS) + ADAM_WD * w_ref[...])
        mo_ref[...] = mn
        vo_ref[...] = vn

    tr = _tile(r, FLAT_TR)
    row = lambda i: (i, 0)
    spec = pl.BlockSpec((tr, wd), row)
    return pl.pallas_call(
        body, name=name, grid=(r // tr,), in_specs=[spec] * 4, out_specs=[spec] * 3,
        out_shape=[jax.ShapeDtypeStruct((r, wd), F32)] * 3,
        compiler_params=_cparams(1),
    )(g, w, m, v)


def _pair_sums(ss, names):
    a = _sibling_take_half(ss, "grad_sibling_half_" + names[0])
    return [_pair_sum(s, a_t, BF16 if n in BIG else F32, f"grad_pair_sum_{n}") for s, a_t, n in zip(ss, a, names)]


def _finish_reduce(ps, bs, names, after=None):
    g = [_chip_sum(p_t, b_t, f"grad_chip_sum_{n}", after if i == 0 else None)
         for i, (p_t, b_t, n) in enumerate(zip(ps, bs, names))]
    return _sibling_join(g, "grad_sibling_join_" + names[0])


WEIGHTS = ("mem_norm", "w_mem_kv", "norm_a", "w_in_a", "conv_w", "conv_b", "w_rec_gate", "b_rec_gate", "w_in_gate",
           "b_in_gate", "lru_lambda", "w_out_a", "kv_norm", "w_kv", "norm_b", "w_in_b", "w_out_b", "final_norm")
SHARD_DIM = {"mem_norm": None, "w_mem_kv": 1, "norm_a": 1, "w_in_a": 2, "conv_w": 2, "conv_b": 1, "w_rec_gate": None,
             "b_rec_gate": 1, "w_in_gate": None, "b_in_gate": 1, "lru_lambda": 1, "w_out_a": 1, "kv_norm": None,
             "w_kv": 1, "norm_b": None, "w_in_b": 2, "w_out_b": 1, "final_norm": None}
BIG = ("w_mem_kv", "w_in_a", "w_out_a", "w_kv", "w_in_b", "w_out_b")
SMALL = ("norm_a", "conv_w", "conv_b", "b_rec_gate", "b_in_gate", "lru_lambda")


def _pad_rows(flat, row_multiple):
    per = FLAT_W * row_multiple
    n = flat.shape[0]
    total = -(-n // per) * per
    return jnp.pad(flat, (0, total - n)).reshape(total // FLAT_W, FLAT_W)


def _flatten(parts, row_multiple):
    return _pad_rows(jnp.concatenate([p.reshape(-1) for p in parts]), row_multiple)


def _unflatten(flat2d, shapes):
    flat = flat2d.reshape(-1)
    out, off = [], 0
    for shp in shapes:
        n = 1
        for s in shp:
            n *= s
        out.append(flat[off:off + n].reshape(shp))
        off += n
    return out


LATE = ("w_kv", "w_in_b", "w_out_b")


def _bf16_rows(w):
    return w.astype(BF16).reshape(-1, w.shape[-1])


def _whole(gathered, shape, dim):
    return jnp.concatenate([gathered[k].reshape(shape) for k in range(N_CHIPS)], axis=dim)


def _gather_weights(local):
    early = [n for n in BIG if n not in LATE]
    small_shapes = [local[n].shape for n in SMALL]
    send_sems, recv_sems, shards, bufs, token = _chip_gather_start([_bf16_rows(local[n]) for n in LATE],
                                                                   "late_weights_gather_start")
    gathered = _chip_all_gather([_bf16_rows(local[n]) for n in early] + [_flatten([local[n] for n in SMALL], 16)],
                                "weights_all_gather")
    full = {n: _whole(g, local[n].shape, SHARD_DIM[n]) for n, g in zip(early, gathered)}
    per_chip = [_unflatten(gathered[-1][k], small_shapes) for k in range(N_CHIPS)]
    for i, n in enumerate(SMALL):
        full[n] = jnp.concatenate([per_chip[k][i] for k in range(N_CHIPS)], axis=SHARD_DIM[n])

    def late(after):
        got = _chip_gather_wait(send_sems, recv_sems, shards, bufs, after, "late_weights_gather_wait")
        return {n: _whole(g, local[n].shape, SHARD_DIM[n]) for n, g in zip(LATE, got)}

    return full, token, late


def _piece(g, name, k):
    dim = SHARD_DIM[name]
    if dim is None:
        return g
    n = g.shape[dim] // N_CHIPS
    return lax.slice_in_dim(g, k * n, (k + 1) * n, axis=dim)


def _local_grads(x, mem, tgt, wts, token, late):
    t, d = x.shape
    depth = wts["w_mem_kv"].shape[0]
    n_a = wts["w_in_a"].shape[0]
    n_b = wts["norm_b"].shape[0]
    nb = wts["w_rec_gate"].shape[1]
    dr = nb * LANES
    dm = wts["w_mem_kv"].shape[2] // 2
    row = lambda v: v.reshape(1, -1)

    wm_all = jnp.concatenate([wts["w_mem_kv"][l] for l in range(depth)], axis=1)
    memkv, memn_bf = _norm_matmul(mem, row(wts["mem_norm"]), wm_all, BF16, "mem_kv_proj", after=token)

    h = x
    saved = []
    vecs = []
    for l in range(n_a):
        proj, u_bf = _norm_matmul(h, row(wts["norm_a"][l]), wts["w_in_a"][l], F32, f"a{l}_in_proj")
        vec = jnp.concatenate([row(wts["conv_b"][l]), row(wts["b_rec_gate"][l]), row(wts["b_in_gate"][l]),
                               row(wts["lru_lambda"][l]), wts["conv_w"][l]], axis=0)
        vecs.append(vec)
        y_rnn, h_rnn, xc = _lru_fwd(proj, vec, wts["w_rec_gate"][l], wts["w_in_gate"][l], f"a{l}_lru_fwd")
        y_mem = _mem_attn_fwd(proj, memkv, l, dr, dm, f"a{l}_mem_fwd")
        mix = jnp.concatenate([y_rnn, y_mem], axis=1)
        h_next = _matmul_res(mix, wts["w_out_a"][l], h, f"a{l}_out_proj")
        saved.append((h, proj, u_bf, mix, h_rnn, xc))
        h = h_next

    h_kv = h
    wts = {**wts, **late(h_kv)}
    kv, ukv_bf = _norm_matmul(h_kv, row(wts["kv_norm"]), wts["w_kv"], BF16, "kv_proj")

    for j in range(n_b):
        l = n_a + j
        proj, u_bf = _norm_matmul(h, row(wts["norm_b"][j]), wts["w_in_b"][j], F32, f"b{j}_in_proj")
        y_sb, o_sb, tl_sb = _sb_fwd(proj, kv, f"b{j}_sb_fwd")
        y_mem = _mem_attn_fwd(proj, memkv, l, dr, dm, f"b{j}_mem_fwd")
        mix = jnp.concatenate([y_sb, y_mem], axis=1)
        h_next = _matmul_res(mix, wts["w_out_b"][j], h, f"b{j}_out_proj")
        saved.append((h, proj, u_bf, mix, o_sb, tl_sb))
        h = h_next

    dh, d_final, loss, dh_bf = _final_loss_bwd(h, row(wts["final_norm"]), tgt, "final_loss_bwd")

    grads = {"final_norm": d_final.reshape(-1)}
    big = {}
    dmemkv = [None] * depth
    g_norm_b = [None] * n_b
    dks, dvs = [], []
    for j in reversed(range(n_b)):
        l = n_a + j
        h_in, proj, u_bf, mix, o_sb, tl_sb = saved[l]
        dmix = _matmul_nt(dh_bf, wts["w_out_b"][j], f"b{j}_dmix")
        big["w_out_b"] = _dw_rows(mix, dh_bf, j, n_b, big.get("w_out_b"), f"b{j}_dw_out")
        dq, dg, dk, dv = _sb_bwd(proj, kv, o_sb, tl_sb, dmix, f"b{j}_sb_bwd")
        dqm, dgm, dkm, dvm = _mem_attn_bwd(proj, memkv, dmix, l, dr, dm, f"b{j}_mem_bwd")
        dmemkv[l] = (dkm, dvm)
        dproj = jnp.concatenate([dq, dg, dqm, dgm], axis=1)
        du = _matmul_nt(dproj, wts["w_in_b"][j], f"b{j}_du")
        big["w_in_b"] = _dw_cols(u_bf, dproj, j, n_b, big.get("w_in_b"), f"b{j}_dw_in")
        dh, dgn, dh_bf = _rms_bwd(du, h_in, row(wts["norm_b"][j]), dh, f"b{j}_rms_bwd")
        g_norm_b[j] = dgn.reshape(-1)
        dks.append(dk)
        dvs.append(dv)
    assert n_b == 2
    dkv = jnp.concatenate([_add_cast(dks[0], dks[1], "dk_sum").T, _add_cast(dvs[0], dvs[1], "dv_sum").T], axis=1)
    du = _matmul_nt(dkv, wts["w_kv"], "kv_du")
    big["w_kv"] = _dw_cols(ukv_bf, dkv, 0, 1, None, "kv_dw")
    dh, dgn, dh_bf = _rms_bwd(du, h_kv, row(wts["kv_norm"]), dh, "kv_rms_bwd")
    grads["kv_norm"] = dgn.reshape(-1)

    early_names = [n for n in BIG if n in big]
    early_p = _pair_sums([big[n].reshape(N_CHIPS, -1, big[n].shape[-1]) for n in early_names], early_names)
    send_sems, recv_sems, early_p, early_b, token = _chip_scatter_start(early_p, "grad_chip_scatter_start")

    g_norm_a = [None] * n_a
    g_wr, g_wi, g_vec = [None] * n_a, [None] * n_a, [None] * n_a
    for l in reversed(range(n_a)):
        h_in, proj, u_bf, mix, h_rnn, xc = saved[l]
        dmix = _matmul_nt(dh_bf, wts["w_out_a"][l], f"a{l}_dmix", after=token if l == n_a - 1 else None)
        big["w_out_a"] = _dw_rows(mix, dh_bf, l, n_a, big.get("w_out_a"), f"a{l}_dw_out")
        dx, dg, g_wr[l], g_wi[l], dvec = _lru_bwd(proj, xc, h_rnn, dmix, vecs[l], wts["w_rec_gate"][l],
                                                  wts["w_in_gate"][l], f"a{l}_lru_bwd")
        g_vec[l] = dvec.transpose(1, 0, 2).reshape(8, dr)
        dqm, dgm, dkm, dvm = _mem_attn_bwd(proj, memkv, dmix, l, dr, dm, f"a{l}_mem_bwd")
        dmemkv[l] = (dkm, dvm)
        dproj = jnp.concatenate([dx, dg, dqm, dgm], axis=1)
        du = _matmul_nt(dproj, wts["w_in_a"][l], f"a{l}_du")
        big["w_in_a"] = _dw_cols(u_bf, dproj, l, n_a, big.get("w_in_a"), f"a{l}_dw_in")
        dh, dgn, dh_bf = _rms_bwd(du, h_in, row(wts["norm_a"][l]), dh, f"a{l}_rms_bwd")
        g_norm_a[l] = dgn.reshape(-1)

    dmemkv_all = jnp.concatenate([jnp.concatenate(p, axis=1) for p in dmemkv], axis=1).astype(BF16)
    pk = d // N_CHIPS
    big["w_mem_kv"] = _matmul_tn(memn_bf, dmemkv_all, "mem_dw", pk, 2 * dm, (N_CHIPS, depth, pk, 2 * dm),
                                 (None, None, pk, 2 * dm), lambda i, j: (i, j, 0, 0))
    dmemn = _matmul_nt(dmemkv_all, wm_all, "mem_du")
    _, dgn, _ = _rms_bwd(dmemn, mem, row(wts["mem_norm"]), jnp.zeros_like(mem), "mem_rms_bwd")
    grads["mem_norm"] = dgn.reshape(-1)
    grads["norm_a"] = jnp.stack(g_norm_a)
    grads["w_rec_gate"] = jnp.stack(g_wr)
    grads["w_in_gate"] = jnp.stack(g_wi)
    gv = jnp.stack(g_vec)
    grads["conv_b"], grads["b_rec_gate"], grads["b_in_gate"], grads["lru_lambda"] = gv[:, 0], gv[:, 1], gv[:, 2], gv[:, 3]
    grads["conv_w"] = gv[:, 4:8]
    grads["norm_b"] = jnp.stack(g_norm_b)
    early_b = _chip_scatter_wait(send_sems, recv_sems, early_p, early_b, dh, "grad_chip_scatter_wait")
    early = (early_names, early_p, early_b)
    big = {n: g.reshape(N_CHIPS, -1, g.shape[-1]) for n, g in big.items() if n not in early_names}
    return loss, dh, grads, big, early


def kernel(x, mem, mem_norm, w_mem_kv, norm_a, w_in_a, conv_w, conv_b, w_rec_gate, b_rec_gate, w_in_gate, b_in_gate, lru_lambda, w_out_a, kv_norm, w_kv, norm_b, w_in_b, w_out_b, final_norm, loss_target, m_mem_norm, m_w_mem_kv, m_norm_a, m_w_in_a, m_conv_w, m_conv_b, m_w_rec_gate, m_b_rec_gate, m_w_in_gate, m_b_in_gate, m_lru_lambda, m_w_out_a, m_kv_norm, m_w_kv, m_norm_b, m_w_in_b, m_w_out_b, m_final_norm, v_mem_norm, v_w_mem_kv, v_norm_a, v_w_in_a, v_conv_w, v_conv_b, v_w_rec_gate, v_b_rec_gate, v_w_in_gate, v_b_in_gate, v_lru_lambda, v_w_out_a, v_kv_norm, v_w_kv, v_norm_b, v_w_in_b, v_w_out_b, v_final_norm):
    local = dict(mem_norm=mem_norm, w_mem_kv=w_mem_kv, norm_a=norm_a, w_in_a=w_in_a, conv_w=conv_w, conv_b=conv_b,
                 w_rec_gate=w_rec_gate, b_rec_gate=b_rec_gate, w_in_gate=w_in_gate, b_in_gate=b_in_gate,
                 lru_lambda=lru_lambda, w_out_a=w_out_a, kv_norm=kv_norm, w_kv=w_kv, norm_b=norm_b, w_in_b=w_in_b,
                 w_out_b=w_out_b, final_norm=final_norm)
    mom = dict(mem_norm=m_mem_norm, w_mem_kv=m_w_mem_kv, norm_a=m_norm_a, w_in_a=m_w_in_a, conv_w=m_conv_w,
               conv_b=m_conv_b, w_rec_gate=m_w_rec_gate, b_rec_gate=m_b_rec_gate, w_in_gate=m_w_in_gate,
               b_in_gate=m_b_in_gate, lru_lambda=m_lru_lambda, w_out_a=m_w_out_a, kv_norm=m_kv_norm, w_kv=m_w_kv,
               norm_b=m_norm_b, w_in_b=m_w_in_b, w_out_b=m_w_out_b, final_norm=m_final_norm)
    var = dict(mem_norm=v_mem_norm, w_mem_kv=v_w_mem_kv, norm_a=v_norm_a, w_in_a=v_w_in_a, conv_w=v_conv_w,
               conv_b=v_conv_b, w_rec_gate=v_w_rec_gate, b_rec_gate=v_b_rec_gate, w_in_gate=v_w_in_gate,
               b_in_gate=v_b_in_gate, lru_lambda=v_lru_lambda, w_out_a=v_w_out_a, kv_norm=v_kv_norm, w_kv=v_w_kv,
               norm_b=v_norm_b, w_in_b=v_w_in_b, w_out_b=v_w_out_b, final_norm=v_final_norm)

    wts, token, late = _gather_weights(local)
    for n in WEIGHTS:
        if SHARD_DIM[n] is None:
            wts[n] = local[n]
    wts["w_rec_gate"] = wts["w_rec_gate"].astype(BF16)
    wts["w_in_gate"] = wts["w_in_gate"].astype(BF16)

    loss, grad_x, grads, big, (early_names, early_p, early_b) = _local_grads(x[0], mem[0], loss_target[0], wts, token, late)

    rest = [n for n in WEIGHTS if n not in BIG]
    row_multiple = 2 * FLAT_TR
    s_rest = jnp.stack([_flatten([_piece(grads[n], n, k) for n in rest], row_multiple) for k in range(N_CHIPS)])
    late_names = [n for n in BIG if n in big] + ["rest"]
    late_p = _pair_sums([big[n] for n in late_names[:-1]] + [s_rest], late_names)
    send_sems, recv_sems, late_p, late_b, token = _chip_scatter_start(late_p, "grad_late_scatter_start")

    g_out, d_out, m_out, v_out = {}, {}, {}, {}

    def update(n, g):
        shape = local[n].shape
        flat = lambda a: a.reshape(-1, shape[-1])
        d, mo, vo = _adamw(g, flat(local[n]), flat(mom[n]), flat(var[n]), f"adamw_{n}")
        g_out[n], d_out[n], m_out[n], v_out[n] = (a.reshape(shape) for a in (g, d, mo, vo))
        return d

    for n, g in zip(early_names, _finish_reduce(early_p, early_b, early_names, after=token)):
        done = update(n, g)
    late_b = _chip_scatter_wait(send_sems, recv_sems, late_p, late_b, done, "grad_late_scatter_wait")
    reduced = dict(zip(late_names, _finish_reduce(late_p, late_b, late_names)))
    for n in late_names[:-1]:
        update(n, reduced[n])
    g_rest = reduced["rest"]
    d_rest, m_rest, v_rest = _adamw(g_rest, *(_flatten([src[n] for n in rest], row_multiple) for src in (local, mom, var)),
                                    "adamw_rest")
    shapes = [local[n].shape for n in rest]
    for out, flat2d in ((g_out, g_rest), (d_out, d_rest), (m_out, m_rest), (v_out, v_rest)):
        out.update(zip(rest, _unflatten(flat2d, shapes)))

    total_loss = lax.psum(loss[0, 0], MESH_AXES)
    return (total_loss, grad_x[None], *[g_out[n] for n in WEIGHTS], *[d_out[n] for n in WEIGHTS],
            *[m_out[n] for n in WEIGHTS], *[v_out[n] for n in WEIGHTS])
```

```python
import functools

import jax
import jax.numpy as jnp
from jax import lax
from jax.experimental import pallas as pl
from jax.experimental.pallas import tpu as pltpu

F32 = jnp.float32
BF16 = jnp.bfloat16

RMS_EPS = 1e-6
LRU_C = 8.0
ADAM_LR = 0.001
ADAM_B1 = 0.9
ADAM_B2 = 0.999
ADAM_EPS = 1e-08
ADAM_WD = 0.01
ADAM_STEP = 10

LANES = 128
VMEM_LIMIT = 56 * 1024 * 1024
FLAT_W = 1024
FLAT_TR = 256
N_CHIPS = 4
MESH_AXES = ("x", "y", "c")

_NT = (((1,), (1,)), ((), ()))
_TN = (((0,), (0,)), ((), ()))
ANY = pl.BlockSpec(memory_space=pl.ANY)
HBM = pl.BlockSpec(memory_space=pltpu.HBM)
SEM = pl.BlockSpec(memory_space=pltpu.SEMAPHORE)
EFFECT = pltpu.SideEffectType.DATAFLOW_SIDE_EFFECTING
MESH = pl.DeviceIdType.MESH


def _cparams(n_axes):
    return pltpu.CompilerParams(dimension_semantics=("arbitrary",) * n_axes, vmem_limit_bytes=VMEM_LIMIT)


def _sigmoid(x):
    return 1.0 / (1.0 + jnp.exp(-x))


def _log1p_pos(e):
    return jnp.where(e < 1e-3, e * (1.0 - e * (0.5 - e * (1.0 / 3.0))), jnp.log(1.0 + e))


def _neg_expm1(x):
    small = -x * (1.0 + x * (0.5 + x * (1.0 / 6.0 + x * (1.0 / 24.0))))
    return jnp.where(x > -0.05, small, 1.0 - jnp.exp(x))


def _tile(n, want):
    if n <= want:
        return n
    t = want
    while n % t:
        t -= LANES
    assert t > 0, (n, want)
    return t


def _norm_matmul(x, g, w, out_dtype, name, after=None):
    m, k = x.shape
    n = w.shape[1]
    tm, tn = _tile(m, 1024), _tile(n, 1024)

    def body(x_ref, g_ref, w_ref, *rest):
        o_ref, u_ref = rest[-2:]

        @pl.when(pl.program_id(1) == 0)
        def _():
            xf = x_ref[...]
            r = lax.rsqrt(jnp.mean(xf * xf, axis=-1, keepdims=True) + RMS_EPS)
            u_ref[...] = ((xf * r) * g_ref[...]).astype(BF16)

        o_ref[...] = jnp.dot(u_ref[...], w_ref[...], preferred_element_type=F32).astype(o_ref.dtype)

    return pl.pallas_call(
        body, name=name, grid=(m // tm, n // tn),
        in_specs=[pl.BlockSpec((tm, k), lambda i, j: (i, 0)), pl.BlockSpec((1, k), lambda i, j: (0, 0)),
                  pl.BlockSpec((k, tn), lambda i, j: (0, j))] + ([] if after is None else [ANY]),
        out_specs=[pl.BlockSpec((tm, tn), lambda i, j: (i, j)), pl.BlockSpec((tm, k), lambda i, j: (i, 0))],
        out_shape=[jax.ShapeDtypeStruct((m, n), out_dtype), jax.ShapeDtypeStruct((m, k), BF16)],
        compiler_params=_cparams(2),
    )(x, g, w, *([] if after is None else [after]))


def _matmul_res(a, b, res, name):
    m, k = a.shape
    n = b.shape[1]
    tm, tn = _tile(m, 1024), _tile(n, 1024)

    def body(a_ref, b_ref, r_ref, o_ref):
        o_ref[...] = r_ref[...] + jnp.dot(a_ref[...], b_ref[...], preferred_element_type=F32)

    return pl.pallas_call(
        body, name=name, grid=(m // tm, n // tn),
        in_specs=[pl.BlockSpec((tm, k), lambda i, j: (i, 0)), pl.BlockSpec((k, tn), lambda i, j: (0, j)),
                  pl.BlockSpec((tm, tn), lambda i, j: (i, j))],
        out_specs=pl.BlockSpec((tm, tn), lambda i, j: (i, j)),
        out_shape=jax.ShapeDtypeStruct((m, n), F32),
        compiler_params=_cparams(2),
    )(a, b, res)


def _matmul_nt(a, b, name, after=None):
    m, n = a.shape
    k = b.shape[0]
    tm, tk = _tile(m, 1024), _tile(k, 512)

    def body(a_ref, b_ref, *rest):
        o_ref = rest[-1]
        o_ref[...] = lax.dot_general(a_ref[...].astype(BF16), b_ref[...], _NT, preferred_element_type=F32)

    in_specs = [pl.BlockSpec((tm, n), lambda i, j: (i, 0)), pl.BlockSpec((tk, n), lambda i, j: (j, 0))]
    args = [a, b]
    if after is not None:
        in_specs.append(ANY)
        args.append(after)
    return pl.pallas_call(
        body, name=name, grid=(m // tm, k // tk), in_specs=in_specs,
        out_specs=pl.BlockSpec((tm, tk), lambda i, j: (i, j)),
        out_shape=jax.ShapeDtypeStruct((m, k), F32),
        compiler_params=_cparams(2),
    )(*args)


def _matmul_tn(a, b, name, tk, tn, out_shape, out_block, out_index, into=None):
    m, k = a.shape
    n = b.shape[1]
    tm = _tile(m, 2048 if b.dtype == BF16 else 1024)

    def body(a_ref, b_ref, *rest):
        o_ref = rest[-1]
        part = lax.dot_general(a_ref[...].astype(BF16), b_ref[...].astype(BF16), _TN, preferred_element_type=F32)

        @pl.when(pl.program_id(2) == 0)
        def _():
            o_ref[...] = part

        @pl.when(pl.program_id(2) != 0)
        def _():
            o_ref[...] += part

    in_specs = [pl.BlockSpec((tm, tk), lambda i, j, s: (s, i)), pl.BlockSpec((tm, tn), lambda i, j, s: (s, j))]
    args = [a, b]
    if into is not None:
        in_specs.append(ANY)
        args.append(into)
    return pl.pallas_call(
        body, name=name, grid=(k // tk, n // tn, m // tm), in_specs=in_specs,
        out_specs=pl.BlockSpec(out_block, lambda i, j, s: out_index(i, j)),
        out_shape=jax.ShapeDtypeStruct(out_shape, F32),
        input_output_aliases={} if into is None else {2: 0},
        compiler_params=_cparams(3),
    )(*args)


def _dw_cols(a, b, layer, n_layers, into, name):
    k, n = a.shape[1], b.shape[1]
    pn = n // N_CHIPS
    tk = _tile(k, 512)
    return _matmul_tn(a, b, name, tk, pn, (N_CHIPS, n_layers, k, pn), (None, None, tk, pn),
                      lambda i, j: (j, layer, i, 0), into)


def _dw_rows(a, b, layer, n_layers, into, name):
    k, n = a.shape[1], b.shape[1]
    pk = k // N_CHIPS
    tn = _tile(n, 2048)
    return _matmul_tn(a, b, name, pk, tn, (N_CHIPS, n_layers, pk, n), (None, None, pk, tn),
                      lambda i, j: (i, layer, 0, j), into)


def _rms_bwd(du, h, g, dres, name):
    m, d = h.shape
    tm = _tile(m, 256)

    def body(du_ref, h_ref, g_ref, dres_ref, dx_ref, dg_ref, dxb_ref):
        xf = h_ref[...]
        r = lax.rsqrt(jnp.mean(xf * xf, axis=-1, keepdims=True) + RMS_EPS)
        xhat = xf * r
        du_v = du_ref[...]
        dxn = du_v * g_ref[...]
        dx = dres_ref[...] + r * (dxn - xhat * jnp.mean(dxn * xhat, axis=-1, keepdims=True))
        dx_ref[...] = dx
        dxb_ref[...] = dx.astype(BF16)
        part = jnp.sum(du_v * xhat, axis=0, keepdims=True)

        @pl.when(pl.program_id(0) == 0)
        def _():
            dg_ref[...] = part

        @pl.when(pl.program_id(0) != 0)
        def _():
            dg_ref[...] += part

    row = lambda i: (i, 0)
    return pl.pallas_call(
        body, name=name, grid=(m // tm,),
        in_specs=[pl.BlockSpec((tm, d), row), pl.BlockSpec((tm, d), row), pl.BlockSpec((1, d), lambda i: (0, 0)),
                  pl.BlockSpec((tm, d), row)],
        out_specs=[pl.BlockSpec((tm, d), row), pl.BlockSpec((1, d), lambda i: (0, 0)), pl.BlockSpec((tm, d), row)],
        out_shape=[jax.ShapeDtypeStruct((m, d), F32), jax.ShapeDtypeStruct((1, d), F32),
                   jax.ShapeDtypeStruct((m, d), BF16)],
        compiler_params=_cparams(1),
    )(du, h, g, dres)


def _final_loss_bwd(h, g, tgt, name):
    m, d = h.shape
    tm = _tile(m, 256)

    def body(h_ref, g_ref, t_ref, dx_ref, dg_ref, loss_ref, dxb_ref):
        xf = h_ref[...]
        r = lax.rsqrt(jnp.mean(xf * xf, axis=-1, keepdims=True) + RMS_EPS)
        xhat = xf * r
        gv = g_ref[...]
        err = xhat * gv - t_ref[...]
        dy = err * (1.0 / d)
        dxn = dy * gv
        dx = r * (dxn - xhat * jnp.mean(dxn * xhat, axis=-1, keepdims=True))
        dx_ref[...] = dx
        dxb_ref[...] = dx.astype(BF16)
        part = jnp.sum(dy * xhat, axis=0, keepdims=True)
        lpart = jnp.sum(jnp.sum(err * err, axis=0, keepdims=True), axis=1, keepdims=True) * (0.5 / d)

        @pl.when(pl.program_id(0) == 0)
        def _():
            dg_ref[...] = part
            loss_ref[...] = lpart

        @pl.when(pl.program_id(0) != 0)
        def _():
            dg_ref[...] += part
            loss_ref[...] += lpart

    row = lambda i: (i, 0)
    fixed = lambda i: (0, 0)
    return pl.pallas_call(
        body, name=name, grid=(m // tm,),
        in_specs=[pl.BlockSpec((tm, d), row), pl.BlockSpec((1, d), fixed), pl.BlockSpec((tm, d), row)],
        out_specs=[pl.BlockSpec((tm, d), row), pl.BlockSpec((1, d), fixed), pl.BlockSpec((1, 1), fixed),
                   pl.BlockSpec((tm, d), row)],
        out_shape=[jax.ShapeDtypeStruct((m, d), F32), jax.ShapeDtypeStruct((1, d), F32),
                   jax.ShapeDtypeStruct((1, 1), F32), jax.ShapeDtypeStruct((m, d), BF16)],
        compiler_params=_cparams(1),
    )(h, g, tgt)


def _add_cast(a, b, name):
    m, n = a.shape
    tm, tn = _tile(m, 512), _tile(n, 2048)

    def body(a_ref, b_ref, o_ref):
        o_ref[...] = (a_ref[...] + b_ref[...]).astype(BF16)

    blk = lambda i, j: (i, j)
    return pl.pallas_call(
        body, name=name, grid=(m // tm, n // tn),
        in_specs=[pl.BlockSpec((tm, tn), blk), pl.BlockSpec((tm, tn), blk)],
        out_specs=pl.BlockSpec((tm, tn), blk),
        out_shape=jax.ShapeDtypeStruct((m, n), BF16),
        compiler_params=_cparams(2),
    )(a, b)


def _mem_attn_fwd(proj, memkv, layer, dr, dm, name):
    t = proj.shape[0]
    nm = memkv.shape[0]
    tm = _tile(t, 512)
    nh = dm // LANES
    scale = LANES ** -0.5
    qb = (2 * dr) // dm

    def body(q_ref, g_ref, k_ref, v_ref, y_ref):
        heads = [slice(hh * LANES, (hh + 1) * LANES) for hh in range(nh)]
        ss = [lax.dot_general(q_ref[:, sl].astype(BF16), k_ref[:, sl], _NT, preferred_element_type=F32) * scale
              for sl in heads]
        ps = [jnp.exp(s - jnp.max(s, axis=-1, keepdims=True)) for s in ss]
        ps = [p / jnp.sum(p, axis=-1, keepdims=True) for p in ps]
        os = [jnp.dot(p.astype(BF16), v_ref[:, sl], preferred_element_type=F32) for p, sl in zip(ps, heads)]
        for o, sl in zip(os, heads):
            gv = g_ref[:, sl]
            y_ref[:, sl] = (o * (gv * _sigmoid(gv))).astype(BF16)

    return pl.pallas_call(
        body, name=name, grid=(t // tm,),
        in_specs=[pl.BlockSpec((tm, dm), lambda i: (i, qb)), pl.BlockSpec((tm, dm), lambda i: (i, qb + 1)),
                  pl.BlockSpec((nm, dm), lambda i: (0, 2 * layer)), pl.BlockSpec((nm, dm), lambda i: (0, 2 * layer + 1))],
        out_specs=pl.BlockSpec((tm, dm), lambda i: (i, 0)),
        out_shape=jax.ShapeDtypeStruct((t, dm), BF16),
        compiler_params=_cparams(1),
    )(proj, proj, memkv, memkv)


def _mem_attn_bwd(proj, memkv, dmix, layer, dr, dm, name):
    t = proj.shape[0]
    nm = memkv.shape[0]
    tm = _tile(t, 512)
    nh = dm // LANES
    scale = LANES ** -0.5
    qb = (2 * dr) // dm
    yb = dr // dm

    def body(q_ref, g_ref, k_ref, v_ref, dy_ref, dq_ref, dg_ref, dk_ref, dv_ref):
        @pl.when(pl.program_id(0) == 0)
        def _():
            dk_ref[...] = jnp.zeros_like(dk_ref)
            dv_ref[...] = jnp.zeros_like(dv_ref)

        heads = [slice(hh * LANES, (hh + 1) * LANES) for hh in range(nh)]
        qs = [q_ref[:, sl].astype(BF16) for sl in heads]
        ss = [lax.dot_general(q, k_ref[:, sl], _NT, preferred_element_type=F32) * scale for q, sl in zip(qs, heads)]
        dos = []
        for sl in heads:
            gv = g_ref[:, sl]
            dos.append((dy_ref[:, sl] * (gv * _sigmoid(gv))).astype(BF16))
        dps = [lax.dot_general(do_bf, v_ref[:, sl], _NT, preferred_element_type=F32) for do_bf, sl in zip(dos, heads)]
        ps = [jnp.exp(s - jnp.max(s, axis=-1, keepdims=True)) for s in ss]
        ps = [p / jnp.sum(p, axis=-1, keepdims=True) for p in ps]
        pbs = [p.astype(BF16) for p in ps]
        os = [jnp.dot(p_bf, v_ref[:, sl], preferred_element_type=F32) for p_bf, sl in zip(pbs, heads)]
        dss = [(p * (dp - jnp.sum(dp * p, axis=-1, keepdims=True)) * scale).astype(BF16) for p, dp in zip(ps, dps)]
        for sl, q, o, p_bf, do_bf, ds in zip(heads, qs, os, pbs, dos, dss):
            gv = g_ref[:, sl]
            sg = _sigmoid(gv)
            dg_ref[:, sl] = (dy_ref[:, sl] * o * (sg * (1.0 + gv * (1.0 - sg)))).astype(BF16)
            dv_ref[:, sl] += lax.dot_general(p_bf, do_bf, _TN, preferred_element_type=F32)
            dq_ref[:, sl] = jnp.dot(ds, k_ref[:, sl], preferred_element_type=F32).astype(BF16)
            dk_ref[:, sl] += lax.dot_general(ds, q, _TN, preferred_element_type=F32)

    fixed = lambda i: (0, 0)
    return pl.pallas_call(
        body, name=name, grid=(t // tm,),
        in_specs=[pl.BlockSpec((tm, dm), lambda i: (i, qb)), pl.BlockSpec((tm, dm), lambda i: (i, qb + 1)),
                  pl.BlockSpec((nm, dm), lambda i: (0, 2 * layer)), pl.BlockSpec((nm, dm), lambda i: (0, 2 * layer + 1)),
                  pl.BlockSpec((tm, dm), lambda i: (i, yb))],
        out_specs=[pl.BlockSpec((tm, dm), lambda i: (i, 0)), pl.BlockSpec((tm, dm), lambda i: (i, 0)),
                   pl.BlockSpec((nm, dm), fixed), pl.BlockSpec((nm, dm), fixed)],
        out_shape=[jax.ShapeDtypeStruct((t, dm), BF16), jax.ShapeDtypeStruct((t, dm), BF16),
                   jax.ShapeDtypeStruct((nm, dm), F32), jax.ShapeDtypeStruct((nm, dm), F32)],
        compiler_params=_cparams(1),
    )(proj, proj, memkv, memkv, dmix)


LRU_CHUNK = 256


def _lru_gates(xc, vec, wr_ref, wi_ref):
    r = _sigmoid(jnp.dot(xc.astype(BF16), wr_ref[...], preferred_element_type=F32) + vec[1:2])
    i = _sigmoid(jnp.dot(xc.astype(BF16), wi_ref[...], preferred_element_type=F32) + vec[2:3])
    lam = vec[3:4]
    cl = -LRU_C * (jnp.maximum(-lam, 0.0) + _log1p_pos(jnp.exp(-jnp.abs(lam))))
    la = cl * r
    a = jnp.exp(la)
    s2 = _neg_expm1(2.0 * la)
    return r, i, cl, a, s2


def _lru_fwd(proj, vec, wr, wi, name):
    t = proj.shape[0]
    nb = wr.shape[0]
    dr = nb * LANES
    c = _tile(t, LRU_CHUNK)
    per = next(p for p in (4, 2, 1) if nb % p == 0)
    w = per * LANES
    ng = nb // per

    def body(x_ref, g_ref, vec_ref, wr_ref, wi_ref, y_ref, h_ref, xc_ref, carry_ref, xprev_ref):
        @pl.when(pl.program_id(1) == 0)
        def _():
            carry_ref[...] = jnp.zeros_like(carry_ref)
            xprev_ref[...] = jnp.zeros_like(xprev_ref)

        rows = lax.broadcasted_iota(jnp.int32, (c, LANES), 0)
        for j in range(per):
            lanes = slice(j * LANES, (j + 1) * LANES)
            x = x_ref[:, lanes]
            vec = vec_ref[:, lanes]
            xprev = xprev_ref[:, lanes]
            xc = vec[7:8] * x + vec[0:1]
            for k in range(1, 4):
                xs = jnp.where(rows < k, pltpu.roll(xprev, k, 0), pltpu.roll(x, k, 0))
                xc = xc + vec[7 - k:8 - k] * xs
            xprev_ref[:, lanes] = x
            xc_ref[:, lanes] = xc

            r, i, cl, a, s2 = _lru_gates(xc, vec, wr_ref.at[j], wi_ref.at[j])
            hh = jnp.sqrt(s2) * (i * xc)
            aa = a
            d = 1
            while d < c:
                keep = rows >= d
                hh = jnp.where(keep, aa * pltpu.roll(hh, d, 0) + hh, hh)
                aa = jnp.where(keep, aa * pltpu.roll(aa, d, 0), aa)
                d *= 2
            hfull = hh + aa * carry_ref[7:8, lanes]
            carry_ref[:, lanes] = hfull[c - 8:c, :]
            h_ref[:, lanes] = hfull
            gv = g_ref[:, lanes]
            y_ref[:, lanes] = (hfull * (gv * _sigmoid(gv))).astype(BF16)

    blk = lambda n, s: (s, n)
    return pl.pallas_call(
        body, name=name, grid=(ng, t // c),
        in_specs=[pl.BlockSpec((c, w), blk), pl.BlockSpec((c, w), lambda n, s: (s, ng + n)),
                  pl.BlockSpec((8, w), lambda n, s: (0, n)),
                  pl.BlockSpec((per, LANES, LANES), lambda n, s: (n, 0, 0)),
                  pl.BlockSpec((per, LANES, LANES), lambda n, s: (n, 0, 0))],
        out_specs=[pl.BlockSpec((c, w), blk)] * 3,
        out_shape=[jax.ShapeDtypeStruct((t, dr), BF16), jax.ShapeDtypeStruct((t, dr), F32),
                   jax.ShapeDtypeStruct((t, dr), F32)],
        scratch_shapes=[pltpu.VMEM((8, w), F32), pltpu.VMEM((c, w), F32)],
        compiler_params=_cparams(2),
    )(proj, proj, vec, wr, wi)


def _lru_bwd(proj, xc_all, h_all, dmix, vec, wr, wi, name):
    t = proj.shape[0]
    nb = wr.shape[0]
    dr = nb * LANES
    c = _tile(t, LRU_CHUNK)
    nc = t // c
    per = next(p for p in (4, 2, 1) if nb % p == 0)
    w = per * LANES
    ng = nb // per

    def body(x_ref, g_ref, xc_ref, h_ref, dy_ref, vec_ref, wr_ref, wi_ref,
             dx_ref, dg_ref, dwr_ref, dwi_ref, dvec_ref, qcarry_ref, dxc_next_ref):
        @pl.when(pl.program_id(1) == 0)
        def _():
            qcarry_ref[...] = jnp.zeros_like(qcarry_ref)
            dxc_next_ref[...] = jnp.zeros_like(dxc_next_ref)
            dwr_ref[...] = jnp.zeros_like(dwr_ref)
            dwi_ref[...] = jnp.zeros_like(dwi_ref)
            dvec_ref[...] = jnp.zeros_like(dvec_ref)

        rows = lax.broadcasted_iota(jnp.int32, (c, LANES), 0)
        for j in range(per):
            lanes = slice(j * LANES, (j + 1) * LANES)
            x = x_ref[:, lanes]
            xc = xc_ref[:, lanes]
            h = h_ref[:, lanes]
            dy = dy_ref[:, lanes]
            gv = g_ref[:, lanes]
            vec = vec_ref[:, lanes]

            r, i, cl, a, s2 = _lru_gates(xc, vec, wr_ref.at[j], wi_ref.at[j])
            s = jnp.sqrt(s2)
            ixc = i * xc
            u = s * ixc
            sg = _sigmoid(gv)
            dh = dy * (gv * sg)
            dg_ref[:, lanes] = (dy * h * (sg * (1.0 + gv * (1.0 - sg)))).astype(BF16)

            aa = a
            qq = a * dh
            d = 1
            while d < c:
                keep = rows < c - d
                qq = jnp.where(keep, qq + aa * pltpu.roll(qq, c - d, 0), qq)
                aa = jnp.where(keep, aa * pltpu.roll(aa, c - d, 0), aa)
                d *= 2
            qin = qcarry_ref[0:1, lanes]
            qfull = qq + aa * qin
            gt = dh + jnp.where(rows == c - 1, qin, pltpu.roll(qfull, c - 1, 0))
            qcarry_ref[:, lanes] = qfull[0:8, :]

            dla = gt * (h - u) - gt * ixc * (a * a) / s
            dixc = gt * s
            di = dixc * xc
            dxc = dixc * i
            dzr = (dla * cl) * (r * (1.0 - r))
            dzi = di * (i * (1.0 - i))
            dzr_bf = dzr.astype(BF16)
            dzi_bf = dzi.astype(BF16)
            dxc = dxc + lax.dot_general(dzr_bf, wr_ref[j], _NT, preferred_element_type=F32)
            dxc = dxc + lax.dot_general(dzi_bf, wi_ref[j], _NT, preferred_element_type=F32)
            xc_bf = xc.astype(BF16)
            dwr_ref[j] += lax.dot_general(xc_bf, dzr_bf, _TN, preferred_element_type=F32)
            dwi_ref[j] += lax.dot_general(xc_bf, dzi_bf, _TN, preferred_element_type=F32)

            lam = vec[3:4]
            dlam = jnp.sum(dla * r, axis=0, keepdims=True) * (LRU_C * _sigmoid(-lam))
            colsum = lambda v: jnp.sum(v, axis=0, keepdims=True)
            dxn = dxc_next_ref[:, lanes]
            dx = vec[7:8] * dxc
            dtaps = [None] * 4
            dtaps[3] = colsum(x * dxc)
            for k in range(1, 4):
                sh = jnp.where(rows < c - k, pltpu.roll(dxc, c - k, 0), pltpu.roll(dxn, c - k, 0))
                dx = dx + vec[7 - k:8 - k] * sh
                dtaps[3 - k] = colsum(x * sh)
            dxc_next_ref[:, lanes] = dxc
            dx_ref[:, lanes] = dx.astype(BF16)
            dvec_ref[j] += jnp.concatenate([colsum(dxc), colsum(dzr), colsum(dzi), dlam] + dtaps, axis=0)

    rev = lambda n, s: (nc - 1 - s, n)
    sq = lambda n, s: (n, 0, 0)
    return pl.pallas_call(
        body, name=name, grid=(ng, nc),
        in_specs=[pl.BlockSpec((c, w), rev), pl.BlockSpec((c, w), lambda n, s: (nc - 1 - s, ng + n)),
                  pl.BlockSpec((c, w), rev), pl.BlockSpec((c, w), rev), pl.BlockSpec((c, w), rev),
                  pl.BlockSpec((8, w), lambda n, s: (0, n)),
                  pl.BlockSpec((per, LANES, LANES), sq), pl.BlockSpec((per, LANES, LANES), sq)],
        out_specs=[pl.BlockSpec((c, w), rev), pl.BlockSpec((c, w), rev),
                   pl.BlockSpec((per, LANES, LANES), sq), pl.BlockSpec((per, LANES, LANES), sq),
                   pl.BlockSpec((per, 8, LANES), sq)],
        out_shape=[jax.ShapeDtypeStruct((t, dr), BF16), jax.ShapeDtypeStruct((t, dr), BF16),
                   jax.ShapeDtypeStruct((nb, LANES, LANES), F32), jax.ShapeDtypeStruct((nb, LANES, LANES), F32),
                   jax.ShapeDtypeStruct((nb, 8, LANES), F32)],
        scratch_shapes=[pltpu.VMEM((8, w), F32), pltpu.VMEM((c, w), F32)],
        compiler_params=_cparams(2),
    )(proj, proj, xc_all, h_all, dmix, vec, wr, wi)


SB_TQ = 1024
SB_TK = 256


def _sb_softplus(z, diag):
    sp = jnp.maximum(z, 0.0) + jnp.log(1.0 + jnp.exp(-jnp.abs(z)))
    mask = None
    if diag:
        mask = lax.broadcasted_iota(jnp.int32, z.shape, 1) < lax.broadcasted_iota(jnp.int32, z.shape, 0)
        sp = jnp.where(mask, sp, 0.0)
    return sp, mask


def _split_dot(v, m):
    hi = v.astype(BF16)
    lo = (v - hi.astype(F32)).astype(BF16)
    return jnp.dot(hi, m, preferred_element_type=F32) + jnp.dot(lo, m, preferred_element_type=F32)


def _tri_ones(kind, tk):
    jj = lax.broadcasted_iota(jnp.int32, (tk, tk), 0)
    ss = lax.broadcasted_iota(jnp.int32, (tk, tk), 1)
    rel = {"ge": jj >= ss, "le": jj <= ss}[kind]
    return jnp.where(rel, 1.0, 0.0).astype(BF16)


def _sb_fwd(proj, kv, name):
    t = proj.shape[0]
    ds = kv.shape[1] // 2
    nh = ds // LANES
    tq = _tile(t, SB_TQ)
    tk = _tile(tq, SB_TK)
    nd = tq // tk
    assert nd % 2 == 0 or t == tq
    scale = LANES ** -0.5

    def body(q_ref, g_ref, k_ref, v_ref, y_ref, o_ref, tl_ref, qbf_ref, acc_ref, run_ref, z_ref, w_ref):
        qi = pl.program_id(1)
        qbf_ref[...] = q_ref[...].astype(BF16)
        tri = _tri_ones("ge", tk)
        acc_ref[...] = jnp.zeros_like(acc_ref)
        run_ref[...] = jnp.zeros_like(run_ref)
        all_rows = [slice(s0, s0 + tk) for s0 in range(0, tq, tk)]

        def weights(zs, groups):
            sps = [_sb_softplus(z, dg) for z, (_, dg) in zip(zs, groups)]
            cums = [_split_dot(sp, tri) for sp, _ in sps]
            ws = []
            for z, (rows, dg), (_, mask), cum in zip(zs, groups, sps, cums):
                run = run_ref[rows, :]
                w = jnp.exp(z - cum - run)
                if dg:
                    w = jnp.where(mask, w, 0.0)
                run_ref[rows, :] = run + cum[:, 0:1]
                ws.append(w.astype(BF16))
            return ws

        n = qi * nd

        def rows_from(u):
            return [slice(s0, s0 + tk) for s0 in range(u * tk, tq, tk)]

        def logits_into(slot, kb, rows_list):
            k = k_ref[pl.ds(pl.multiple_of(kb * tk, tk), tk), :]
            seen = slice(rows_list[0].start, tq)
            z_ref[slot, seen, :] = lax.dot_general(qbf_ref[seen, :], k, _NT, preferred_element_type=F32) * scale

        def add_values(kb, rows_list):
            v = v_ref[pl.ds(pl.multiple_of(kb * tk, tk), tk), :]
            seen = slice(rows_list[0].start, tq)
            acc_ref[seen, :] += jnp.dot(w_ref[seen, :], v, preferred_element_type=F32)

        def weigh(slot, groups):
            for (rows, _), w in zip(groups, weights([z_ref[slot, rows, :] for rows, _ in groups], groups)):
                w_ref[rows, :] = w

        logits_into(0, n + nd - 1, rows_from(nd - 1))
        for i, u in enumerate(reversed(range(nd))):
            slot = i % 2
            if i > 0:
                add_values(n + u + 1, rows_from(u + 1))
            if u > 0:
                logits_into(1 - slot, n + u - 1, rows_from(u - 1))
            else:
                logits_into(1 - slot, jnp.maximum(n - 1, 0), all_rows)
            weigh(slot, [(rows, j == 0) for j, rows in enumerate(rows_from(u))])

        def half_step(j, slot):
            kb = n - 1 - j
            add_values(kb + 1, all_rows)
            logits_into(1 - slot, jnp.maximum(kb - 1, 0), all_rows)
            weigh(slot, [(rows, False) for rows in all_rows])

        def step(i, carry):
            half_step(2 * i, nd % 2)
            half_step(2 * i + 1, 1 - nd % 2)
            return carry

        lax.fori_loop(0, n // 2, step, 0)
        add_values(0, all_rows)
        o = acc_ref[...]
        o_ref[...] = o
        tl_ref[...] = jnp.broadcast_to(run_ref[...], (tq, LANES))
        gv = g_ref[...]
        y_ref[...] = (o * (gv * _sigmoid(gv))).astype(BF16)

    blk = lambda h, i: (i, h)
    return pl.pallas_call(
        body, name=name, grid=(nh, t // tq),
        in_specs=[pl.BlockSpec((tq, LANES), blk), pl.BlockSpec((tq, LANES), lambda h, i: (i, nh + h)),
                  pl.BlockSpec((t, LANES), lambda h, i: (0, h)), pl.BlockSpec((t, LANES), lambda h, i: (0, nh + h))],
        out_specs=[pl.BlockSpec((tq, LANES), blk)] * 3,
        out_shape=[jax.ShapeDtypeStruct((t, ds), BF16), jax.ShapeDtypeStruct((t, ds), F32),
                   jax.ShapeDtypeStruct((t, ds), F32)],
        scratch_shapes=[pltpu.VMEM((tq, LANES), BF16), pltpu.VMEM((tq, LANES), F32), pltpu.VMEM((tq, 1), F32),
                        pltpu.VMEM((2, tq, tk), F32), pltpu.VMEM((tq, tk), BF16)],
        compiler_params=_cparams(2),
    )(proj, proj, kv, kv)


def _sb_bwd(proj, kv, o_all, tl_all, dmix, name):
    t = proj.shape[0]
    ds = kv.shape[1] // 2
    nh = ds // LANES
    tq = _tile(t, SB_TQ)
    tk = _tile(tq, SB_TK)
    nd = tq // tk
    scale = LANES ** -0.5

    def body(q_ref, g_ref, k_ref, v_ref, o_ref, tl_ref, dy_ref, dq_ref, dg_ref, dk_ref, dv_ref,
             qbf_ref, dobf_ref, qt_ref, dot_ref, acc_ref, left_ref, rune_ref, z_ref, dw_ref, wp_ref, dzp_ref):
        qi = pl.program_id(1)

        @pl.when(qi == 0)
        def _():
            dk_ref[...] = jnp.zeros_like(dk_ref)
            dv_ref[...] = jnp.zeros_like(dv_ref)

        qbf_ref[...] = q_ref[...].astype(BF16)
        qt_ref[...] = q_ref[...].T.astype(BF16)
        gv = g_ref[...]
        sg = _sigmoid(gv)
        dy = dy_ref[...]
        do = dy * (gv * sg)
        dobf_ref[...] = do.astype(BF16)
        dot_ref[...] = do.T.astype(BF16)
        dg_ref[...] = (dy * o_ref[...] * (sg * (1.0 + gv * (1.0 - sg)))).astype(BF16)
        tri = _tri_ones("le", tk)
        acc_ref[...] = jnp.zeros_like(acc_ref)
        left_ref[...] = tl_ref[:, 0:1]
        rune_ref[...] = jnp.zeros_like(rune_ref)
        wp_ref[...] = jnp.zeros_like(wp_ref)
        dzp_ref[...] = jnp.zeros_like(dzp_ref)
        all_rows = [slice(s0, s0 + tk) for s0 in range(0, tq, tk)]

        def grads(zs, dws, groups):
            sps = [_sb_softplus(z, dg) for z, (_, dg) in zip(zs, groups)]
            cums = [_split_dot(sp, tri) for sp, _ in sps]
            ws, es, lbs = [], [], []
            for z, (rows, dg), (sp, mask), cum, dw in zip(zs, groups, sps, cums, dws):
                left = left_ref[rows, :]
                lb = z - sp
                w = jnp.exp(lb - (left - cum))
                if dg:
                    w = jnp.where(mask, w, 0.0)
                left_ref[rows, :] = left - cum[:, tk - 1:tk]
                ws.append(w.astype(BF16))
                es.append(dw * w)
                lbs.append(lb)
            cumes = [_split_dot(e, tri) for e in es]
            dzs = []
            for (rows, dg), (_, mask), lb, e, cume in zip(groups, sps, lbs, es, cumes):
                rune = rune_ref[rows, :]
                dz = (e - jnp.exp(lb) * (rune + cume)) * scale
                if dg:
                    dz = jnp.where(mask, dz, 0.0)
                rune_ref[rows, :] = rune + cume[:, tk - 1:tk]
                dzs.append(dz.astype(BF16))
            return dzs, ws

        n = qi * nd

        def rows_from(u):
            return [slice(s0, s0 + tk) for s0 in range(u * tk, tq, tk)]

        def logits_into(slot, kb, rows_list):
            k0 = pl.multiple_of(kb * tk, tk)
            k = k_ref[pl.ds(k0, tk), :]
            v = v_ref[pl.ds(k0, tk), :]
            seen = slice(rows_list[0].start, tq)
            z_ref[slot, seen, :] = lax.dot_general(qbf_ref[seen, :], k, _NT, preferred_element_type=F32) * scale
            dw_ref[slot, seen, :] = lax.dot_general(dobf_ref[seen, :], v, _NT, preferred_element_type=F32)

        def apply_stored(kb, u):
            k0 = pl.multiple_of(kb * tk, tk)
            k = k_ref[pl.ds(k0, tk), :]
            seen = slice(u * tk, tq)
            acc_ref[seen, :] += jnp.dot(dzp_ref[seen, :], k, preferred_element_type=F32)
            dk_ref[:, pl.ds(k0, tk)] += jnp.dot(qt_ref[:, seen], dzp_ref[seen, :], preferred_element_type=F32)
            dv_ref[:, pl.ds(k0, tk)] += jnp.dot(dot_ref[:, seen], wp_ref[seen, :], preferred_element_type=F32)

        def differentiate(slot, groups):
            dzs, ws = grads([z_ref[slot, rows, :] for rows, _ in groups], [dw_ref[slot, rows, :] for rows, _ in groups],
                            groups)
            for (rows, _), dz, w in zip(groups, dzs, ws):
                dzp_ref[rows, :] = dz
                wp_ref[rows, :] = w

        logits_into(0, 0, all_rows)

        def half_step(j, slot):
            apply_stored(jnp.maximum(j - 1, 0), 0)
            logits_into(1 - slot, j + 1, all_rows)
            differentiate(slot, [(rows, False) for rows in all_rows])

        def step(i, carry):
            half_step(2 * i, 0)
            half_step(2 * i + 1, 1)
            return carry

        lax.fori_loop(0, n // 2, step, 0)
        for u in range(nd):
            slot = u % 2
            if u == 0:
                apply_stored(jnp.maximum(n - 1, 0), 0)
            else:
                apply_stored(n + u - 1, u - 1)
            if u + 1 < nd:
                logits_into(1 - slot, n + u + 1, rows_from(u + 1))
            differentiate(slot, [(rows, j == 0) for j, rows in enumerate(rows_from(u))])
        apply_stored(n + nd - 1, nd - 1)
        dq_ref[...] = acc_ref[...].astype(BF16)

    blk = lambda h, i: (i, h)
    whole = lambda h, i: (0, h)
    return pl.pallas_call(
        body, name=name, grid=(nh, t // tq),
        in_specs=[pl.BlockSpec((tq, LANES), blk), pl.BlockSpec((tq, LANES), lambda h, i: (i, nh + h)),
                  pl.BlockSpec((t, LANES), whole), pl.BlockSpec((t, LANES), lambda h, i: (0, nh + h)),
                  pl.BlockSpec((tq, LANES), blk), pl.BlockSpec((tq, LANES), blk), pl.BlockSpec((tq, LANES), blk)],
        out_specs=[pl.BlockSpec((tq, LANES), blk), pl.BlockSpec((tq, LANES), blk),
                   pl.BlockSpec((LANES, t), lambda h, i: (h, 0)), pl.BlockSpec((LANES, t), lambda h, i: (h, 0))],
        out_shape=[jax.ShapeDtypeStruct((t, ds), BF16), jax.ShapeDtypeStruct((t, ds), BF16),
                   jax.ShapeDtypeStruct((ds, t), F32), jax.ShapeDtypeStruct((ds, t), F32)],
        scratch_shapes=[pltpu.VMEM((tq, LANES), BF16), pltpu.VMEM((tq, LANES), BF16),
                        pltpu.VMEM((LANES, tq), BF16), pltpu.VMEM((LANES, tq), BF16), pltpu.VMEM((tq, LANES), F32),
                        pltpu.VMEM((tq, 1), F32), pltpu.VMEM((tq, 1), F32),
                        pltpu.VMEM((2, tq, tk), F32), pltpu.VMEM((2, tq, tk), F32),
                        pltpu.VMEM((tq, tk), BF16), pltpu.VMEM((tq, tk), BF16)],
        compiler_params=_cparams(2),
    )(proj, proj, kv, kv, o_all, tl_all, dmix)


def _place():
    x, y, c = lax.axis_index("x"), lax.axis_index("y"), lax.axis_index("c")
    chips = [(1 - x, y), (x, 1 - y), (1 - x, 1 - y)]
    return x, y, c, chips


def _remote(src, dst, send_sems, recv_sems, k, to):
    return pltpu.make_async_remote_copy(src_ref=src, dst_ref=dst, send_sem=send_sems.at[k], recv_sem=recv_sems.at[k],
                                        device_id=to, device_id_type=MESH)


def _my_chip():
    return 2 * lax.axis_index("x") + lax.axis_index("y")


def _place_own(shard, name):
    r, w = shard.shape
    tr = _tile(r, FLAT_TR)

    def body(x_ref, o_ref):
        o_ref[...] = x_ref[...]

    return pl.pallas_call(
        body, name=name, out_shape=jax.ShapeDtypeStruct((N_CHIPS, r, w), shard.dtype), grid=(r // tr,),
        in_specs=[pl.BlockSpec((tr, w), lambda i: (i, 0))],
        out_specs=pl.BlockSpec((None, tr, w), lambda i: (_my_chip(), i, 0)),
        compiler_params=_cparams(1),
    )(shard)


def _chip_all_gather(shards, name):
    n = len(shards)

    def body(*refs):
        x_refs, out_refs, send_sems, recv_sems = refs[:n], refs[2 * n:3 * n], refs[3 * n], refs[3 * n + 1]
        x, y, c, chips = _place()
        me = 2 * x + y
        sibling = (x, y, 1 - c)

        def rows(t, core):
            rh = x_refs[t].shape[0] // 2
            return pl.ds(core * rh, rh)

        first = [_remote(x_refs[t].at[rows(t, c)], out_refs[t].at[me, rows(t, c)], send_sems, recv_sems, 6 * t + k,
                         (cx, cy, c)) for t in range(n) for k, (cx, cy) in enumerate(chips)]
        for cp in first:
            cp.start()
        passed = []
        for k, (cx, cy) in enumerate(chips):
            for t in range(n):
                got = out_refs[t].at[2 * cx + cy, rows(t, c)]
                _remote(got, got, send_sems, recv_sems, 6 * t + k, (cx, cy, c)).wait_recv()
                fwd = _remote(got, got, send_sems, recv_sems, 6 * t + 3 + k, sibling)
                fwd.start()
                passed.append(fwd)
        for k, (cx, cy) in enumerate(chips):
            for t in range(n):
                got = out_refs[t].at[2 * cx + cy, rows(t, 1 - c)]
                _remote(got, got, send_sems, recv_sems, 6 * t + 3 + k, sibling).wait_recv()
        for cp in first + passed:
            cp.wait_send()

    bufs = [_place_own(s, f"{name}_own{t}") for t, s in enumerate(shards)]
    return pl.pallas_call(
        body, name=name, in_specs=[ANY] * (2 * n), out_specs=[ANY] * n,
        out_shape=[jax.ShapeDtypeStruct((N_CHIPS,) + s.shape, s.dtype) for s in shards],
        input_output_aliases={n + t: t for t in range(n)},
        scratch_shapes=[pltpu.SemaphoreType.DMA((6 * n,)), pltpu.SemaphoreType.DMA((6 * n,))],
    )(*shards, *bufs)


def _chip_gather_start(shards, name):
    n = len(shards)

    def body(*refs):
        x_refs, buf_refs, send_sems, recv_sems, token = refs[:n], refs[n:2 * n], refs[2 * n], refs[2 * n + 1], refs[-1]
        x, y, c, chips = _place()
        me = 2 * x + y
        for t in range(n):
            for k, (cx, cy) in enumerate(chips):
                _remote(x_refs[t], buf_refs[t].at[me], send_sems, recv_sems, 3 * t + k, (cx, cy, c)).start()
        token[...] = jnp.zeros_like(token)

    bufs = [_place_own(s, f"{name}_own{t}") for t, s in enumerate(shards)]
    hbm = [pltpu.HBM(a.shape, a.dtype) for a in list(shards) + bufs]
    outs = pl.pallas_call(
        body, name=name, in_specs=[HBM] * (2 * n),
        out_shape=(pltpu.SemaphoreType.DMA((3 * n,)), pltpu.SemaphoreType.DMA((3 * n,)), *hbm,
                   jax.ShapeDtypeStruct((8, LANES), F32)),
        out_specs=(SEM, SEM, *[HBM] * (2 * n), pl.BlockSpec(memory_space=pltpu.VMEM)),
        input_output_aliases={t: 2 + t for t in range(2 * n)},
        compiler_params=pltpu.CompilerParams(has_side_effects=EFFECT),
    )(*[pltpu.with_memory_space_constraint(a, pltpu.HBM) for a in list(shards) + bufs])
    return outs[0], outs[1], outs[2:2 + n], outs[2 + n:2 + 2 * n], outs[-1]


def _chip_gather_wait(send_sems, recv_sems, shards, bufs, after, name):
    n = len(shards)

    def body(*refs):
        x_refs, buf_refs, send, recv = refs[:n], refs[n:2 * n], refs[2 * n], refs[2 * n + 1]
        x, y, c, chips = _place()
        for t in range(n):
            for k, (cx, cy) in enumerate(chips):
                cp = _remote(x_refs[t], buf_refs[t].at[2 * cx + cy], send, recv, 3 * t + k, (cx, cy, c))
                cp.wait_send()
                cp.wait_recv()

    hbm = [pltpu.HBM(a.shape, a.dtype) for a in list(shards) + list(bufs)]
    outs = pl.pallas_call(
        body, name=name, in_specs=[HBM] * (2 * n) + [SEM, SEM, ANY],
        out_shape=tuple(hbm), out_specs=tuple([HBM] * (2 * n)),
        input_output_aliases={t: t for t in range(2 * n)},
        compiler_params=pltpu.CompilerParams(has_side_effects=EFFECT),
    )(*shards, *bufs, send_sems, recv_sems, after)
    return outs[n:]


def _sibling_take_half(ss, name):
    n = len(ss)

    def body(*refs):
        s_refs, a_refs, send_sems, recv_sems = refs[:n], refs[n:2 * n], refs[2 * n], refs[2 * n + 1]
        x, y, c, _ = _place()
        cps = []
        for t in range(n):
            rh = s_refs[t].shape[1] // 2
            cps.append(_remote(s_refs[t].at[:, pl.ds((1 - c) * rh, rh), :], a_refs[t], send_sems, recv_sems, t,
                               (x, y, 1 - c)))
        for cp in cps:
            cp.start()
        for cp in cps:
            cp.wait()

    return pl.pallas_call(
        body, name=name, in_specs=[ANY] * n, out_specs=[ANY] * n,
        out_shape=[jax.ShapeDtypeStruct((s.shape[0], s.shape[1] // 2, s.shape[2]), s.dtype) for s in ss],
        scratch_shapes=[pltpu.SemaphoreType.DMA((n,)), pltpu.SemaphoreType.DMA((n,))],
    )(*ss)


def _pair_sum(s, a, dtype, name):
    n, r, w = s.shape
    rh = r // 2
    tr = _tile(rh, FLAT_TR)
    nblk = rh // tr

    def body(s_ref, a_ref, o_ref):
        o_ref[...] = (s_ref[...] + a_ref[...]).astype(dtype)

    return pl.pallas_call(
        body, name=name, out_shape=jax.ShapeDtypeStruct((n, rh, w), dtype), grid=(n, nblk),
        in_specs=[pl.BlockSpec((None, tr, w), lambda k, i: (k, lax.axis_index("c") * nblk + i, 0)),
                  pl.BlockSpec((None, tr, w), lambda k, i: (k, i, 0))],
        out_specs=pl.BlockSpec((None, tr, w), lambda k, i: (k, i, 0)),
        compiler_params=_cparams(2),
    )(s, a)


def _chip_scatter_start(ps, name):
    n = len(ps)

    def body(*refs):
        p_refs, b_refs, send_sems, recv_sems, token = refs[:n], refs[n:2 * n], refs[2 * n], refs[2 * n + 1], refs[-1]
        x, y, c, chips = _place()
        me = 2 * x + y
        for t in range(n):
            for k, (cx, cy) in enumerate(chips):
                _remote(p_refs[t].at[2 * cx + cy], b_refs[t].at[me], send_sems, recv_sems, 3 * t + k, (cx, cy, c)).start()
        token[...] = jnp.zeros_like(token)

    hbm = [pltpu.HBM(p.shape, p.dtype) for p in ps]
    outs = pl.pallas_call(
        body, name=name, in_specs=[HBM] * (2 * n),
        out_shape=(pltpu.SemaphoreType.DMA((3 * n,)), pltpu.SemaphoreType.DMA((3 * n,)), *hbm, *hbm,
                   jax.ShapeDtypeStruct((8, LANES), F32)),
        out_specs=(SEM, SEM, *[HBM] * (2 * n), pl.BlockSpec(memory_space=pltpu.VMEM)),
        input_output_aliases={t: 2 + t for t in range(2 * n)},
        compiler_params=pltpu.CompilerParams(has_side_effects=EFFECT),
    )(*[pltpu.with_memory_space_constraint(p, pltpu.HBM) for p in ps],
      *[pltpu.with_memory_space_constraint(lax.empty(p.shape, p.dtype), pltpu.HBM) for p in ps])
    return outs[0], outs[1], outs[2:2 + n], outs[2 + n:2 + 2 * n], outs[-1]


def _chip_scatter_wait(send_sems, recv_sems, ps, bs, after, name):
    n = len(ps)

    def body(*refs):
        p_refs, b_refs, send, recv = refs[:n], refs[n:2 * n], refs[2 * n], refs[2 * n + 1]
        x, y, c, chips = _place()
        for t in range(n):
            for k, (cx, cy) in enumerate(chips):
                cp = _remote(p_refs[t].at[2 * cx + cy], b_refs[t].at[2 * cx + cy], send, recv, 3 * t + k, (cx, cy, c))
                cp.wait_send()
                cp.wait_recv()

    hbm = [pltpu.HBM(p.shape, p.dtype) for p in ps]
    outs = pl.pallas_call(
        body, name=name, in_specs=[HBM] * (2 * n) + [SEM, SEM, ANY],
        out_shape=(*hbm, *hbm), out_specs=tuple([HBM] * (2 * n)),
        input_output_aliases={t: t for t in range(2 * n)},
        compiler_params=pltpu.CompilerParams(has_side_effects=EFFECT),
    )(*ps, *bs, send_sems, recv_sems, after)
    return outs[n:]


def _chip_sum(p, b, name, after=None):
    n, rh, w = p.shape
    tr = _tile(rh, FLAT_TR)
    nblk = rh // tr

    def body(p_ref, b0_ref, b1_ref, b2_ref, b3_ref, *rest):
        o_ref = rest[-1]
        me = _my_chip()
        own = p_ref[...]
        t = [jnp.where(me == k, own, b_ref[...]).astype(F32) for k, b_ref in enumerate((b0_ref, b1_ref, b2_ref, b3_ref))]
        o_ref[...] = ((t[0] + t[1]) + t[2]) + t[3]

    def other(k):
        return lambda i: (jnp.where(_my_chip() == k, (k + 1) % N_CHIPS, k), i, 0)

    return pl.pallas_call(
        body, name=name, out_shape=jax.ShapeDtypeStruct((2 * rh, w), F32), grid=(nblk,),
        in_specs=[pl.BlockSpec((None, tr, w), lambda i: (_my_chip(), i, 0))]
        + [pl.BlockSpec((None, tr, w), other(k)) for k in range(N_CHIPS)] + ([] if after is None else [ANY]),
        out_specs=pl.BlockSpec((tr, w), lambda i: (lax.axis_index("c") * nblk + i, 0)),
        compiler_params=_cparams(1),
    )(p, b, b, b, b, *([] if after is None else [after]))


def _sibling_join(gs, name):
    n = len(gs)

    def body(*refs):
        g_refs, send_sems, recv_sems = refs[n:2 * n], refs[2 * n], refs[2 * n + 1]
        x, y, c, _ = _place()
        cps = []
        for t in range(n):
            rh = g_refs[t].shape[0] // 2
            mine = g_refs[t].at[pl.ds(c * rh, rh)]
            cps.append(_remote(mine, mine, send_sems, recv_sems, t, (x, y, 1 - c)))
        for cp in cps:
            cp.start()
        for t in range(n):
            rh = g_refs[t].shape[0] // 2
            theirs = g_refs[t].at[pl.ds((1 - c) * rh, rh)]
            _remote(theirs, theirs, send_sems, recv_sems, t, (x, y, 1 - c)).wait_recv()
        for cp in cps:
            cp.wait_send()

    return pl.pallas_call(
        body, name=name, in_specs=[ANY] * n, out_specs=[ANY] * n,
        out_shape=[jax.ShapeDtypeStruct(g.shape, g.dtype) for g in gs],
        input_output_aliases={t: t for t in range(n)},
        scratch_shapes=[pltpu.SemaphoreType.DMA((n,)), pltpu.SemaphoreType.DMA((n,))],
    )(*gs)


def _adamw(g, w, m, v, name):
    r, wd = g.shape

    def body(g_ref, w_ref, m_ref, v_ref, d_ref, mo_ref, vo_ref):
        gv = g_ref[...]
        mn = ADAM_B1 * m_ref[...] + (1.0 - ADAM_B1) * gv
        vn = ADAM_B2 * v_ref[...] + (1.0 - ADAM_B2) * (gv * gv)
        m_hat = mn / (1.0 - ADAM_B1 ** ADAM_STEP)
        v_hat = vn / (1.0 - ADAM_B2 ** ADAM_STEP)
        d_ref[...] = -ADAM_LR * (m_hat / (jnp.sqrt(v_hat) + ADAM_EPS) + ADAM_WD * w_ref[...])
        mo_ref[...] = mn
        vo_ref[...] = vn

    tr = _tile(r, FLAT_TR)
    row = lambda i: (i, 0)
    spec = pl.BlockSpec((tr, wd), row)
    return pl.pallas_call(
        body, name=name, grid=(r // tr,), in_specs=[spec] * 4, out_specs=[spec] * 3,
        out_shape=[jax.ShapeDtypeStruct((r, wd), F32)] * 3,
        compiler_params=_cparams(1),
    )(g, w, m, v)


def _pair_sums(ss, names):
    a = _sibling_take_half(ss, "grad_sibling_half_" + names[0])
    return [_pair_sum(s, a_t, BF16 if n in BIG else F32, f"grad_pair_sum_{n}") for s, a_t, n in zip(ss, a, names)]


def _finish_reduce(ps, bs, names, after=None):
    g = [_chip_sum(p_t, b_t, f"grad_chip_sum_{n}", after if i == 0 else None)
         for i, (p_t, b_t, n) in enumerate(zip(ps, bs, names))]
    return _sibling_join(g, "grad_sibling_join_" + names[0])


WEIGHTS = ("mem_norm", "w_mem_kv", "norm_a", "w_in_a", "conv_w", "conv_b", "w_rec_gate", "b_rec_gate", "w_in_gate",
           "b_in_gate", "lru_lambda", "w_out_a", "kv_norm", "w_kv", "norm_b", "w_in_b", "w_out_b", "final_norm")
SHARD_DIM = {"mem_norm": None, "w_mem_kv": 1, "norm_a": 1, "w_in_a": 2, "conv_w": 2, "conv_b": 1, "w_rec_gate": None,
             "b_rec_gate": 1, "w_in_gate": None, "b_in_gate": 1, "lru_lambda": 1, "w_out_a": 1, "kv_norm": None,
             "w_kv": 1, "norm_b": None, "w_in_b": 2, "w_out_b": 1, "final_norm": None}
BIG = ("w_mem_kv", "w_in_a", "w_out_a", "w_kv", "w_in_b", "w_out_b")
SMALL = ("norm_a", "conv_w", "conv_b", "b_rec_gate", "b_in_gate", "lru_lambda")


def _pad_rows(flat, row_multiple):
    per = FLAT_W * row_multiple
    n = flat.shape[0]
    total = -(-n // per) * per
    return jnp.pad(flat, (0, total - n)).reshape(total // FLAT_W, FLAT_W)


def _flatten(parts, row_multiple):
    return _pad_rows(jnp.concatenate([p.reshape(-1) for p in parts]), row_multiple)


def _unflatten(flat2d, shapes):
    flat = flat2d.reshape(-1)
    out, off = [], 0
    for shp in shapes:
        n = 1
        for s in shp:
            n *= s
        out.append(flat[off:off + n].reshape(shp))
        off += n
    return out


LATE = ("w_kv", "w_in_b", "w_out_b")


def _bf16_rows(w):
    return w.astype(BF16).reshape(-1, w.shape[-1])


def _whole(gathered, shape, dim):
    return jnp.concatenate([gathered[k].reshape(shape) for k in range(N_CHIPS)], axis=dim)


def _gather_weights(local):
    early = [n for n in BIG if n not in LATE]
    small_shapes = [local[n].shape for n in SMALL]
    send_sems, recv_sems, shards, bufs, token = _chip_gather_start([_bf16_rows(local[n]) for n in LATE],
                                                                   "late_weights_gather_start")
    gathered = _chip_all_gather([_bf16_rows(local[n]) for n in early] + [_flatten([local[n] for n in SMALL], 16)],
                                "weights_all_gather")
    full = {n: _whole(g, local[n].shape, SHARD_DIM[n]) for n, g in zip(early, gathered)}
    per_chip = [_unflatten(gathered[-1][k], small_shapes) for k in range(N_CHIPS)]
    for i, n in enumerate(SMALL):
        full[n] = jnp.concatenate([per_chip[k][i] for k in range(N_CHIPS)], axis=SHARD_DIM[n])

    def late(after):
        got = _chip_gather_wait(send_sems, recv_sems, shards, bufs, after, "late_weights_gather_wait")
        return {n: _whole(g, local[n].shape, SHARD_DIM[n]) for n, g in zip(LATE, got)}

    return full, token, late


def _piece(g, name, k):
    dim = SHARD_DIM[name]
    if dim is None:
        return g
    n = g.shape[dim] // N_CHIPS
    return lax.slice_in_dim(g, k * n, (k + 1) * n, axis=dim)


def _local_grads(x, mem, tgt, wts, token, late):
    t, d = x.shape
    depth = wts["w_mem_kv"].shape[0]
    n_a = wts["w_in_a"].shape[0]
    n_b = wts["norm_b"].shape[0]
    nb = wts["w_rec_gate"].shape[1]
    dr = nb * LANES
    dm = wts["w_mem_kv"].shape[2] // 2
    row = lambda v: v.reshape(1, -1)

    wm_all = jnp.concatenate([wts["w_mem_kv"][l] for l in range(depth)], axis=1)
    memkv, memn_bf = _norm_matmul(mem, row(wts["mem_norm"]), wm_all, BF16, "mem_kv_proj", after=token)

    h = x
    saved = []
    vecs = []
    for l in range(n_a):
        proj, u_bf = _norm_matmul(h, row(wts["norm_a"][l]), wts["w_in_a"][l], F32, f"a{l}_in_proj")
        vec = jnp.concatenate([row(wts["conv_b"][l]), row(wts["b_rec_gate"][l]), row(wts["b_in_gate"][l]),
                               row(wts["lru_lambda"][l]), wts["conv_w"][l]], axis=0)
        vecs.append(vec)
        y_rnn, h_rnn, xc = _lru_fwd(proj, vec, wts["w_rec_gate"][l], wts["w_in_gate"][l], f"a{l}_lru_fwd")
        y_mem = _mem_attn_fwd(proj, memkv, l, dr, dm, f"a{l}_mem_fwd")
        mix = jnp.concatenate([y_rnn, y_mem], axis=1)
        h_next = _matmul_res(mix, wts["w_out_a"][l], h, f"a{l}_out_proj")
        saved.append((h, proj, u_bf, mix, h_rnn, xc))
        h = h_next

    h_kv = h
    wts = {**wts, **late(h_kv)}
    kv, ukv_bf = _norm_matmul(h_kv, row(wts["kv_norm"]), wts["w_kv"], BF16, "kv_proj")

    for j in range(n_b):
        l = n_a + j
        proj, u_bf = _norm_matmul(h, row(wts["norm_b"][j]), wts["w_in_b"][j], F32, f"b{j}_in_proj")
        y_sb, o_sb, tl_sb = _sb_fwd(proj, kv, f"b{j}_sb_fwd")
        y_mem = _mem_attn_fwd(proj, memkv, l, dr, dm, f"b{j}_mem_fwd")
        mix = jnp.concatenate([y_sb, y_mem], axis=1)
        h_next = _matmul_res(mix, wts["w_out_b"][j], h, f"b{j}_out_proj")
        saved.append((h, proj, u_bf, mix, o_sb, tl_sb))
        h = h_next

    dh, d_final, loss, dh_bf = _final_loss_bwd(h, row(wts["final_norm"]), tgt, "final_loss_bwd")

    grads = {"final_norm": d_final.reshape(-1)}
    big = {}
    dmemkv = [None] * depth
    g_norm_b = [None] * n_b
    dks, dvs = [], []
    for j in reversed(range(n_b)):
        l = n_a + j
        h_in, proj, u_bf, mix, o_sb, tl_sb = saved[l]
        dmix = _matmul_nt(dh_bf, wts["w_out_b"][j], f"b{j}_dmix")
        big["w_out_b"] = _dw_rows(mix, dh_bf, j, n_b, big.get("w_out_b"), f"b{j}_dw_out")
        dq, dg, dk, dv = _sb_bwd(proj, kv, o_sb, tl_sb, dmix, f"b{j}_sb_bwd")
        dqm, dgm, dkm, dvm = _mem_attn_bwd(proj, memkv, dmix, l, dr, dm, f"b{j}_mem_bwd")
        dmemkv[l] = (dkm, dvm)
        dproj = jnp.concatenate([dq, dg, dqm, dgm], axis=1)
        du = _matmul_nt(dproj, wts["w_in_b"][j], f"b{j}_du")
        big["w_in_b"] = _dw_cols(u_bf, dproj, j, n_b, big.get("w_in_b"), f"b{j}_dw_in")
        dh, dgn, dh_bf = _rms_bwd(du, h_in, row(wts["norm_b"][j]), dh, f"b{j}_rms_bwd")
        g_norm_b[j] = dgn.reshape(-1)
        dks.append(dk)
        dvs.append(dv)
    assert n_b == 2
    dkv = jnp.concatenate([_add_cast(dks[0], dks[1], "dk_sum").T, _add_cast(dvs[0], dvs[1], "dv_sum").T], axis=1)
    du = _matmul_nt(dkv, wts["w_kv"], "kv_du")
    big["w_kv"] = _dw_cols(ukv_bf, dkv, 0, 1, None, "kv_dw")
    dh, dgn, dh_bf = _rms_bwd(du, h_kv, row(wts["kv_norm"]), dh, "kv_rms_bwd")
    grads["kv_norm"] = dgn.reshape(-1)

    early_names = [n for n in BIG if n in big]
    early_p = _pair_sums([big[n].reshape(N_CHIPS, -1, big[n].shape[-1]) for n in early_names], early_names)
    send_sems, recv_sems, early_p, early_b, token = _chip_scatter_start(early_p, "grad_chip_scatter_start")

    g_norm_a = [None] * n_a
    g_wr, g_wi, g_vec = [None] * n_a, [None] * n_a, [None] * n_a
    for l in reversed(range(n_a)):
        h_in, proj, u_bf, mix, h_rnn, xc = saved[l]
        dmix = _matmul_nt(dh_bf, wts["w_out_a"][l], f"a{l}_dmix", after=token if l == n_a - 1 else None)
        big["w_out_a"] = _dw_rows(mix, dh_bf, l, n_a, big.get("w_out_a"), f"a{l}_dw_out")
        dx, dg, g_wr[l], g_wi[l], dvec = _lru_bwd(proj, xc, h_rnn, dmix, vecs[l], wts["w_rec_gate"][l],
                                                  wts["w_in_gate"][l], f"a{l}_lru_bwd")
        g_vec[l] = dvec.transpose(1, 0, 2).reshape(8, dr)
        dqm, dgm, dkm, dvm = _mem_attn_bwd(proj, memkv, dmix, l, dr, dm, f"a{l}_mem_bwd")
        dmemkv[l] = (dkm, dvm)
        dproj = jnp.concatenate([dx, dg, dqm, dgm], axis=1)
        du = _matmul_nt(dproj, wts["w_in_a"][l], f"a{l}_du")
        big["w_in_a"] = _dw_cols(u_bf, dproj, l, n_a, big.get("w_in_a"), f"a{l}_dw_in")
        dh, dgn, dh_bf = _rms_bwd(du, h_in, row(wts["norm_a"][l]), dh, f"a{l}_rms_bwd")
        g_norm_a[l] = dgn.reshape(-1)

    dmemkv_all = jnp.concatenate([jnp.concatenate(p, axis=1) for p in dmemkv], axis=1).astype(BF16)
    pk = d // N_CHIPS
    big["w_mem_kv"] = _matmul_tn(memn_bf, dmemkv_all, "mem_dw", pk, 2 * dm, (N_CHIPS, depth, pk, 2 * dm),
                                 (None, None, pk, 2 * dm), lambda i, j: (i, j, 0, 0))
    dmemn = _matmul_nt(dmemkv_all, wm_all, "mem_du")
    _, dgn, _ = _rms_bwd(dmemn, mem, row(wts["mem_norm"]), jnp.zeros_like(mem), "mem_rms_bwd")
    grads["mem_norm"] = dgn.reshape(-1)
    grads["norm_a"] = jnp.stack(g_norm_a)
    grads["w_rec_gate"] = jnp.stack(g_wr)
    grads["w_in_gate"] = jnp.stack(g_wi)
    gv = jnp.stack(g_vec)
    grads["conv_b"], grads["b_rec_gate"], grads["b_in_gate"], grads["lru_lambda"] = gv[:, 0], gv[:, 1], gv[:, 2], gv[:, 3]
    grads["conv_w"] = gv[:, 4:8]
    grads["norm_b"] = jnp.stack(g_norm_b)
    early_b = _chip_scatter_wait(send_sems, recv_sems, early_p, early_b, dh, "grad_chip_scatter_wait")
    early = (early_names, early_p, early_b)
    big = {n: g.reshape(N_CHIPS, -1, g.shape[-1]) for n, g in big.items() if n not in early_names}
    return loss, dh, grads, big, early


def kernel(x, mem, mem_norm, w_mem_kv, norm_a, w_in_a, conv_w, conv_b, w_rec_gate, b_rec_gate, w_in_gate, b_in_gate, lru_lambda, w_out_a, kv_norm, w_kv, norm_b, w_in_b, w_out_b, final_norm, loss_target, m_mem_norm, m_w_mem_kv, m_norm_a, m_w_in_a, m_conv_w, m_conv_b, m_w_rec_gate, m_b_rec_gate, m_w_in_gate, m_b_in_gate, m_lru_lambda, m_w_out_a, m_kv_norm, m_w_kv, m_norm_b, m_w_in_b, m_w_out_b, m_final_norm, v_mem_norm, v_w_mem_kv, v_norm_a, v_w_in_a, v_conv_w, v_conv_b, v_w_rec_gate, v_b_rec_gate, v_w_in_gate, v_b_in_gate, v_lru_lambda, v_w_out_a, v_kv_norm, v_w_kv, v_norm_b, v_w_in_b, v_w_out_b, v_final_norm):
    local = dict(mem_norm=mem_norm, w_mem_kv=w_mem_kv, norm_a=norm_a, w_in_a=w_in_a, conv_w=conv_w, conv_b=conv_b,
                 w_rec_gate=w_rec_gate, b_rec_gate=b_rec_gate, w_in_gate=w_in_gate, b_in_gate=b_in_gate,
                 lru_lambda=lru_lambda, w_out_a=w_out_a, kv_norm=kv_norm, w_kv=w_kv, norm_b=norm_b, w_in_b=w_in_b,
                 w_out_b=w_out_b, final_norm=final_norm)
    mom = dict(mem_norm=m_mem_norm, w_mem_kv=m_w_mem_kv, norm_a=m_norm_a, w_in_a=m_w_in_a, conv_w=m_conv_w,
               conv_b=m_conv_b, w_rec_gate=m_w_rec_gate, b_rec_gate=m_b_rec_gate, w_in_gate=m_w_in_gate,
               b_in_gate=m_b_in_gate, lru_lambda=m_lru_lambda, w_out_a=m_w_out_a, kv_norm=m_kv_norm, w_kv=m_w_kv,
               norm_b=m_norm_b, w_in_b=m_w_in_b, w_out_b=m_w_out_b, final_norm=m_final_norm)
    var = dict(mem_norm=v_mem_norm, w_mem_kv=v_w_mem_kv, norm_a=v_norm_a, w_in_a=v_w_in_a, conv_w=v_conv_w,
               conv_b=v_conv_b, w_rec_gate=v_w_rec_gate, b_rec_gate=v_b_rec_gate, w_in_gate=v_w_in_gate,
               b_in_gate=v_b_in_gate, lru_lambda=v_lru_lambda, w_out_a=v_w_out_a, kv_norm=v_kv_norm, w_kv=v_w_kv,
               norm_b=v_norm_b, w_in_b=v_w_in_b, w_out_b=v_w_out_b, final_norm=v_final_norm)

    wts, token, late = _gather_weights(local)
    for n in WEIGHTS:
        if SHARD_DIM[n] is None:
            wts[n] = local[n]
    wts["w_rec_gate"] = wts["w_rec_gate"].astype(BF16)
    wts["w_in_gate"] = wts["w_in_gate"].astype(BF16)

    loss, grad_x, grads, big, (early_names, early_p, early_b) = _local_grads(x[0], mem[0], loss_target[0], wts, token, late)

    rest = [n for n in WEIGHTS if n not in BIG]
    row_multiple = 2 * FLAT_TR
    s_rest = jnp.stack([_flatten([_piece(grads[n], n, k) for n in rest], row_multiple) for k in range(N_CHIPS)])
    late_names = [n for n in BIG if n in big] + ["rest"]
    late_p = _pair_sums([big[n] for n in late_names[:-1]] + [s_rest], late_names)
    send_sems, recv_sems, late_p, late_b, token = _chip_scatter_start(late_p, "grad_late_scatter_start")

    g_out, d_out, m_out, v_out = {}, {}, {}, {}

    def update(n, g):
        shape = local[n].shape
        flat = lambda a: a.reshape(-1, shape[-1])
        d, mo, vo = _adamw(g, flat(local[n]), flat(mom[n]), flat(var[n]), f"adamw_{n}")
        g_out[n], d_out[n], m_out[n], v_out[n] = (a.reshape(shape) for a in (g, d, mo, vo))
        return d

    for n, g in zip(early_names, _finish_reduce(early_p, early_b, early_names, after=token)):
        done = update(n, g)
    late_b = _chip_scatter_wait(send_sems, recv_sems, late_p, late_b, done, "grad_late_scatter_wait")
    reduced = dict(zip(late_names, _finish_reduce(late_p, late_b, late_names)))
    for n in late_names[:-1]:
        update(n, reduced[n])
    g_rest = reduced["rest"]
    d_rest, m_rest, v_rest = _adamw(g_rest, *(_flatten([src[n] for n in rest], row_multiple) for src in (local, mom, var)),
                                    "adamw_rest")
    shapes = [local[n].shape for n in rest]
    for out, flat2d in ((g_out, g_rest), (d_out, d_rest), (m_out, m_rest), (v_out, v_rest)):
        out.update(zip(rest, _unflatten(flat2d, shapes)))

    total_loss = lax.psum(loss[0, 0], MESH_AXES)
    return (total_loss, grad_x[None], *[g_out[n] for n in WEIGHTS], *[d_out[n] for n in WEIGHTS],
            *[m_out[n] for n in WEIGHTS], *[v_out[n] for n in WEIGHTS])
```
